```python
import jax, jax.numpy as jnp
from jax import lax
import numpy as np

D_MODEL = 2048
BATCH = 2
SEQ = 4096
DEPTH = 1

N_META = 16
N_ATT_HEADS = 8
N_KV_HEADS = 2
ATT_GROUP = N_ATT_HEADS // N_KV_HEADS
ATT_HEAD_DIM = 128
N_IDX_HEADS = 16
IDX_HEAD_DIM = 64
TOPK_MAX = 256
Q_BLOCK = 128
HG_HEADS = 8
HG_KEY_DIM = 128
HG_VAL_DIM = 128
HG_CHUNK = 64
D_FF = 5632
LN_EPS = 1e-5
RMS_EPS = 1e-6

ATT_Q_W = N_ATT_HEADS * ATT_HEAD_DIM
ATT_KV_W = N_KV_HEADS * ATT_HEAD_DIM
IDX_Q_W = N_IDX_HEADS * IDX_HEAD_DIM
HG_K_W = HG_HEADS * HG_KEY_DIM
HG_V_W = HG_HEADS * HG_VAL_DIM
IN_SPLITS = (ATT_Q_W, ATT_KV_W, ATT_KV_W, IDX_Q_W, IDX_HEAD_DIM, N_IDX_HEADS, HG_K_W, HG_K_W, HG_V_W, HG_V_W, D_MODEL, D_MODEL)
D_IN_PROJ = sum(IN_SPLITS)

kernel_name = 'hybrid_dsa_hgrn2_macaron_block'


def layer_norm(x, g, b):
    xf = x.astype(jnp.float32)
    mu = jnp.mean(xf, -1, keepdims=True)
    var = jnp.mean(jnp.square(xf - mu), -1, keepdims=True)
    return ((xf - mu) * lax.rsqrt(var + LN_EPS) * g + b).astype(x.dtype)


def swiglu(x, w_gate, w_up, w_down):
    return (jax.nn.silu(x @ w_gate) * (x @ w_up)) @ w_down


def alibi_slopes(n):
    return jnp.exp2(-8.0 * (jnp.arange(n, dtype=jnp.float32) + 1.0) / n)


def dsa_attention(q, k, v, iq, ik, iw):
    B, T = q.shape[0], q.shape[1]
    L = T - N_META
    k_sel = min(TOPK_MAX, L // 4)
    slopes = alibi_slopes(N_ATT_HEADS).reshape(N_KV_HEADS, ATT_GROUP)
    ik_real = ik[:, N_META:].astype(jnp.float32)
    key_pos = N_META + jnp.arange(L)
    meta_pos = jnp.arange(N_META)

    def block(t0, qlen):
        t_pos = t0 + jnp.arange(qlen)
        qb = lax.dynamic_slice_in_dim(q, t0, qlen, axis=1)
        iqb = lax.dynamic_slice_in_dim(iq, t0, qlen, axis=1).astype(jnp.float32)
        iwb = lax.dynamic_slice_in_dim(iw, t0, qlen, axis=1).astype(jnp.float32)
        s = jnp.einsum('bqhd,bsd->bqhs', iqb, ik_real) * (IDX_HEAD_DIM ** -0.5)
        score = jnp.einsum('bqhs,bqh->bqs', jax.nn.relu(s), iwb)
        score = jnp.where(key_pos[None, None, :] <= t_pos[None, :, None], score, -jnp.inf)
        vals, sel = lax.top_k(score, k_sel)
        pos = jnp.concatenate([jnp.broadcast_to(meta_pos, (B, qlen, N_META)), sel + N_META], -1)
        valid = jnp.concatenate([jnp.broadcast_to(meta_pos[None, None, :] <= t_pos[None, :, None], (B, qlen, N_META)), jnp.isfinite(vals)], -1)
        kg = jax.vmap(lambda kb, pb: kb[pb])(k, pos)
        vg = jax.vmap(lambda vb, pb: vb[pb])(v, pos)
        logits = jnp.einsum('bqkgd,bqnkd->bqkgn', qb, kg).astype(jnp.float32) * (ATT_HEAD_DIM ** -0.5)
        dist = jnp.abs(t_pos[None, :, None] - pos).astype(jnp.float32)
        logits = logits - slopes[None, None, :, :, None] * dist[:, :, None, None, :]
        logits = jnp.where(valid[:, :, None, None, :], logits, -jnp.inf)
        p = jax.nn.softmax(logits, axis=-1).astype(v.dtype)
        return jnp.einsum('bqkgn,bqnkd->bqkgd', p, vg)

    meta_out = block(0, N_META)
    n_blocks = L // Q_BLOCK
    real = lax.map(lambda i: block(N_META + i * Q_BLOCK, Q_BLOCK), jnp.arange(n_blocks))
    real = jnp.moveaxis(real, 0, 1).reshape(B, L, N_KV_HEADS, ATT_GROUP, ATT_HEAD_DIM)
    return jnp.concatenate([meta_out, real], axis=1).reshape(B, T, ATT_Q_W)


def _hgrn2_chunk(state, chunk):
    q, k, v, logf = chunk
    b = jnp.cumsum(logf, axis=2)
    causal = jnp.tril(jnp.ones((HG_CHUNK, HG_CHUNK), dtype=bool))
    diff = b[:, :, :, None, :] - b[:, :, None, :, :]
    decay = jnp.exp(jnp.where(causal[None, None, :, :, None], diff, -jnp.inf))
    scores = jnp.einsum('bhtd,bhsd,bhtsd->bhts', q, k, decay)
    o = jnp.einsum('bhts,bhsv->bhtv', scores, v) + jnp.einsum('bhtd,bhdv->bhtv', q * jnp.exp(b), state)
    b_last = b[:, :, -1:, :]
    state = jnp.exp(b_last[:, :, 0, :])[..., None] * state + jnp.einsum('bhsd,bhsv->bhdv', k * jnp.exp(b_last - b), v)
    return state, o


def hgrn2(q, f, i, gate, lb, norm_g):
    B, T = q.shape[0], q.shape[1]
    f32 = jnp.float32
    fg = lb + (1.0 - lb) * jax.nn.sigmoid(f.astype(f32))
    qh = jax.nn.silu(q.astype(f32)).reshape(B, T, HG_HEADS, HG_KEY_DIM)
    kh = (1.0 - fg).reshape(B, T, HG_HEADS, HG_KEY_DIM)
    logf = jnp.log(fg).reshape(B, T, HG_HEADS, HG_KEY_DIM)
    vh = i.astype(f32).reshape(B, T, HG_HEADS, HG_VAL_DIM)
    pad = HG_CHUNK - N_META

    def to_chunks(a):
        a = jnp.pad(a, ((0, 0), (pad, 0), (0, 0), (0, 0)))
        n = a.shape[1] // HG_CHUNK
        return a.reshape(B, n, HG_CHUNK, HG_HEADS, a.shape[-1]).transpose(1, 0, 3, 2, 4)

    state0 = jnp.zeros((B, HG_HEADS, HG_KEY_DIM, HG_VAL_DIM), f32)
    _, o = lax.scan(_hgrn2_chunk, state0, (to_chunks(qh), to_chunks(kh), to_chunks(vh), to_chunks(logf)))
    o = o.transpose(1, 0, 3, 2, 4).reshape(B, -1, HG_HEADS, HG_VAL_DIM)[:, pad:]
    o = o * lax.rsqrt(jnp.mean(jnp.square(o), -1, keepdims=True) + RMS_EPS) * norm_g
    o = o * jax.nn.silu(gate.astype(f32)).reshape(B, T, HG_HEADS, HG_VAL_DIM)
    return o.reshape(B, T, HG_V_W).astype(gate.dtype)


def hybrid_mixer(h, w_in, idx_k_norm_g, idx_k_norm_b, lb, hg_norm_g, w_branch_att, w_branch_hg, w_out):
    B, T, _ = h.shape
    offsets = np.cumsum(IN_SPLITS)[:-1].tolist()
    aq, ak, av, iq, ik, iw, hq, hf, hi, hgate, g_att, g_hg = jnp.split(h @ w_in, offsets, axis=-1)
    aq = aq.reshape(B, T, N_KV_HEADS, ATT_GROUP, ATT_HEAD_DIM)
    ak = ak.reshape(B, T, N_KV_HEADS, ATT_HEAD_DIM)
    av = av.reshape(B, T, N_KV_HEADS, ATT_HEAD_DIM)
    iq = iq.reshape(B, T, N_IDX_HEADS, IDX_HEAD_DIM)
    ik = layer_norm(ik, idx_k_norm_g, idx_k_norm_b)
    iw = iw * (N_IDX_HEADS ** -0.5)
    y_att = dsa_attention(aq, ak, av, iq, ik, iw)
    y_hg = hgrn2(hq, hf, hi, hgate, lb, hg_norm_g)
    merged = jax.nn.sigmoid(g_att) * (y_att @ w_branch_att) + jax.nn.sigmoid(g_hg) * (y_hg @ w_branch_hg)
    return merged @ w_out


def setup_inputs(seed: int = 0) -> dict:
    key = jax.random.key(seed)
    ks = jax.random.split(key, 24)
    f32 = jnp.float32
    beta = (8.0 * DEPTH) ** -0.25

    def dense(k, shape, fan_in, scale=1.0):
        return jax.random.normal(k, shape, f32) * (scale * fan_in ** -0.5)

    def gain(k, shape):
        return 1.0 + 0.02 * jax.random.normal(k, shape, f32)

    def bias(k, shape):
        return 0.02 * jax.random.normal(k, shape, f32)

    return {
        'x': jax.random.normal(ks[0], (BATCH, SEQ, D_MODEL), f32),
        'meta': jax.random.normal(ks[1], (N_META, D_MODEL), f32),
        'ffn1_w_gate': dense(ks[2], (DEPTH, D_MODEL, D_FF), D_MODEL),
        'ffn1_w_up': dense(ks[3], (DEPTH, D_MODEL, D_FF), D_MODEL),
        'ffn1_w_down': dense(ks[4], (DEPTH, D_FF, D_MODEL), D_FF, beta),
        'ln1_g': gain(ks[5], (DEPTH, D_MODEL)),
        'ln1_b': bias(ks[6], (DEPTH, D_MODEL)),
        'w_in': dense(ks[7], (DEPTH, D_MODEL, D_IN_PROJ), D_MODEL),
        'idx_k_norm_g': gain(ks[8], (DEPTH, IDX_HEAD_DIM)),
        'idx_k_norm_b': bias(ks[9], (DEPTH, IDX_HEAD_DIM)),
        'hg_lb_logits': 0.1 * jax.random.normal(ks[10], (DEPTH + 1, HG_K_W), f32),
        'hg_norm_g': gain(ks[11], (DEPTH, HG_HEADS, HG_VAL_DIM)),
        'w_branch_att': dense(ks[12], (DEPTH, ATT_Q_W, D_MODEL), ATT_Q_W),
        'w_branch_hg': dense(ks[13], (DEPTH, HG_V_W, D_MODEL), HG_V_W),
        'w_out': dense(ks[14], (DEPTH, D_MODEL, D_MODEL), D_MODEL, beta),
        'ln2_g': gain(ks[15], (DEPTH, D_MODEL)),
        'ln2_b': bias(ks[16], (DEPTH, D_MODEL)),
        'ffn2_w_gate': dense(ks[17], (DEPTH, D_MODEL, D_FF), D_MODEL),
        'ffn2_w_up': dense(ks[18], (DEPTH, D_MODEL, D_FF), D_MODEL),
        'ffn2_w_down': dense(ks[19], (DEPTH, D_FF, D_MODEL), D_FF, beta),
        'ln3_g': gain(ks[20], (DEPTH, D_MODEL)),
        'ln3_b': bias(ks[21], (DEPTH, D_MODEL)),
    }


def reference(x, meta, ffn1_w_gate, ffn1_w_up, ffn1_w_down, ln1_g, ln1_b, w_in, idx_k_norm_g, idx_k_norm_b, hg_lb_logits, hg_norm_g, w_branch_att, w_branch_hg, w_out, ln2_g, ln2_b, ffn2_w_gate, ffn2_w_up, ffn2_w_down, ln3_g, ln3_b):
    B = x.shape[0]
    alpha = (2.0 * DEPTH) ** 0.25
    lb_all = jnp.cumsum(jax.nn.softmax(hg_lb_logits.astype(jnp.float32), axis=0), axis=0)
    h = jnp.concatenate([jnp.broadcast_to(meta[None].astype(x.dtype), (B, N_META, D_MODEL)), x], axis=1)
    for l in range(DEPTH):
        h = layer_norm(alpha * h + 0.5 * swiglu(h, ffn1_w_gate[l], ffn1_w_up[l], ffn1_w_down[l]), ln1_g[l], ln1_b[l])
        mix = hybrid_mixer(h, w_in[l], idx_k_norm_g[l], idx_k_norm_b[l], lb_all[l], hg_norm_g[l], w_branch_att[l], w_branch_hg[l], w_out[l])
        h = layer_norm(alpha * h + mix, ln2_g[l], ln2_b[l])
        h = layer_norm(alpha * h + 0.5 * swiglu(h, ffn2_w_gate[l], ffn2_w_up[l], ffn2_w_down[l]), ln3_g[l], ln3_b[l])
    return h[:, N_META:]
```

```python
import functools

import jax
import jax.numpy as jnp
from jax import lax
from jax.experimental import pallas as pl
from jax.experimental.pallas import tpu as pltpu

D_MODEL = 2048
N_META = 16
N_ATT_HEADS = 8
N_KV_HEADS = 2
ATT_GROUP = N_ATT_HEADS // N_KV_HEADS
ATT_HEAD_DIM = 128
N_IDX_HEADS = 16
IDX_HEAD_DIM = 64
TOPK = 256
HG_HEADS = 8
HG_DIM = 128
D_FF = 5632
LN_EPS = 1e-5
RMS_EPS = 1e-6
ALPHA = 2.0 ** 0.25

ATT_Q_W = N_ATT_HEADS * ATT_HEAD_DIM
ATT_KV_W = N_KV_HEADS * ATT_HEAD_DIM
IDX_Q_W = N_IDX_HEADS * IDX_HEAD_DIM
HG_W = HG_HEADS * HG_DIM

LANES = 128
VMEM_LIMIT = 56 * 1024 * 1024

F32 = jnp.float32
BF16 = jnp.bfloat16
INT_MIN = -2 ** 31


def _dot(a, b):
    return jnp.dot(a, b, preferred_element_type=F32)


def _dot_nt(a, b):
    return lax.dot_general(a, b, (((1,), (1,)), ((), ())), preferred_element_type=F32)


def _dot_tn(a, b):
    return lax.dot_general(a, b, (((0,), (0,)), ((), ())), preferred_element_type=F32)


def _layer_norm(y, g, b):
    mu = jnp.mean(y, axis=-1, keepdims=True)
    d = y - mu
    var = jnp.mean(d * d, axis=-1, keepdims=True)
    return d * lax.rsqrt(var + LN_EPS) * g + b


def _silu(x):
    return x * jax.nn.sigmoid(x)


def _ffn_ln_kernel(x_ref, wg_ref, wu_ref, wd_ref, g_ref, b_ref, o_ref, ob_ref, xb_scr, acc_scr):
    j = pl.program_id(1)

    @pl.when(j == 0)
    def _():
        xb_scr[...] = x_ref[...].astype(BF16)
        acc_scr[...] = jnp.zeros_like(acc_scr)

    xb = xb_scr[...]
    gate = _dot(xb, wg_ref[...])
    up = _dot(xb, wu_ref[...])
    a = (_silu(gate) * up).astype(BF16)
    acc_scr[...] += _dot(a, wd_ref[...])

    @pl.when(j == pl.num_programs(1) - 1)
    def _():
        y = ALPHA * x_ref[...] + 0.5 * acc_scr[...]
        out = _layer_norm(y, g_ref[...], b_ref[...])
        o_ref[...] = out
        ob_ref[...] = out.astype(BF16)


def _ffn_ln(x, wg, wu, wd, g, b, *, tm, tf):
    m = x.shape[0]
    grid = (m // tm, D_FF // tf)
    return pl.pallas_call(
        _ffn_ln_kernel,
        out_shape=(jax.ShapeDtypeStruct((m, D_MODEL), F32), jax.ShapeDtypeStruct((m, D_MODEL), BF16)),
        grid=grid,
        in_specs=[
            pl.BlockSpec((tm, D_MODEL), lambda i, j: (i, 0)),
            pl.BlockSpec((D_MODEL, tf), lambda i, j: (0, j)),
            pl.BlockSpec((D_MODEL, tf), lambda i, j: (0, j)),
            pl.BlockSpec((tf, D_MODEL), lambda i, j: (j, 0)),
            pl.BlockSpec((1, D_MODEL), lambda i, j: (0, 0)),
            pl.BlockSpec((1, D_MODEL), lambda i, j: (0, 0)),
        ],
        out_specs=(
            pl.BlockSpec((tm, D_MODEL), lambda i, j: (i, 0)),
            pl.BlockSpec((tm, D_MODEL), lambda i, j: (i, 0)),
        ),
        scratch_shapes=[pltpu.VMEM((tm, D_MODEL), BF16), pltpu.VMEM((tm, D_MODEL), F32)],
        compiler_params=pltpu.CompilerParams(
            dimension_semantics=("parallel", "arbitrary"), vmem_limit_bytes=VMEM_LIMIT),
        name="ffn_ln",
    )(x, wg, wu, wd, g, b)


def _mm_kernel(x_ref, w_ref, o_ref):
    o_ref[...] = _dot(x_ref[...], w_ref[...]).astype(o_ref.dtype)


def _idx_kernel(x_ref, w_ref, g_ref, b_ref, o_ref):
    p = _dot(x_ref[...], w_ref[...])
    lane = lax.broadcasted_iota(jnp.int32, p.shape, 1)
    is_k = lane < IDX_HEAD_DIM
    mu = jnp.sum(jnp.where(is_k, p, 0.0), axis=-1, keepdims=True) * (1.0 / IDX_HEAD_DIM)
    d = jnp.where(is_k, p - mu, 0.0)
    var = jnp.sum(d * d, axis=-1, keepdims=True) * (1.0 / IDX_HEAD_DIM)
    kn = d * lax.rsqrt(var + LN_EPS) * g_ref[...] + b_ref[...]
    w_scale = (N_IDX_HEADS ** -0.5) * (IDX_HEAD_DIM ** -0.5)
    o_ref[...] = jnp.where(is_k, kn, p * w_scale)


def _proj(xb, w, out_dtype, *, tm, tn):
    m, n = xb.shape[0], w.shape[1]
    return pl.pallas_call(
        _mm_kernel,
        out_shape=jax.ShapeDtypeStruct((m, n), out_dtype),
        grid=(m // tm, n // tn),
        in_specs=[pl.BlockSpec((tm, D_MODEL), lambda i, j: (i, 0)),
                  pl.BlockSpec((D_MODEL, tn), lambda i, j: (0, j))],
        out_specs=pl.BlockSpec((tm, tn), lambda i, j: (i, j)),
        compiler_params=pltpu.CompilerParams(
            dimension_semantics=("parallel", "arbitrary"), vmem_limit_bytes=VMEM_LIMIT),
        name="proj",
    )(xb, w)


def _proj_idx(xb, w, g, b, *, tm):
    m = xb.shape[0]
    return pl.pallas_call(
        _idx_kernel,
        out_shape=jax.ShapeDtypeStruct((m, LANES), F32),
        grid=(m // tm,),
        in_specs=[pl.BlockSpec((tm, D_MODEL), lambda i: (i, 0)),
                  pl.BlockSpec((D_MODEL, LANES), lambda i: (0, 0)),
                  pl.BlockSpec((1, LANES), lambda i: (0, 0)),
                  pl.BlockSpec((1, LANES), lambda i: (0, 0))],
        out_specs=pl.BlockSpec((tm, LANES), lambda i: (i, 0)),
        compiler_params=pltpu.CompilerParams(
            dimension_semantics=("parallel",), vmem_limit_bytes=VMEM_LIMIT),
        name="proj_idx",
    )(xb, w, g, b)


def _attn_kernel(aq_ref, iq_ref, iw_ref, iklo_ref, ikhi_ref, k_ref, v_ref, km_ref, vm_ref,
                 o_ref, key_scr, m_scr, l_scr, acc_scr, *, tq, kc):
    qi = pl.program_id(1)
    n_kc = ((qi + 1) * tq + kc - 1) // kc
    row = qi * tq + lax.broadcasted_iota(jnp.int32, (tq, 1), 0)

    iw = iw_ref[...]

    def score_body(j, carry):
        off = pl.multiple_of(j * kc, kc)
        klo = iklo_ref[pl.ds(off, kc), :]
        khi = ikhi_ref[pl.ds(off, kc), :]
        acc = jnp.zeros((tq, kc), F32)
        for p in range(N_IDX_HEADS // 2):
            q2 = iq_ref[:, p * LANES:(p + 1) * LANES]
            acc = acc + jnp.maximum(_dot_nt(q2, klo), 0.0) * iw[:, 2 * p:2 * p + 1]
            acc = acc + jnp.maximum(_dot_nt(q2, khi), 0.0) * iw[:, 2 * p + 1:2 * p + 2]
        col = off + lax.broadcasted_iota(jnp.int32, (tq, kc), 1)
        bits = pltpu.bitcast(acc, jnp.int32)
        key = bits ^ ((bits >> 31) & 0x7FFFFFFF)
        key_scr[:, pl.ds(off, kc)] = jnp.where(col <= row, key, INT_MIN)
        return carry

    lax.fori_loop(0, n_kc, score_body, 0)

    def count_ge(cand):
        cb = jnp.broadcast_to(cand, (tq, LANES))

        def body(j, acc):
            off = pl.multiple_of(j * kc, kc)
            for c in range(kc // LANES):
                ks = key_scr[:, pl.ds(off + c * LANES, LANES)]
                acc = acc + jnp.where(ks >= cb, 1.0, 0.0)
            return acc

        acc = lax.fori_loop(0, n_kc, body, jnp.zeros((tq, LANES), F32))
        return jnp.sum(acc, axis=1, keepdims=True)

    kf = float(TOPK)
    thr = jnp.where(count_ge(jnp.zeros((tq, 1), jnp.int32)) >= kf, 0, INT_MIN).astype(jnp.int32)

    def bit_body(i, thr):
        cand = thr | jnp.left_shift(jnp.int32(1), 30 - i)
        return jnp.where(count_ge(cand) >= kf, cand, thr)

    thr = lax.fori_loop(0, 31, bit_body, thr)
    thr = jnp.maximum(thr, INT_MIN + 1)

    scale = ATT_HEAD_DIM ** -0.5
    pos_row = (row + N_META).astype(F32)
    lane_f = lax.broadcasted_iota(jnp.int32, (1, LANES), 1).astype(F32)
    meta_ok = lax.broadcasted_iota(jnp.int32, (1, LANES), 1) < N_META
    for kvh in range(N_KV_HEADS):
        heads = [kvh * ATT_GROUP + g for g in range(ATT_GROUP)]
        slopes = [2.0 ** (-8.0 * (h + 1) / N_ATT_HEADS) for h in heads]
        q4 = jnp.concatenate([aq_ref[:, h * LANES:(h + 1) * LANES] for h in heads], axis=0)
        km = km_ref[:, kvh * LANES:(kvh + 1) * LANES]
        vm = vm_ref[:, kvh * LANES:(kvh + 1) * LANES]
        sm = _dot_nt(q4, km) * scale
        dm = lane_f - pos_row
        lm = jnp.concatenate(
            [jnp.where(meta_ok, sm[g * tq:(g + 1) * tq] + slopes[g] * dm, -jnp.inf)
             for g in range(ATT_GROUP)], axis=0)
        m0 = jnp.max(lm, axis=1, keepdims=True)
        p0 = jnp.exp(lm - m0)
        m_scr[...] = m0
        l_scr[...] = jnp.sum(p0, axis=1, keepdims=True)
        acc_scr[...] = _dot(p0.astype(BF16), vm)

        def att_body(j, carry):
            off = pl.multiple_of(j * kc, kc)
            kblk = k_ref[pl.ds(off, kc), kvh * LANES:(kvh + 1) * LANES]
            vblk = v_ref[pl.ds(off, kc), kvh * LANES:(kvh + 1) * LANES]
            s = _dot_nt(q4, kblk) * scale
            sel = key_scr[:, pl.ds(off, kc)] >= thr
            col = off + lax.broadcasted_iota(jnp.int32, (tq, kc), 1)
            d = (col - row).astype(F32)
            lg = jnp.concatenate(
                [jnp.where(sel, s[g * tq:(g + 1) * tq] + slopes[g] * d, -jnp.inf)
                 for g in range(ATT_GROUP)], axis=0)
            m_old = m_scr[...]
            m_new = jnp.maximum(m_old, jnp.max(lg, axis=1, keepdims=True))
            a = jnp.exp(m_old - m_new)
            p = jnp.exp(lg - m_new)
            m_scr[...] = m_new
            l_scr[...] = a * l_scr[...] + jnp.sum(p, axis=1, keepdims=True)
            acc_scr[...] = a * acc_scr[...] + _dot(p.astype(BF16), vblk)
            return carry

        lax.fori_loop(0, n_kc, att_body, 0)
        out = acc_scr[...] / l_scr[...]
        for g, h in enumerate(heads):
            o_ref[:, h * LANES:(h + 1) * LANES] = out[g * tq:(g + 1) * tq].astype(o_ref.dtype)


def _attention(p_att, iw, ik_lo, ik_hi, km, vm, *, batch, seq, tq, kc):
    m = batch * seq
    nq = seq // tq
    kcol = (ATT_Q_W + IDX_Q_W) // ATT_KV_W
    return pl.pallas_call(
        functools.partial(_attn_kernel, tq=tq, kc=kc),
        out_shape=jax.ShapeDtypeStruct((m, ATT_Q_W), BF16),
        grid=(batch, nq),
        in_specs=[
            pl.BlockSpec((tq, ATT_Q_W), lambda b, q: (b * nq + q, 0)),
            pl.BlockSpec((tq, IDX_Q_W), lambda b, q: (b * nq + q, 1)),
            pl.BlockSpec((tq, N_IDX_HEADS), lambda b, q: (b * nq + q, 0)),
            pl.BlockSpec((seq, LANES), lambda b, q: (b, 0)),
            pl.BlockSpec((seq, LANES), lambda b, q: (b, 0)),
            pl.BlockSpec((seq, ATT_KV_W), lambda b, q: (b, kcol)),
            pl.BlockSpec((seq, ATT_KV_W), lambda b, q: (b, kcol + 1)),
            pl.BlockSpec((LANES, ATT_KV_W), lambda b, q: (0, 0)),
            pl.BlockSpec((LANES, ATT_KV_W), lambda b, q: (0, 0)),
        ],
        out_specs=pl.BlockSpec((tq, ATT_Q_W), lambda b, q: (b * nq + q, 0)),
        scratch_shapes=[
            pltpu.VMEM((tq, seq), jnp.int32),
            pltpu.VMEM((ATT_GROUP * tq, 1), F32),
            pltpu.VMEM((ATT_GROUP * tq, 1), F32),
            pltpu.VMEM((ATT_GROUP * tq, ATT_HEAD_DIM), F32),
        ],
        compiler_params=pltpu.CompilerParams(
            dimension_semantics=("parallel", "arbitrary"), vmem_limit_bytes=VMEM_LIMIT),
        name="dsa_attention",
    )(p_att, p_att, iw, ik_lo, ik_hi, p_att, p_att, km, vm)


def _split3(x):
    hi = x.astype(BF16)
    r = x - hi.astype(F32)
    mid = r.astype(BF16)
    lo = (r - mid.astype(F32)).astype(BF16)
    return hi, mid, lo


def _hgrn_kernel(hf_ref, hr_q_ref, hr_i_ref, hr_g_ref, lb_ref, ng_ref, s0_ref,
                 y_ref, sT_out_ref, sT_scr, *, chunk, sub, heads):
    c_idx = pl.program_id(2)

    @pl.when(c_idx == 0)
    def _():
        sT_scr[...] = s0_ref[...]

    n_sub = chunk // sub
    tri = (lax.broadcasted_iota(jnp.int32, (chunk, chunk), 0)
           >= lax.broadcasted_iota(jnp.int32, (chunk, chunk), 1)).astype(BF16)
    t_iota = lax.broadcasted_iota(jnp.int32, (sub, LANES), 0)

    for g in range(heads):
        cs = slice(g * HG_DIM, (g + 1) * HG_DIM)
        lb = lb_ref[:, cs]
        q = _silu(hr_q_ref[:, cs].astype(F32))
        fg = lb + (1.0 - lb) * jax.nn.sigmoid(hf_ref[:, cs])
        kk = 1.0 - fg
        logf = jnp.log(fg)
        v = hr_i_ref[:, cs].astype(F32)
        vb = v.astype(BF16)
        l_hi, l_mid, l_lo = _split3(logf)
        b = _dot(tri, l_hi) + _dot(tri, l_mid) + _dot(tri, l_lo)
        sT = sT_scr[g]
        b_last = b[chunk - 1:chunk, :]
        o_state = _dot_nt((q * jnp.exp(b)).astype(BF16), sT.astype(BF16))
        khat = (kk * jnp.exp(b_last - b)).astype(BF16)
        sT_scr[g] = sT * jnp.exp(b_last) + _dot_tn(vb, khat)
        rows = []
        for i in range(n_sub):
            r0, r1 = i * sub, (i + 1) * sub
            bs, qs, ks, vs = b[r0:r1], q[r0:r1], kk[r0:r1], v[r0:r1]
            o_i = o_state[r0:r1]
            if i > 0:
                r_i = b[r0 - 1:r0, :]
                qt = (qs * jnp.exp(bs - r_i)).astype(BF16)
                kt = (kk[:r0] * jnp.exp(r_i - b[:r0])).astype(BF16)
                a_off = _dot_nt(qt, kt)
                o_i = o_i + _dot(a_off.astype(BF16), vb[:r0])
            for s in range(sub):
                e = jnp.exp(jnp.where(t_iota >= s, bs - bs[s:s + 1], -jnp.inf))
                a_col = jnp.sum(qs * e * ks[s:s + 1], axis=1, keepdims=True)
                o_i = o_i + a_col * vs[s:s + 1]
            rows.append(o_i)
        o = jnp.concatenate(rows, axis=0) if n_sub > 1 else rows[0]
        o = o * lax.rsqrt(jnp.mean(o * o, axis=-1, keepdims=True) + RMS_EPS) * ng_ref[:, cs]
        o = o * _silu(hr_g_ref[:, cs].astype(F32))
        y_ref[:, cs] = o.astype(y_ref.dtype)

    @pl.when(c_idx == pl.num_programs(2) - 1)
    def _():
        sT_out_ref[0] = sT_scr[...]


def _hgrn(hf, hr, lb, ng, s0, *, batch, seq, chunk, sub, heads):
    m = batch * seq
    nc = seq // chunk
    hw = heads * HG_DIM
    nhg = HG_HEADS // heads
    return pl.pallas_call(
        functools.partial(_hgrn_kernel, chunk=chunk, sub=sub, heads=heads),
        out_shape=(jax.ShapeDtypeStruct((m, HG_W), BF16),
                   jax.ShapeDtypeStruct((batch, HG_HEADS, HG_DIM, HG_DIM), F32)),
        grid=(batch, nhg, nc),
        in_specs=[
            pl.BlockSpec((chunk, hw), lambda b, h, c: (b * nc + c, h)),
            pl.BlockSpec((chunk, hw), lambda b, h, c: (b * nc + c, h)),
            pl.BlockSpec((chunk, hw), lambda b, h, c: (b * nc + c, nhg + h)),
            pl.BlockSpec((chunk, hw), lambda b, h, c: (b * nc + c, 2 * nhg + h)),
            pl.BlockSpec((1, hw), lambda b, h, c: (0, h)),
            pl.BlockSpec((1, hw), lambda b, h, c: (0, h)),
            pl.BlockSpec((heads, HG_DIM, HG_DIM), lambda b, h, c: (h, 0, 0)),
        ],
        out_specs=(
            pl.BlockSpec((chunk, hw), lambda b, h, c: (b * nc + c, h)),
            pl.BlockSpec((1, heads, HG_DIM, HG_DIM), lambda b, h, c: (b, h, 0, 0)),
        ),
        scratch_shapes=[pltpu.VMEM((heads, HG_DIM, HG_DIM), F32)],
        compiler_params=pltpu.CompilerParams(
            dimension_semantics=("parallel", "parallel", "arbitrary"), vmem_limit_bytes=VMEM_LIMIT),
        name="hgrn2",
    )(hf, hr, hr, hr, lb, ng, s0)


def _merge_kernel(ya_ref, yh_ref, gts_a_ref, gts_h_ref, h_ref, wa_ref, wh_ref, wo_ref, g_ref, b_ref,
                  o_ref, ob_ref, acc_scr):
    j = pl.program_id(1)

    @pl.when(j == 0)
    def _():
        acc_scr[...] = jnp.zeros_like(acc_scr)

    ga = jax.nn.sigmoid(gts_a_ref[...].astype(F32))
    gh = jax.nn.sigmoid(gts_h_ref[...].astype(F32))
    merged = ga * _dot(ya_ref[...], wa_ref[...]) + gh * _dot(yh_ref[...], wh_ref[...])
    acc_scr[...] += _dot(merged.astype(BF16), wo_ref[...])

    @pl.when(j == pl.num_programs(1) - 1)
    def _():
        out = _layer_norm(ALPHA * h_ref[...] + acc_scr[...], g_ref[...], b_ref[...])
        o_ref[...] = out
        ob_ref[...] = out.astype(BF16)


def _merge(ya, yh, gates, h1, wa, wh, wo, g, b, *, tm, tn):
    m = ya.shape[0]
    nj = D_MODEL // tn
    return pl.pallas_call(
        _merge_kernel,
        out_shape=(jax.ShapeDtypeStruct((m, D_MODEL), F32), jax.ShapeDtypeStruct((m, D_MODEL), BF16)),
        grid=(m // tm, nj),
        in_specs=[
            pl.BlockSpec((tm, ATT_Q_W), lambda i, j: (i, 0)),
            pl.BlockSpec((tm, HG_W), lambda i, j: (i, 0)),
            pl.BlockSpec((tm, tn), lambda i, j: (i, j)),
            pl.BlockSpec((tm, tn), lambda i, j: (i, nj + j)),
            pl.BlockSpec((tm, D_MODEL), lambda i, j: (i, 0)),
            pl.BlockSpec((ATT_Q_W, tn), lambda i, j: (0, j)),
            pl.BlockSpec((HG_W, tn), lambda i, j: (0, j)),
            pl.BlockSpec((tn, D_MODEL), lambda i, j: (j, 0)),
            pl.BlockSpec((1, D_MODEL), lambda i, j: (0, 0)),
            pl.BlockSpec((1, D_MODEL), lambda i, j: (0, 0)),
        ],
        out_specs=(
            pl.BlockSpec((tm, D_MODEL), lambda i, j: (i, 0)),
            pl.BlockSpec((tm, D_MODEL), lambda i, j: (i, 0)),
        ),
        scratch_shapes=[pltpu.VMEM((tm, D_MODEL), F32)],
        compiler_params=pltpu.CompilerParams(
            dimension_semantics=("parallel", "arbitrary"), vmem_limit_bytes=VMEM_LIMIT),
        name="merge_ln",
    )(ya, yh, gates, gates, h1, wa, wh, wo, g, b)


def kernel(x, meta, ffn1_w_gate, ffn1_w_up, ffn1_w_down, ln1_g, ln1_b, w_in, idx_k_norm_g, idx_k_norm_b,
           hg_lb_logits, hg_norm_g, w_branch_att, w_branch_hg, w_out, ln2_g, ln2_b,
           ffn2_w_gate, ffn2_w_up, ffn2_w_down, ln3_g, ln3_b):
    batch, seq, _ = x.shape
    m = batch * seq
    xr = x.reshape(m, D_MODEL)
    bf = lambda w: w.astype(BF16)
    row = lambda v: v.reshape(1, -1)

    splits = (ATT_Q_W, ATT_KV_W, ATT_KV_W, IDX_Q_W, IDX_HEAD_DIM, N_IDX_HEADS, HG_W, HG_W, HG_W, HG_W,
              D_MODEL, D_MODEL)
    offs = [0]
    for s in splits:
        offs.append(offs[-1] + s)
    w = w_in[0]
    seg = lambda i: w[:, offs[i]:offs[i + 1]]
    w_att = bf(jnp.concatenate([seg(0), seg(3), seg(1), seg(2)], axis=1))
    w_idx = bf(jnp.pad(jnp.concatenate([seg(4), seg(5)], axis=1),
                       ((0, 0), (0, LANES - IDX_HEAD_DIM - N_IDX_HEADS))))
    w_hf = bf(seg(7))
    w_hr = bf(jnp.concatenate([seg(6), seg(8), seg(9)], axis=1))
    w_gates = bf(jnp.concatenate([seg(10), seg(11)], axis=1))
    idx_g = jnp.pad(idx_k_norm_g[0], (0, LANES - IDX_HEAD_DIM)).reshape(1, LANES)
    idx_b = jnp.pad(idx_k_norm_b[0], (0, LANES - IDX_HEAD_DIM)).reshape(1, LANES)
    lb = jnp.cumsum(jax.nn.softmax(hg_lb_logits.astype(F32), axis=0), axis=0)[0].reshape(1, HG_W)
    ng = hg_norm_g[0].reshape(1, HG_W)

    f1 = (bf(ffn1_w_gate[0]), bf(ffn1_w_up[0]), bf(ffn1_w_down[0]), row(ln1_g[0]), row(ln1_b[0]))
    f2 = (bf(ffn2_w_gate[0]), bf(ffn2_w_up[0]), bf(ffn2_w_down[0]), row(ln3_g[0]), row(ln3_b[0]))

    _, hm_b = _ffn_ln(meta.astype(F32), *f1, tm=N_META, tf=512)
    pm_att = _proj(hm_b, w_att, BF16, tm=N_META, tn=512)
    pm_hf = _proj(hm_b, w_hf, F32, tm=N_META, tn=512)
    pm_hr = _proj(hm_b, w_hr, BF16, tm=N_META, tn=512)
    kv0 = ATT_Q_W + IDX_Q_W
    km = jnp.pad(pm_att[:, kv0:kv0 + ATT_KV_W], ((0, LANES - N_META), (0, 0)))
    vm = jnp.pad(pm_att[:, kv0 + ATT_KV_W:], ((0, LANES - N_META), (0, 0)))
    s_zero = jnp.zeros((HG_HEADS, HG_DIM, HG_DIM), F32)
    _, s_meta = _hgrn(pm_hf, pm_hr, lb, ng, s_zero, batch=1, seq=N_META, chunk=N_META, sub=N_META, heads=4)

    h1, h1b = _ffn_ln(xr, *f1, tm=512, tf=512)
    p_att = _proj(h1b, w_att, BF16, tm=1024, tn=512)
    p_idx = _proj_idx(h1b, w_idx, idx_g, idx_b, tm=1024)
    p_hf = _proj(h1b, w_hf, F32, tm=1024, tn=512)
    p_hr = _proj(h1b, w_hr, BF16, tm=1024, tn=512)
    p_gates = _proj(h1b, w_gates, BF16, tm=1024, tn=512)

    ikn = p_idx[:, :IDX_HEAD_DIM].astype(BF16)
    ik_lo = jnp.pad(ikn, ((0, 0), (0, LANES - IDX_HEAD_DIM)))
    ik_hi = jnp.pad(ikn, ((0, 0), (LANES - IDX_HEAD_DIM, 0)))
    iw = p_idx[:, IDX_HEAD_DIM:IDX_HEAD_DIM + N_IDX_HEADS]
    y_att = _attention(p_att, iw, ik_lo, ik_hi, km, vm, batch=batch, seq=seq, tq=256, kc=512)
    y_hg, _ = _hgrn(p_hf, p_hr, lb, ng, s_meta[0], batch=batch, seq=seq, chunk=64, sub=16, heads=4)

    h2, _ = _merge(y_att, y_hg, p_gates, h1, bf(w_branch_att[0]), bf(w_branch_hg[0]), bf(w_out[0]),
                   row(ln2_g[0]), row(ln2_b[0]), tm=512, tn=512)
    out, _ = _ffn_ln(h2, *f2, tm=512, tf=512)
    return out.reshape(batch, seq, D_MODEL)
```

```python
import functools
import math

import jax
import jax.numpy as jnp
import numpy as np
from jax import lax
from jax.experimental import pallas as pl
from jax.experimental.pallas import tpu as pltpu

D_MODEL = 2048
N_META = 16
N_ATT_HEADS = 8
N_KV_HEADS = 2
ATT_GROUP = N_ATT_HEADS // N_KV_HEADS
ATT_HEAD_DIM = 128
N_IDX_HEADS = 16
IDX_HEAD_DIM = 64
TOPK = 256
HG_HEADS = 8
HG_DIM = 128
D_FF = 5632
LN_EPS = 1e-5
RMS_EPS = 1e-6
ALPHA = 2.0 ** 0.25

ATT_Q_W = N_ATT_HEADS * ATT_HEAD_DIM
ATT_KV_W = N_KV_HEADS * ATT_HEAD_DIM
IDX_Q_W = N_IDX_HEADS * IDX_HEAD_DIM
HG_W = HG_HEADS * HG_DIM

LANES = 128
SUBLANES = 8
VMEM_LIMIT = 56 * 1024 * 1024

F32 = jnp.float32
BF16 = jnp.bfloat16
INT_MIN = -2 ** 31
F32_LOWEST = float(np.finfo(np.float32).min)
LOG2E = math.log2(math.e)
ATT_Q_SCALE = ATT_HEAD_DIM ** -0.5 * LOG2E
POS_SHIFT = 6
POS_SPLIT = 1 << POS_SHIFT


def _dot(a, b):
    return jnp.dot(a, b, preferred_element_type=F32)


def _dot_nt(a, b):
    return lax.dot_general(a, b, (((1,), (1,)), ((), ())), preferred_element_type=F32)


def _dot_tn(a, b):
    return lax.dot_general(a, b, (((0,), (0,)), ((), ())), preferred_element_type=F32)


def _layer_norm(y, g, b):
    mu = jnp.mean(y, axis=-1, keepdims=True)
    d = y - mu
    var = jnp.mean(d * d, axis=-1, keepdims=True)
    return d * lax.rsqrt(var + LN_EPS) * g + b


def _silu(x):
    return x * jax.nn.sigmoid(x)


def _bf16_parts(c, n=3):
    parts = []
    for _ in range(n):
        p = float(np.asarray(c, np.float32).astype(jnp.bfloat16).astype(np.float32))
        parts.append(p)
        c = c - p
    return parts


def _ffn_ln_kernel(x_ref, wg_ref, wu_ref, wd_ref, g_ref, b_ref, o_ref, ob_ref, xb_scr, acc_scr):
    j = pl.program_id(1)

    @pl.when(j == 0)
    def _():
        xb_scr[...] = x_ref[...].astype(BF16)
        acc_scr[...] = jnp.zeros_like(acc_scr)

    xb = xb_scr[...]
    gate = _dot(xb, wg_ref[...])
    up = _dot(xb, wu_ref[...])
    a = (_silu(gate) * up).astype(BF16)
    acc_scr[...] += _dot(a, wd_ref[...])

    @pl.when(j == pl.num_programs(1) - 1)
    def _():
        y = ALPHA * x_ref[...] + 0.5 * acc_scr[...]
        out = _layer_norm(y, g_ref[...], b_ref[...])
        o_ref[...] = out
        ob_ref[...] = out.astype(BF16)


def _ffn_ln(x, wg, wu, wd, g, b, *, tm, tf):
    m = x.shape[0]
    grid = (m // tm, D_FF // tf)
    return pl.pallas_call(
        _ffn_ln_kernel,
        out_shape=(jax.ShapeDtypeStruct((m, D_MODEL), F32), jax.ShapeDtypeStruct((m, D_MODEL), BF16)),
        grid=grid,
        in_specs=[
            pl.BlockSpec((tm, D_MODEL), lambda i, j: (i, 0)),
            pl.BlockSpec((D_MODEL, tf), lambda i, j: (0, j)),
            pl.BlockSpec((D_MODEL, tf), lambda i, j: (0, j)),
            pl.BlockSpec((tf, D_MODEL), lambda i, j: (j, 0)),
            pl.BlockSpec((1, D_MODEL), lambda i, j: (0, 0)),
            pl.BlockSpec((1, D_MODEL), lambda i, j: (0, 0)),
        ],
        out_specs=(
            pl.BlockSpec((tm, D_MODEL), lambda i, j: (i, 0)),
            pl.BlockSpec((tm, D_MODEL), lambda i, j: (i, 0)),
        ),
        scratch_shapes=[pltpu.VMEM((tm, D_MODEL), BF16), pltpu.VMEM((tm, D_MODEL), F32)],
        compiler_params=pltpu.CompilerParams(
            dimension_semantics=("parallel", "arbitrary"), vmem_limit_bytes=VMEM_LIMIT),
        name="ffn_ln",
    )(x, wg, wu, wd, g, b)


def _mm_kernel(x_ref, w_ref, o_ref, *, scaled_tiles, scale):
    acc = _dot(x_ref[...], w_ref[...])
    if scaled_tiles:
        acc = acc * jnp.where(pl.program_id(1) < scaled_tiles, scale, 1.0)
    o_ref[...] = acc.astype(o_ref.dtype)


def _idx_kernel(x_ref, w_ref, g_ref, b_ref, o_ref):
    p = _dot(x_ref[...], w_ref[...])
    lane = lax.broadcasted_iota(jnp.int32, p.shape, 1)
    is_k = lane < IDX_HEAD_DIM
    mu = jnp.sum(jnp.where(is_k, p, 0.0), axis=-1, keepdims=True) * (1.0 / IDX_HEAD_DIM)
    d = jnp.where(is_k, p - mu, 0.0)
    var = jnp.sum(d * d, axis=-1, keepdims=True) * (1.0 / IDX_HEAD_DIM)
    kn = d * lax.rsqrt(var + LN_EPS) * g_ref[...] + b_ref[...]
    w_scale = (N_IDX_HEADS ** -0.5) * (IDX_HEAD_DIM ** -0.5)
    o_ref[...] = jnp.where(is_k, kn, p * w_scale)


def _proj(xb, w, out_dtype, *, tm, tn, scaled_tiles=0, scale=1.0):
    m, n = xb.shape[0], w.shape[1]
    return pl.pallas_call(
        functools.partial(_mm_kernel, scaled_tiles=scaled_tiles, scale=scale),
        out_shape=jax.ShapeDtypeStruct((m, n), out_dtype),
        grid=(m // tm, n // tn),
        in_specs=[pl.BlockSpec((tm, D_MODEL), lambda i, j: (i, 0)),
                  pl.BlockSpec((D_MODEL, tn), lambda i, j: (0, j))],
        out_specs=pl.BlockSpec((tm, tn), lambda i, j: (i, j)),
        compiler_params=pltpu.CompilerParams(
            dimension_semantics=("parallel", "arbitrary"), vmem_limit_bytes=VMEM_LIMIT),
        name="proj",
    )(xb, w)


def _proj_idx(xb, w, g, b, *, tm):
    m = xb.shape[0]
    return pl.pallas_call(
        _idx_kernel,
        out_shape=jax.ShapeDtypeStruct((m, LANES), F32),
        grid=(m // tm,),
        in_specs=[pl.BlockSpec((tm, D_MODEL), lambda i: (i, 0)),
                  pl.BlockSpec((D_MODEL, LANES), lambda i: (0, 0)),
                  pl.BlockSpec((1, LANES), lambda i: (0, 0)),
                  pl.BlockSpec((1, LANES), lambda i: (0, 0))],
        out_specs=pl.BlockSpec((tm, LANES), lambda i: (i, 0)),
        compiler_params=pltpu.CompilerParams(
            dimension_semantics=("parallel",), vmem_limit_bytes=VMEM_LIMIT),
        name="proj_idx",
    )(xb, w, g, b)


ATT_ROW_TILE = 32


def _key_to_f32(key):
    return pltpu.bitcast(key ^ ((key >> 31) & 0x7FFFFFFF), F32)


def _attn_kernel(aq_ref, iq_ref, iwt_ref, iklo_ref, ikhi_ref, k_ref, v_ref, km_ref, vm_ref,
                 o_ref,
                 score_scr, mb_scr, kaug_scr, vaug_scr, qaug_scr, s_scr, p_scr,
                 macc_scr, m_scr, acc_scr, *, tq, kc, seq):
    qi = pl.program_id(1)
    n_kc = ((qi + 1) * tq + kc - 1) // kc
    kf = float(TOPK)

    @pl.when(qi == 0)
    def _():
        r = lax.broadcasted_iota(jnp.int32, (LANES + seq, LANES), 0)
        lane = lax.broadcasted_iota(jnp.int32, (LANES + seq, LANES), 1)
        pos = jnp.where(r < LANES, r, r - LANES + N_META)
        feat = jnp.where(lane < 3, pos >> POS_SHIFT, jnp.where(lane < 6, pos & (POS_SPLIT - 1), 0))
        feat = feat.astype(F32).astype(BF16)
        ones = jnp.ones((LANES + seq, LANES), BF16)
        for kvh in range(N_KV_HEADS):
            cs = slice(kvh * LANES, (kvh + 1) * LANES)
            kaug_scr[kvh, 0:LANES, 0:LANES] = km_ref[:, cs]
            kaug_scr[kvh, LANES:, 0:LANES] = k_ref[:, cs]
            kaug_scr[kvh, :, LANES:] = feat
            vaug_scr[kvh, 0:LANES, 0:LANES] = vm_ref[:, cs]
            vaug_scr[kvh, LANES:, 0:LANES] = v_ref[:, cs]
            vaug_scr[kvh, :, LANES:] = ones

    qcol = qi * tq + lax.broadcasted_iota(jnp.int32, (1, tq), 1)
    iwt = iwt_ref[...]

    def score_body(j, carry):
        off = pl.multiple_of(j * kc, kc)
        klo = iklo_ref[pl.ds(off, kc), :]
        khi = ikhi_ref[pl.ds(off, kc), :]
        acc = jnp.zeros((kc, tq), F32)
        for p in range(N_IDX_HEADS // 2):
            q2 = iq_ref[:, p * LANES:(p + 1) * LANES]
            acc = acc + jnp.maximum(_dot_nt(klo, q2), 0.0) * iwt[2 * p:2 * p + 1, :]
            acc = acc + jnp.maximum(_dot_nt(khi, q2), 0.0) * iwt[2 * p + 1:2 * p + 2, :]
        krow = off + lax.broadcasted_iota(jnp.int32, (kc, tq), 0)
        score_scr[pl.ds(off, kc), :] = jnp.where(krow <= qcol, acc, -jnp.inf)
        return carry

    lax.fori_loop(0, n_kc, score_body, 0)

    n_acc = 8
    acc_rows = n_acc * SUBLANES

    def count_ge(cand):
        def body(j, acc):
            off = pl.multiple_of(j * kc, kc)
            w = jnp.where(score_scr[pl.ds(off, kc), :] >= cand, 1.0, 0.0)
            return acc + jnp.sum(w.reshape(kc // acc_rows, acc_rows, tq), axis=0)

        acc = lax.fori_loop(0, n_kc, body, jnp.zeros((acc_rows, tq), F32))
        return jnp.sum(acc, axis=0, keepdims=True)

    c0 = count_ge(jnp.zeros((1, tq), F32))
    ok0 = c0 >= kf
    thr0 = jnp.where(ok0, 0, INT_MIN).astype(jnp.int32)
    cnt0 = jnp.where(ok0, c0, 0.0)

    def bit_body(i, carry):
        thr, cnt = carry
        cand = thr | jnp.left_shift(jnp.int32(1), 30 - i)
        c = count_ge(_key_to_f32(cand))
        ok = c >= kf
        return jnp.where(ok, cand, thr), jnp.where(ok, c, cnt)

    thr, cnt = lax.fori_loop(0, 31, bit_body, (thr0, cnt0))
    t_f = jnp.where(thr == INT_MIN, F32_LOWEST, _key_to_f32(thr))

    @pl.when(jnp.max(cnt) > kf)
    def _():
        def gt_body(j, acc):
            off = pl.multiple_of(j * kc, kc)
            w = jnp.where(score_scr[pl.ds(off, kc), :] > t_f, 1.0, 0.0)
            return acc + jnp.sum(w, axis=0, keepdims=True)

        need = kf - lax.fori_loop(0, n_kc, gt_body, jnp.zeros((1, tq), F32))
        lower = (lax.broadcasted_iota(jnp.int32, (kc, kc), 0)
                 >= lax.broadcasted_iota(jnp.int32, (kc, kc), 1)).astype(BF16)

        def tie_body(j, before):
            off = pl.multiple_of(j * kc, kc)
            sc = score_scr[pl.ds(off, kc), :]
            eq = sc == t_f
            eqf = jnp.where(eq, 1.0, 0.0)
            rank = before + _dot(lower, eqf.astype(BF16))
            score_scr[pl.ds(off, kc), :] = jnp.where(eq & (rank > need), -jnp.inf, sc)
            return before + jnp.sum(eqf, axis=0, keepdims=True)

        lax.fori_loop(0, n_kc, tie_body, jnp.zeros((1, tq), F32))

    eye = (lax.broadcasted_iota(jnp.int32, (tq, tq), 0)
           == lax.broadcasted_iota(jnp.int32, (tq, tq), 1)).astype(BF16)

    def mask_body(j, carry):
        off = pl.multiple_of(j * kc, kc)
        sel_t = jnp.where(score_scr[pl.ds(off, kc), :] >= t_f, 1.0, 0.0).astype(BF16)
        sel = _dot_nt(eye, sel_t)
        mb_scr[:, pl.ds(off, kc)] = jnp.where(sel > 0.5, 0.0, -jnp.inf)
        return carry

    lax.fori_loop(0, n_kc, mask_body, 0)

    rt = ATT_ROW_TILE
    lane_q = lax.broadcasted_iota(jnp.int32, (1, LANES), 1)
    meta_mask = jnp.where(lax.broadcasted_iota(jnp.int32, (rt, LANES), 1) < N_META, 0.0, -jnp.inf)

    def logits_to_scratch(kvh, krow0, width):
        s_scr[:, 0:width] = _dot_nt(qaug_scr[...], kaug_scr[kvh, pl.ds(krow0, width), :])

    def max_chunk(kvh, krow0, width, mask_fn):
        logits_to_scratch(kvh, krow0, width)

        def tile_body(qr, carry):
            r0 = pl.multiple_of(qr * rt, rt)
            mb = mask_fn(r0)
            for g in range(ATT_GROUP):
                rows = pl.ds(g * tq + r0, rt)
                s = s_scr[rows, 0:width] + mb
                mx = s[:, 0:LANES]
                for c in range(1, width // LANES):
                    mx = jnp.maximum(mx, s[:, c * LANES:(c + 1) * LANES])
                macc_scr[rows, :] = jnp.maximum(macc_scr[rows, :], mx)
            return carry

        lax.fori_loop(0, tq // rt, tile_body, 0)

    def pv_chunk(kvh, krow0, width, mask_fn):
        logits_to_scratch(kvh, krow0, width)

        def tile_body(qr, carry):
            r0 = pl.multiple_of(qr * rt, rt)
            mb = mask_fn(r0)
            for g in range(ATT_GROUP):
                rows = pl.ds(g * tq + r0, rt)
                m_row = m_scr[rows, :]
                for c in range(width // LANES):
                    cols = slice(c * LANES, (c + 1) * LANES)
                    p_scr[rows, cols] = jnp.exp2(s_scr[rows, cols] + mb[:, cols] - m_row).astype(BF16)
            return carry

        lax.fori_loop(0, tq // rt, tile_body, 0)
        vblk = vaug_scr[kvh, pl.ds(krow0, width), :]
        half = ATT_GROUP * tq // 2
        for r0 in (0, half):
            acc_scr[r0:r0 + half, :] += _dot(p_scr[r0:r0 + half, 0:width], vblk)

    for kvh in range(N_KV_HEADS):
        for g in range(ATT_GROUP):
            h = kvh * ATT_GROUP + g
            parts = _bf16_parts(2.0 ** (-8.0 * (h + 1) / N_ATT_HEADS) * LOG2E)
            vals = [POS_SPLIT * p for p in parts] + parts
            qfeat = jnp.zeros((1, LANES), F32)
            for i, val in enumerate(vals):
                qfeat = jnp.where(lane_q == i, val, qfeat)
            qaug_scr[g * tq:(g + 1) * tq, 0:LANES] = aq_ref[:, h * LANES:(h + 1) * LANES]
            qaug_scr[g * tq:(g + 1) * tq, LANES:] = jnp.broadcast_to(qfeat, (tq, LANES)).astype(BF16)

        for chunk_fn in (max_chunk, pv_chunk):
            if chunk_fn is max_chunk:
                macc_scr[...] = jnp.full_like(macc_scr, -jnp.inf)
            else:
                m_scr[...] = jnp.broadcast_to(jnp.max(macc_scr[...], axis=1, keepdims=True), m_scr.shape)
                acc_scr[...] = jnp.zeros_like(acc_scr)
            chunk_fn(kvh, 0, LANES, lambda r0: meta_mask)

            def att_body(j, carry, kvh=kvh, chunk_fn=chunk_fn):
                off = pl.multiple_of(j * kc, kc)
                chunk_fn(kvh, LANES + off, kc, lambda r0: mb_scr[pl.ds(r0, rt), pl.ds(off, kc)])
                return carry

            lax.fori_loop(0, n_kc, att_body, 0)

        acc = acc_scr[...]
        out = acc[:, 0:LANES] / acc[:, LANES:LANES + 1]
        for g in range(ATT_GROUP):
            h = kvh * ATT_GROUP + g
            o_ref[:, h * LANES:(h + 1) * LANES] = out[g * tq:(g + 1) * tq].astype(o_ref.dtype)


def _attention(p_att, iw_t, ik_lo, ik_hi, km, vm, *, batch, seq, tq, kc):
    m = batch * seq
    nq = seq // tq
    kcol = (ATT_Q_W + IDX_Q_W) // ATT_KV_W
    rows4 = ATT_GROUP * tq
    return pl.pallas_call(
        functools.partial(_attn_kernel, tq=tq, kc=kc, seq=seq),
        out_shape=jax.ShapeDtypeStruct((m, ATT_Q_W), BF16),
        grid=(batch, nq),
        in_specs=[
            pl.BlockSpec((tq, ATT_Q_W), lambda b, q: (b * nq + q, 0)),
            pl.BlockSpec((tq, IDX_Q_W), lambda b, q: (b * nq + q, 1)),
            pl.BlockSpec((N_IDX_HEADS, tq), lambda b, q: (0, b * nq + q)),
            pl.BlockSpec((seq, LANES), lambda b, q: (b, 0)),
            pl.BlockSpec((seq, LANES), lambda b, q: (b, 0)),
            pl.BlockSpec((seq, ATT_KV_W), lambda b, q: (b, kcol)),
            pl.BlockSpec((seq, ATT_KV_W), lambda b, q: (b, kcol + 1)),
            pl.BlockSpec((LANES, ATT_KV_W), lambda b, q: (0, 0)),
            pl.BlockSpec((LANES, ATT_KV_W), lambda b, q: (0, 0)),
        ],
        out_specs=pl.BlockSpec((tq, ATT_Q_W), lambda b, q: (b * nq + q, 0)),
        scratch_shapes=[
            pltpu.VMEM((seq, tq), F32),
            pltpu.VMEM((tq, seq), F32),
            pltpu.VMEM((N_KV_HEADS, LANES + seq, 2 * LANES), BF16),
            pltpu.VMEM((N_KV_HEADS, LANES + seq, 2 * LANES), BF16),
            pltpu.VMEM((rows4, 2 * LANES), BF16),
            pltpu.VMEM((rows4, kc), F32),
            pltpu.VMEM((rows4, kc), BF16),
            pltpu.VMEM((rows4, LANES), F32),
            pltpu.VMEM((rows4, LANES), F32),
            pltpu.VMEM((rows4, 2 * LANES), F32),
        ],
        compiler_params=pltpu.CompilerParams(
            dimension_semantics=("parallel", "arbitrary"), vmem_limit_bytes=VMEM_LIMIT),
        name="dsa_attention",
    )(p_att, p_att, iw_t, ik_lo, ik_hi, p_att, p_att, km, vm)


def _split3(x):
    hi = x.astype(BF16)
    r = x - hi.astype(F32)
    mid = r.astype(BF16)
    lo = (r - mid.astype(F32)).astype(BF16)
    return hi, mid, lo


def _hgrn_kernel(hf_ref, hr_q_ref, hr_i_ref, hr_g_ref, lb_ref, ng_ref, s0_ref,
                 y_ref, sT_out_ref, sT_scr, *, chunk, sub, heads):
    c_idx = pl.program_id(2)

    @pl.when(c_idx == 0)
    def _():
        sT_scr[...] = s0_ref[...]

    n_sub = chunk // sub
    tri = (lax.broadcasted_iota(jnp.int32, (chunk, chunk), 0)
           >= lax.broadcasted_iota(jnp.int32, (chunk, chunk), 1)).astype(BF16)
    t_iota = lax.broadcasted_iota(jnp.int32, (sub, LANES), 0)

    for g in range(heads):
        cs = slice(g * HG_DIM, (g + 1) * HG_DIM)
        lb = lb_ref[:, cs]
        q = _silu(hr_q_ref[:, cs].astype(F32))
        fg = lb + (1.0 - lb) * jax.nn.sigmoid(hf_ref[:, cs])
        kk = 1.0 - fg
        logf = jnp.log(fg)
        v = hr_i_ref[:, cs].astype(F32)
        vb = v.astype(BF16)
        l_hi, l_mid, l_lo = _split3(logf)
        b = _dot(tri, l_hi) + _dot(tri, l_mid) + _dot(tri, l_lo)
        sT = sT_scr[g]
        b_last = b[chunk - 1:chunk, :]
        o_state = _dot_nt((q * jnp.exp(b)).astype(BF16), sT.astype(BF16))
        khat = (kk * jnp.exp(b_last - b)).astype(BF16)
        sT_scr[g] = sT * jnp.exp(b_last) + _dot_tn(vb, khat)
        rows = []
        for i in range(n_sub):
            r0, r1 = i * sub, (i + 1) * sub
            bs, qs, ks, vs = b[r0:r1], q[r0:r1], kk[r0:r1], v[r0:r1]
            o_i = o_state[r0:r1]
            if i > 0:
                r_i = b[r0 - 1:r0, :]
                qt = (qs * jnp.exp(bs - r_i)).astype(BF16)
                kt = (kk[:r0] * jnp.exp(r_i - b[:r0])).astype(BF16)
                a_off = _dot_nt(qt, kt)
                o_i = o_i + _dot(a_off.astype(BF16), vb[:r0])
            for s in range(sub):
                e = jnp.exp(jnp.where(t_iota >= s, bs - bs[s:s + 1], -jnp.inf))
                a_col = jnp.sum(qs * e * ks[s:s + 1], axis=1, keepdims=True)
                o_i = o_i + a_col * vs[s:s + 1]
            rows.append(o_i)
        o = jnp.concatenate(rows, axis=0) if n_sub > 1 else rows[0]
        o = o * lax.rsqrt(jnp.mean(o * o, axis=-1, keepdims=True) + RMS_EPS) * ng_ref[:, cs]
        o = o * _silu(hr_g_ref[:, cs].astype(F32))
        y_ref[:, cs] = o.astype(y_ref.dtype)

    @pl.when(c_idx == pl.num_programs(2) - 1)
    def _():
        sT_out_ref[0] = sT_scr[...]


def _hgrn(hf, hr, lb, ng, s0, *, batch, seq, chunk, sub, heads):
    m = batch * seq
    nc = seq // chunk
    hw = heads * HG_DIM
    nhg = HG_HEADS // heads
    return pl.pallas_call(
        functools.partial(_hgrn_kernel, chunk=chunk, sub=sub, heads=heads),
        out_shape=(jax.ShapeDtypeStruct((m, HG_W), BF16),
                   jax.ShapeDtypeStruct((batch, HG_HEADS, HG_DIM, HG_DIM), F32)),
        grid=(batch, nhg, nc),
        in_specs=[
            pl.BlockSpec((chunk, hw), lambda b, h, c: (b * nc + c, h)),
            pl.BlockSpec((chunk, hw), lambda b, h, c: (b * nc + c, h)),
            pl.BlockSpec((chunk, hw), lambda b, h, c: (b * nc + c, nhg + h)),
            pl.BlockSpec((chunk, hw), lambda b, h, c: (b * nc + c, 2 * nhg + h)),
            pl.BlockSpec((1, hw), lambda b, h, c: (0, h)),
            pl.BlockSpec((1, hw), lambda b, h, c: (0, h)),
            pl.BlockSpec((heads, HG_DIM, HG_DIM), lambda b, h, c: (h, 0, 0)),
        ],
        out_specs=(
            pl.BlockSpec((chunk, hw), lambda b, h, c: (b * nc + c, h)),
            pl.BlockSpec((1, heads, HG_DIM, HG_DIM), lambda b, h, c: (b, h, 0, 0)),
        ),
        scratch_shapes=[pltpu.VMEM((heads, HG_DIM, HG_DIM), F32)],
        compiler_params=pltpu.CompilerParams(
            dimension_semantics=("parallel", "parallel", "arbitrary"), vmem_limit_bytes=VMEM_LIMIT),
        name="hgrn2",
    )(hf, hr, hr, hr, lb, ng, s0)


def _merge_kernel(ya_ref, yh_ref, gts_a_ref, gts_h_ref, h_ref, wa_ref, wh_ref, wo_ref, g_ref, b_ref,
                  o_ref, ob_ref, acc_scr):
    j = pl.program_id(1)

    @pl.when(j == 0)
    def _():
        acc_scr[...] = jnp.zeros_like(acc_scr)

    ga = jax.nn.sigmoid(gts_a_ref[...].astype(F32))
    gh = jax.nn.sigmoid(gts_h_ref[...].astype(F32))
    merged = ga * _dot(ya_ref[...], wa_ref[...]) + gh * _dot(yh_ref[...], wh_ref[...])
    acc_scr[...] += _dot(merged.astype(BF16), wo_ref[...])

    @pl.when(j == pl.num_programs(1) - 1)
    def _():
        out = _layer_norm(ALPHA * h_ref[...] + acc_scr[...], g_ref[...], b_ref[...])
        o_ref[...] = out
        ob_ref[...] = out.astype(BF16)


def _merge(ya, yh, gates, h1, wa, wh, wo, g, b, *, tm, tn):
    m = ya.shape[0]
    nj = D_MODEL // tn
    return pl.pallas_call(
        _merge_kernel,
        out_shape=(jax.ShapeDtypeStruct((m, D_MODEL), F32), jax.ShapeDtypeStruct((m, D_MODEL), BF16)),
        grid=(m // tm, nj),
        in_specs=[
            pl.BlockSpec((tm, ATT_Q_W), lambda i, j: (i, 0)),
            pl.BlockSpec((tm, HG_W), lambda i, j: (i, 0)),
            pl.BlockSpec((tm, tn), lambda i, j: (i, j)),
            pl.BlockSpec((tm, tn), lambda i, j: (i, nj + j)),
            pl.BlockSpec((tm, D_MODEL), lambda i, j: (i, 0)),
            pl.BlockSpec((ATT_Q_W, tn), lambda i, j: (0, j)),
            pl.BlockSpec((HG_W, tn), lambda i, j: (0, j)),
            pl.BlockSpec((tn, D_MODEL), lambda i, j: (j, 0)),
            pl.BlockSpec((1, D_MODEL), lambda i, j: (0, 0)),
            pl.BlockSpec((1, D_MODEL), lambda i, j: (0, 0)),
        ],
        out_specs=(
            pl.BlockSpec((tm, D_MODEL), lambda i, j: (i, 0)),
            pl.BlockSpec((tm, D_MODEL), lambda i, j: (i, 0)),
        ),
        scratch_shapes=[pltpu.VMEM((tm, D_MODEL), F32)],
        compiler_params=pltpu.CompilerParams(
            dimension_semantics=("parallel", "arbitrary"), vmem_limit_bytes=VMEM_LIMIT),
        name="merge_ln",
    )(ya, yh, gates, gates, h1, wa, wh, wo, g, b)


def kernel(x, meta, ffn1_w_gate, ffn1_w_up, ffn1_w_down, ln1_g, ln1_b, w_in, idx_k_norm_g, idx_k_norm_b,
           hg_lb_logits, hg_norm_g, w_branch_att, w_branch_hg, w_out, ln2_g, ln2_b,
           ffn2_w_gate, ffn2_w_up, ffn2_w_down, ln3_g, ln3_b):
    batch, seq, _ = x.shape
    m = batch * seq
    xr = x.reshape(m, D_MODEL)
    bf = lambda w: w.astype(BF16)
    row = lambda v: v.reshape(1, -1)

    splits = (ATT_Q_W, ATT_KV_W, ATT_KV_W, IDX_Q_W, IDX_HEAD_DIM, N_IDX_HEADS, HG_W, HG_W, HG_W, HG_W,
              D_MODEL, D_MODEL)
    offs = [0]
    for s in splits:
        offs.append(offs[-1] + s)
    w = w_in[0]
    seg = lambda i: w[:, offs[i]:offs[i + 1]]
    w_att = bf(jnp.concatenate([seg(0), seg(3), seg(1), seg(2)], axis=1))
    w_idx = bf(jnp.pad(jnp.concatenate([seg(4), seg(5)], axis=1),
                       ((0, 0), (0, LANES - IDX_HEAD_DIM - N_IDX_HEADS))))
    w_hf = bf(seg(7))
    w_hr = bf(jnp.concatenate([seg(6), seg(8), seg(9)], axis=1))
    w_gates = bf(jnp.concatenate([seg(10), seg(11)], axis=1))
    idx_g = jnp.pad(idx_k_norm_g[0], (0, LANES - IDX_HEAD_DIM)).reshape(1, LANES)
    idx_b = jnp.pad(idx_k_norm_b[0], (0, LANES - IDX_HEAD_DIM)).reshape(1, LANES)
    lb = jnp.cumsum(jax.nn.softmax(hg_lb_logits.astype(F32), axis=0), axis=0)[0].reshape(1, HG_W)
    ng = hg_norm_g[0].reshape(1, HG_W)

    f1 = (bf(ffn1_w_gate[0]), bf(ffn1_w_up[0]), bf(ffn1_w_down[0]), row(ln1_g[0]), row(ln1_b[0]))
    f2 = (bf(ffn2_w_gate[0]), bf(ffn2_w_up[0]), bf(ffn2_w_down[0]), row(ln3_g[0]), row(ln3_b[0]))
    tn = 512
    q_tiles = ATT_Q_W // tn

    _, hm_b = _ffn_ln(meta.astype(F32), *f1, tm=N_META, tf=512)
    pm_att = _proj(hm_b, w_att, BF16, tm=N_META, tn=tn, scaled_tiles=q_tiles, scale=ATT_Q_SCALE)
    pm_hf = _proj(hm_b, w_hf, F32, tm=N_META, tn=tn)
    pm_hr = _proj(hm_b, w_hr, BF16, tm=N_META, tn=tn)
    kv0 = ATT_Q_W + IDX_Q_W
    km = jnp.pad(pm_att[:, kv0:kv0 + ATT_KV_W], ((0, LANES - N_META), (0, 0)))
    vm = jnp.pad(pm_att[:, kv0 + ATT_KV_W:], ((0, LANES - N_META), (0, 0)))
    s_zero = jnp.zeros((HG_HEADS, HG_DIM, HG_DIM), F32)
    _, s_meta = _hgrn(pm_hf, pm_hr, lb, ng, s_zero, batch=1, seq=N_META, chunk=N_META, sub=N_META, heads=4)

    h1, h1b = _ffn_ln(xr, *f1, tm=512, tf=512)
    p_att = _proj(h1b, w_att, BF16, tm=1024, tn=tn, scaled_tiles=q_tiles, scale=ATT_Q_SCALE)
    p_idx = _proj_idx(h1b, w_idx, idx_g, idx_b, tm=1024)
    p_hf = _proj(h1b, w_hf, F32, tm=1024, tn=tn)
    p_hr = _proj(h1b, w_hr, BF16, tm=1024, tn=tn)
    p_gates = _proj(h1b, w_gates, BF16, tm=1024, tn=tn)

    ikn = p_idx[:, :IDX_HEAD_DIM].astype(BF16)
    ik_lo = jnp.pad(ikn, ((0, 0), (0, LANES - IDX_HEAD_DIM)))
    ik_hi = jnp.pad(ikn, ((0, 0), (LANES - IDX_HEAD_DIM, 0)))
    iw_t = p_idx[:, IDX_HEAD_DIM:IDX_HEAD_DIM + N_IDX_HEADS].T
    y_att = _attention(p_att, iw_t, ik_lo, ik_hi, km, vm, batch=batch, seq=seq, tq=256, kc=512)
    y_hg, _ = _hgrn(p_hf, p_hr, lb, ng, s_meta[0], batch=batch, seq=seq, chunk=64, sub=16, heads=4)

    h2, _ = _merge(y_att, y_hg, p_gates, h1, bf(w_branch_att[0]), bf(w_branch_hg[0]), bf(w_out[0]),
                   row(ln2_g[0]), row(ln2_b[0]), tm=512, tn=512)
    out, _ = _ffn_ln(h2, *f2, tm=512, tf=512)
    return out.reshape(batch, seq, D_MODEL)
```

```python
import functools
import math

import jax
import jax.numpy as jnp
import numpy as np
from jax import lax
from jax.experimental import pallas as pl
from jax.experimental.pallas import tpu as pltpu

D_MODEL = 2048
N_META = 16
N_ATT_HEADS = 8
N_KV_HEADS = 2
ATT_GROUP = N_ATT_HEADS // N_KV_HEADS
ATT_HEAD_DIM = 128
N_IDX_HEADS = 16
IDX_HEAD_DIM = 64
TOPK = 256
HG_HEADS = 8
HG_DIM = 128
D_FF = 5632
LN_EPS = 1e-5
RMS_EPS = 1e-6
ALPHA = 2.0 ** 0.25

ATT_Q_W = N_ATT_HEADS * ATT_HEAD_DIM
ATT_KV_W = N_KV_HEADS * ATT_HEAD_DIM
IDX_Q_W = N_IDX_HEADS * IDX_HEAD_DIM
HG_W = HG_HEADS * HG_DIM

LANES = 128
SUBLANES = 8
VMEM_LIMIT = 56 * 1024 * 1024

F32 = jnp.float32
BF16 = jnp.bfloat16
INT_MIN = -2 ** 31
F32_LOWEST = float(np.finfo(np.float32).min)
LOG2E = math.log2(math.e)
ATT_Q_SCALE = ATT_HEAD_DIM ** -0.5 * LOG2E
POS_SHIFT = 6
POS_SPLIT = 1 << POS_SHIFT


def _dot(a, b):
    return jnp.dot(a, b, preferred_element_type=F32)


def _dot_nt(a, b):
    return lax.dot_general(a, b, (((1,), (1,)), ((), ())), preferred_element_type=F32)


def _dot_tn(a, b):
    return lax.dot_general(a, b, (((0,), (0,)), ((), ())), preferred_element_type=F32)


def _layer_norm(y, g, b):
    mu = jnp.mean(y, axis=-1, keepdims=True)
    d = y - mu
    var = jnp.mean(d * d, axis=-1, keepdims=True)
    return d * lax.rsqrt(var + LN_EPS) * g + b


def _silu(x):
    return x * jax.nn.sigmoid(x)


def _bf16_parts(c, n=3):
    parts = []
    for _ in range(n):
        p = float(np.asarray(c, np.float32).astype(jnp.bfloat16).astype(np.float32))
        parts.append(p)
        c = c - p
    return parts


def _ffn_ln_kernel(x_ref, wg_ref, wu_ref, wd_ref, g_ref, b_ref, *rest):
    *out_refs, xb_scr, acc_scr = rest
    j = pl.program_id(1)

    @pl.when(j == 0)
    def _():
        xb_scr[...] = x_ref[...].astype(BF16)
        acc_scr[...] = jnp.zeros_like(acc_scr)

    xb = xb_scr[...]
    gate = _dot(xb, wg_ref[...])
    up = _dot(xb, wu_ref[...])
    a = (_silu(gate) * up).astype(BF16)
    acc_scr[...] += _dot(a, wd_ref[...])

    @pl.when(j == pl.num_programs(1) - 1)
    def _():
        y = ALPHA * x_ref[...] + 0.5 * acc_scr[...]
        out = _layer_norm(y, g_ref[...], b_ref[...])
        for o_ref in out_refs:
            o_ref[...] = out.astype(o_ref.dtype)


def _ffn_ln(x, wg, wu, wd, g, b, *, tm, tf, out_dtypes):
    m = x.shape[0]
    grid = (m // tm, D_FF // tf)
    return pl.pallas_call(
        _ffn_ln_kernel,
        out_shape=tuple(jax.ShapeDtypeStruct((m, D_MODEL), dt) for dt in out_dtypes),
        grid=grid,
        in_specs=[
            pl.BlockSpec((tm, D_MODEL), lambda i, j: (i, 0)),
            pl.BlockSpec((D_MODEL, tf), lambda i, j: (0, j)),
            pl.BlockSpec((D_MODEL, tf), lambda i, j: (0, j)),
            pl.BlockSpec((tf, D_MODEL), lambda i, j: (j, 0)),
            pl.BlockSpec((1, D_MODEL), lambda i, j: (0, 0)),
            pl.BlockSpec((1, D_MODEL), lambda i, j: (0, 0)),
        ],
        out_specs=tuple(pl.BlockSpec((tm, D_MODEL), lambda i, j: (i, 0)) for _ in out_dtypes),
        scratch_shapes=[pltpu.VMEM((tm, D_MODEL), BF16), pltpu.VMEM((tm, D_MODEL), F32)],
        compiler_params=pltpu.CompilerParams(
            dimension_semantics=("parallel", "arbitrary"), vmem_limit_bytes=VMEM_LIMIT),
        name="ffn_ln",
    )(x, wg, wu, wd, g, b)


def _mm_kernel(x_ref, w_ref, o_ref, *, scaled_tiles, scale):
    acc = _dot(x_ref[...], w_ref[...])
    if scaled_tiles:
        acc = acc * jnp.where(pl.program_id(1) < scaled_tiles, scale, 1.0)
    o_ref[...] = acc.astype(o_ref.dtype)


def _idx_kernel(x_ref, w_ref, g_ref, b_ref, o_ref):
    p = _dot(x_ref[...], w_ref[...])
    lane = lax.broadcasted_iota(jnp.int32, p.shape, 1)
    is_k = lane < IDX_HEAD_DIM
    mu = jnp.sum(jnp.where(is_k, p, 0.0), axis=-1, keepdims=True) * (1.0 / IDX_HEAD_DIM)
    d = jnp.where(is_k, p - mu, 0.0)
    var = jnp.sum(d * d, axis=-1, keepdims=True) * (1.0 / IDX_HEAD_DIM)
    kn = d * lax.rsqrt(var + LN_EPS) * g_ref[...] + b_ref[...]
    w_scale = (N_IDX_HEADS ** -0.5) * (IDX_HEAD_DIM ** -0.5)
    o_ref[...] = jnp.where(is_k, kn, p * w_scale)


def _proj(xb, w, out_dtype, *, tm, tn, scaled_tiles=0, scale=1.0):
    m, n = xb.shape[0], w.shape[1]
    return pl.pallas_call(
        functools.partial(_mm_kernel, scaled_tiles=scaled_tiles, scale=scale),
        out_shape=jax.ShapeDtypeStruct((m, n), out_dtype),
        grid=(m // tm, n // tn),
        in_specs=[pl.BlockSpec((tm, D_MODEL), lambda i, j: (i, 0)),
                  pl.BlockSpec((D_MODEL, tn), lambda i, j: (0, j))],
        out_specs=pl.BlockSpec((tm, tn), lambda i, j: (i, j)),
        compiler_params=pltpu.CompilerParams(
            dimension_semantics=("parallel", "arbitrary"), vmem_limit_bytes=VMEM_LIMIT),
        name="proj",
    )(xb, w)


def _proj_idx(xb, w, g, b, *, tm):
    m = xb.shape[0]
    return pl.pallas_call(
        _idx_kernel,
        out_shape=jax.ShapeDtypeStruct((m, LANES), F32),
        grid=(m // tm,),
        in_specs=[pl.BlockSpec((tm, D_MODEL), lambda i: (i, 0)),
                  pl.BlockSpec((D_MODEL, LANES), lambda i: (0, 0)),
                  pl.BlockSpec((1, LANES), lambda i: (0, 0)),
                  pl.BlockSpec((1, LANES), lambda i: (0, 0))],
        out_specs=pl.BlockSpec((tm, LANES), lambda i: (i, 0)),
        compiler_params=pltpu.CompilerParams(
            dimension_semantics=("parallel",), vmem_limit_bytes=VMEM_LIMIT),
        name="proj_idx",
    )(xb, w, g, b)


ATT_ROW_TILE = 32


def _key_to_f32(key):
    return pltpu.bitcast(key ^ ((key >> 31) & 0x7FFFFFFF), F32)


def _attn_kernel(aq_ref, iq_ref, iwt_ref, iklo_ref, ikhi_ref, k_ref, v_ref, km_ref, vm_ref,
                 o_ref,
                 score_scr, mb_scr, kaug_scr, vaug_scr, qaug_scr, s_scr, p_scr,
                 macc_scr, m_scr, acc_scr, *, tq, kc, seq):
    qi = pl.program_id(1)
    n_kc = ((qi + 1) * tq + kc - 1) // kc
    kf = float(TOPK)

    @pl.when(qi == 0)
    def _():
        r = lax.broadcasted_iota(jnp.int32, (LANES + seq, LANES), 0)
        lane = lax.broadcasted_iota(jnp.int32, (LANES + seq, LANES), 1)
        pos = jnp.where(r < LANES, r, r - LANES + N_META)
        feat = jnp.where(lane < 3, pos >> POS_SHIFT, jnp.where(lane < 6, pos & (POS_SPLIT - 1), 0))
        feat = feat.astype(F32).astype(BF16)
        ones = jnp.ones((LANES + seq, LANES), BF16)
        for kvh in range(N_KV_HEADS):
            cs = slice(kvh * LANES, (kvh + 1) * LANES)
            kaug_scr[kvh, 0:LANES, 0:LANES] = km_ref[:, cs]
            kaug_scr[kvh, LANES:, 0:LANES] = k_ref[:, cs]
            kaug_scr[kvh, :, LANES:] = feat
            vaug_scr[kvh, 0:LANES, 0:LANES] = vm_ref[:, cs]
            vaug_scr[kvh, LANES:, 0:LANES] = v_ref[:, cs]
            vaug_scr[kvh, :, LANES:] = ones

    qcol = qi * tq + lax.broadcasted_iota(jnp.int32, (1, tq), 1)
    iwt = iwt_ref[...]

    def score_body(j, carry):
        off = pl.multiple_of(j * kc, kc)
        klo = iklo_ref[pl.ds(off, kc), :]
        khi = ikhi_ref[pl.ds(off, kc), :]
        acc = jnp.zeros((kc, tq), F32)
        for p in range(N_IDX_HEADS // 2):
            q2 = iq_ref[:, p * LANES:(p + 1) * LANES]
            acc = acc + jnp.maximum(_dot_nt(klo, q2), 0.0) * iwt[2 * p:2 * p + 1, :]
            acc = acc + jnp.maximum(_dot_nt(khi, q2), 0.0) * iwt[2 * p + 1:2 * p + 2, :]
        krow = off + lax.broadcasted_iota(jnp.int32, (kc, tq), 0)
        score_scr[pl.ds(off, kc), :] = jnp.where(krow <= qcol, acc, -jnp.inf)
        return carry

    lax.fori_loop(0, n_kc, score_body, 0)

    n_acc = 8
    acc_rows = n_acc * SUBLANES

    def count_ge(cand):
        def body(j, acc):
            off = pl.multiple_of(j * kc, kc)
            w = jnp.where(score_scr[pl.ds(off, kc), :] >= cand, 1.0, 0.0)
            return acc + jnp.sum(w.reshape(kc // acc_rows, acc_rows, tq), axis=0)

        acc = lax.fori_loop(0, n_kc, body, jnp.zeros((acc_rows, tq), F32))
        return jnp.sum(acc, axis=0, keepdims=True)

    c0 = count_ge(jnp.zeros((1, tq), F32))
    ok0 = c0 >= kf
    thr0 = jnp.where(ok0, 0, INT_MIN).astype(jnp.int32)
    cnt0 = jnp.where(ok0, c0, 0.0)

    def bit_body(i, carry):
        thr, cnt = carry
        cand = thr | jnp.left_shift(jnp.int32(1), 30 - i)
        c = count_ge(_key_to_f32(cand))
        ok = c >= kf
        return jnp.where(ok, cand, thr), jnp.where(ok, c, cnt)

    thr, cnt = lax.fori_loop(0, 31, bit_body, (thr0, cnt0))
    t_f = jnp.where(thr == INT_MIN, F32_LOWEST, _key_to_f32(thr))

    @pl.when(jnp.max(cnt) > kf)
    def _():
        def gt_body(j, acc):
            off = pl.multiple_of(j * kc, kc)
            w = jnp.where(score_scr[pl.ds(off, kc), :] > t_f, 1.0, 0.0)
            return acc + jnp.sum(w, axis=0, keepdims=True)

        need = kf - lax.fori_loop(0, n_kc, gt_body, jnp.zeros((1, tq), F32))
        lower = (lax.broadcasted_iota(jnp.int32, (kc, kc), 0)
                 >= lax.broadcasted_iota(jnp.int32, (kc, kc), 1)).astype(BF16)

        def tie_body(j, before):
            off = pl.multiple_of(j * kc, kc)
            sc = score_scr[pl.ds(off, kc), :]
            eq = sc == t_f
            eqf = jnp.where(eq, 1.0, 0.0)
            rank = before + _dot(lower, eqf.astype(BF16))
            score_scr[pl.ds(off, kc), :] = jnp.where(eq & (rank > need), -jnp.inf, sc)
            return before + jnp.sum(eqf, axis=0, keepdims=True)

        lax.fori_loop(0, n_kc, tie_body, jnp.zeros((1, tq), F32))

    eye = (lax.broadcasted_iota(jnp.int32, (tq, tq), 0)
           == lax.broadcasted_iota(jnp.int32, (tq, tq), 1)).astype(BF16)

    def mask_body(j, carry):
        off = pl.multiple_of(j * kc, kc)
        sel_t = jnp.where(score_scr[pl.ds(off, kc), :] >= t_f, 1.0, 0.0).astype(BF16)
        sel = _dot_nt(eye, sel_t)
        mb_scr[:, pl.ds(off, kc)] = jnp.where(sel > 0.5, 0.0, -jnp.inf)
        return carry

    lax.fori_loop(0, n_kc, mask_body, 0)

    rt = ATT_ROW_TILE
    lane_q = lax.broadcasted_iota(jnp.int32, (1, LANES), 1)
    meta_mask = jnp.where(lax.broadcasted_iota(jnp.int32, (rt, LANES), 1) < N_META, 0.0, -jnp.inf)

    def logits_to_scratch(kvh, krow0, width):
        s_scr[:, 0:width] = _dot_nt(qaug_scr[...], kaug_scr[kvh, pl.ds(krow0, width), :])

    def max_chunk(kvh, krow0, width, mask_fn):
        logits_to_scratch(kvh, krow0, width)

        def tile_body(qr, carry):
            r0 = pl.multiple_of(qr * rt, rt)
            mb = mask_fn(r0)
            for g in range(ATT_GROUP):
                rows = pl.ds(g * tq + r0, rt)
                s = s_scr[rows, 0:width] + mb
                mx = s[:, 0:LANES]
                for c in range(1, width // LANES):
                    mx = jnp.maximum(mx, s[:, c * LANES:(c + 1) * LANES])
                macc_scr[rows, :] = jnp.maximum(macc_scr[rows, :], mx)
            return carry

        lax.fori_loop(0, tq // rt, tile_body, 0)

    def pv_chunk(kvh, krow0, width, mask_fn):
        logits_to_scratch(kvh, krow0, width)

        def tile_body(qr, carry):
            r0 = pl.multiple_of(qr * rt, rt)
            mb = mask_fn(r0)
            for g in range(ATT_GROUP):
                rows = pl.ds(g * tq + r0, rt)
                m_row = m_scr[rows, :]
                for c in range(width // LANES):
                    cols = slice(c * LANES, (c + 1) * LANES)
                    p_scr[rows, cols] = jnp.exp2(s_scr[rows, cols] + mb[:, cols] - m_row).astype(BF16)
            return carry

        lax.fori_loop(0, tq // rt, tile_body, 0)
        vblk = vaug_scr[kvh, pl.ds(krow0, width), :]
        half = ATT_GROUP * tq // 2
        for r0 in (0, half):
            acc_scr[r0:r0 + half, :] += _dot(p_scr[r0:r0 + half, 0:width], vblk)

    for kvh in range(N_KV_HEADS):
        for g in range(ATT_GROUP):
            h = kvh * ATT_GROUP + g
            parts = _bf16_parts(2.0 ** (-8.0 * (h + 1) / N_ATT_HEADS) * LOG2E)
            vals = [POS_SPLIT * p for p in parts] + parts
            qfeat = jnp.zeros((1, LANES), F32)
            for i, val in enumerate(vals):
                qfeat = jnp.where(lane_q == i, val, qfeat)
            qaug_scr[g * tq:(g + 1) * tq, 0:LANES] = aq_ref[:, h * LANES:(h + 1) * LANES]
            qaug_scr[g * tq:(g + 1) * tq, LANES:] = jnp.broadcast_to(qfeat, (tq, LANES)).astype(BF16)

        for chunk_fn in (max_chunk, pv_chunk):
            if chunk_fn is max_chunk:
                macc_scr[...] = jnp.full_like(macc_scr, -jnp.inf)
            else:
                m_scr[...] = jnp.broadcast_to(jnp.max(macc_scr[...], axis=1, keepdims=True), m_scr.shape)
                acc_scr[...] = jnp.zeros_like(acc_scr)
            chunk_fn(kvh, 0, LANES, lambda r0: meta_mask)

            def att_body(j, carry, kvh=kvh, chunk_fn=chunk_fn):
                off = pl.multiple_of(j * kc, kc)
                chunk_fn(kvh, LANES + off, kc, lambda r0: mb_scr[pl.ds(r0, rt), pl.ds(off, kc)])
                return carry

            lax.fori_loop(0, n_kc, att_body, 0)

        acc = acc_scr[...]
        out = acc[:, 0:LANES] / acc[:, LANES:LANES + 1]
        for g in range(ATT_GROUP):
            h = kvh * ATT_GROUP + g
            o_ref[:, h * LANES:(h + 1) * LANES] = out[g * tq:(g + 1) * tq].astype(o_ref.dtype)


def _attention(p_att, iw_t, ik_lo, ik_hi, km, vm, *, batch, seq, tq, kc):
    m = batch * seq
    nq = seq // tq
    kcol = (ATT_Q_W + IDX_Q_W) // ATT_KV_W
    rows4 = ATT_GROUP * tq
    return pl.pallas_call(
        functools.partial(_attn_kernel, tq=tq, kc=kc, seq=seq),
        out_shape=jax.ShapeDtypeStruct((m, ATT_Q_W), BF16),
        grid=(batch, nq),
        in_specs=[
            pl.BlockSpec((tq, ATT_Q_W), lambda b, q: (b * nq + q, 0)),
            pl.BlockSpec((tq, IDX_Q_W), lambda b, q: (b * nq + q, 1)),
            pl.BlockSpec((N_IDX_HEADS, tq), lambda b, q: (0, b * nq + q)),
            pl.BlockSpec((seq, LANES), lambda b, q: (b, 0)),
            pl.BlockSpec((seq, LANES), lambda b, q: (b, 0)),
            pl.BlockSpec((seq, ATT_KV_W), lambda b, q: (b, kcol)),
            pl.BlockSpec((seq, ATT_KV_W), lambda b, q: (b, kcol + 1)),
            pl.BlockSpec((LANES, ATT_KV_W), lambda b, q: (0, 0)),
            pl.BlockSpec((LANES, ATT_KV_W), lambda b, q: (0, 0)),
        ],
        out_specs=pl.BlockSpec((tq, ATT_Q_W), lambda b, q: (b * nq + q, 0)),
        scratch_shapes=[
            pltpu.VMEM((seq, tq), F32),
            pltpu.VMEM((tq, seq), F32),
            pltpu.VMEM((N_KV_HEADS, LANES + seq, 2 * LANES), BF16),
            pltpu.VMEM((N_KV_HEADS, LANES + seq, 2 * LANES), BF16),
            pltpu.VMEM((rows4, 2 * LANES), BF16),
            pltpu.VMEM((rows4, kc), F32),
            pltpu.VMEM((rows4, kc), BF16),
            pltpu.VMEM((rows4, LANES), F32),
            pltpu.VMEM((rows4, LANES), F32),
            pltpu.VMEM((rows4, 2 * LANES), F32),
        ],
        compiler_params=pltpu.CompilerParams(
            dimension_semantics=("parallel", "arbitrary"), vmem_limit_bytes=VMEM_LIMIT),
        name="dsa_attention",
    )(p_att, p_att, iw_t, ik_lo, ik_hi, p_att, p_att, km, vm)


def _split3(x):
    hi = x.astype(BF16)
    r = x - hi.astype(F32)
    mid = r.astype(BF16)
    lo = (r - mid.astype(F32)).astype(BF16)
    return hi, mid, lo


HG_SAFE_EXPONENT = 80.0


def _hgrn_kernel(hf_ref, hr_q_ref, hr_i_ref, hr_g_ref, lb_ref, ng_ref, s0_ref,
                 y_ref, sT_out_ref, sT_scr, *, chunk, sub, heads, n_inner):
    c_idx = pl.program_id(2)

    @pl.when(c_idx == 0)
    def _():
        sT_scr[...] = s0_ref[...]

    n_sub = chunk // sub
    causal = (lax.broadcasted_iota(jnp.int32, (chunk, chunk), 0)
              >= lax.broadcasted_iota(jnp.int32, (chunk, chunk), 1))
    tri = causal.astype(BF16)
    t_iota = lax.broadcasted_iota(jnp.int32, (sub, LANES), 0)

    def head_inputs(r0, g):
        rows, cs = pl.ds(r0, chunk), slice(g * HG_DIM, (g + 1) * HG_DIM)
        lb = lb_ref[:, cs]
        q = _silu(hr_q_ref[rows, cs].astype(F32))
        fg = lb + (1.0 - lb) * jax.nn.sigmoid(hf_ref[rows, cs])
        v = hr_i_ref[rows, cs].astype(F32)
        l_hi, l_mid, l_lo = _split3(jnp.log(fg))
        b = _dot(tri, l_hi) + _dot(tri, l_mid) + _dot(tri, l_lo)
        return q, 1.0 - fg, v, b

    def carry_state(g, q, kk, vb, b):
        sT = sT_scr[g]
        b_last = b[chunk - 1:chunk, :]
        qe = (q * jnp.exp(b)).astype(BF16)
        o_state = _dot_nt(qe, sT.astype(BF16))
        khat = (kk * jnp.exp(b_last - b)).astype(BF16)
        sT_scr[g] = sT * jnp.exp(b_last) + _dot_tn(vb, khat)
        return qe, o_state

    def finish(r0, g, o):
        rows, cs = pl.ds(r0, chunk), slice(g * HG_DIM, (g + 1) * HG_DIM)
        o = o * lax.rsqrt(jnp.mean(o * o, axis=-1, keepdims=True) + RMS_EPS) * ng_ref[:, cs]
        o = o * _silu(hr_g_ref[rows, cs].astype(F32))
        y_ref[rows, cs] = o.astype(y_ref.dtype)

    def factored_chunk(c, carry):
        rows = pl.ds(pl.multiple_of(c * chunk, chunk), chunk)
        hs = [slice(g * HG_DIM, (g + 1) * HG_DIM) for g in range(heads)]
        lb = lb_ref[...]
        q = _silu(hr_q_ref[rows, :].astype(F32))
        fg = lb + (1.0 - lb) * jax.nn.sigmoid(hf_ref[rows, :])
        kk = 1.0 - fg
        vb = hr_i_ref[rows, :]
        l_hi, l_mid, l_lo = _split3(jnp.log(fg))
        b = _dot(tri, l_hi) + _dot(tri, l_mid) + _dot(tri, l_lo)
        b_last = b[chunk - 1:chunk, :]
        qe = (q * jnp.exp(b)).astype(BF16)
        ke = (kk * jnp.exp(-b)).astype(BF16)
        khat = (kk * jnp.exp(b_last - b)).astype(BF16)
        keep = jnp.exp(b_last)
        s_old = [sT_scr[g] for g in range(heads)]
        a = [_dot_nt(qe[:, cs], ke[:, cs]) for cs in hs]
        o_state = [_dot_nt(qe[:, cs], s_old[g].astype(BF16)) for g, cs in enumerate(hs)]
        s_add = [_dot_tn(vb[:, cs], khat[:, cs]) for cs in hs]
        a = [jnp.where(causal, x, 0.0).astype(BF16) for x in a]
        o = [o_state[g] + _dot(a[g], vb[:, cs]) for g, cs in enumerate(hs)]
        for g, cs in enumerate(hs):
            sT_scr[g] = s_old[g] * keep[:, cs] + s_add[g]
        o = [x * lax.rsqrt(jnp.mean(x * x, axis=-1, keepdims=True) + RMS_EPS) for x in o]
        o = jnp.concatenate(o, axis=1) * ng_ref[...] * _silu(hr_g_ref[rows, :].astype(F32))
        y_ref[rows, :] = o.astype(y_ref.dtype)
        return carry

    def guarded_chunk(c, carry):
        r0 = pl.multiple_of(c * chunk, chunk)
        for g in range(heads):
            q, kk, v, b = head_inputs(r0, g)
            vb = v.astype(BF16)
            _, o_state = carry_state(g, q, kk, vb, b)
            rows = []
            for i in range(n_sub):
                s0, s1 = i * sub, (i + 1) * sub
                bs, qs, ks, vs = b[s0:s1], q[s0:s1], kk[s0:s1], v[s0:s1]
                o_i = o_state[s0:s1]
                if i > 0:
                    r_i = b[s0 - 1:s0, :]
                    qt = (qs * jnp.exp(bs - r_i)).astype(BF16)
                    kt = (kk[:s0] * jnp.exp(r_i - b[:s0])).astype(BF16)
                    a_off = _dot_nt(qt, kt)
                    o_i = o_i + _dot(a_off.astype(BF16), vb[:s0])
                for s in range(sub):
                    e = jnp.exp(jnp.where(t_iota >= s, bs - bs[s:s + 1], -jnp.inf))
                    a_col = jnp.sum(qs * e * ks[s:s + 1], axis=1, keepdims=True)
                    o_i = o_i + a_col * vs[s:s + 1]
                rows.append(o_i)
            finish(r0, g, jnp.concatenate(rows, axis=0) if n_sub > 1 else rows[0])
        return carry

    factoring_safe = jnp.max(-jnp.log(lb_ref[...])) * chunk <= HG_SAFE_EXPONENT

    @pl.when(factoring_safe)
    def _():
        lax.fori_loop(0, n_inner, factored_chunk, 0)

    @pl.when(jnp.logical_not(factoring_safe))
    def _():
        lax.fori_loop(0, n_inner, guarded_chunk, 0)

    @pl.when(c_idx == pl.num_programs(2) - 1)
    def _():
        sT_out_ref[0] = sT_scr[...]


def _hgrn(hf, hr, lb, ng, s0, *, batch, seq, chunk, sub, heads, n_inner):
    m = batch * seq
    rows = chunk * n_inner
    nc = seq // rows
    hw = heads * HG_DIM
    nhg = HG_HEADS // heads
    return pl.pallas_call(
        functools.partial(_hgrn_kernel, chunk=chunk, sub=sub, heads=heads, n_inner=n_inner),
        out_shape=(jax.ShapeDtypeStruct((m, HG_W), BF16),
                   jax.ShapeDtypeStruct((batch, HG_HEADS, HG_DIM, HG_DIM), F32)),
        grid=(batch, nhg, nc),
        in_specs=[
            pl.BlockSpec((rows, hw), lambda b, h, c: (b * nc + c, h)),
            pl.BlockSpec((rows, hw), lambda b, h, c: (b * nc + c, h)),
            pl.BlockSpec((rows, hw), lambda b, h, c: (b * nc + c, nhg + h)),
            pl.BlockSpec((rows, hw), lambda b, h, c: (b * nc + c, 2 * nhg + h)),
            pl.BlockSpec((1, hw), lambda b, h, c: (0, h)),
            pl.BlockSpec((1, hw), lambda b, h, c: (0, h)),
            pl.BlockSpec((heads, HG_DIM, HG_DIM), lambda b, h, c: (h, 0, 0)),
        ],
        out_specs=(
            pl.BlockSpec((rows, hw), lambda b, h, c: (b * nc + c, h)),
            pl.BlockSpec((1, heads, HG_DIM, HG_DIM), lambda b, h, c: (b, h, 0, 0)),
        ),
        scratch_shapes=[pltpu.VMEM((heads, HG_DIM, HG_DIM), F32)],
        compiler_params=pltpu.CompilerParams(
            dimension_semantics=("parallel", "parallel", "arbitrary"), vmem_limit_bytes=VMEM_LIMIT),
        name="hgrn2",
    )(hf, hr, hr, hr, lb, ng, s0)


def _merge_kernel(ya_ref, yh_ref, gts_a_ref, gts_h_ref, h_ref, wa_ref, wh_ref, wo_ref, g_ref, b_ref,
                  o_ref, acc_scr):
    j = pl.program_id(1)

    @pl.when(j == 0)
    def _():
        acc_scr[...] = jnp.zeros_like(acc_scr)

    ga = jax.nn.sigmoid(gts_a_ref[...].astype(F32))
    gh = jax.nn.sigmoid(gts_h_ref[...].astype(F32))
    merged = ga * _dot(ya_ref[...], wa_ref[...]) + gh * _dot(yh_ref[...], wh_ref[...])
    acc_scr[...] += _dot(merged.astype(BF16), wo_ref[...])

    @pl.when(j == pl.num_programs(1) - 1)
    def _():
        o_ref[...] = _layer_norm(ALPHA * h_ref[...] + acc_scr[...], g_ref[...], b_ref[...])


def _merge(ya, yh, gates, h1, wa, wh, wo, g, b, *, tm, tn):
    m = ya.shape[0]
    nj = D_MODEL // tn
    return pl.pallas_call(
        _merge_kernel,
        out_shape=jax.ShapeDtypeStruct((m, D_MODEL), F32),
        grid=(m // tm, nj),
        in_specs=[
            pl.BlockSpec((tm, ATT_Q_W), lambda i, j: (i, 0)),
            pl.BlockSpec((tm, HG_W), lambda i, j: (i, 0)),
            pl.BlockSpec((tm, tn), lambda i, j: (i, j)),
            pl.BlockSpec((tm, tn), lambda i, j: (i, nj + j)),
            pl.BlockSpec((tm, D_MODEL), lambda i, j: (i, 0)),
            pl.BlockSpec((ATT_Q_W, tn), lambda i, j: (0, j)),
            pl.BlockSpec((HG_W, tn), lambda i, j: (0, j)),
            pl.BlockSpec((tn, D_MODEL), lambda i, j: (j, 0)),
            pl.BlockSpec((1, D_MODEL), lambda i, j: (0, 0)),
            pl.BlockSpec((1, D_MODEL), lambda i, j: (0, 0)),
        ],
        out_specs=pl.BlockSpec((tm, D_MODEL), lambda i, j: (i, 0)),
        scratch_shapes=[pltpu.VMEM((tm, D_MODEL), F32)],
        compiler_params=pltpu.CompilerParams(
            dimension_semantics=("parallel", "arbitrary"), vmem_limit_bytes=VMEM_LIMIT),
        name="merge_ln",
    )(ya, yh, gates, gates, h1, wa, wh, wo, g, b)


def kernel(x, meta, ffn1_w_gate, ffn1_w_up, ffn1_w_down, ln1_g, ln1_b, w_in, idx_k_norm_g, idx_k_norm_b,
           hg_lb_logits, hg_norm_g, w_branch_att, w_branch_hg, w_out, ln2_g, ln2_b,
           ffn2_w_gate, ffn2_w_up, ffn2_w_down, ln3_g, ln3_b):
    batch, seq, _ = x.shape
    m = batch * seq
    xr = x.reshape(m, D_MODEL)
    bf = lambda w: w.astype(BF16)
    row = lambda v: v.reshape(1, -1)

    splits = (ATT_Q_W, ATT_KV_W, ATT_KV_W, IDX_Q_W, IDX_HEAD_DIM, N_IDX_HEADS, HG_W, HG_W, HG_W, HG_W,
              D_MODEL, D_MODEL)
    offs = [0]
    for s in splits:
        offs.append(offs[-1] + s)
    w = bf(w_in[0])
    seg = lambda i: w[:, offs[i]:offs[i + 1]]
    w_att = jnp.concatenate([seg(0), seg(3), seg(1), seg(2)], axis=1)
    w_idx = jnp.pad(jnp.concatenate([seg(4), seg(5)], axis=1),
                    ((0, 0), (0, LANES - IDX_HEAD_DIM - N_IDX_HEADS)))
    w_hf = seg(7)
    w_hr = jnp.concatenate([seg(6), seg(8), seg(9)], axis=1)
    w_gates = jnp.concatenate([seg(10), seg(11)], axis=1)
    idx_g = jnp.pad(idx_k_norm_g[0], (0, LANES - IDX_HEAD_DIM)).reshape(1, LANES)
    idx_b = jnp.pad(idx_k_norm_b[0], (0, LANES - IDX_HEAD_DIM)).reshape(1, LANES)
    lb = jnp.cumsum(jax.nn.softmax(hg_lb_logits.astype(F32), axis=0), axis=0)[0].reshape(1, HG_W)
    ng = hg_norm_g[0].reshape(1, HG_W)

    f1 = (bf(ffn1_w_gate[0]), bf(ffn1_w_up[0]), bf(ffn1_w_down[0]), row(ln1_g[0]), row(ln1_b[0]))
    f2 = (bf(ffn2_w_gate[0]), bf(ffn2_w_up[0]), bf(ffn2_w_down[0]), row(ln3_g[0]), row(ln3_b[0]))
    tn = 512
    q_tiles = ATT_Q_W // tn

    (hm_b,) = _ffn_ln(meta.astype(F32), *f1, tm=N_META, tf=512, out_dtypes=(BF16,))
    pm_att = _proj(hm_b, w_att, BF16, tm=N_META, tn=tn, scaled_tiles=q_tiles, scale=ATT_Q_SCALE)
    pm_hf = _proj(hm_b, w_hf, F32, tm=N_META, tn=tn)
    pm_hr = _proj(hm_b, w_hr, BF16, tm=N_META, tn=tn)
    kv0 = ATT_Q_W + IDX_Q_W
    km = jnp.pad(pm_att[:, kv0:kv0 + ATT_KV_W], ((0, LANES - N_META), (0, 0)))
    vm = jnp.pad(pm_att[:, kv0 + ATT_KV_W:], ((0, LANES - N_META), (0, 0)))
    s_zero = jnp.zeros((HG_HEADS, HG_DIM, HG_DIM), F32)
    _, s_meta = _hgrn(pm_hf, pm_hr, lb, ng, s_zero, batch=1, seq=N_META, chunk=N_META, sub=N_META,
                      heads=HG_HEADS, n_inner=1)

    h1, h1b = _ffn_ln(xr, *f1, tm=512, tf=512, out_dtypes=(F32, BF16))
    p_att = _proj(h1b, w_att, BF16, tm=1024, tn=tn, scaled_tiles=q_tiles, scale=ATT_Q_SCALE)
    p_idx = _proj_idx(h1b, w_idx, idx_g, idx_b, tm=1024)
    p_hf = _proj(h1b, w_hf, F32, tm=1024, tn=tn)
    p_hr = _proj(h1b, w_hr, BF16, tm=1024, tn=tn)
    p_gates = _proj(h1b, w_gates, BF16, tm=1024, tn=tn)

    ikn = p_idx[:, :IDX_HEAD_DIM].astype(BF16)
    ik_lo = jnp.pad(ikn, ((0, 0), (0, LANES - IDX_HEAD_DIM)))
    ik_hi = jnp.pad(ikn, ((0, 0), (LANES - IDX_HEAD_DIM, 0)))
    iw_t = p_idx[:, IDX_HEAD_DIM:IDX_HEAD_DIM + N_IDX_HEADS].T
    y_att = _attention(p_att, iw_t, ik_lo, ik_hi, km, vm, batch=batch, seq=seq, tq=256, kc=512)
    y_hg, _ = _hgrn(p_hf, p_hr, lb, ng, s_meta[0], batch=batch, seq=seq, chunk=64, sub=16,
                    heads=HG_HEADS, n_inner=4)

    h2 = _merge(y_att, y_hg, p_gates, h1, bf(w_branch_att[0]), bf(w_branch_hg[0]), bf(w_out[0]),
                row(ln2_g[0]), row(ln2_b[0]), tm=512, tn=512)
    (out,) = _ffn_ln(h2, *f2, tm=512, tf=512, out_dtypes=(F32,))
    return out.reshape(batch, seq, D_MODEL)
```

```python
import functools
import math

import jax
import jax.numpy as jnp
import numpy as np
from jax import lax
from jax.experimental import pallas as pl
from jax.experimental.pallas import tpu as pltpu

D_MODEL = 2048
N_META = 16
N_ATT_HEADS = 8
N_KV_HEADS = 2
ATT_GROUP = N_ATT_HEADS // N_KV_HEADS
ATT_HEAD_DIM = 128
N_IDX_HEADS = 16
IDX_HEAD_DIM = 64
TOPK = 256
HG_HEADS = 8
HG_DIM = 128
D_FF = 5632
LN_EPS = 1e-5
RMS_EPS = 1e-6
ALPHA = 2.0 ** 0.25

ATT_Q_W = N_ATT_HEADS * ATT_HEAD_DIM
ATT_KV_W = N_KV_HEADS * ATT_HEAD_DIM
IDX_Q_W = N_IDX_HEADS * IDX_HEAD_DIM
HG_W = HG_HEADS * HG_DIM

LANES = 128
SUBLANES = 8
VMEM_LIMIT = 56 * 1024 * 1024

F32 = jnp.float32
BF16 = jnp.bfloat16
INT_MIN = -2 ** 31
F32_LOWEST = float(np.finfo(np.float32).min)
LOG2E = math.log2(math.e)
ATT_Q_SCALE = ATT_HEAD_DIM ** -0.5 * LOG2E
POS_SHIFT = 6
POS_SPLIT = 1 << POS_SHIFT


def _dot(a, b):
    return jnp.dot(a, b, preferred_element_type=F32)


def _dot_nt(a, b):
    return lax.dot_general(a, b, (((1,), (1,)), ((), ())), preferred_element_type=F32)


def _dot_tn(a, b):
    return lax.dot_general(a, b, (((0,), (0,)), ((), ())), preferred_element_type=F32)


def _layer_norm(y, g, b):
    mu = jnp.mean(y, axis=-1, keepdims=True)
    d = y - mu
    var = jnp.mean(d * d, axis=-1, keepdims=True)
    return d * lax.rsqrt(var + LN_EPS) * g + b


def _silu(x):
    return x * jax.nn.sigmoid(x)


def _bf16_parts(c, n=3):
    parts = []
    for _ in range(n):
        p = float(np.asarray(c, np.float32).astype(jnp.bfloat16).astype(np.float32))
        parts.append(p)
        c = c - p
    return parts


def _ffn_ln_kernel(x_ref, wg_ref, wu_ref, wd_ref, g_ref, b_ref, *rest):
    *out_refs, xb_scr, acc_scr = rest
    j = pl.program_id(1)

    @pl.when(j == 0)
    def _():
        xb_scr[...] = x_ref[...].astype(BF16)
        acc_scr[...] = jnp.zeros_like(acc_scr)

    xb = xb_scr[...]
    gate = _dot(xb, wg_ref[...])
    up = _dot(xb, wu_ref[...])
    a = (_silu(gate) * up).astype(BF16)
    acc_scr[...] += _dot(a, wd_ref[...])

    @pl.when(j == pl.num_programs(1) - 1)
    def _():
        y = ALPHA * x_ref[...] + 0.5 * acc_scr[...]
        out = _layer_norm(y, g_ref[...], b_ref[...])
        for o_ref in out_refs:
            o_ref[...] = out.astype(o_ref.dtype)


def _ffn_ln(x, wg, wu, wd, g, b, *, tm, tf, out_dtypes):
    m = x.shape[0]
    grid = (m // tm, D_FF // tf)
    return pl.pallas_call(
        _ffn_ln_kernel,
        out_shape=tuple(jax.ShapeDtypeStruct((m, D_MODEL), dt) for dt in out_dtypes),
        grid=grid,
        in_specs=[
            pl.BlockSpec((tm, D_MODEL), lambda i, j: (i, 0)),
            pl.BlockSpec((D_MODEL, tf), lambda i, j: (0, j)),
            pl.BlockSpec((D_MODEL, tf), lambda i, j: (0, j)),
            pl.BlockSpec((tf, D_MODEL), lambda i, j: (j, 0)),
            pl.BlockSpec((1, D_MODEL), lambda i, j: (0, 0)),
            pl.BlockSpec((1, D_MODEL), lambda i, j: (0, 0)),
        ],
        out_specs=tuple(pl.BlockSpec((tm, D_MODEL), lambda i, j: (i, 0)) for _ in out_dtypes),
        scratch_shapes=[pltpu.VMEM((tm, D_MODEL), BF16), pltpu.VMEM((tm, D_MODEL), F32)],
        compiler_params=pltpu.CompilerParams(
            dimension_semantics=("parallel", "arbitrary"), vmem_limit_bytes=VMEM_LIMIT),
        name="ffn_ln",
    )(x, wg, wu, wd, g, b)


def _mm_kernel(x_ref, w_ref, o_ref, *, scaled_tiles, scale):
    acc = _dot(x_ref[...], w_ref[...])
    if scaled_tiles:
        acc = acc * jnp.where(pl.program_id(1) < scaled_tiles, scale, 1.0)
    o_ref[...] = acc.astype(o_ref.dtype)


def _idx_kernel(x_ref, w_ref, g_ref, b_ref, o_ref):
    p = _dot(x_ref[...], w_ref[...])
    lane = lax.broadcasted_iota(jnp.int32, p.shape, 1)
    is_k = lane < IDX_HEAD_DIM
    mu = jnp.sum(jnp.where(is_k, p, 0.0), axis=-1, keepdims=True) * (1.0 / IDX_HEAD_DIM)
    d = jnp.where(is_k, p - mu, 0.0)
    var = jnp.sum(d * d, axis=-1, keepdims=True) * (1.0 / IDX_HEAD_DIM)
    kn = d * lax.rsqrt(var + LN_EPS) * g_ref[...] + b_ref[...]
    w_scale = (N_IDX_HEADS ** -0.5) * (IDX_HEAD_DIM ** -0.5)
    o_ref[...] = jnp.where(is_k, kn, p * w_scale)


def _proj(xb, w, out_dtype, *, tm, tn, scaled_tiles=0, scale=1.0):
    m, n = xb.shape[0], w.shape[1]
    return pl.pallas_call(
        functools.partial(_mm_kernel, scaled_tiles=scaled_tiles, scale=scale),
        out_shape=jax.ShapeDtypeStruct((m, n), out_dtype),
        grid=(m // tm, n // tn),
        in_specs=[pl.BlockSpec((tm, D_MODEL), lambda i, j: (i, 0)),
                  pl.BlockSpec((D_MODEL, tn), lambda i, j: (0, j))],
        out_specs=pl.BlockSpec((tm, tn), lambda i, j: (i, j)),
        compiler_params=pltpu.CompilerParams(
            dimension_semantics=("parallel", "arbitrary"), vmem_limit_bytes=VMEM_LIMIT),
        name="proj",
    )(xb, w)


def _proj_idx(xb, w, g, b, *, tm):
    m = xb.shape[0]
    return pl.pallas_call(
        _idx_kernel,
        out_shape=jax.ShapeDtypeStruct((m, LANES), F32),
        grid=(m // tm,),
        in_specs=[pl.BlockSpec((tm, D_MODEL), lambda i: (i, 0)),
                  pl.BlockSpec((D_MODEL, LANES), lambda i: (0, 0)),
                  pl.BlockSpec((1, LANES), lambda i: (0, 0)),
                  pl.BlockSpec((1, LANES), lambda i: (0, 0))],
        out_specs=pl.BlockSpec((tm, LANES), lambda i: (i, 0)),
        compiler_params=pltpu.CompilerParams(
            dimension_semantics=("parallel",), vmem_limit_bytes=VMEM_LIMIT),
        name="proj_idx",
    )(xb, w, g, b)


ATT_ROW_TILE = 32


def _key_to_f32(key):
    return pltpu.bitcast(key ^ ((key >> 31) & 0x7FFFFFFF), F32)


def _attn_kernel(aq_ref, iq_ref, iwt_ref, iklo_ref, ikhi_ref, k_ref, v_ref, km_ref, vm_ref,
                 o_ref,
                 score_scr, mb_scr, kaug_scr, vaug_scr, qaug_scr, s0_scr, s1_scr, p_scr,
                 macc_scr, m_scr, acc_scr, *, tq, kc, seq):
    qi = pl.program_id(1)
    n_kc = ((qi + 1) * tq + kc - 1) // kc
    kf = float(TOPK)

    @pl.when(qi == 0)
    def _():
        r = lax.broadcasted_iota(jnp.int32, (LANES + seq, LANES), 0)
        lane = lax.broadcasted_iota(jnp.int32, (LANES + seq, LANES), 1)
        pos = jnp.where(r < LANES, r, r - LANES + N_META)
        feat = jnp.where(lane < 3, pos >> POS_SHIFT, jnp.where(lane < 6, pos & (POS_SPLIT - 1), 0))
        feat = feat.astype(F32).astype(BF16)
        ones = jnp.ones((LANES + seq, LANES), BF16)
        for kvh in range(N_KV_HEADS):
            cs = slice(kvh * LANES, (kvh + 1) * LANES)
            kaug_scr[kvh, 0:LANES, 0:LANES] = km_ref[:, cs]
            kaug_scr[kvh, LANES:, 0:LANES] = k_ref[:, cs]
            kaug_scr[kvh, :, LANES:] = feat
            vaug_scr[kvh, 0:LANES, 0:LANES] = vm_ref[:, cs]
            vaug_scr[kvh, LANES:, 0:LANES] = v_ref[:, cs]
            vaug_scr[kvh, :, LANES:] = ones

    qcol = qi * tq + lax.broadcasted_iota(jnp.int32, (1, tq), 1)
    iwt = iwt_ref[...]

    def score_body(j, carry):
        off = pl.multiple_of(j * kc, kc)
        klo = iklo_ref[pl.ds(off, kc), :]
        khi = ikhi_ref[pl.ds(off, kc), :]
        acc = jnp.zeros((kc, tq), F32)
        for p in range(N_IDX_HEADS // 2):
            q2 = iq_ref[:, p * LANES:(p + 1) * LANES]
            acc = acc + jnp.maximum(_dot_nt(klo, q2), 0.0) * iwt[2 * p:2 * p + 1, :]
            acc = acc + jnp.maximum(_dot_nt(khi, q2), 0.0) * iwt[2 * p + 1:2 * p + 2, :]
        krow = off + lax.broadcasted_iota(jnp.int32, (kc, tq), 0)
        score_scr[pl.ds(off, kc), :] = jnp.where(krow <= qcol, acc, -jnp.inf)
        return carry

    lax.fori_loop(0, n_kc, score_body, 0)

    n_acc = 8
    acc_rows = n_acc * SUBLANES

    def count_ge(cand):
        def body(j, acc):
            off = pl.multiple_of(j * kc, kc)
            w = jnp.where(score_scr[pl.ds(off, kc), :] >= cand, 1.0, 0.0)
            return acc + jnp.sum(w.reshape(kc // acc_rows, acc_rows, tq), axis=0)

        acc = lax.fori_loop(0, n_kc, body, jnp.zeros((acc_rows, tq), F32))
        return jnp.sum(acc, axis=0, keepdims=True)

    c0 = count_ge(jnp.zeros((1, tq), F32))
    ok0 = c0 >= kf
    thr0 = jnp.where(ok0, 0, INT_MIN).astype(jnp.int32)
    cnt0 = jnp.where(ok0, c0, 0.0)

    def bit_body(i, carry):
        thr, cnt = carry
        cand = thr | jnp.left_shift(jnp.int32(1), 30 - i)
        c = count_ge(_key_to_f32(cand))
        ok = c >= kf
        return jnp.where(ok, cand, thr), jnp.where(ok, c, cnt)

    thr, cnt = lax.fori_loop(0, 31, bit_body, (thr0, cnt0))
    t_f = jnp.where(thr == INT_MIN, F32_LOWEST, _key_to_f32(thr))

    @pl.when(jnp.max(cnt) > kf)
    def _():
        def gt_body(j, acc):
            off = pl.multiple_of(j * kc, kc)
            w = jnp.where(score_scr[pl.ds(off, kc), :] > t_f, 1.0, 0.0)
            return acc + jnp.sum(w, axis=0, keepdims=True)

        need = kf - lax.fori_loop(0, n_kc, gt_body, jnp.zeros((1, tq), F32))
        lower = (lax.broadcasted_iota(jnp.int32, (kc, kc), 0)
                 >= lax.broadcasted_iota(jnp.int32, (kc, kc), 1)).astype(BF16)

        def tie_body(j, before):
            off = pl.multiple_of(j * kc, kc)
            sc = score_scr[pl.ds(off, kc), :]
            eq = sc == t_f
            eqf = jnp.where(eq, 1.0, 0.0)
            rank = before + _dot(lower, eqf.astype(BF16))
            score_scr[pl.ds(off, kc), :] = jnp.where(eq & (rank > need), -jnp.inf, sc)
            return before + jnp.sum(eqf, axis=0, keepdims=True)

        lax.fori_loop(0, n_kc, tie_body, jnp.zeros((1, tq), F32))

    eye = (lax.broadcasted_iota(jnp.int32, (tq, tq), 0)
           == lax.broadcasted_iota(jnp.int32, (tq, tq), 1)).astype(BF16)

    def mask_body(j, carry):
        off = pl.multiple_of(j * kc, kc)
        sel_t = jnp.where(score_scr[pl.ds(off, kc), :] >= t_f, 1.0, 0.0).astype(BF16)
        sel = _dot_nt(eye, sel_t)
        mb_scr[:, pl.ds(off, kc)] = jnp.where(sel > 0.5, 0.0, -jnp.inf)
        return carry

    lax.fori_loop(0, n_kc, mask_body, 0)

    rt = ATT_ROW_TILE
    lane_q = lax.broadcasted_iota(jnp.int32, (1, LANES), 1)
    meta_mask = jnp.where(lax.broadcasted_iota(jnp.int32, (rt, LANES), 1) < N_META, 0.0, -jnp.inf)

    s_slots = (s0_scr, s1_scr)

    def logits(kvh, krow0, width, slot):
        s_slots[slot][:, 0:width] = _dot_nt(qaug_scr[kvh], kaug_scr[kvh, pl.ds(krow0, width), :])

    def max_tiles(kvh, slot, width, mask_fn):
        s_ref = s_slots[slot]
        for r0 in range(0, tq, rt):
            mb = mask_fn(r0)
            for g in range(ATT_GROUP):
                rows = slice(g * tq + r0, g * tq + r0 + rt)
                mx = s_ref[rows, 0:LANES] + mb[:, 0:LANES]
                for c in range(1, width // LANES):
                    cols = slice(c * LANES, (c + 1) * LANES)
                    mx = jnp.maximum(mx, s_ref[rows, cols] + mb[:, cols])
                macc_scr[rows, :] = jnp.maximum(macc_scr[rows, :], mx)

    def exp_tiles(kvh, slot, width, mask_fn):
        s_ref = s_slots[slot]
        for r0 in range(0, tq, rt):
            mb = mask_fn(r0)
            for g in range(ATT_GROUP):
                rows = slice(g * tq + r0, g * tq + r0 + rt)
                m_row = m_scr[rows, :]
                for c in range(width // LANES):
                    cols = slice(c * LANES, (c + 1) * LANES)
                    p_scr[rows, cols] = jnp.exp2(s_ref[rows, cols] + mb[:, cols] - m_row).astype(BF16)

    def add_pv(kvh, krow0, width):
        vblk = vaug_scr[kvh, pl.ds(krow0, width), :]
        half = ATT_GROUP * tq // 2
        for r0 in (0, half):
            acc_scr[kvh, r0:r0 + half, :] += _dot(p_scr[r0:r0 + half, 0:width], vblk)

    def run_pass(kvh, tiles_fn, with_pv):
        def krow(j):
            return pl.multiple_of(LANES + j * kc, LANES)

        def step(j, slot, lookahead):
            off = pl.multiple_of(j * kc, kc)
            if lookahead:
                logits(kvh, krow(j + 1), kc, 1 - slot)
            tiles_fn(kvh, slot, kc, lambda r0: mb_scr[r0:r0 + rt, pl.ds(off, kc)])
            if with_pv:
                add_pv(kvh, krow(j), kc)

        logits(kvh, 0, LANES, 0)
        tiles_fn(kvh, 0, LANES, lambda r0: meta_mask)
        if with_pv:
            add_pv(kvh, 0, LANES)

            def serial_body(j, carry):
                off = pl.multiple_of(j * kc, kc)
                logits(kvh, krow(j), kc, 0)
                tiles_fn(kvh, 0, kc, lambda r0: mb_scr[r0:r0 + rt, pl.ds(off, kc)])
                add_pv(kvh, krow(j), kc)
                return carry

            lax.fori_loop(0, n_kc, serial_body, 0)
            return
        logits(kvh, krow(0), kc, 0)
        n_pairs = (n_kc - 1) // 2

        def pair_body(i, carry):
            step(2 * i, 0, True)
            step(2 * i + 1, 1, True)
            return carry

        lax.fori_loop(0, n_pairs, pair_body, 0)
        j0 = 2 * n_pairs
        two_left = n_kc - j0 == 2

        @pl.when(two_left)
        def _():
            step(j0, 0, True)
            step(j0 + 1, 1, False)

        @pl.when(jnp.logical_not(two_left))
        def _():
            step(j0, 0, False)

    for h in range(N_ATT_HEADS):
        kvh, g = divmod(h, ATT_GROUP)
        parts = _bf16_parts(2.0 ** (-8.0 * (h + 1) / N_ATT_HEADS) * LOG2E)
        vals = [POS_SPLIT * p for p in parts] + parts
        qfeat = jnp.zeros((1, LANES), F32)
        for i, val in enumerate(vals):
            qfeat = jnp.where(lane_q == i, val, qfeat)
        qaug_scr[kvh, g * tq:(g + 1) * tq, 0:LANES] = aq_ref[:, h * LANES:(h + 1) * LANES]
        qaug_scr[kvh, g * tq:(g + 1) * tq, LANES:] = jnp.broadcast_to(qfeat, (tq, LANES)).astype(BF16)

    for kvh in range(N_KV_HEADS):
        macc_scr[...] = jnp.full_like(macc_scr, -jnp.inf)
        run_pass(kvh, max_tiles, with_pv=False)
        m_scr[...] = jnp.broadcast_to(jnp.max(macc_scr[...], axis=1, keepdims=True), m_scr.shape)
        acc_scr[kvh] = jnp.zeros(acc_scr.shape[1:], F32)
        run_pass(kvh, exp_tiles, with_pv=True)

    for h in range(N_ATT_HEADS):
        kvh, g = divmod(h, ATT_GROUP)
        acc = acc_scr[kvh, g * tq:(g + 1) * tq, :]
        o_ref[:, h * LANES:(h + 1) * LANES] = (acc[:, 0:LANES] / acc[:, LANES:LANES + 1]).astype(o_ref.dtype)


def _attention(p_att, iw_t, ik_lo, ik_hi, km, vm, *, batch, seq, tq, kc):
    m = batch * seq
    nq = seq // tq
    kcol = (ATT_Q_W + IDX_Q_W) // ATT_KV_W
    rows4 = ATT_GROUP * tq
    return pl.pallas_call(
        functools.partial(_attn_kernel, tq=tq, kc=kc, seq=seq),
        out_shape=jax.ShapeDtypeStruct((m, ATT_Q_W), BF16),
        grid=(batch, nq),
        in_specs=[
            pl.BlockSpec((tq, ATT_Q_W), lambda b, q: (b * nq + q, 0)),
            pl.BlockSpec((tq, IDX_Q_W), lambda b, q: (b * nq + q, 1)),
            pl.BlockSpec((N_IDX_HEADS, tq), lambda b, q: (0, b * nq + q)),
            pl.BlockSpec((seq, LANES), lambda b, q: (b, 0)),
            pl.BlockSpec((seq, LANES), lambda b, q: (b, 0)),
            pl.BlockSpec((seq, ATT_KV_W), lambda b, q: (b, kcol)),
            pl.BlockSpec((seq, ATT_KV_W), lambda b, q: (b, kcol + 1)),
            pl.BlockSpec((LANES, ATT_KV_W), lambda b, q: (0, 0)),
            pl.BlockSpec((LANES, ATT_KV_W), lambda b, q: (0, 0)),
        ],
        out_specs=pl.BlockSpec((tq, ATT_Q_W), lambda b, q: (b * nq + q, 0)),
        scratch_shapes=[
            pltpu.VMEM((seq, tq), F32),
            pltpu.VMEM((tq, seq), F32),
            pltpu.VMEM((N_KV_HEADS, LANES + seq, 2 * LANES), BF16),
            pltpu.VMEM((N_KV_HEADS, LANES + seq, 2 * LANES), BF16),
            pltpu.VMEM((N_KV_HEADS, rows4, 2 * LANES), BF16),
            pltpu.VMEM((rows4, kc), F32),
            pltpu.VMEM((rows4, kc), F32),
            pltpu.VMEM((rows4, kc), BF16),
            pltpu.VMEM((rows4, LANES), F32),
            pltpu.VMEM((rows4, LANES), F32),
            pltpu.VMEM((N_KV_HEADS, rows4, 2 * LANES), F32),
        ],
        compiler_params=pltpu.CompilerParams(
            dimension_semantics=("parallel", "arbitrary"), vmem_limit_bytes=VMEM_LIMIT),
        name="dsa_attention",
    )(p_att, p_att, iw_t, ik_lo, ik_hi, p_att, p_att, km, vm)


def _split3(x):
    hi = x.astype(BF16)
    r = x - hi.astype(F32)
    mid = r.astype(BF16)
    lo = (r - mid.astype(F32)).astype(BF16)
    return hi, mid, lo


HG_SAFE_EXPONENT = 80.0


def _hgrn_kernel(hf_ref, hr_q_ref, hr_i_ref, hr_g_ref, lb_ref, ng_ref, s0_ref,
                 y_ref, sT_out_ref, sT_scr, *, chunk, sub, heads, n_inner):
    c_idx = pl.program_id(2)

    @pl.when(c_idx == 0)
    def _():
        sT_scr[...] = s0_ref[...]

    n_sub = chunk // sub
    causal = (lax.broadcasted_iota(jnp.int32, (chunk, chunk), 0)
              >= lax.broadcasted_iota(jnp.int32, (chunk, chunk), 1))
    tri = causal.astype(BF16)
    t_iota = lax.broadcasted_iota(jnp.int32, (sub, LANES), 0)

    def head_inputs(r0, g):
        rows, cs = pl.ds(r0, chunk), slice(g * HG_DIM, (g + 1) * HG_DIM)
        lb = lb_ref[:, cs]
        q = _silu(hr_q_ref[rows, cs].astype(F32))
        fg = lb + (1.0 - lb) * jax.nn.sigmoid(hf_ref[rows, cs])
        v = hr_i_ref[rows, cs].astype(F32)
        l_hi, l_mid, l_lo = _split3(jnp.log(fg))
        b = _dot(tri, l_hi) + _dot(tri, l_mid) + _dot(tri, l_lo)
        return q, 1.0 - fg, v, b

    def carry_state(g, q, kk, vb, b):
        sT = sT_scr[g]
        b_last = b[chunk - 1:chunk, :]
        qe = (q * jnp.exp(b)).astype(BF16)
        o_state = _dot_nt(qe, sT.astype(BF16))
        khat = (kk * jnp.exp(b_last - b)).astype(BF16)
        sT_scr[g] = sT * jnp.exp(b_last) + _dot_tn(vb, khat)
        return qe, o_state

    def finish(r0, g, o):
        rows, cs = pl.ds(r0, chunk), slice(g * HG_DIM, (g + 1) * HG_DIM)
        o = o * lax.rsqrt(jnp.mean(o * o, axis=-1, keepdims=True) + RMS_EPS) * ng_ref[:, cs]
        o = o * _silu(hr_g_ref[rows, cs].astype(F32))
        y_ref[rows, cs] = o.astype(y_ref.dtype)

    def factored_chunk(c, carry):
        rows = pl.ds(pl.multiple_of(c * chunk, chunk), chunk)
        hs = [slice(g * HG_DIM, (g + 1) * HG_DIM) for g in range(heads)]
        lb = lb_ref[...]
        q = _silu(hr_q_ref[rows, :].astype(F32))
        fg = lb + (1.0 - lb) * jax.nn.sigmoid(hf_ref[rows, :])
        kk = 1.0 - fg
        vb = hr_i_ref[rows, :]
        l_hi, l_mid, l_lo = _split3(jnp.log(fg))
        b = _dot(tri, l_hi) + _dot(tri, l_mid) + _dot(tri, l_lo)
        b_last = b[chunk - 1:chunk, :]
        qe = (q * jnp.exp(b)).astype(BF16)
        ke = (kk * jnp.exp(-b)).astype(BF16)
        khat = (kk * jnp.exp(b_last - b)).astype(BF16)
        keep = jnp.exp(b_last)
        s_old = [sT_scr[g] for g in range(heads)]
        a = [_dot_nt(qe[:, cs], ke[:, cs]) for cs in hs]
        o_state = [_dot_nt(qe[:, cs], s_old[g].astype(BF16)) for g, cs in enumerate(hs)]
        s_add = [_dot_tn(vb[:, cs], khat[:, cs]) for cs in hs]
        a = [jnp.where(causal, x, 0.0).astype(BF16) for x in a]
        o = [o_state[g] + _dot(a[g], vb[:, cs]) for g, cs in enumerate(hs)]
        for g, cs in enumerate(hs):
            sT_scr[g] = s_old[g] * keep[:, cs] + s_add[g]
        o = [x * lax.rsqrt(jnp.mean(x * x, axis=-1, keepdims=True) + RMS_EPS) for x in o]
        o = jnp.concatenate(o, axis=1) * ng_ref[...] * _silu(hr_g_ref[rows, :].astype(F32))
        y_ref[rows, :] = o.astype(y_ref.dtype)
        return carry

    def guarded_chunk(c, carry):
        r0 = pl.multiple_of(c * chunk, chunk)
        for g in range(heads):
            q, kk, v, b = head_inputs(r0, g)
            vb = v.astype(BF16)
            _, o_state = carry_state(g, q, kk, vb, b)
            rows = []
            for i in range(n_sub):
                s0, s1 = i * sub, (i + 1) * sub
                bs, qs, ks, vs = b[s0:s1], q[s0:s1], kk[s0:s1], v[s0:s1]
                o_i = o_state[s0:s1]
                if i > 0:
                    r_i = b[s0 - 1:s0, :]
                    qt = (qs * jnp.exp(bs - r_i)).astype(BF16)
                    kt = (kk[:s0] * jnp.exp(r_i - b[:s0])).astype(BF16)
                    a_off = _dot_nt(qt, kt)
                    o_i = o_i + _dot(a_off.astype(BF16), vb[:s0])
                for s in range(sub):
                    e = jnp.exp(jnp.where(t_iota >= s, bs - bs[s:s + 1], -jnp.inf))
                    a_col = jnp.sum(qs * e * ks[s:s + 1], axis=1, keepdims=True)
                    o_i = o_i + a_col * vs[s:s + 1]
                rows.append(o_i)
            finish(r0, g, jnp.concatenate(rows, axis=0) if n_sub > 1 else rows[0])
        return carry

    factoring_safe = jnp.max(-jnp.log(lb_ref[...])) * chunk <= HG_SAFE_EXPONENT

    @pl.when(factoring_safe)
    def _():
        lax.fori_loop(0, n_inner, factored_chunk, 0)

    @pl.when(jnp.logical_not(factoring_safe))
    def _():
        lax.fori_loop(0, n_inner, guarded_chunk, 0)

    @pl.when(c_idx == pl.num_programs(2) - 1)
    def _():
        sT_out_ref[0] = sT_scr[...]


def _hgrn(hf, hr, lb, ng, s0, *, batch, seq, chunk, sub, heads, n_inner):
    m = batch * seq
    rows = chunk * n_inner
    nc = seq // rows
    hw = heads * HG_DIM
    nhg = HG_HEADS // heads
    return pl.pallas_call(
        functools.partial(_hgrn_kernel, chunk=chunk, sub=sub, heads=heads, n_inner=n_inner),
        out_shape=(jax.ShapeDtypeStruct((m, HG_W), BF16),
                   jax.ShapeDtypeStruct((batch, HG_HEADS, HG_DIM, HG_DIM), F32)),
        grid=(batch, nhg, nc),
        in_specs=[
            pl.BlockSpec((rows, hw), lambda b, h, c: (b * nc + c, h)),
            pl.BlockSpec((rows, hw), lambda b, h, c: (b * nc + c, h)),
            pl.BlockSpec((rows, hw), lambda b, h, c: (b * nc + c, nhg + h)),
            pl.BlockSpec((rows, hw), lambda b, h, c: (b * nc + c, 2 * nhg + h)),
            pl.BlockSpec((1, hw), lambda b, h, c: (0, h)),
            pl.BlockSpec((1, hw), lambda b, h, c: (0, h)),
            pl.BlockSpec((heads, HG_DIM, HG_DIM), lambda b, h, c: (h, 0, 0)),
        ],
        out_specs=(
            pl.BlockSpec((rows, hw), lambda b, h, c: (b * nc + c, h)),
            pl.BlockSpec((1, heads, HG_DIM, HG_DIM), lambda b, h, c: (b, h, 0, 0)),
        ),
        scratch_shapes=[pltpu.VMEM((heads, HG_DIM, HG_DIM), F32)],
        compiler_params=pltpu.CompilerParams(
            dimension_semantics=("parallel", "parallel", "arbitrary"), vmem_limit_bytes=VMEM_LIMIT),
        name="hgrn2",
    )(hf, hr, hr, hr, lb, ng, s0)


def _merge_kernel(ya_ref, yh_ref, gts_a_ref, gts_h_ref, h_ref, wa_ref, wh_ref, wo_ref, g_ref, b_ref,
                  o_ref, acc_scr):
    j = pl.program_id(1)

    @pl.when(j == 0)
    def _():
        acc_scr[...] = jnp.zeros_like(acc_scr)

    ga = jax.nn.sigmoid(gts_a_ref[...].astype(F32))
    gh = jax.nn.sigmoid(gts_h_ref[...].astype(F32))
    merged = ga * _dot(ya_ref[...], wa_ref[...]) + gh * _dot(yh_ref[...], wh_ref[...])
    acc_scr[...] += _dot(merged.astype(BF16), wo_ref[...])

    @pl.when(j == pl.num_programs(1) - 1)
    def _():
        o_ref[...] = _layer_norm(ALPHA * h_ref[...] + acc_scr[...], g_ref[...], b_ref[...])


def _merge(ya, yh, gates, h1, wa, wh, wo, g, b, *, tm, tn):
    m = ya.shape[0]
    nj = D_MODEL // tn
    return pl.pallas_call(
        _merge_kernel,
        out_shape=jax.ShapeDtypeStruct((m, D_MODEL), F32),
        grid=(m // tm, nj),
        in_specs=[
            pl.BlockSpec((tm, ATT_Q_W), lambda i, j: (i, 0)),
            pl.BlockSpec((tm, HG_W), lambda i, j: (i, 0)),
            pl.BlockSpec((tm, tn), lambda i, j: (i, j)),
            pl.BlockSpec((tm, tn), lambda i, j: (i, nj + j)),
            pl.BlockSpec((tm, D_MODEL), lambda i, j: (i, 0)),
            pl.BlockSpec((ATT_Q_W, tn), lambda i, j: (0, j)),
            pl.BlockSpec((HG_W, tn), lambda i, j: (0, j)),
            pl.BlockSpec((tn, D_MODEL), lambda i, j: (j, 0)),
            pl.BlockSpec((1, D_MODEL), lambda i, j: (0, 0)),
            pl.BlockSpec((1, D_MODEL), lambda i, j: (0, 0)),
        ],
        out_specs=pl.BlockSpec((tm, D_MODEL), lambda i, j: (i, 0)),
        scratch_shapes=[pltpu.VMEM((tm, D_MODEL), F32)],
        compiler_params=pltpu.CompilerParams(
            dimension_semantics=("parallel", "arbitrary"), vmem_limit_bytes=VMEM_LIMIT),
        name="merge_ln",
    )(ya, yh, gates, gates, h1, wa, wh, wo, g, b)


def kernel(x, meta, ffn1_w_gate, ffn1_w_up, ffn1_w_down, ln1_g, ln1_b, w_in, idx_k_norm_g, idx_k_norm_b,
           hg_lb_logits, hg_norm_g, w_branch_att, w_branch_hg, w_out, ln2_g, ln2_b,
           ffn2_w_gate, ffn2_w_up, ffn2_w_down, ln3_g, ln3_b):
    batch, seq, _ = x.shape
    m = batch * seq
    xr = x.reshape(m, D_MODEL)
    bf = lambda w: w.astype(BF16)
    row = lambda v: v.reshape(1, -1)

    splits = (ATT_Q_W, ATT_KV_W, ATT_KV_W, IDX_Q_W, IDX_HEAD_DIM, N_IDX_HEADS, HG_W, HG_W, HG_W, HG_W,
              D_MODEL, D_MODEL)
    offs = [0]
    for s in splits:
        offs.append(offs[-1] + s)
    w = bf(w_in[0])
    seg = lambda i: w[:, offs[i]:offs[i + 1]]
    w_att = jnp.concatenate([seg(0), seg(3), seg(1), seg(2)], axis=1)
    w_idx = jnp.pad(jnp.concatenate([seg(4), seg(5)], axis=1),
                    ((0, 0), (0, LANES - IDX_HEAD_DIM - N_IDX_HEADS)))
    w_hf = seg(7)
    w_hr = jnp.concatenate([seg(6), seg(8), seg(9)], axis=1)
    w_gates = jnp.concatenate([seg(10), seg(11)], axis=1)
    idx_g = jnp.pad(idx_k_norm_g[0], (0, LANES - IDX_HEAD_DIM)).reshape(1, LANES)
    idx_b = jnp.pad(idx_k_norm_b[0], (0, LANES - IDX_HEAD_DIM)).reshape(1, LANES)
    lb = jnp.cumsum(jax.nn.softmax(hg_lb_logits.astype(F32), axis=0), axis=0)[0].reshape(1, HG_W)
    ng = hg_norm_g[0].reshape(1, HG_W)

    f1 = (bf(ffn1_w_gate[0]), bf(ffn1_w_up[0]), bf(ffn1_w_down[0]), row(ln1_g[0]), row(ln1_b[0]))
    f2 = (bf(ffn2_w_gate[0]), bf(ffn2_w_up[0]), bf(ffn2_w_down[0]), row(ln3_g[0]), row(ln3_b[0]))
    tn = 512
    q_tiles = ATT_Q_W // tn

    (hm_b,) = _ffn_ln(meta.astype(F32), *f1, tm=N_META, tf=512, out_dtypes=(BF16,))
    pm_att = _proj(hm_b, w_att, BF16, tm=N_META, tn=tn, scaled_tiles=q_tiles, scale=ATT_Q_SCALE)
    pm_hf = _proj(hm_b, w_hf, F32, tm=N_META, tn=tn)
    pm_hr = _proj(hm_b, w_hr, BF16, tm=N_META, tn=tn)
    kv0 = ATT_Q_W + IDX_Q_W
    km = jnp.pad(pm_att[:, kv0:kv0 + ATT_KV_W], ((0, LANES - N_META), (0, 0)))
    vm = jnp.pad(pm_att[:, kv0 + ATT_KV_W:], ((0, LANES - N_META), (0, 0)))
    s_zero = jnp.zeros((HG_HEADS, HG_DIM, HG_DIM), F32)
    _, s_meta = _hgrn(pm_hf, pm_hr, lb, ng, s_zero, batch=1, seq=N_META, chunk=N_META, sub=N_META,
                      heads=HG_HEADS, n_inner=1)

    h1, h1b = _ffn_ln(xr, *f1, tm=512, tf=512, out_dtypes=(F32, BF16))
    p_att = _proj(h1b, w_att, BF16, tm=1024, tn=tn, scaled_tiles=q_tiles, scale=ATT_Q_SCALE)
    p_idx = _proj_idx(h1b, w_idx, idx_g, idx_b, tm=1024)
    p_hf = _proj(h1b, w_hf, F32, tm=1024, tn=tn)
    p_hr = _proj(h1b, w_hr, BF16, tm=1024, tn=tn)
    p_gates = _proj(h1b, w_gates, BF16, tm=1024, tn=tn)

    ikn = p_idx[:, :IDX_HEAD_DIM].astype(BF16)
    ik_lo = jnp.pad(ikn, ((0, 0), (0, LANES - IDX_HEAD_DIM)))
    ik_hi = jnp.pad(ikn, ((0, 0), (LANES - IDX_HEAD_DIM, 0)))
    iw_t = p_idx[:, IDX_HEAD_DIM:IDX_HEAD_DIM + N_IDX_HEADS].T
    y_att = _attention(p_att, iw_t, ik_lo, ik_hi, km, vm, batch=batch, seq=seq, tq=256, kc=512)
    y_hg, _ = _hgrn(p_hf, p_hr, lb, ng, s_meta[0], batch=batch, seq=seq, chunk=64, sub=16,
                    heads=HG_HEADS, n_inner=4)

    h2 = _merge(y_att, y_hg, p_gates, h1, bf(w_branch_att[0]), bf(w_branch_hg[0]), bf(w_out[0]),
                row(ln2_g[0]), row(ln2_b[0]), tm=512, tn=512)
    (out,) = _ffn_ln(h2, *f2, tm=512, tf=512, out_dtypes=(F32,))
    return out.reshape(batch, seq, D_MODEL)
```

```python
import functools
import math

import jax
import jax.numpy as jnp
import numpy as np
from jax import lax
from jax.experimental import pallas as pl
from jax.experimental.pallas import tpu as pltpu

D_MODEL = 2048
N_META = 16
N_ATT_HEADS = 8
N_KV_HEADS = 2
ATT_GROUP = N_ATT_HEADS // N_KV_HEADS
ATT_HEAD_DIM = 128
N_IDX_HEADS = 16
IDX_HEAD_DIM = 64
TOPK = 256
HG_HEADS = 8
HG_DIM = 128
D_FF = 5632
LN_EPS = 1e-5
RMS_EPS = 1e-6
ALPHA = 2.0 ** 0.25

ATT_Q_W = N_ATT_HEADS * ATT_HEAD_DIM
ATT_KV_W = N_KV_HEADS * ATT_HEAD_DIM
IDX_Q_W = N_IDX_HEADS * IDX_HEAD_DIM
HG_W = HG_HEADS * HG_DIM

LANES = 128
SUBLANES = 8
VMEM_LIMIT = 56 * 1024 * 1024

F32 = jnp.float32
BF16 = jnp.bfloat16
INT_MIN = -2 ** 31
F32_LOWEST = float(np.finfo(np.float32).min)
LOG2E = math.log2(math.e)
ATT_Q_SCALE = ATT_HEAD_DIM ** -0.5 * LOG2E
POS_SHIFT = 6
POS_SPLIT = 1 << POS_SHIFT


def _dot(a, b):
    return jnp.dot(a, b, preferred_element_type=F32)


def _dot_nt(a, b):
    return lax.dot_general(a, b, (((1,), (1,)), ((), ())), preferred_element_type=F32)


def _dot_tn(a, b):
    return lax.dot_general(a, b, (((0,), (0,)), ((), ())), preferred_element_type=F32)


def _layer_norm(y, g, b):
    mu = jnp.mean(y, axis=-1, keepdims=True)
    d = y - mu
    var = jnp.mean(d * d, axis=-1, keepdims=True)
    return d * lax.rsqrt(var + LN_EPS) * g + b


def _silu(x):
    return x * jax.nn.sigmoid(x)


def _bf16_parts(c, n=3):
    parts = []
    for _ in range(n):
        p = float(np.asarray(c, np.float32).astype(jnp.bfloat16).astype(np.float32))
        parts.append(p)
        c = c - p
    return parts


def _ffn_ln_kernel(x_ref, wg_ref, wu_ref, wd_ref, g_ref, b_ref, *rest):
    *out_refs, xb_scr, acc_scr = rest
    j = pl.program_id(1)

    @pl.when(j == 0)
    def _():
        xb_scr[...] = x_ref[...].astype(BF16)
        acc_scr[...] = jnp.zeros_like(acc_scr)

    xb = xb_scr[...]
    gate = _dot(xb, wg_ref[...])
    up = _dot(xb, wu_ref[...])
    a = (_silu(gate) * up).astype(BF16)
    acc_scr[...] += _dot(a, wd_ref[...])

    @pl.when(j == pl.num_programs(1) - 1)
    def _():
        y = ALPHA * x_ref[...] + 0.5 * acc_scr[...]
        out = _layer_norm(y, g_ref[...], b_ref[...])
        for o_ref in out_refs:
            o_ref[...] = out.astype(o_ref.dtype)


def _ffn_ln(x, wg, wu, wd, g, b, *, tm, tf, out_dtypes):
    m = x.shape[0]
    grid = (m // tm, D_FF // tf)
    return pl.pallas_call(
        _ffn_ln_kernel,
        out_shape=tuple(jax.ShapeDtypeStruct((m, D_MODEL), dt) for dt in out_dtypes),
        grid=grid,
        in_specs=[
            pl.BlockSpec((tm, D_MODEL), lambda i, j: (i, 0)),
            pl.BlockSpec((D_MODEL, tf), lambda i, j: (0, j)),
            pl.BlockSpec((D_MODEL, tf), lambda i, j: (0, j)),
            pl.BlockSpec((tf, D_MODEL), lambda i, j: (j, 0)),
            pl.BlockSpec((1, D_MODEL), lambda i, j: (0, 0)),
            pl.BlockSpec((1, D_MODEL), lambda i, j: (0, 0)),
        ],
        out_specs=tuple(pl.BlockSpec((tm, D_MODEL), lambda i, j: (i, 0)) for _ in out_dtypes),
        scratch_shapes=[pltpu.VMEM((tm, D_MODEL), BF16), pltpu.VMEM((tm, D_MODEL), F32)],
        compiler_params=pltpu.CompilerParams(
            dimension_semantics=("parallel", "arbitrary"), vmem_limit_bytes=VMEM_LIMIT),
        name="ffn_ln",
    )(x, wg, wu, wd, g, b)


IN_SPLITS = (ATT_Q_W, ATT_KV_W, ATT_KV_W, IDX_Q_W, IDX_HEAD_DIM, N_IDX_HEADS, HG_W, HG_W, HG_W, HG_W,
             D_MODEL, D_MODEL)
IN_OFFS = tuple(int(v) for v in np.cumsum((0,) + IN_SPLITS))
PROJ_TN = 512
PROJ_LAYOUT = (("aq", IN_OFFS[0], ATT_Q_W), ("iq", IN_OFFS[3], IDX_Q_W), ("hq", IN_OFFS[6], HG_W),
               ("hf", IN_OFFS[7], HG_W), ("hi", IN_OFFS[8], HG_W), ("hg", IN_OFFS[9], HG_W),
               ("ga", IN_OFFS[10], D_MODEL), ("gb", IN_OFFS[11], D_MODEL), ("kv", IN_OFFS[1], 2 * ATT_KV_W))
PROJ_COL = {}
PROJ_TILE_SRC = []
for _name, _src, _width in PROJ_LAYOUT:
    PROJ_COL[_name] = len(PROJ_TILE_SRC) * PROJ_TN
    PROJ_TILE_SRC += [_src + t * PROJ_TN for t in range(_width // PROJ_TN)]
PROJ_W = len(PROJ_TILE_SRC) * PROJ_TN
PROJ_SRC_ALIGN = 16
assert all(s % PROJ_SRC_ALIGN == 0 for s in PROJ_TILE_SRC)


def _proj_all_kernel(src_ref, x_ref, wt_ref, o_ref, *, scaled_tiles, scale):
    acc = _dot_nt(x_ref[...], wt_ref[...].astype(BF16))
    acc = acc * jnp.where(pl.program_id(1) < scaled_tiles, scale, 1.0)
    o_ref[...] = acc.astype(o_ref.dtype)


def _idx_kernel(x_ref, w_ref, g_ref, b_ref, o_ref):
    p = _dot(x_ref[...], w_ref[...])
    lane = lax.broadcasted_iota(jnp.int32, p.shape, 1)
    is_k = lane < IDX_HEAD_DIM
    mu = jnp.sum(jnp.where(is_k, p, 0.0), axis=-1, keepdims=True) * (1.0 / IDX_HEAD_DIM)
    d = jnp.where(is_k, p - mu, 0.0)
    var = jnp.sum(d * d, axis=-1, keepdims=True) * (1.0 / IDX_HEAD_DIM)
    kn = d * lax.rsqrt(var + LN_EPS) * g_ref[...] + b_ref[...]
    w_scale = (N_IDX_HEADS ** -0.5) * (IDX_HEAD_DIM ** -0.5)
    o_ref[...] = jnp.where(is_k, kn, p * w_scale)


def _proj_all(xb, w_in_t, *, tm):
    m = xb.shape[0]
    n_tiles = len(PROJ_TILE_SRC)
    return pl.pallas_call(
        functools.partial(_proj_all_kernel, scaled_tiles=ATT_Q_W // PROJ_TN, scale=ATT_Q_SCALE),
        out_shape=jax.ShapeDtypeStruct((m, PROJ_W), BF16),
        grid_spec=pltpu.PrefetchScalarGridSpec(
            num_scalar_prefetch=1,
            grid=(m // tm, n_tiles),
            in_specs=[pl.BlockSpec((tm, D_MODEL), lambda i, j, src: (i, 0)),
                      pl.BlockSpec((pl.Element(PROJ_TN), pl.Element(D_MODEL)),
                                   lambda i, j, src: (src[j] * PROJ_SRC_ALIGN, 0))],
            out_specs=pl.BlockSpec((tm, PROJ_TN), lambda i, j, src: (i, j)),
        ),
        compiler_params=pltpu.CompilerParams(
            dimension_semantics=("parallel", "arbitrary"), vmem_limit_bytes=VMEM_LIMIT),
        name="proj_all",
    )(jnp.asarray([s // PROJ_SRC_ALIGN for s in PROJ_TILE_SRC], jnp.int32), xb, w_in_t)


def _proj_idx(xb, w, g, b, *, tm):
    m = xb.shape[0]
    return pl.pallas_call(
        _idx_kernel,
        out_shape=jax.ShapeDtypeStruct((m, LANES), F32),
        grid=(m // tm,),
        in_specs=[pl.BlockSpec((tm, D_MODEL), lambda i: (i, 0)),
                  pl.BlockSpec((D_MODEL, LANES), lambda i: (0, 0)),
                  pl.BlockSpec((1, LANES), lambda i: (0, 0)),
                  pl.BlockSpec((1, LANES), lambda i: (0, 0))],
        out_specs=pl.BlockSpec((tm, LANES), lambda i: (i, 0)),
        compiler_params=pltpu.CompilerParams(
            dimension_semantics=("parallel",), vmem_limit_bytes=VMEM_LIMIT),
        name="proj_idx",
    )(xb, w, g, b)


ATT_ROW_TILE = 32


def _key_to_f32(key):
    return pltpu.bitcast(key ^ ((key >> 31) & 0x7FFFFFFF), F32)


def _attn_kernel(aq_ref, iq_ref, iwt_ref, iklo_ref, ikhi_ref, k_ref, v_ref, km_ref, vm_ref,
                 o_ref,
                 score_scr, mb_scr, kaug_scr, vaug_scr, qaug_scr, s0_scr, s1_scr, p_scr,
                 macc_scr, m_scr, acc_scr, *, tq, kc, seq):
    qi = pl.program_id(1)
    n_kc = ((qi + 1) * tq + kc - 1) // kc
    kf = float(TOPK)

    @pl.when(qi == 0)
    def _():
        r = lax.broadcasted_iota(jnp.int32, (LANES + seq, LANES), 0)
        lane = lax.broadcasted_iota(jnp.int32, (LANES + seq, LANES), 1)
        pos = jnp.where(r < LANES, r, r - LANES + N_META)
        feat = jnp.where(lane < 3, pos >> POS_SHIFT, jnp.where(lane < 6, pos & (POS_SPLIT - 1), 0))
        feat = feat.astype(F32).astype(BF16)
        ones = jnp.ones((LANES + seq, LANES), BF16)
        for kvh in range(N_KV_HEADS):
            cs = slice(kvh * LANES, (kvh + 1) * LANES)
            kaug_scr[kvh, 0:LANES, 0:LANES] = km_ref[:, cs]
            kaug_scr[kvh, LANES:, 0:LANES] = k_ref[:, cs]
            kaug_scr[kvh, :, LANES:] = feat
            vaug_scr[kvh, 0:LANES, 0:LANES] = vm_ref[:, cs]
            vaug_scr[kvh, LANES:, 0:LANES] = v_ref[:, cs]
            vaug_scr[kvh, :, LANES:] = ones

    qcol = qi * tq + lax.broadcasted_iota(jnp.int32, (1, tq), 1)
    iwt = iwt_ref[...]

    def score_body(j, carry):
        off = pl.multiple_of(j * kc, kc)
        klo = iklo_ref[pl.ds(off, kc), :]
        khi = ikhi_ref[pl.ds(off, kc), :]
        acc = jnp.zeros((kc, tq), F32)
        for p in range(N_IDX_HEADS // 2):
            q2 = iq_ref[:, p * LANES:(p + 1) * LANES]
            acc = acc + jnp.maximum(_dot_nt(klo, q2), 0.0) * iwt[2 * p:2 * p + 1, :]
            acc = acc + jnp.maximum(_dot_nt(khi, q2), 0.0) * iwt[2 * p + 1:2 * p + 2, :]
        krow = off + lax.broadcasted_iota(jnp.int32, (kc, tq), 0)
        score_scr[pl.ds(off, kc), :] = jnp.where(krow <= qcol, acc, -jnp.inf)
        return carry

    lax.fori_loop(0, n_kc, score_body, 0)

    n_acc = 8
    acc_rows = n_acc * SUBLANES

    def count_ge(cand):
        def body(j, acc):
            off = pl.multiple_of(j * kc, kc)
            w = jnp.where(score_scr[pl.ds(off, kc), :] >= cand, 1.0, 0.0)
            return acc + jnp.sum(w.reshape(kc // acc_rows, acc_rows, tq), axis=0)

        acc = lax.fori_loop(0, n_kc, body, jnp.zeros((acc_rows, tq), F32))
        return jnp.sum(acc, axis=0, keepdims=True)

    c0 = count_ge(jnp.zeros((1, tq), F32))
    ok0 = c0 >= kf
    thr0 = jnp.where(ok0, 0, INT_MIN).astype(jnp.int32)
    cnt0 = jnp.where(ok0, c0, 0.0)

    def bit_body(i, carry):
        thr, cnt = carry
        cand = thr | jnp.left_shift(jnp.int32(1), 30 - i)
        c = count_ge(_key_to_f32(cand))
        ok = c >= kf
        return jnp.where(ok, cand, thr), jnp.where(ok, c, cnt)

    thr, cnt = lax.fori_loop(0, 31, bit_body, (thr0, cnt0))
    t_f = jnp.where(thr == INT_MIN, F32_LOWEST, _key_to_f32(thr))

    @pl.when(jnp.max(cnt) > kf)
    def _():
        def gt_body(j, acc):
            off = pl.multiple_of(j * kc, kc)
            w = jnp.where(score_scr[pl.ds(off, kc), :] > t_f, 1.0, 0.0)
            return acc + jnp.sum(w, axis=0, keepdims=True)

        need = kf - lax.fori_loop(0, n_kc, gt_body, jnp.zeros((1, tq), F32))
        lower = (lax.broadcasted_iota(jnp.int32, (kc, kc), 0)
                 >= lax.broadcasted_iota(jnp.int32, (kc, kc), 1)).astype(BF16)

        def tie_body(j, before):
            off = pl.multiple_of(j * kc, kc)
            sc = score_scr[pl.ds(off, kc), :]
            eq = sc == t_f
            eqf = jnp.where(eq, 1.0, 0.0)
            rank = before + _dot(lower, eqf.astype(BF16))
            score_scr[pl.ds(off, kc), :] = jnp.where(eq & (rank > need), -jnp.inf, sc)
            return before + jnp.sum(eqf, axis=0, keepdims=True)

        lax.fori_loop(0, n_kc, tie_body, jnp.zeros((1, tq), F32))

    eye = (lax.broadcasted_iota(jnp.int32, (tq, tq), 0)
           == lax.broadcasted_iota(jnp.int32, (tq, tq), 1)).astype(BF16)

    def mask_body(j, carry):
        off = pl.multiple_of(j * kc, kc)
        sel_t = jnp.where(score_scr[pl.ds(off, kc), :] >= t_f, 1.0, 0.0).astype(BF16)
        sel = _dot_nt(eye, sel_t)
        mb_scr[:, pl.ds(off, kc)] = jnp.where(sel > 0.5, 0.0, -jnp.inf)
        return carry

    lax.fori_loop(0, n_kc, mask_body, 0)

    rt = ATT_ROW_TILE
    lane_q = lax.broadcasted_iota(jnp.int32, (1, LANES), 1)
    meta_mask = jnp.where(lax.broadcasted_iota(jnp.int32, (rt, LANES), 1) < N_META, 0.0, -jnp.inf)

    s_slots = (s0_scr, s1_scr)

    def logits(kvh, krow0, width, slot):
        s_slots[slot][:, 0:width] = _dot_nt(qaug_scr[kvh], kaug_scr[kvh, pl.ds(krow0, width), :])

    def max_tiles(kvh, slot, width, mask_fn):
        s_ref = s_slots[slot]
        for r0 in range(0, tq, rt):
            mb = mask_fn(r0)
            for g in range(ATT_GROUP):
                rows = slice(g * tq + r0, g * tq + r0 + rt)
                mx = s_ref[rows, 0:LANES] + mb[:, 0:LANES]
                for c in range(1, width // LANES):
                    cols = slice(c * LANES, (c + 1) * LANES)
                    mx = jnp.maximum(mx, s_ref[rows, cols] + mb[:, cols])
                macc_scr[rows, :] = jnp.maximum(macc_scr[rows, :], mx)

    def exp_tiles(kvh, slot, width, mask_fn):
        s_ref = s_slots[slot]
        for r0 in range(0, tq, rt):
            mb = mask_fn(r0)
            for g in range(ATT_GROUP):
                rows = slice(g * tq + r0, g * tq + r0 + rt)
                m_row = m_scr[rows, :]
                for c in range(width // LANES):
                    cols = slice(c * LANES, (c + 1) * LANES)
                    p_scr[rows, cols] = jnp.exp2(s_ref[rows, cols] + mb[:, cols] - m_row).astype(BF16)

    def add_pv(kvh, krow0, width):
        vblk = vaug_scr[kvh, pl.ds(krow0, width), :]
        half = ATT_GROUP * tq // 2
        for r0 in (0, half):
            acc_scr[kvh, r0:r0 + half, :] += _dot(p_scr[r0:r0 + half, 0:width], vblk)

    def run_pass(kvh, tiles_fn, with_pv):
        def krow(j):
            return pl.multiple_of(LANES + j * kc, LANES)

        def step(j, slot, lookahead):
            off = pl.multiple_of(j * kc, kc)
            if lookahead:
                logits(kvh, krow(j + 1), kc, 1 - slot)
            tiles_fn(kvh, slot, kc, lambda r0: mb_scr[r0:r0 + rt, pl.ds(off, kc)])
            if with_pv:
                add_pv(kvh, krow(j), kc)

        logits(kvh, 0, LANES, 0)
        tiles_fn(kvh, 0, LANES, lambda r0: meta_mask)
        if with_pv:
            add_pv(kvh, 0, LANES)

            def serial_body(j, carry):
                off = pl.multiple_of(j * kc, kc)
                logits(kvh, krow(j), kc, 0)
                tiles_fn(kvh, 0, kc, lambda r0: mb_scr[r0:r0 + rt, pl.ds(off, kc)])
                add_pv(kvh, krow(j), kc)
                return carry

            lax.fori_loop(0, n_kc, serial_body, 0)
            return
        logits(kvh, krow(0), kc, 0)
        n_pairs = (n_kc - 1) // 2

        def pair_body(i, carry):
            step(2 * i, 0, True)
            step(2 * i + 1, 1, True)
            return carry

        lax.fori_loop(0, n_pairs, pair_body, 0)
        j0 = 2 * n_pairs
        two_left = n_kc - j0 == 2

        @pl.when(two_left)
        def _():
            step(j0, 0, True)
            step(j0 + 1, 1, False)

        @pl.when(jnp.logical_not(two_left))
        def _():
            step(j0, 0, False)

    for h in range(N_ATT_HEADS):
        kvh, g = divmod(h, ATT_GROUP)
        parts = _bf16_parts(2.0 ** (-8.0 * (h + 1) / N_ATT_HEADS) * LOG2E)
        vals = [POS_SPLIT * p for p in parts] + parts
        qfeat = jnp.zeros((1, LANES), F32)
        for i, val in enumerate(vals):
            qfeat = jnp.where(lane_q == i, val, qfeat)
        qaug_scr[kvh, g * tq:(g + 1) * tq, 0:LANES] = aq_ref[:, h * LANES:(h + 1) * LANES]
        qaug_scr[kvh, g * tq:(g + 1) * tq, LANES:] = jnp.broadcast_to(qfeat, (tq, LANES)).astype(BF16)

    for kvh in range(N_KV_HEADS):
        macc_scr[...] = jnp.full_like(macc_scr, -jnp.inf)
        run_pass(kvh, max_tiles, with_pv=False)
        m_scr[...] = jnp.broadcast_to(jnp.max(macc_scr[...], axis=1, keepdims=True), m_scr.shape)
        acc_scr[kvh] = jnp.zeros(acc_scr.shape[1:], F32)
        run_pass(kvh, exp_tiles, with_pv=True)

    for h in range(N_ATT_HEADS):
        kvh, g = divmod(h, ATT_GROUP)
        acc = acc_scr[kvh, g * tq:(g + 1) * tq, :]
        o_ref[:, h * LANES:(h + 1) * LANES] = (acc[:, 0:LANES] / acc[:, LANES:LANES + 1]).astype(o_ref.dtype)


def _attention(p_att, iw_t, ik_lo, ik_hi, km, vm, *, batch, seq, tq, kc):
    m = batch * seq
    nq = seq // tq
    qcol, icol, kcol = PROJ_COL["aq"] // ATT_Q_W, PROJ_COL["iq"] // IDX_Q_W, PROJ_COL["kv"] // ATT_KV_W
    rows4 = ATT_GROUP * tq
    return pl.pallas_call(
        functools.partial(_attn_kernel, tq=tq, kc=kc, seq=seq),
        out_shape=jax.ShapeDtypeStruct((m, ATT_Q_W), BF16),
        grid=(batch, nq),
        in_specs=[
            pl.BlockSpec((tq, ATT_Q_W), lambda b, q: (b * nq + q, qcol)),
            pl.BlockSpec((tq, IDX_Q_W), lambda b, q: (b * nq + q, icol)),
            pl.BlockSpec((N_IDX_HEADS, tq), lambda b, q: (0, b * nq + q)),
            pl.BlockSpec((seq, LANES), lambda b, q: (b, 0)),
            pl.BlockSpec((seq, LANES), lambda b, q: (b, 0)),
            pl.BlockSpec((seq, ATT_KV_W), lambda b, q: (b, kcol)),
            pl.BlockSpec((seq, ATT_KV_W), lambda b, q: (b, kcol + 1)),
            pl.BlockSpec((LANES, ATT_KV_W), lambda b, q: (0, 0)),
            pl.BlockSpec((LANES, ATT_KV_W), lambda b, q: (0, 0)),
        ],
        out_specs=pl.BlockSpec((tq, ATT_Q_W), lambda b, q: (b * nq + q, 0)),
        scratch_shapes=[
            pltpu.VMEM((seq, tq), F32),
            pltpu.VMEM((tq, seq), F32),
            pltpu.VMEM((N_KV_HEADS, LANES + seq, 2 * LANES), BF16),
            pltpu.VMEM((N_KV_HEADS, LANES + seq, 2 * LANES), BF16),
            pltpu.VMEM((N_KV_HEADS, rows4, 2 * LANES), BF16),
            pltpu.VMEM((rows4, kc), F32),
            pltpu.VMEM((rows4, kc), F32),
            pltpu.VMEM((rows4, kc), BF16),
            pltpu.VMEM((rows4, LANES), F32),
            pltpu.VMEM((rows4, LANES), F32),
            pltpu.VMEM((N_KV_HEADS, rows4, 2 * LANES), F32),
        ],
        compiler_params=pltpu.CompilerParams(
            dimension_semantics=("parallel", "arbitrary"), vmem_limit_bytes=VMEM_LIMIT),
        name="dsa_attention",
    )(p_att, p_att, iw_t, ik_lo, ik_hi, p_att, p_att, km, vm)


def _split3(x):
    hi = x.astype(BF16)
    r = x - hi.astype(F32)
    mid = r.astype(BF16)
    lo = (r - mid.astype(F32)).astype(BF16)
    return hi, mid, lo


HG_SAFE_EXPONENT = 80.0


def _hgrn_kernel(hf_ref, hr_q_ref, hr_i_ref, hr_g_ref, lb_ref, ng_ref, s0_ref,
                 y_ref, sT_out_ref, sT_scr, *, chunk, sub, heads, n_inner):
    c_idx = pl.program_id(2)

    @pl.when(c_idx == 0)
    def _():
        sT_scr[...] = s0_ref[...]

    n_sub = chunk // sub
    causal = (lax.broadcasted_iota(jnp.int32, (chunk, chunk), 0)
              >= lax.broadcasted_iota(jnp.int32, (chunk, chunk), 1))
    tri = causal.astype(BF16)
    t_iota = lax.broadcasted_iota(jnp.int32, (sub, LANES), 0)

    def head_inputs(r0, g):
        rows, cs = pl.ds(r0, chunk), slice(g * HG_DIM, (g + 1) * HG_DIM)
        lb = lb_ref[:, cs]
        q = _silu(hr_q_ref[rows, cs].astype(F32))
        fg = lb + (1.0 - lb) * jax.nn.sigmoid(hf_ref[rows, cs].astype(F32))
        v = hr_i_ref[rows, cs].astype(F32)
        l_hi, l_mid, l_lo = _split3(jnp.log(fg))
        b = _dot(tri, l_hi) + _dot(tri, l_mid) + _dot(tri, l_lo)
        return q, 1.0 - fg, v, b

    def carry_state(g, q, kk, vb, b):
        sT = sT_scr[g]
        b_last = b[chunk - 1:chunk, :]
        qe = (q * jnp.exp(b)).astype(BF16)
        o_state = _dot_nt(qe, sT.astype(BF16))
        khat = (kk * jnp.exp(b_last - b)).astype(BF16)
        sT_scr[g] = sT * jnp.exp(b_last) + _dot_tn(vb, khat)
        return qe, o_state

    def finish(r0, g, o):
        rows, cs = pl.ds(r0, chunk), slice(g * HG_DIM, (g + 1) * HG_DIM)
        o = o * lax.rsqrt(jnp.mean(o * o, axis=-1, keepdims=True) + RMS_EPS) * ng_ref[:, cs]
        o = o * _silu(hr_g_ref[rows, cs].astype(F32))
        y_ref[rows, cs] = o.astype(y_ref.dtype)

    def factored_chunk(c, carry):
        rows = pl.ds(pl.multiple_of(c * chunk, chunk), chunk)
        hs = [slice(g * HG_DIM, (g + 1) * HG_DIM) for g in range(heads)]
        lb = lb_ref[...]
        q = _silu(hr_q_ref[rows, :].astype(F32))
        fg = lb + (1.0 - lb) * jax.nn.sigmoid(hf_ref[rows, :].astype(F32))
        kk = 1.0 - fg
        vb = hr_i_ref[rows, :]
        l_hi, l_mid, l_lo = _split3(jnp.log(fg))
        b = _dot(tri, l_hi) + _dot(tri, l_mid) + _dot(tri, l_lo)
        b_last = b[chunk - 1:chunk, :]
        qe = (q * jnp.exp(b)).astype(BF16)
        ke = (kk * jnp.exp(-b)).astype(BF16)
        khat = (kk * jnp.exp(b_last - b)).astype(BF16)
        keep = jnp.exp(b_last)
        s_old = [sT_scr[g] for g in range(heads)]
        a = [_dot_nt(qe[:, cs], ke[:, cs]) for cs in hs]
        o_state = [_dot_nt(qe[:, cs], s_old[g].astype(BF16)) for g, cs in enumerate(hs)]
        s_add = [_dot_tn(vb[:, cs], khat[:, cs]) for cs in hs]
        a = [jnp.where(causal, x, 0.0).astype(BF16) for x in a]
        o = [o_state[g] + _dot(a[g], vb[:, cs]) for g, cs in enumerate(hs)]
        for g, cs in enumerate(hs):
            sT_scr[g] = s_old[g] * keep[:, cs] + s_add[g]
        o = [x * lax.rsqrt(jnp.mean(x * x, axis=-1, keepdims=True) + RMS_EPS) for x in o]
        o = jnp.concatenate(o, axis=1) * ng_ref[...] * _silu(hr_g_ref[rows, :].astype(F32))
        y_ref[rows, :] = o.astype(y_ref.dtype)
        return carry

    def guarded_chunk(c, carry):
        r0 = pl.multiple_of(c * chunk, chunk)
        for g in range(heads):
            q, kk, v, b = head_inputs(r0, g)
            vb = v.astype(BF16)
            _, o_state = carry_state(g, q, kk, vb, b)
            rows = []
            for i in range(n_sub):
                s0, s1 = i * sub, (i + 1) * sub
                bs, qs, ks, vs = b[s0:s1], q[s0:s1], kk[s0:s1], v[s0:s1]
                o_i = o_state[s0:s1]
                if i > 0:
                    r_i = b[s0 - 1:s0, :]
                    qt = (qs * jnp.exp(bs - r_i)).astype(BF16)
                    kt = (kk[:s0] * jnp.exp(r_i - b[:s0])).astype(BF16)
                    a_off = _dot_nt(qt, kt)
                    o_i = o_i + _dot(a_off.astype(BF16), vb[:s0])
                for s in range(sub):
                    e = jnp.exp(jnp.where(t_iota >= s, bs - bs[s:s + 1], -jnp.inf))
                    a_col = jnp.sum(qs * e * ks[s:s + 1], axis=1, keepdims=True)
                    o_i = o_i + a_col * vs[s:s + 1]
                rows.append(o_i)
            finish(r0, g, jnp.concatenate(rows, axis=0) if n_sub > 1 else rows[0])
        return carry

    factoring_safe = jnp.max(-jnp.log(lb_ref[...])) * chunk <= HG_SAFE_EXPONENT

    @pl.when(factoring_safe)
    def _():
        lax.fori_loop(0, n_inner, factored_chunk, 0)

    @pl.when(jnp.logical_not(factoring_safe))
    def _():
        lax.fori_loop(0, n_inner, guarded_chunk, 0)

    @pl.when(c_idx == pl.num_programs(2) - 1)
    def _():
        sT_out_ref[0] = sT_scr[...]


def _hgrn(p, lb, ng, s0, *, batch, seq, chunk, sub, heads, n_inner):
    m = batch * seq
    rows = chunk * n_inner
    nc = seq // rows
    hw = heads * HG_DIM
    nhg = HG_HEADS // heads
    cf, cq, ci, cg = (PROJ_COL[k] // hw for k in ("hf", "hq", "hi", "hg"))
    return pl.pallas_call(
        functools.partial(_hgrn_kernel, chunk=chunk, sub=sub, heads=heads, n_inner=n_inner),
        out_shape=(jax.ShapeDtypeStruct((m, HG_W), BF16),
                   jax.ShapeDtypeStruct((batch, HG_HEADS, HG_DIM, HG_DIM), F32)),
        grid=(batch, nhg, nc),
        in_specs=[
            pl.BlockSpec((rows, hw), lambda b, h, c: (b * nc + c, cf + h)),
            pl.BlockSpec((rows, hw), lambda b, h, c: (b * nc + c, cq + h)),
            pl.BlockSpec((rows, hw), lambda b, h, c: (b * nc + c, ci + h)),
            pl.BlockSpec((rows, hw), lambda b, h, c: (b * nc + c, cg + h)),
            pl.BlockSpec((1, hw), lambda b, h, c: (0, h)),
            pl.BlockSpec((1, hw), lambda b, h, c: (0, h)),
            pl.BlockSpec((heads, HG_DIM, HG_DIM), lambda b, h, c: (h, 0, 0)),
        ],
        out_specs=(
            pl.BlockSpec((rows, hw), lambda b, h, c: (b * nc + c, h)),
            pl.BlockSpec((1, heads, HG_DIM, HG_DIM), lambda b, h, c: (b, h, 0, 0)),
        ),
        scratch_shapes=[pltpu.VMEM((heads, HG_DIM, HG_DIM), F32)],
        compiler_params=pltpu.CompilerParams(
            dimension_semantics=("parallel", "parallel", "arbitrary"), vmem_limit_bytes=VMEM_LIMIT),
        name="hgrn2",
    )(p, p, p, p, lb, ng, s0)


def _merge_kernel(ya_ref, yh_ref, gts_a_ref, gts_h_ref, h_ref, wa_ref, wh_ref, wo_ref, g_ref, b_ref,
                  o_ref, acc_scr):
    j = pl.program_id(1)

    @pl.when(j == 0)
    def _():
        acc_scr[...] = jnp.zeros_like(acc_scr)

    ga = jax.nn.sigmoid(gts_a_ref[...].astype(F32))
    gh = jax.nn.sigmoid(gts_h_ref[...].astype(F32))
    merged = ga * _dot(ya_ref[...], wa_ref[...]) + gh * _dot(yh_ref[...], wh_ref[...])
    acc_scr[...] += _dot(merged.astype(BF16), wo_ref[...])

    @pl.when(j == pl.num_programs(1) - 1)
    def _():
        o_ref[...] = _layer_norm(ALPHA * h_ref[...] + acc_scr[...], g_ref[...], b_ref[...])


def _merge(ya, yh, gates, h1, wa, wh, wo, g, b, *, tm, tn):
    m = ya.shape[0]
    nj = D_MODEL // tn
    ca, cb = PROJ_COL["ga"] // tn, PROJ_COL["gb"] // tn
    return pl.pallas_call(
        _merge_kernel,
        out_shape=jax.ShapeDtypeStruct((m, D_MODEL), F32),
        grid=(m // tm, nj),
        in_specs=[
            pl.BlockSpec((tm, ATT_Q_W), lambda i, j: (i, 0)),
            pl.BlockSpec((tm, HG_W), lambda i, j: (i, 0)),
            pl.BlockSpec((tm, tn), lambda i, j: (i, ca + j)),
            pl.BlockSpec((tm, tn), lambda i, j: (i, cb + j)),
            pl.BlockSpec((tm, D_MODEL), lambda i, j: (i, 0)),
            pl.BlockSpec((ATT_Q_W, tn), lambda i, j: (0, j)),
            pl.BlockSpec((HG_W, tn), lambda i, j: (0, j)),
            pl.BlockSpec((tn, D_MODEL), lambda i, j: (j, 0)),
            pl.BlockSpec((1, D_MODEL), lambda i, j: (0, 0)),
            pl.BlockSpec((1, D_MODEL), lambda i, j: (0, 0)),
        ],
        out_specs=pl.BlockSpec((tm, D_MODEL), lambda i, j: (i, 0)),
        scratch_shapes=[pltpu.VMEM((tm, D_MODEL), F32)],
        compiler_params=pltpu.CompilerParams(
            dimension_semantics=("parallel", "arbitrary"), vmem_limit_bytes=VMEM_LIMIT),
        name="merge_ln",
    )(ya, yh, gates, gates, h1, wa, wh, wo, g, b)


def kernel(x, meta, ffn1_w_gate, ffn1_w_up, ffn1_w_down, ln1_g, ln1_b, w_in, idx_k_norm_g, idx_k_norm_b,
           hg_lb_logits, hg_norm_g, w_branch_att, w_branch_hg, w_out, ln2_g, ln2_b,
           ffn2_w_gate, ffn2_w_up, ffn2_w_down, ln3_g, ln3_b):
    batch, seq, _ = x.shape
    m = batch * seq
    xr = x.reshape(m, D_MODEL)
    bf = lambda w: w.astype(BF16)
    row = lambda v: v.reshape(1, -1)

    w_in_t = w_in[0].T
    w_idx = bf(jnp.pad(w_in_t[IN_OFFS[4]:IN_OFFS[6]].T, ((0, 0), (0, LANES - IDX_HEAD_DIM - N_IDX_HEADS))))
    idx_g = jnp.pad(idx_k_norm_g[0], (0, LANES - IDX_HEAD_DIM)).reshape(1, LANES)
    idx_b = jnp.pad(idx_k_norm_b[0], (0, LANES - IDX_HEAD_DIM)).reshape(1, LANES)
    lb = jnp.cumsum(jax.nn.softmax(hg_lb_logits.astype(F32), axis=0), axis=0)[0].reshape(1, HG_W)
    ng = hg_norm_g[0].reshape(1, HG_W)

    f1 = (bf(ffn1_w_gate[0]), bf(ffn1_w_up[0]), bf(ffn1_w_down[0]), row(ln1_g[0]), row(ln1_b[0]))
    f2 = (bf(ffn2_w_gate[0]), bf(ffn2_w_up[0]), bf(ffn2_w_down[0]), row(ln3_g[0]), row(ln3_b[0]))

    (hm_b,) = _ffn_ln(meta.astype(F32), *f1, tm=N_META, tf=512, out_dtypes=(BF16,))
    pm = _proj_all(hm_b, w_in_t, tm=N_META)
    kv0 = PROJ_COL["kv"]
    km = jnp.pad(pm[:, kv0:kv0 + ATT_KV_W], ((0, LANES - N_META), (0, 0)))
    vm = jnp.pad(pm[:, kv0 + ATT_KV_W:kv0 + 2 * ATT_KV_W], ((0, LANES - N_META), (0, 0)))
    s_zero = jnp.zeros((HG_HEADS, HG_DIM, HG_DIM), F32)
    _, s_meta = _hgrn(pm, lb, ng, s_zero, batch=1, seq=N_META, chunk=N_META, sub=N_META,
                      heads=HG_HEADS, n_inner=1)

    h1, h1b = _ffn_ln(xr, *f1, tm=512, tf=512, out_dtypes=(F32, BF16))
    p = _proj_all(h1b, w_in_t, tm=1024)
    p_idx = _proj_idx(h1b, w_idx, idx_g, idx_b, tm=1024)

    ikn = p_idx[:, :IDX_HEAD_DIM].astype(BF16)
    ik_lo = jnp.pad(ikn, ((0, 0), (0, LANES - IDX_HEAD_DIM)))
    ik_hi = jnp.pad(ikn, ((0, 0), (LANES - IDX_HEAD_DIM, 0)))
    iw_t = p_idx[:, IDX_HEAD_DIM:IDX_HEAD_DIM + N_IDX_HEADS].T
    y_att = _attention(p, iw_t, ik_lo, ik_hi, km, vm, batch=batch, seq=seq, tq=256, kc=512)
    y_hg, _ = _hgrn(p, lb, ng, s_meta[0], batch=batch, seq=seq, chunk=64, sub=16,
                    heads=HG_HEADS, n_inner=4)

    h2 = _merge(y_att, y_hg, p, h1, bf(w_branch_att[0]), bf(w_branch_hg[0]), bf(w_out[0]),
                row(ln2_g[0]), row(ln2_b[0]), tm=512, tn=512)
    (out,) = _ffn_ln(h2, *f2, tm=512, tf=512, out_dtypes=(F32,))
    return out.reshape(batch, seq, D_MODEL)
```

```python
import functools
import math

import jax
import jax.numpy as jnp
import numpy as np
from jax import lax
from jax.experimental import pallas as pl
from jax.experimental.pallas import tpu as pltpu

D_MODEL = 2048
N_META = 16
N_ATT_HEADS = 8
N_KV_HEADS = 2
ATT_GROUP = N_ATT_HEADS // N_KV_HEADS
ATT_HEAD_DIM = 128
N_IDX_HEADS = 16
IDX_HEAD_DIM = 64
TOPK = 256
HG_HEADS = 8
HG_DIM = 128
D_FF = 5632
LN_EPS = 1e-5
RMS_EPS = 1e-6
ALPHA = 2.0 ** 0.25

ATT_Q_W = N_ATT_HEADS * ATT_HEAD_DIM
ATT_KV_W = N_KV_HEADS * ATT_HEAD_DIM
IDX_Q_W = N_IDX_HEADS * IDX_HEAD_DIM
HG_W = HG_HEADS * HG_DIM

LANES = 128
SUBLANES = 8
VMEM_LIMIT = 56 * 1024 * 1024

F32 = jnp.float32
BF16 = jnp.bfloat16
INT_MIN = -2 ** 31
F32_LOWEST = float(np.finfo(np.float32).min)
LOG2E = math.log2(math.e)
ATT_Q_SCALE = ATT_HEAD_DIM ** -0.5 * LOG2E
POS_SHIFT = 6
POS_SPLIT = 1 << POS_SHIFT


def _dot(a, b):
    return jnp.dot(a, b, preferred_element_type=F32)


def _dot_nt(a, b):
    return lax.dot_general(a, b, (((1,), (1,)), ((), ())), preferred_element_type=F32)


def _dot_tn(a, b):
    return lax.dot_general(a, b, (((0,), (0,)), ((), ())), preferred_element_type=F32)


def _layer_norm(y, g, b):
    mu = jnp.mean(y, axis=-1, keepdims=True)
    d = y - mu
    var = jnp.mean(d * d, axis=-1, keepdims=True)
    return d * lax.rsqrt(var + LN_EPS) * g + b


def _silu(x):
    return x * jax.nn.sigmoid(x)


def _bf16_parts(c, n=3):
    parts = []
    for _ in range(n):
        p = float(np.asarray(c, np.float32).astype(jnp.bfloat16).astype(np.float32))
        parts.append(p)
        c = c - p
    return parts


FFN_TM = 1024
FFN_TF = 256


def _ffn_ln_kernel(x_ref, wg_ref, wu_ref, wd_ref, g_ref, b_ref, o_ref, xb_scr):
    j = pl.program_id(1)

    @pl.when(j == 0)
    def _():
        xb_scr[...] = x_ref[...].astype(BF16)
        o_ref[...] = jnp.zeros_like(o_ref)

    xb = xb_scr[...]
    gate = _dot(xb, wg_ref[...].astype(BF16))
    up = _dot(xb, wu_ref[...].astype(BF16))
    a = (_silu(gate) * up).astype(BF16)
    o_ref[...] += _dot(a, wd_ref[...].astype(BF16))

    @pl.when(j == pl.num_programs(1) - 1)
    def _():
        y = ALPHA * x_ref[...] + 0.5 * o_ref[...]
        o_ref[...] = _layer_norm(y, g_ref[...], b_ref[...])


def _ffn_ln(x, wg, wu, wd, g, b, *, tm, tf):
    m = x.shape[0]
    grid = (m // tm, D_FF // tf)
    return pl.pallas_call(
        _ffn_ln_kernel,
        out_shape=jax.ShapeDtypeStruct((m, D_MODEL), F32),
        grid=grid,
        in_specs=[
            pl.BlockSpec((tm, D_MODEL), lambda i, j: (i, 0), pipeline_mode=pl.Buffered(1)),
            pl.BlockSpec((D_MODEL, tf), lambda i, j: (0, j)),
            pl.BlockSpec((D_MODEL, tf), lambda i, j: (0, j)),
            pl.BlockSpec((tf, D_MODEL), lambda i, j: (j, 0)),
            pl.BlockSpec((1, D_MODEL), lambda i, j: (0, 0)),
            pl.BlockSpec((1, D_MODEL), lambda i, j: (0, 0)),
        ],
        out_specs=pl.BlockSpec((tm, D_MODEL), lambda i, j: (i, 0)),
        scratch_shapes=[pltpu.VMEM((tm, D_MODEL), BF16)],
        compiler_params=pltpu.CompilerParams(
            dimension_semantics=("parallel", "arbitrary"), vmem_limit_bytes=VMEM_LIMIT),
        name="ffn_ln",
    )(x, wg, wu, wd, g, b)


IN_SPLITS = (ATT_Q_W, ATT_KV_W, ATT_KV_W, IDX_Q_W, IDX_HEAD_DIM, N_IDX_HEADS, HG_W, HG_W, HG_W, HG_W,
             D_MODEL, D_MODEL)
IN_OFFS = tuple(int(v) for v in np.cumsum((0,) + IN_SPLITS))
PROJ_TN = 512
PROJ_LAYOUT = (("aq", IN_OFFS[0], ATT_Q_W), ("iq", IN_OFFS[3], IDX_Q_W), ("hq", IN_OFFS[6], HG_W),
               ("hf", IN_OFFS[7], HG_W), ("hi", IN_OFFS[8], HG_W), ("hg", IN_OFFS[9], HG_W),
               ("ga", IN_OFFS[10], D_MODEL), ("gb", IN_OFFS[11], D_MODEL), ("kv", IN_OFFS[1], 2 * ATT_KV_W))
PROJ_COL = {}
PROJ_TILE_SRC = []
for _name, _src, _width in PROJ_LAYOUT:
    PROJ_COL[_name] = len(PROJ_TILE_SRC) * PROJ_TN
    PROJ_TILE_SRC += [_src + t * PROJ_TN for t in range(_width // PROJ_TN)]
PROJ_W = len(PROJ_TILE_SRC) * PROJ_TN
PROJ_SRC_ALIGN = 16
assert all(s % PROJ_SRC_ALIGN == 0 for s in PROJ_TILE_SRC)


def _proj_all_kernel(src_ref, x_ref, wt_ref, o_ref, xb_scr, *, scaled_tiles, scale):
    @pl.when(pl.program_id(1) == 0)
    def _():
        xb_scr[...] = x_ref[...].astype(BF16)

    acc = _dot_nt(xb_scr[...], wt_ref[...].astype(BF16))
    acc = acc * jnp.where(pl.program_id(1) < scaled_tiles, scale, 1.0)
    o_ref[...] = acc.astype(o_ref.dtype)


def _idx_kernel(x_ref, w_ref, g_ref, b_ref, o_ref):
    p = _dot(x_ref[...].astype(BF16), w_ref[...])
    lane = lax.broadcasted_iota(jnp.int32, p.shape, 1)
    is_k = lane < IDX_HEAD_DIM
    mu = jnp.sum(jnp.where(is_k, p, 0.0), axis=-1, keepdims=True) * (1.0 / IDX_HEAD_DIM)
    d = jnp.where(is_k, p - mu, 0.0)
    var = jnp.sum(d * d, axis=-1, keepdims=True) * (1.0 / IDX_HEAD_DIM)
    kn = d * lax.rsqrt(var + LN_EPS) * g_ref[...] + b_ref[...]
    w_scale = (N_IDX_HEADS ** -0.5) * (IDX_HEAD_DIM ** -0.5)
    o_ref[...] = jnp.where(is_k, kn, p * w_scale)


def _proj_all(xb, w_in_t, *, tm):
    m = xb.shape[0]
    n_tiles = len(PROJ_TILE_SRC)
    return pl.pallas_call(
        functools.partial(_proj_all_kernel, scaled_tiles=ATT_Q_W // PROJ_TN, scale=ATT_Q_SCALE),
        out_shape=jax.ShapeDtypeStruct((m, PROJ_W), BF16),
        grid_spec=pltpu.PrefetchScalarGridSpec(
            num_scalar_prefetch=1,
            grid=(m // tm, n_tiles),
            in_specs=[pl.BlockSpec((tm, D_MODEL), lambda i, j, src: (i, 0)),
                      pl.BlockSpec((pl.Element(PROJ_TN), pl.Element(D_MODEL)),
                                   lambda i, j, src: (src[j] * PROJ_SRC_ALIGN, 0))],
            out_specs=pl.BlockSpec((tm, PROJ_TN), lambda i, j, src: (i, j)),
            scratch_shapes=[pltpu.VMEM((tm, D_MODEL), BF16)],
        ),
        compiler_params=pltpu.CompilerParams(
            dimension_semantics=("parallel", "arbitrary"), vmem_limit_bytes=VMEM_LIMIT),
        name="proj_all",
    )(jnp.asarray([s // PROJ_SRC_ALIGN for s in PROJ_TILE_SRC], jnp.int32), xb, w_in_t)


def _proj_idx(xb, w, g, b, *, tm):
    m = xb.shape[0]
    return pl.pallas_call(
        _idx_kernel,
        out_shape=jax.ShapeDtypeStruct((m, LANES), F32),
        grid=(m // tm,),
        in_specs=[pl.BlockSpec((tm, D_MODEL), lambda i: (i, 0)),
                  pl.BlockSpec((D_MODEL, LANES), lambda i: (0, 0)),
                  pl.BlockSpec((1, LANES), lambda i: (0, 0)),
                  pl.BlockSpec((1, LANES), lambda i: (0, 0))],
        out_specs=pl.BlockSpec((tm, LANES), lambda i: (i, 0)),
        compiler_params=pltpu.CompilerParams(
            dimension_semantics=("parallel",), vmem_limit_bytes=VMEM_LIMIT),
        name="proj_idx",
    )(xb, w, g, b)


ATT_ROW_TILE = 32


def _key_to_f32(key):
    return pltpu.bitcast(key ^ ((key >> 31) & 0x7FFFFFFF), F32)


def _attn_kernel(aq_ref, iq_ref, iwt_ref, iklo_ref, ikhi_ref, k_ref, v_ref, km_ref, vm_ref,
                 o_ref,
                 score_scr, mb_scr, kaug_scr, vaug_scr, qaug_scr, s0_scr, s1_scr, p_scr,
                 macc_scr, m_scr, acc_scr, *, tq, kc, seq):
    qi = pl.program_id(1)
    n_kc = ((qi + 1) * tq + kc - 1) // kc
    kf = float(TOPK)

    @pl.when(qi == 0)
    def _():
        r = lax.broadcasted_iota(jnp.int32, (LANES + seq, LANES), 0)
        lane = lax.broadcasted_iota(jnp.int32, (LANES + seq, LANES), 1)
        pos = jnp.where(r < LANES, r, r - LANES + N_META)
        feat = jnp.where(lane < 3, pos >> POS_SHIFT, jnp.where(lane < 6, pos & (POS_SPLIT - 1), 0))
        feat = feat.astype(F32).astype(BF16)
        ones = jnp.ones((LANES + seq, LANES), BF16)
        for kvh in range(N_KV_HEADS):
            cs = slice(kvh * LANES, (kvh + 1) * LANES)
            kaug_scr[kvh, 0:LANES, 0:LANES] = km_ref[:, cs]
            kaug_scr[kvh, LANES:, 0:LANES] = k_ref[:, cs]
            kaug_scr[kvh, :, LANES:] = feat
            vaug_scr[kvh, 0:LANES, 0:LANES] = vm_ref[:, cs]
            vaug_scr[kvh, LANES:, 0:LANES] = v_ref[:, cs]
            vaug_scr[kvh, :, LANES:] = ones

    qcol = qi * tq + lax.broadcasted_iota(jnp.int32, (1, tq), 1)
    iwt = iwt_ref[...]

    def score_body(j, carry):
        off = pl.multiple_of(j * kc, kc)
        klo = iklo_ref[pl.ds(off, kc), :]
        khi = ikhi_ref[pl.ds(off, kc), :]
        acc = jnp.zeros((kc, tq), F32)
        for p in range(N_IDX_HEADS // 2):
            q2 = iq_ref[:, p * LANES:(p + 1) * LANES]
            acc = acc + jnp.maximum(_dot_nt(klo, q2), 0.0) * iwt[2 * p:2 * p + 1, :]
            acc = acc + jnp.maximum(_dot_nt(khi, q2), 0.0) * iwt[2 * p + 1:2 * p + 2, :]
        krow = off + lax.broadcasted_iota(jnp.int32, (kc, tq), 0)
        score_scr[pl.ds(off, kc), :] = jnp.where(krow <= qcol, acc, -jnp.inf)
        return carry

    lax.fori_loop(0, n_kc, score_body, 0)

    n_acc = 8
    acc_rows = n_acc * SUBLANES

    def count_ge(cand):
        def body(j, acc):
            off = pl.multiple_of(j * kc, kc)
            w = jnp.where(score_scr[pl.ds(off, kc), :] >= cand, 1.0, 0.0)
            return acc + jnp.sum(w.reshape(kc // acc_rows, acc_rows, tq), axis=0)

        acc = lax.fori_loop(0, n_kc, body, jnp.zeros((acc_rows, tq), F32))
        return jnp.sum(acc, axis=0, keepdims=True)

    c0 = count_ge(jnp.zeros((1, tq), F32))
    ok0 = c0 >= kf
    thr0 = jnp.where(ok0, 0, INT_MIN).astype(jnp.int32)
    cnt0 = jnp.where(ok0, c0, 0.0)

    def bit_body(i, carry):
        thr, cnt = carry
        cand = thr | jnp.left_shift(jnp.int32(1), 30 - i)
        c = count_ge(_key_to_f32(cand))
        ok = c >= kf
        return jnp.where(ok, cand, thr), jnp.where(ok, c, cnt)

    thr, cnt = lax.fori_loop(0, 31, bit_body, (thr0, cnt0))
    t_f = jnp.where(thr == INT_MIN, F32_LOWEST, _key_to_f32(thr))

    @pl.when(jnp.max(cnt) > kf)
    def _():
        def gt_body(j, acc):
            off = pl.multiple_of(j * kc, kc)
            w = jnp.where(score_scr[pl.ds(off, kc), :] > t_f, 1.0, 0.0)
            return acc + jnp.sum(w, axis=0, keepdims=True)

        need = kf - lax.fori_loop(0, n_kc, gt_body, jnp.zeros((1, tq), F32))
        lower = (lax.broadcasted_iota(jnp.int32, (kc, kc), 0)
                 >= lax.broadcasted_iota(jnp.int32, (kc, kc), 1)).astype(BF16)

        def tie_body(j, before):
            off = pl.multiple_of(j * kc, kc)
            sc = score_scr[pl.ds(off, kc), :]
            eq = sc == t_f
            eqf = jnp.where(eq, 1.0, 0.0)
            rank = before + _dot(lower, eqf.astype(BF16))
            score_scr[pl.ds(off, kc), :] = jnp.where(eq & (rank > need), -jnp.inf, sc)
            return before + jnp.sum(eqf, axis=0, keepdims=True)

        lax.fori_loop(0, n_kc, tie_body, jnp.zeros((1, tq), F32))

    eye = (lax.broadcasted_iota(jnp.int32, (tq, tq), 0)
           == lax.broadcasted_iota(jnp.int32, (tq, tq), 1)).astype(BF16)

    def mask_body(j, carry):
        off = pl.multiple_of(j * kc, kc)
        sel_t = jnp.where(score_scr[pl.ds(off, kc), :] >= t_f, 1.0, 0.0).astype(BF16)
        sel = _dot_nt(eye, sel_t)
        mb_scr[:, pl.ds(off, kc)] = jnp.where(sel > 0.5, 0.0, -jnp.inf)
        return carry

    lax.fori_loop(0, n_kc, mask_body, 0)

    rt = ATT_ROW_TILE
    lane_q = lax.broadcasted_iota(jnp.int32, (1, LANES), 1)
    meta_mask = jnp.where(lax.broadcasted_iota(jnp.int32, (rt, LANES), 1) < N_META, 0.0, -jnp.inf)

    s_slots = (s0_scr, s1_scr)

    def logits(kvh, krow0, width, slot):
        s_slots[slot][:, 0:width] = _dot_nt(qaug_scr[kvh], kaug_scr[kvh, pl.ds(krow0, width), :])

    def max_tiles(kvh, slot, width, mask_fn):
        s_ref = s_slots[slot]
        for r0 in range(0, tq, rt):
            mb = mask_fn(r0)
            for g in range(ATT_GROUP):
                rows = slice(g * tq + r0, g * tq + r0 + rt)
                mx = s_ref[rows, 0:LANES] + mb[:, 0:LANES]
                for c in range(1, width // LANES):
                    cols = slice(c * LANES, (c + 1) * LANES)
                    mx = jnp.maximum(mx, s_ref[rows, cols] + mb[:, cols])
                macc_scr[rows, :] = jnp.maximum(macc_scr[rows, :], mx)

    def exp_tiles(kvh, slot, width, mask_fn):
        s_ref = s_slots[slot]
        for r0 in range(0, tq, rt):
            mb = mask_fn(r0)
            for g in range(ATT_GROUP):
                rows = slice(g * tq + r0, g * tq + r0 + rt)
                m_row = m_scr[rows, :]
                for c in range(width // LANES):
                    cols = slice(c * LANES, (c + 1) * LANES)
                    p_scr[rows, cols] = jnp.exp2(s_ref[rows, cols] + mb[:, cols] - m_row).astype(BF16)

    def add_pv(kvh, krow0, width):
        vblk = vaug_scr[kvh, pl.ds(krow0, width), :]
        half = ATT_GROUP * tq // 2
        for r0 in (0, half):
            acc_scr[kvh, r0:r0 + half, :] += _dot(p_scr[r0:r0 + half, 0:width], vblk)

    def run_pass(kvh, tiles_fn, with_pv):
        def krow(j):
            return pl.multiple_of(LANES + j * kc, LANES)

        def step(j, slot, lookahead):
            off = pl.multiple_of(j * kc, kc)
            if lookahead:
                logits(kvh, krow(j + 1), kc, 1 - slot)
            tiles_fn(kvh, slot, kc, lambda r0: mb_scr[r0:r0 + rt, pl.ds(off, kc)])
            if with_pv:
                add_pv(kvh, krow(j), kc)

        logits(kvh, 0, LANES, 0)
        tiles_fn(kvh, 0, LANES, lambda r0: meta_mask)
        if with_pv:
            add_pv(kvh, 0, LANES)

            def serial_body(j, carry):
                off = pl.multiple_of(j * kc, kc)
                logits(kvh, krow(j), kc, 0)
                tiles_fn(kvh, 0, kc, lambda r0: mb_scr[r0:r0 + rt, pl.ds(off, kc)])
                add_pv(kvh, krow(j), kc)
                return carry

            lax.fori_loop(0, n_kc, serial_body, 0)
            return
        logits(kvh, krow(0), kc, 0)
        n_pairs = (n_kc - 1) // 2

        def pair_body(i, carry):
            step(2 * i, 0, True)
            step(2 * i + 1, 1, True)
            return carry

        lax.fori_loop(0, n_pairs, pair_body, 0)
        j0 = 2 * n_pairs
        two_left = n_kc - j0 == 2

        @pl.when(two_left)
        def _():
            step(j0, 0, True)
            step(j0 + 1, 1, False)

        @pl.when(jnp.logical_not(two_left))
        def _():
            step(j0, 0, False)

    for h in range(N_ATT_HEADS):
        kvh, g = divmod(h, ATT_GROUP)
        parts = _bf16_parts(2.0 ** (-8.0 * (h + 1) / N_ATT_HEADS) * LOG2E)
        vals = [POS_SPLIT * p for p in parts] + parts
        qfeat = jnp.zeros((1, LANES), F32)
        for i, val in enumerate(vals):
            qfeat = jnp.where(lane_q == i, val, qfeat)
        qaug_scr[kvh, g * tq:(g + 1) * tq, 0:LANES] = aq_ref[:, h * LANES:(h + 1) * LANES]
        qaug_scr[kvh, g * tq:(g + 1) * tq, LANES:] = jnp.broadcast_to(qfeat, (tq, LANES)).astype(BF16)

    for kvh in range(N_KV_HEADS):
        macc_scr[...] = jnp.full_like(macc_scr, -jnp.inf)
        run_pass(kvh, max_tiles, with_pv=False)
        m_scr[...] = jnp.broadcast_to(jnp.max(macc_scr[...], axis=1, keepdims=True), m_scr.shape)
        acc_scr[kvh] = jnp.zeros(acc_scr.shape[1:], F32)
        run_pass(kvh, exp_tiles, with_pv=True)

    for h in range(N_ATT_HEADS):
        kvh, g = divmod(h, ATT_GROUP)
        acc = acc_scr[kvh, g * tq:(g + 1) * tq, :]
        o_ref[:, h * LANES:(h + 1) * LANES] = (acc[:, 0:LANES] / acc[:, LANES:LANES + 1]).astype(o_ref.dtype)


def _attention(p_att, iw_t, ik_lo, ik_hi, km, vm, *, batch, seq, tq, kc):
    m = batch * seq
    nq = seq // tq
    qcol, icol, kcol = PROJ_COL["aq"] // ATT_Q_W, PROJ_COL["iq"] // IDX_Q_W, PROJ_COL["kv"] // ATT_KV_W
    rows4 = ATT_GROUP * tq
    return pl.pallas_call(
        functools.partial(_attn_kernel, tq=tq, kc=kc, seq=seq),
        out_shape=jax.ShapeDtypeStruct((m, ATT_Q_W), BF16),
        grid=(batch, nq),
        in_specs=[
            pl.BlockSpec((tq, ATT_Q_W), lambda b, q: (b * nq + q, qcol)),
            pl.BlockSpec((tq, IDX_Q_W), lambda b, q: (b * nq + q, icol)),
            pl.BlockSpec((N_IDX_HEADS, tq), lambda b, q: (0, b * nq + q)),
            pl.BlockSpec((seq, LANES), lambda b, q: (b, 0)),
            pl.BlockSpec((seq, LANES), lambda b, q: (b, 0)),
            pl.BlockSpec((seq, ATT_KV_W), lambda b, q: (b, kcol)),
            pl.BlockSpec((seq, ATT_KV_W), lambda b, q: (b, kcol + 1)),
            pl.BlockSpec((LANES, ATT_KV_W), lambda b, q: (0, 0)),
            pl.BlockSpec((LANES, ATT_KV_W), lambda b, q: (0, 0)),
        ],
        out_specs=pl.BlockSpec((tq, ATT_Q_W), lambda b, q: (b * nq + q, 0)),
        scratch_shapes=[
            pltpu.VMEM((seq, tq), F32),
            pltpu.VMEM((tq, seq), F32),
            pltpu.VMEM((N_KV_HEADS, LANES + seq, 2 * LANES), BF16),
            pltpu.VMEM((N_KV_HEADS, LANES + seq, 2 * LANES), BF16),
            pltpu.VMEM((N_KV_HEADS, rows4, 2 * LANES), BF16),
            pltpu.VMEM((rows4, kc), F32),
            pltpu.VMEM((rows4, kc), F32),
            pltpu.VMEM((rows4, kc), BF16),
            pltpu.VMEM((rows4, LANES), F32),
            pltpu.VMEM((rows4, LANES), F32),
            pltpu.VMEM((N_KV_HEADS, rows4, 2 * LANES), F32),
        ],
        compiler_params=pltpu.CompilerParams(
            dimension_semantics=("parallel", "arbitrary"), vmem_limit_bytes=VMEM_LIMIT),
        name="dsa_attention",
    )(p_att, p_att, iw_t, ik_lo, ik_hi, p_att, p_att, km, vm)


def _split3(x):
    hi = x.astype(BF16)
    r = x - hi.astype(F32)
    mid = r.astype(BF16)
    lo = (r - mid.astype(F32)).astype(BF16)
    return hi, mid, lo


HG_SAFE_EXPONENT = 80.0


def _hgrn_kernel(hf_ref, hr_q_ref, hr_i_ref, hr_g_ref, lb_ref, ng_ref, s0_ref,
                 y_ref, sT_out_ref, sT_scr, *, chunk, sub, heads, n_inner):
    c_idx = pl.program_id(2)

    @pl.when(c_idx == 0)
    def _():
        sT_scr[...] = s0_ref[...]

    n_sub = chunk // sub
    causal = (lax.broadcasted_iota(jnp.int32, (chunk, chunk), 0)
              >= lax.broadcasted_iota(jnp.int32, (chunk, chunk), 1))
    tri = causal.astype(BF16)
    t_iota = lax.broadcasted_iota(jnp.int32, (sub, LANES), 0)

    def head_inputs(r0, g):
        rows, cs = pl.ds(r0, chunk), slice(g * HG_DIM, (g + 1) * HG_DIM)
        lb = lb_ref[:, cs]
        q = _silu(hr_q_ref[rows, cs].astype(F32))
        fg = lb + (1.0 - lb) * jax.nn.sigmoid(hf_ref[rows, cs].astype(F32))
        v = hr_i_ref[rows, cs].astype(F32)
        l_hi, l_mid, l_lo = _split3(jnp.log(fg))
        b = _dot(tri, l_hi) + _dot(tri, l_mid) + _dot(tri, l_lo)
        return q, 1.0 - fg, v, b

    def carry_state(g, q, kk, vb, b):
        sT = sT_scr[g]
        b_last = b[chunk - 1:chunk, :]
        qe = (q * jnp.exp(b)).astype(BF16)
        o_state = _dot_nt(qe, sT.astype(BF16))
        khat = (kk * jnp.exp(b_last - b)).astype(BF16)
        sT_scr[g] = sT * jnp.exp(b_last) + _dot_tn(vb, khat)
        return qe, o_state

    def finish(r0, g, o):
        rows, cs = pl.ds(r0, chunk), slice(g * HG_DIM, (g + 1) * HG_DIM)
        o = o * lax.rsqrt(jnp.mean(o * o, axis=-1, keepdims=True) + RMS_EPS) * ng_ref[:, cs]
        o = o * _silu(hr_g_ref[rows, cs].astype(F32))
        y_ref[rows, cs] = o.astype(y_ref.dtype)

    def factored_chunk(c, carry):
        rows = pl.ds(pl.multiple_of(c * chunk, chunk), chunk)
        hs = [slice(g * HG_DIM, (g + 1) * HG_DIM) for g in range(heads)]
        lb = lb_ref[...]
        q = _silu(hr_q_ref[rows, :].astype(F32))
        fg = lb + (1.0 - lb) * jax.nn.sigmoid(hf_ref[rows, :].astype(F32))
        kk = 1.0 - fg
        vb = hr_i_ref[rows, :]
        l_hi, l_mid, l_lo = _split3(jnp.log(fg))
        b = _dot(tri, l_hi) + _dot(tri, l_mid) + _dot(tri, l_lo)
        b_last = b[chunk - 1:chunk, :]
        qe = (q * jnp.exp(b)).astype(BF16)
        ke = (kk * jnp.exp(-b)).astype(BF16)
        khat = (kk * jnp.exp(b_last - b)).astype(BF16)
        keep = jnp.exp(b_last)
        s_old = [sT_scr[g] for g in range(heads)]
        a = [_dot_nt(qe[:, cs], ke[:, cs]) for cs in hs]
        o_state = [_dot_nt(qe[:, cs], s_old[g].astype(BF16)) for g, cs in enumerate(hs)]
        s_add = [_dot_tn(vb[:, cs], khat[:, cs]) for cs in hs]
        a = [jnp.where(causal, x, 0.0).astype(BF16) for x in a]
        o = [o_state[g] + _dot(a[g], vb[:, cs]) for g, cs in enumerate(hs)]
        for g, cs in enumerate(hs):
            sT_scr[g] = s_old[g] * keep[:, cs] + s_add[g]
        o = [x * lax.rsqrt(jnp.mean(x * x, axis=-1, keepdims=True) + RMS_EPS) for x in o]
        o = jnp.concatenate(o, axis=1) * ng_ref[...] * _silu(hr_g_ref[rows, :].astype(F32))
        y_ref[rows, :] = o.astype(y_ref.dtype)
        return carry

    def guarded_chunk(c, carry):
        r0 = pl.multiple_of(c * chunk, chunk)
        for g in range(heads):
            q, kk, v, b = head_inputs(r0, g)
            vb = v.astype(BF16)
            _, o_state = carry_state(g, q, kk, vb, b)
            rows = []
            for i in range(n_sub):
                s0, s1 = i * sub, (i + 1) * sub
                bs, qs, ks, vs = b[s0:s1], q[s0:s1], kk[s0:s1], v[s0:s1]
                o_i = o_state[s0:s1]
                if i > 0:
                    r_i = b[s0 - 1:s0, :]
                    qt = (qs * jnp.exp(bs - r_i)).astype(BF16)
                    kt = (kk[:s0] * jnp.exp(r_i - b[:s0])).astype(BF16)
                    a_off = _dot_nt(qt, kt)
                    o_i = o_i + _dot(a_off.astype(BF16), vb[:s0])
                for s in range(sub):
                    e = jnp.exp(jnp.where(t_iota >= s, bs - bs[s:s + 1], -jnp.inf))
                    a_col = jnp.sum(qs * e * ks[s:s + 1], axis=1, keepdims=True)
                    o_i = o_i + a_col * vs[s:s + 1]
                rows.append(o_i)
            finish(r0, g, jnp.concatenate(rows, axis=0) if n_sub > 1 else rows[0])
        return carry

    factoring_safe = jnp.max(-jnp.log(lb_ref[...])) * chunk <= HG_SAFE_EXPONENT

    @pl.when(factoring_safe)
    def _():
        lax.fori_loop(0, n_inner, factored_chunk, 0)

    @pl.when(jnp.logical_not(factoring_safe))
    def _():
        lax.fori_loop(0, n_inner, guarded_chunk, 0)

    @pl.when(c_idx == pl.num_programs(2) - 1)
    def _():
        sT_out_ref[0] = sT_scr[...]


def _hgrn(p, lb, ng, s0, *, batch, seq, chunk, sub, heads, n_inner):
    m = batch * seq
    rows = chunk * n_inner
    nc = seq // rows
    hw = heads * HG_DIM
    nhg = HG_HEADS // heads
    cf, cq, ci, cg = (PROJ_COL[k] // hw for k in ("hf", "hq", "hi", "hg"))
    return pl.pallas_call(
        functools.partial(_hgrn_kernel, chunk=chunk, sub=sub, heads=heads, n_inner=n_inner),
        out_shape=(jax.ShapeDtypeStruct((m, HG_W), BF16),
                   jax.ShapeDtypeStruct((batch, HG_HEADS, HG_DIM, HG_DIM), F32)),
        grid=(batch, nhg, nc),
        in_specs=[
            pl.BlockSpec((rows, hw), lambda b, h, c: (b * nc + c, cf + h)),
            pl.BlockSpec((rows, hw), lambda b, h, c: (b * nc + c, cq + h)),
            pl.BlockSpec((rows, hw), lambda b, h, c: (b * nc + c, ci + h)),
            pl.BlockSpec((rows, hw), lambda b, h, c: (b * nc + c, cg + h)),
            pl.BlockSpec((1, hw), lambda b, h, c: (0, h)),
            pl.BlockSpec((1, hw), lambda b, h, c: (0, h)),
            pl.BlockSpec((heads, HG_DIM, HG_DIM), lambda b, h, c: (h, 0, 0)),
        ],
        out_specs=(
            pl.BlockSpec((rows, hw), lambda b, h, c: (b * nc + c, h)),
            pl.BlockSpec((1, heads, HG_DIM, HG_DIM), lambda b, h, c: (b, h, 0, 0)),
        ),
        scratch_shapes=[pltpu.VMEM((heads, HG_DIM, HG_DIM), F32)],
        compiler_params=pltpu.CompilerParams(
            dimension_semantics=("parallel", "parallel", "arbitrary"), vmem_limit_bytes=VMEM_LIMIT),
        name="hgrn2",
    )(p, p, p, p, lb, ng, s0)


def _merge_kernel(ya_ref, yh_ref, gts_a_ref, gts_h_ref, h_ref, wa_ref, wh_ref, wo_ref, g_ref, b_ref,
                  o_ref, acc_scr):
    j = pl.program_id(1)

    @pl.when(j == 0)
    def _():
        acc_scr[...] = jnp.zeros_like(acc_scr)

    ga = jax.nn.sigmoid(gts_a_ref[...].astype(F32))
    gh = jax.nn.sigmoid(gts_h_ref[...].astype(F32))
    merged = ga * _dot(ya_ref[...], wa_ref[...]) + gh * _dot(yh_ref[...], wh_ref[...])
    acc_scr[...] += _dot(merged.astype(BF16), wo_ref[...])

    @pl.when(j == pl.num_programs(1) - 1)
    def _():
        o_ref[...] = _layer_norm(ALPHA * h_ref[...] + acc_scr[...], g_ref[...], b_ref[...])


def _merge(ya, yh, gates, h1, wa, wh, wo, g, b, *, tm, tn):
    m = ya.shape[0]
    nj = D_MODEL // tn
    ca, cb = PROJ_COL["ga"] // tn, PROJ_COL["gb"] // tn
    return pl.pallas_call(
        _merge_kernel,
        out_shape=jax.ShapeDtypeStruct((m, D_MODEL), F32),
        grid=(m // tm, nj),
        in_specs=[
            pl.BlockSpec((tm, ATT_Q_W), lambda i, j: (i, 0)),
            pl.BlockSpec((tm, HG_W), lambda i, j: (i, 0)),
            pl.BlockSpec((tm, tn), lambda i, j: (i, ca + j)),
            pl.BlockSpec((tm, tn), lambda i, j: (i, cb + j)),
            pl.BlockSpec((tm, D_MODEL), lambda i, j: (i, 0)),
            pl.BlockSpec((ATT_Q_W, tn), lambda i, j: (0, j)),
            pl.BlockSpec((HG_W, tn), lambda i, j: (0, j)),
            pl.BlockSpec((tn, D_MODEL), lambda i, j: (j, 0)),
            pl.BlockSpec((1, D_MODEL), lambda i, j: (0, 0)),
            pl.BlockSpec((1, D_MODEL), lambda i, j: (0, 0)),
        ],
        out_specs=pl.BlockSpec((tm, D_MODEL), lambda i, j: (i, 0)),
        scratch_shapes=[pltpu.VMEM((tm, D_MODEL), F32)],
        compiler_params=pltpu.CompilerParams(
            dimension_semantics=("parallel", "arbitrary"), vmem_limit_bytes=VMEM_LIMIT),
        name="merge_ln",
    )(ya, yh, gates, gates, h1, wa, wh, wo, g, b)


def kernel(x, meta, ffn1_w_gate, ffn1_w_up, ffn1_w_down, ln1_g, ln1_b, w_in, idx_k_norm_g, idx_k_norm_b,
           hg_lb_logits, hg_norm_g, w_branch_att, w_branch_hg, w_out, ln2_g, ln2_b,
           ffn2_w_gate, ffn2_w_up, ffn2_w_down, ln3_g, ln3_b):
    batch, seq, _ = x.shape
    m = batch * seq
    xr = x.reshape(m, D_MODEL)
    bf = lambda w: w.astype(BF16)
    row = lambda v: v.reshape(1, -1)

    w_in_t = w_in[0].T
    w_idx = bf(jnp.pad(w_in_t[IN_OFFS[4]:IN_OFFS[6]].T, ((0, 0), (0, LANES - IDX_HEAD_DIM - N_IDX_HEADS))))
    idx_g = jnp.pad(idx_k_norm_g[0], (0, LANES - IDX_HEAD_DIM)).reshape(1, LANES)
    idx_b = jnp.pad(idx_k_norm_b[0], (0, LANES - IDX_HEAD_DIM)).reshape(1, LANES)
    lb = jnp.cumsum(jax.nn.softmax(hg_lb_logits.astype(F32), axis=0), axis=0)[0].reshape(1, HG_W)
    ng = hg_norm_g[0].reshape(1, HG_W)

    f1 = (ffn1_w_gate[0], ffn1_w_up[0], ffn1_w_down[0], row(ln1_g[0]), row(ln1_b[0]))
    f2 = (ffn2_w_gate[0], ffn2_w_up[0], ffn2_w_down[0], row(ln3_g[0]), row(ln3_b[0]))

    hm = _ffn_ln(meta.astype(F32), *f1, tm=N_META, tf=FFN_TF)
    pm = _proj_all(hm, w_in_t, tm=N_META)
    kv0 = PROJ_COL["kv"]
    km = jnp.pad(pm[:, kv0:kv0 + ATT_KV_W], ((0, LANES - N_META), (0, 0)))
    vm = jnp.pad(pm[:, kv0 + ATT_KV_W:kv0 + 2 * ATT_KV_W], ((0, LANES - N_META), (0, 0)))
    s_zero = jnp.zeros((HG_HEADS, HG_DIM, HG_DIM), F32)
    _, s_meta = _hgrn(pm, lb, ng, s_zero, batch=1, seq=N_META, chunk=N_META, sub=N_META,
                      heads=HG_HEADS, n_inner=1)

    h1 = _ffn_ln(xr, *f1, tm=FFN_TM, tf=FFN_TF)
    p = _proj_all(h1, w_in_t, tm=1024)
    p_idx = _proj_idx(h1, w_idx, idx_g, idx_b, tm=1024)

    ikn = p_idx[:, :IDX_HEAD_DIM].astype(BF16)
    ik_lo = jnp.pad(ikn, ((0, 0), (0, LANES - IDX_HEAD_DIM)))
    ik_hi = jnp.pad(ikn, ((0, 0), (LANES - IDX_HEAD_DIM, 0)))
    iw_t = p_idx[:, IDX_HEAD_DIM:IDX_HEAD_DIM + N_IDX_HEADS].T
    y_att = _attention(p, iw_t, ik_lo, ik_hi, km, vm, batch=batch, seq=seq, tq=256, kc=512)
    y_hg, _ = _hgrn(p, lb, ng, s_meta[0], batch=batch, seq=seq, chunk=64, sub=16,
                    heads=HG_HEADS, n_inner=4)

    h2 = _merge(y_att, y_hg, p, h1, bf(w_branch_att[0]), bf(w_branch_hg[0]), bf(w_out[0]),
                row(ln2_g[0]), row(ln2_b[0]), tm=512, tn=512)
    out = _ffn_ln(h2, *f2, tm=FFN_TM, tf=FFN_TF)
    return out.reshape(batch, seq, D_MODEL)
```

```python
import functools
import math

import jax
import jax.numpy as jnp
import numpy as np
from jax import lax
from jax.experimental import pallas as pl
from jax.experimental.pallas import tpu as pltpu

D_MODEL = 2048
N_META = 16
N_ATT_HEADS = 8
N_KV_HEADS = 2
ATT_GROUP = N_ATT_HEADS // N_KV_HEADS
ATT_HEAD_DIM = 128
N_IDX_HEADS = 16
IDX_HEAD_DIM = 64
TOPK = 256
HG_HEADS = 8
HG_DIM = 128
D_FF = 5632
LN_EPS = 1e-5
RMS_EPS = 1e-6
ALPHA = 2.0 ** 0.25

ATT_Q_W = N_ATT_HEADS * ATT_HEAD_DIM
ATT_KV_W = N_KV_HEADS * ATT_HEAD_DIM
IDX_Q_W = N_IDX_HEADS * IDX_HEAD_DIM
HG_W = HG_HEADS * HG_DIM

LANES = 128
SUBLANES = 8
VMEM_LIMIT = 56 * 1024 * 1024

F32 = jnp.float32
BF16 = jnp.bfloat16
INT_MIN = -2 ** 31
F32_LOWEST = float(np.finfo(np.float32).min)
LOG2E = math.log2(math.e)
ATT_Q_SCALE = ATT_HEAD_DIM ** -0.5 * LOG2E
POS_SHIFT = 6
POS_SPLIT = 1 << POS_SHIFT


def _dot(a, b):
    return jnp.dot(a, b, preferred_element_type=F32)


def _dot_nt(a, b):
    return lax.dot_general(a, b, (((1,), (1,)), ((), ())), preferred_element_type=F32)


def _dot_tn(a, b):
    return lax.dot_general(a, b, (((0,), (0,)), ((), ())), preferred_element_type=F32)


def _layer_norm(y, g, b):
    mu = jnp.mean(y, axis=-1, keepdims=True)
    d = y - mu
    var = jnp.mean(d * d, axis=-1, keepdims=True)
    return d * lax.rsqrt(var + LN_EPS) * g + b


def _silu(x):
    return x * jax.nn.sigmoid(x)


def _bf16_parts(c, n=3):
    parts = []
    for _ in range(n):
        p = float(np.asarray(c, np.float32).astype(jnp.bfloat16).astype(np.float32))
        parts.append(p)
        c = c - p
    return parts


FFN_TM = 1024
FFN_TF = 256
LN_ROWS = 128


def _ffn_ln_kernel(x_ref, wg_ref, wu_ref, wd_ref, g_ref, b_ref, o_ref, xb_scr):
    j = pl.program_id(1)

    @pl.when(j == 0)
    def _():
        xb_scr[...] = x_ref[...].astype(BF16)
        o_ref[...] = jnp.zeros_like(o_ref)

    xb = xb_scr[...]
    gate = _dot(xb, wg_ref[...].astype(BF16))
    up = _dot(xb, wu_ref[...].astype(BF16))
    a = (_silu(gate) * up).astype(BF16)
    o_ref[...] += _dot(a, wd_ref[...].astype(BF16))

    @pl.when(j == pl.num_programs(1) - 1)
    def _():
        n_rows = min(LN_ROWS, o_ref.shape[0])

        def ln_rows(r, carry):
            rows = pl.ds(pl.multiple_of(r * n_rows, n_rows), n_rows)
            y = ALPHA * x_ref[rows, :] + 0.5 * o_ref[rows, :]
            o_ref[rows, :] = _layer_norm(y, g_ref[...], b_ref[...])
            return carry

        lax.fori_loop(0, o_ref.shape[0] // n_rows, ln_rows, 0)


def _ffn_ln(x, wg, wu, wd, g, b, *, tm, tf):
    m = x.shape[0]
    grid = (m // tm, D_FF // tf)
    return pl.pallas_call(
        _ffn_ln_kernel,
        out_shape=jax.ShapeDtypeStruct((m, D_MODEL), F32),
        grid=grid,
        in_specs=[
            pl.BlockSpec((tm, D_MODEL), lambda i, j: (i, 0)),
            pl.BlockSpec((D_MODEL, tf), lambda i, j: (0, j)),
            pl.BlockSpec((D_MODEL, tf), lambda i, j: (0, j)),
            pl.BlockSpec((tf, D_MODEL), lambda i, j: (j, 0)),
            pl.BlockSpec((1, D_MODEL), lambda i, j: (0, 0)),
            pl.BlockSpec((1, D_MODEL), lambda i, j: (0, 0)),
        ],
        out_specs=pl.BlockSpec((tm, D_MODEL), lambda i, j: (i, 0)),
        scratch_shapes=[pltpu.VMEM((tm, D_MODEL), BF16)],
        compiler_params=pltpu.CompilerParams(
            dimension_semantics=("parallel", "arbitrary"), vmem_limit_bytes=VMEM_LIMIT),
        name="ffn_ln",
    )(x, wg, wu, wd, g, b)


IN_SPLITS = (ATT_Q_W, ATT_KV_W, ATT_KV_W, IDX_Q_W, IDX_HEAD_DIM, N_IDX_HEADS, HG_W, HG_W, HG_W, HG_W,
             D_MODEL, D_MODEL)
IN_OFFS = tuple(int(v) for v in np.cumsum((0,) + IN_SPLITS))
PROJ_TN = 512
PROJ_LAYOUT = (("aq", IN_OFFS[0], ATT_Q_W), ("iq", IN_OFFS[3], IDX_Q_W), ("hq", IN_OFFS[6], HG_W),
               ("hf", IN_OFFS[7], HG_W), ("hi", IN_OFFS[8], HG_W), ("hg", IN_OFFS[9], HG_W),
               ("ga", IN_OFFS[10], D_MODEL), ("gb", IN_OFFS[11], D_MODEL), ("kv", IN_OFFS[1], 2 * ATT_KV_W))
PROJ_COL = {}
PROJ_TILE_SRC = []
for _name, _src, _width in PROJ_LAYOUT:
    PROJ_COL[_name] = len(PROJ_TILE_SRC) * PROJ_TN
    PROJ_TILE_SRC += [_src + t * PROJ_TN for t in range(_width // PROJ_TN)]
PROJ_W = len(PROJ_TILE_SRC) * PROJ_TN
PROJ_SRC_ALIGN = 16
assert all(s % PROJ_SRC_ALIGN == 0 for s in PROJ_TILE_SRC)


def _proj_all_kernel(src_ref, x_ref, wt_ref, o_ref, xb_scr, *, scaled_tiles, scale):
    @pl.when(pl.program_id(1) == 0)
    def _():
        xb_scr[...] = x_ref[...].astype(BF16)

    acc = _dot_nt(xb_scr[...], wt_ref[...].astype(BF16))
    acc = acc * jnp.where(pl.program_id(1) < scaled_tiles, scale, 1.0)
    o_ref[...] = acc.astype(o_ref.dtype)


def _idx_kernel(x_ref, w_ref, g_ref, b_ref, o_ref):
    p = _dot(x_ref[...].astype(BF16), w_ref[...])
    lane = lax.broadcasted_iota(jnp.int32, p.shape, 1)
    is_k = lane < IDX_HEAD_DIM
    mu = jnp.sum(jnp.where(is_k, p, 0.0), axis=-1, keepdims=True) * (1.0 / IDX_HEAD_DIM)
    d = jnp.where(is_k, p - mu, 0.0)
    var = jnp.sum(d * d, axis=-1, keepdims=True) * (1.0 / IDX_HEAD_DIM)
    kn = d * lax.rsqrt(var + LN_EPS) * g_ref[...] + b_ref[...]
    w_scale = (N_IDX_HEADS ** -0.5) * (IDX_HEAD_DIM ** -0.5)
    o_ref[...] = jnp.where(is_k, kn, p * w_scale)


def _proj_all(xb, w_in_t, *, tm):
    m = xb.shape[0]
    n_tiles = len(PROJ_TILE_SRC)
    return pl.pallas_call(
        functools.partial(_proj_all_kernel, scaled_tiles=ATT_Q_W // PROJ_TN, scale=ATT_Q_SCALE),
        out_shape=jax.ShapeDtypeStruct((m, PROJ_W), BF16),
        grid_spec=pltpu.PrefetchScalarGridSpec(
            num_scalar_prefetch=1,
            grid=(m // tm, n_tiles),
            in_specs=[pl.BlockSpec((tm, D_MODEL), lambda i, j, src: (i, 0)),
                      pl.BlockSpec((pl.Element(PROJ_TN), pl.Element(D_MODEL)),
                                   lambda i, j, src: (src[j] * PROJ_SRC_ALIGN, 0))],
            out_specs=pl.BlockSpec((tm, PROJ_TN), lambda i, j, src: (i, j)),
            scratch_shapes=[pltpu.VMEM((tm, D_MODEL), BF16)],
        ),
        compiler_params=pltpu.CompilerParams(
            dimension_semantics=("parallel", "arbitrary"), vmem_limit_bytes=VMEM_LIMIT),
        name="proj_all",
    )(jnp.asarray([s // PROJ_SRC_ALIGN for s in PROJ_TILE_SRC], jnp.int32), xb, w_in_t)


def _proj_idx(xb, w, g, b, *, tm):
    m = xb.shape[0]
    return pl.pallas_call(
        _idx_kernel,
        out_shape=jax.ShapeDtypeStruct((m, LANES), F32),
        grid=(m // tm,),
        in_specs=[pl.BlockSpec((tm, D_MODEL), lambda i: (i, 0)),
                  pl.BlockSpec((D_MODEL, LANES), lambda i: (0, 0)),
                  pl.BlockSpec((1, LANES), lambda i: (0, 0)),
                  pl.BlockSpec((1, LANES), lambda i: (0, 0))],
        out_specs=pl.BlockSpec((tm, LANES), lambda i: (i, 0)),
        compiler_params=pltpu.CompilerParams(
            dimension_semantics=("parallel",), vmem_limit_bytes=VMEM_LIMIT),
        name="proj_idx",
    )(xb, w, g, b)


ATT_ROW_TILE = 32


def _key_to_f32(key):
    return pltpu.bitcast(key ^ ((key >> 31) & 0x7FFFFFFF), F32)


def _attn_kernel(aq_ref, iq_ref, iwt_ref, iklo_ref, ikhi_ref, k_ref, v_ref, km_ref, vm_ref,
                 o_ref,
                 score_scr, mb_scr, kaug_scr, vaug_scr, qaug_scr, s0_scr, s1_scr, p_scr,
                 macc_scr, m_scr, acc_scr, *, tq, kc, seq):
    qi = pl.program_id(1)
    n_kc = ((qi + 1) * tq + kc - 1) // kc
    kf = float(TOPK)

    @pl.when(qi == 0)
    def _():
        r = lax.broadcasted_iota(jnp.int32, (LANES + seq, LANES), 0)
        lane = lax.broadcasted_iota(jnp.int32, (LANES + seq, LANES), 1)
        pos = jnp.where(r < LANES, r, r - LANES + N_META)
        feat = jnp.where(lane < 3, pos >> POS_SHIFT, jnp.where(lane < 6, pos & (POS_SPLIT - 1), 0))
        feat = feat.astype(F32).astype(BF16)
        ones = jnp.ones((LANES + seq, LANES), BF16)
        for kvh in range(N_KV_HEADS):
            cs = slice(kvh * LANES, (kvh + 1) * LANES)
            kaug_scr[kvh, 0:LANES, 0:LANES] = km_ref[:, cs]
            kaug_scr[kvh, LANES:, 0:LANES] = k_ref[:, cs]
            kaug_scr[kvh, :, LANES:] = feat
            vaug_scr[kvh, 0:LANES, 0:LANES] = vm_ref[:, cs]
            vaug_scr[kvh, LANES:, 0:LANES] = v_ref[:, cs]
            vaug_scr[kvh, :, LANES:] = ones

    qcol = qi * tq + lax.broadcasted_iota(jnp.int32, (1, tq), 1)
    iwt = iwt_ref[...]

    def score_body(j, carry):
        off = pl.multiple_of(j * kc, kc)
        klo = iklo_ref[pl.ds(off, kc), :]
        khi = ikhi_ref[pl.ds(off, kc), :]
        acc = jnp.zeros((kc, tq), F32)
        for p in range(N_IDX_HEADS // 2):
            q2 = iq_ref[:, p * LANES:(p + 1) * LANES]
            acc = acc + jnp.maximum(_dot_nt(klo, q2), 0.0) * iwt[2 * p:2 * p + 1, :]
            acc = acc + jnp.maximum(_dot_nt(khi, q2), 0.0) * iwt[2 * p + 1:2 * p + 2, :]
        krow = off + lax.broadcasted_iota(jnp.int32, (kc, tq), 0)
        score_scr[pl.ds(off, kc), :] = jnp.where(krow <= qcol, acc, -jnp.inf)
        return carry

    lax.fori_loop(0, n_kc, score_body, 0)

    n_acc = 8
    acc_rows = n_acc * SUBLANES

    def count_ge(cand):
        def body(j, acc):
            off = pl.multiple_of(j * kc, kc)
            w = jnp.where(score_scr[pl.ds(off, kc), :] >= cand, 1.0, 0.0)
            return acc + jnp.sum(w.reshape(kc // acc_rows, acc_rows, tq), axis=0)

        acc = lax.fori_loop(0, n_kc, body, jnp.zeros((acc_rows, tq), F32))
        return jnp.sum(acc, axis=0, keepdims=True)

    c0 = count_ge(jnp.zeros((1, tq), F32))
    ok0 = c0 >= kf
    thr0 = jnp.where(ok0, 0, INT_MIN).astype(jnp.int32)
    cnt0 = jnp.where(ok0, c0, 0.0)

    short_row = qcol + 1 < TOPK

    def unsettled(cnt):
        return jnp.sum(jnp.where((cnt == kf) | short_row, 0.0, 1.0))

    n_bits = 31
    group = 4

    def bit_cond(carry):
        i, _, _, pending = carry
        return jnp.logical_and(i < n_bits, pending > 0.0)

    def bit_body(carry):
        i, thr, cnt, _ = carry
        for b in range(group):
            shift = jnp.maximum(n_bits - 1 - i - b, 0)
            bit = jnp.where(i + b < n_bits, jnp.left_shift(jnp.int32(1), shift), 0)
            cand = thr | bit
            c = count_ge(_key_to_f32(cand))
            ok = c >= kf
            thr, cnt = jnp.where(ok, cand, thr), jnp.where(ok, c, cnt)
        return i + group, thr, cnt, unsettled(cnt)

    _, thr, cnt, _ = lax.while_loop(bit_cond, bit_body, (jnp.int32(0), thr0, cnt0, unsettled(cnt0)))
    t_f = jnp.where(thr == INT_MIN, F32_LOWEST, _key_to_f32(thr))

    @pl.when(jnp.max(cnt) > kf)
    def _():
        def gt_body(j, acc):
            off = pl.multiple_of(j * kc, kc)
            w = jnp.where(score_scr[pl.ds(off, kc), :] > t_f, 1.0, 0.0)
            return acc + jnp.sum(w, axis=0, keepdims=True)

        need = kf - lax.fori_loop(0, n_kc, gt_body, jnp.zeros((1, tq), F32))
        lower = (lax.broadcasted_iota(jnp.int32, (kc, kc), 0)
                 >= lax.broadcasted_iota(jnp.int32, (kc, kc), 1)).astype(BF16)

        def tie_body(j, before):
            off = pl.multiple_of(j * kc, kc)
            sc = score_scr[pl.ds(off, kc), :]
            eq = sc == t_f
            eqf = jnp.where(eq, 1.0, 0.0)
            rank = before + _dot(lower, eqf.astype(BF16))
            score_scr[pl.ds(off, kc), :] = jnp.where(eq & (rank > need), -jnp.inf, sc)
            return before + jnp.sum(eqf, axis=0, keepdims=True)

        lax.fori_loop(0, n_kc, tie_body, jnp.zeros((1, tq), F32))

    eye = (lax.broadcasted_iota(jnp.int32, (tq, tq), 0)
           == lax.broadcasted_iota(jnp.int32, (tq, tq), 1)).astype(BF16)

    def mask_body(j, carry):
        off = pl.multiple_of(j * kc, kc)
        sel_t = jnp.where(score_scr[pl.ds(off, kc), :] >= t_f, 1.0, 0.0).astype(BF16)
        sel = _dot_nt(eye, sel_t)
        mb_scr[:, pl.ds(off, kc)] = jnp.where(sel > 0.5, 0.0, -jnp.inf)
        return carry

    lax.fori_loop(0, n_kc, mask_body, 0)

    rt = ATT_ROW_TILE
    lane_q = lax.broadcasted_iota(jnp.int32, (1, LANES), 1)
    meta_mask = jnp.where(lax.broadcasted_iota(jnp.int32, (rt, LANES), 1) < N_META, 0.0, -jnp.inf)

    s_slots = (s0_scr, s1_scr)

    def logits(kvh, krow0, width, slot):
        s_slots[slot][:, 0:width] = _dot_nt(qaug_scr[kvh], kaug_scr[kvh, pl.ds(krow0, width), :])

    def max_tiles(kvh, slot, width, mask_fn):
        s_ref = s_slots[slot]
        for r0 in range(0, tq, rt):
            mb = mask_fn(r0)
            for g in range(ATT_GROUP):
                rows = slice(g * tq + r0, g * tq + r0 + rt)
                mx = s_ref[rows, 0:LANES] + mb[:, 0:LANES]
                for c in range(1, width // LANES):
                    cols = slice(c * LANES, (c + 1) * LANES)
                    mx = jnp.maximum(mx, s_ref[rows, cols] + mb[:, cols])
                macc_scr[rows, :] = jnp.maximum(macc_scr[rows, :], mx)

    def exp_tiles(kvh, slot, width, mask_fn):
        s_ref = s_slots[slot]
        for r0 in range(0, tq, rt):
            mb = mask_fn(r0)
            for g in range(ATT_GROUP):
                rows = slice(g * tq + r0, g * tq + r0 + rt)
                m_row = m_scr[rows, :]
                for c in range(width // LANES):
                    cols = slice(c * LANES, (c + 1) * LANES)
                    p_scr[rows, cols] = jnp.exp2(s_ref[rows, cols] + mb[:, cols] - m_row).astype(BF16)

    def add_pv(kvh, krow0, width):
        vblk = vaug_scr[kvh, pl.ds(krow0, width), :]
        half = ATT_GROUP * tq // 2
        for r0 in (0, half):
            acc_scr[kvh, r0:r0 + half, :] += _dot(p_scr[r0:r0 + half, 0:width], vblk)

    def run_pass(kvh, tiles_fn, with_pv):
        def krow(j):
            return pl.multiple_of(LANES + j * kc, LANES)

        def step(j, slot, lookahead):
            off = pl.multiple_of(j * kc, kc)
            if lookahead:
                logits(kvh, krow(j + 1), kc, 1 - slot)
            tiles_fn(kvh, slot, kc, lambda r0: mb_scr[r0:r0 + rt, pl.ds(off, kc)])
            if with_pv:
                add_pv(kvh, krow(j), kc)

        logits(kvh, 0, LANES, 0)
        tiles_fn(kvh, 0, LANES, lambda r0: meta_mask)
        if with_pv:
            add_pv(kvh, 0, LANES)

            def serial_body(j, carry):
                off = pl.multiple_of(j * kc, kc)
                logits(kvh, krow(j), kc, 0)
                tiles_fn(kvh, 0, kc, lambda r0: mb_scr[r0:r0 + rt, pl.ds(off, kc)])
                add_pv(kvh, krow(j), kc)
                return carry

            lax.fori_loop(0, n_kc, serial_body, 0)
            return
        logits(kvh, krow(0), kc, 0)
        n_pairs = (n_kc - 1) // 2

        def pair_body(i, carry):
            step(2 * i, 0, True)
            step(2 * i + 1, 1, True)
            return carry

        lax.fori_loop(0, n_pairs, pair_body, 0)
        j0 = 2 * n_pairs
        two_left = n_kc - j0 == 2

        @pl.when(two_left)
        def _():
            step(j0, 0, True)
            step(j0 + 1, 1, False)

        @pl.when(jnp.logical_not(two_left))
        def _():
            step(j0, 0, False)

    for h in range(N_ATT_HEADS):
        kvh, g = divmod(h, ATT_GROUP)
        parts = _bf16_parts(2.0 ** (-8.0 * (h + 1) / N_ATT_HEADS) * LOG2E)
        vals = [POS_SPLIT * p for p in parts] + parts
        qfeat = jnp.zeros((1, LANES), F32)
        for i, val in enumerate(vals):
            qfeat = jnp.where(lane_q == i, val, qfeat)
        qaug_scr[kvh, g * tq:(g + 1) * tq, 0:LANES] = aq_ref[:, h * LANES:(h + 1) * LANES]
        qaug_scr[kvh, g * tq:(g + 1) * tq, LANES:] = jnp.broadcast_to(qfeat, (tq, LANES)).astype(BF16)

    for kvh in range(N_KV_HEADS):
        macc_scr[...] = jnp.full_like(macc_scr, -jnp.inf)
        run_pass(kvh, max_tiles, with_pv=False)
        m_scr[...] = jnp.broadcast_to(jnp.max(macc_scr[...], axis=1, keepdims=True), m_scr.shape)
        acc_scr[kvh] = jnp.zeros(acc_scr.shape[1:], F32)
        run_pass(kvh, exp_tiles, with_pv=True)

    for h in range(N_ATT_HEADS):
        kvh, g = divmod(h, ATT_GROUP)
        acc = acc_scr[kvh, g * tq:(g + 1) * tq, :]
        o_ref[:, h * LANES:(h + 1) * LANES] = (acc[:, 0:LANES] / acc[:, LANES:LANES + 1]).astype(o_ref.dtype)


def _attention(p_att, iw_t, ik_lo, ik_hi, km, vm, *, batch, seq, tq, kc):
    m = batch * seq
    nq = seq // tq
    qcol, icol, kcol = PROJ_COL["aq"] // ATT_Q_W, PROJ_COL["iq"] // IDX_Q_W, PROJ_COL["kv"] // ATT_KV_W
    rows4 = ATT_GROUP * tq
    return pl.pallas_call(
        functools.partial(_attn_kernel, tq=tq, kc=kc, seq=seq),
        out_shape=jax.ShapeDtypeStruct((m, ATT_Q_W), BF16),
        grid=(batch, nq),
        in_specs=[
            pl.BlockSpec((tq, ATT_Q_W), lambda b, q: (b * nq + q, qcol)),
            pl.BlockSpec((tq, IDX_Q_W), lambda b, q: (b * nq + q, icol)),
            pl.BlockSpec((N_IDX_HEADS, tq), lambda b, q: (0, b * nq + q)),
            pl.BlockSpec((seq, LANES), lambda b, q: (b, 0)),
            pl.BlockSpec((seq, LANES), lambda b, q: (b, 0)),
            pl.BlockSpec((seq, ATT_KV_W), lambda b, q: (b, kcol)),
            pl.BlockSpec((seq, ATT_KV_W), lambda b, q: (b, kcol + 1)),
            pl.BlockSpec((LANES, ATT_KV_W), lambda b, q: (0, 0)),
            pl.BlockSpec((LANES, ATT_KV_W), lambda b, q: (0, 0)),
        ],
        out_specs=pl.BlockSpec((tq, ATT_Q_W), lambda b, q: (b * nq + q, 0)),
        scratch_shapes=[
            pltpu.VMEM((seq, tq), F32),
            pltpu.VMEM((tq, seq), F32),
            pltpu.VMEM((N_KV_HEADS, LANES + seq, 2 * LANES), BF16),
            pltpu.VMEM((N_KV_HEADS, LANES + seq, 2 * LANES), BF16),
            pltpu.VMEM((N_KV_HEADS, rows4, 2 * LANES), BF16),
            pltpu.VMEM((rows4, kc), F32),
            pltpu.VMEM((rows4, kc), F32),
            pltpu.VMEM((rows4, kc), BF16),
            pltpu.VMEM((rows4, LANES), F32),
            pltpu.VMEM((rows4, LANES), F32),
            pltpu.VMEM((N_KV_HEADS, rows4, 2 * LANES), F32),
        ],
        compiler_params=pltpu.CompilerParams(
            dimension_semantics=("parallel", "arbitrary"), vmem_limit_bytes=VMEM_LIMIT),
        name="dsa_attention",
    )(p_att, p_att, iw_t, ik_lo, ik_hi, p_att, p_att, km, vm)


def _split3(x):
    hi = x.astype(BF16)
    r = x - hi.astype(F32)
    mid = r.astype(BF16)
    lo = (r - mid.astype(F32)).astype(BF16)
    return hi, mid, lo


HG_SAFE_EXPONENT = 80.0


def _hgrn_kernel(hf_ref, hr_q_ref, hr_i_ref, hr_g_ref, lb_ref, ng_ref, s0_ref,
                 y_ref, sT_out_ref, sT_scr, *, chunk, sub, heads, n_inner):
    c_idx = pl.program_id(2)

    @pl.when(c_idx == 0)
    def _():
        sT_scr[...] = s0_ref[...]

    n_sub = chunk // sub
    causal = (lax.broadcasted_iota(jnp.int32, (chunk, chunk), 0)
              >= lax.broadcasted_iota(jnp.int32, (chunk, chunk), 1))
    tri = causal.astype(BF16)
    t_iota = lax.broadcasted_iota(jnp.int32, (sub, LANES), 0)

    def head_inputs(r0, g):
        rows, cs = pl.ds(r0, chunk), slice(g * HG_DIM, (g + 1) * HG_DIM)
        lb = lb_ref[:, cs]
        q = _silu(hr_q_ref[rows, cs].astype(F32))
        fg = lb + (1.0 - lb) * jax.nn.sigmoid(hf_ref[rows, cs].astype(F32))
        v = hr_i_ref[rows, cs].astype(F32)
        l_hi, l_mid, l_lo = _split3(jnp.log(fg))
        b = _dot(tri, l_hi) + _dot(tri, l_mid) + _dot(tri, l_lo)
        return q, 1.0 - fg, v, b

    def carry_state(g, q, kk, vb, b):
        sT = sT_scr[g]
        b_last = b[chunk - 1:chunk, :]
        qe = (q * jnp.exp(b)).astype(BF16)
        o_state = _dot_nt(qe, sT.astype(BF16))
        khat = (kk * jnp.exp(b_last - b)).astype(BF16)
        sT_scr[g] = sT * jnp.exp(b_last) + _dot_tn(vb, khat)
        return qe, o_state

    def finish(r0, g, o):
        rows, cs = pl.ds(r0, chunk), slice(g * HG_DIM, (g + 1) * HG_DIM)
        o = o * lax.rsqrt(jnp.mean(o * o, axis=-1, keepdims=True) + RMS_EPS) * ng_ref[:, cs]
        o = o * _silu(hr_g_ref[rows, cs].astype(F32))
        y_ref[rows, cs] = o.astype(y_ref.dtype)

    def factored_chunk(c, carry):
        rows = pl.ds(pl.multiple_of(c * chunk, chunk), chunk)
        hs = [slice(g * HG_DIM, (g + 1) * HG_DIM) for g in range(heads)]
        lb = lb_ref[...]
        q = _silu(hr_q_ref[rows, :].astype(F32))
        fg = lb + (1.0 - lb) * jax.nn.sigmoid(hf_ref[rows, :].astype(F32))
        kk = 1.0 - fg
        vb = hr_i_ref[rows, :]
        l_hi, l_mid, l_lo = _split3(jnp.log(fg))
        b = _dot(tri, l_hi) + _dot(tri, l_mid) + _dot(tri, l_lo)
        b_last = b[chunk - 1:chunk, :]
        qe = (q * jnp.exp(b)).astype(BF16)
        ke = (kk * jnp.exp(-b)).astype(BF16)
        khat = (kk * jnp.exp(b_last - b)).astype(BF16)
        keep = jnp.exp(b_last)
        s_old = [sT_scr[g] for g in range(heads)]
        a = [_dot_nt(qe[:, cs], ke[:, cs]) for cs in hs]
        o_state = [_dot_nt(qe[:, cs], s_old[g].astype(BF16)) for g, cs in enumerate(hs)]
        s_add = [_dot_tn(vb[:, cs], khat[:, cs]) for cs in hs]
        a = [jnp.where(causal, x, 0.0).astype(BF16) for x in a]
        o = [o_state[g] + _dot(a[g], vb[:, cs]) for g, cs in enumerate(hs)]
        for g, cs in enumerate(hs):
            sT_scr[g] = s_old[g] * keep[:, cs] + s_add[g]
        o = [x * lax.rsqrt(jnp.mean(x * x, axis=-1, keepdims=True) + RMS_EPS) for x in o]
        o = jnp.concatenate(o, axis=1) * ng_ref[...] * _silu(hr_g_ref[rows, :].astype(F32))
        y_ref[rows, :] = o.astype(y_ref.dtype)
        return carry

    def guarded_chunk(c, carry):
        r0 = pl.multiple_of(c * chunk, chunk)
        for g in range(heads):
            q, kk, v, b = head_inputs(r0, g)
            vb = v.astype(BF16)
            _, o_state = carry_state(g, q, kk, vb, b)
            rows = []
            for i in range(n_sub):
                s0, s1 = i * sub, (i + 1) * sub
                bs, qs, ks, vs = b[s0:s1], q[s0:s1], kk[s0:s1], v[s0:s1]
                o_i = o_state[s0:s1]
                if i > 0:
                    r_i = b[s0 - 1:s0, :]
                    qt = (qs * jnp.exp(bs - r_i)).astype(BF16)
                    kt = (kk[:s0] * jnp.exp(r_i - b[:s0])).astype(BF16)
                    a_off = _dot_nt(qt, kt)
                    o_i = o_i + _dot(a_off.astype(BF16), vb[:s0])
                for s in range(sub):
                    e = jnp.exp(jnp.where(t_iota >= s, bs - bs[s:s + 1], -jnp.inf))
                    a_col = jnp.sum(qs * e * ks[s:s + 1], axis=1, keepdims=True)
                    o_i = o_i + a_col * vs[s:s + 1]
                rows.append(o_i)
            finish(r0, g, jnp.concatenate(rows, axis=0) if n_sub > 1 else rows[0])
        return carry

    factoring_safe = jnp.max(-jnp.log(lb_ref[...])) * chunk <= HG_SAFE_EXPONENT

    @pl.when(factoring_safe)
    def _():
        lax.fori_loop(0, n_inner, factored_chunk, 0)

    @pl.when(jnp.logical_not(factoring_safe))
    def _():
        lax.fori_loop(0, n_inner, guarded_chunk, 0)

    @pl.when(c_idx == pl.num_programs(2) - 1)
    def _():
        sT_out_ref[0] = sT_scr[...]


def _hgrn(p, lb, ng, s0, *, batch, seq, chunk, sub, heads, n_inner):
    m = batch * seq
    rows = chunk * n_inner
    nc = seq // rows
    hw = heads * HG_DIM
    nhg = HG_HEADS // heads
    cf, cq, ci, cg = (PROJ_COL[k] // hw for k in ("hf", "hq", "hi", "hg"))
    return pl.pallas_call(
        functools.partial(_hgrn_kernel, chunk=chunk, sub=sub, heads=heads, n_inner=n_inner),
        out_shape=(jax.ShapeDtypeStruct((m, HG_W), BF16),
                   jax.ShapeDtypeStruct((batch, HG_HEADS, HG_DIM, HG_DIM), F32)),
        grid=(batch, nhg, nc),
        in_specs=[
            pl.BlockSpec((rows, hw), lambda b, h, c: (b * nc + c, cf + h)),
            pl.BlockSpec((rows, hw), lambda b, h, c: (b * nc + c, cq + h)),
            pl.BlockSpec((rows, hw), lambda b, h, c: (b * nc + c, ci + h)),
            pl.BlockSpec((rows, hw), lambda b, h, c: (b * nc + c, cg + h)),
            pl.BlockSpec((1, hw), lambda b, h, c: (0, h)),
            pl.BlockSpec((1, hw), lambda b, h, c: (0, h)),
            pl.BlockSpec((heads, HG_DIM, HG_DIM), lambda b, h, c: (h, 0, 0)),
        ],
        out_specs=(
            pl.BlockSpec((rows, hw), lambda b, h, c: (b * nc + c, h)),
            pl.BlockSpec((1, heads, HG_DIM, HG_DIM), lambda b, h, c: (b, h, 0, 0)),
        ),
        scratch_shapes=[pltpu.VMEM((heads, HG_DIM, HG_DIM), F32)],
        compiler_params=pltpu.CompilerParams(
            dimension_semantics=("parallel", "parallel", "arbitrary"), vmem_limit_bytes=VMEM_LIMIT),
        name="hgrn2",
    )(p, p, p, p, lb, ng, s0)


def _merge_kernel(ya_ref, yh_ref, gts_a_ref, gts_h_ref, h_ref, wa_ref, wh_ref, wo_ref, g_ref, b_ref,
                  o_ref, acc_scr):
    j = pl.program_id(1)

    @pl.when(j == 0)
    def _():
        acc_scr[...] = jnp.zeros_like(acc_scr)

    ga = jax.nn.sigmoid(gts_a_ref[...].astype(F32))
    gh = jax.nn.sigmoid(gts_h_ref[...].astype(F32))
    merged = ga * _dot(ya_ref[...], wa_ref[...]) + gh * _dot(yh_ref[...], wh_ref[...])
    acc_scr[...] += _dot(merged.astype(BF16), wo_ref[...])

    @pl.when(j == pl.num_programs(1) - 1)
    def _():
        o_ref[...] = _layer_norm(ALPHA * h_ref[...] + acc_scr[...], g_ref[...], b_ref[...])


def _merge(ya, yh, gates, h1, wa, wh, wo, g, b, *, tm, tn):
    m = ya.shape[0]
    nj = D_MODEL // tn
    ca, cb = PROJ_COL["ga"] // tn, PROJ_COL["gb"] // tn
    return pl.pallas_call(
        _merge_kernel,
        out_shape=jax.ShapeDtypeStruct((m, D_MODEL), F32),
        grid=(m // tm, nj),
        in_specs=[
            pl.BlockSpec((tm, ATT_Q_W), lambda i, j: (i, 0)),
            pl.BlockSpec((tm, HG_W), lambda i, j: (i, 0)),
            pl.BlockSpec((tm, tn), lambda i, j: (i, ca + j)),
            pl.BlockSpec((tm, tn), lambda i, j: (i, cb + j)),
            pl.BlockSpec((tm, D_MODEL), lambda i, j: (i, 0)),
            pl.BlockSpec((ATT_Q_W, tn), lambda i, j: (0, j)),
            pl.BlockSpec((HG_W, tn), lambda i, j: (0, j)),
            pl.BlockSpec((tn, D_MODEL), lambda i, j: (j, 0)),
            pl.BlockSpec((1, D_MODEL), lambda i, j: (0, 0)),
            pl.BlockSpec((1, D_MODEL), lambda i, j: (0, 0)),
        ],
        out_specs=pl.BlockSpec((tm, D_MODEL), lambda i, j: (i, 0)),
        scratch_shapes=[pltpu.VMEM((tm, D_MODEL), F32)],
        compiler_params=pltpu.CompilerParams(
            dimension_semantics=("parallel", "arbitrary"), vmem_limit_bytes=VMEM_LIMIT),
        name="merge_ln",
    )(ya, yh, gates, gates, h1, wa, wh, wo, g, b)


def kernel(x, meta, ffn1_w_gate, ffn1_w_up, ffn1_w_down, ln1_g, ln1_b, w_in, idx_k_norm_g, idx_k_norm_b,
           hg_lb_logits, hg_norm_g, w_branch_att, w_branch_hg, w_out, ln2_g, ln2_b,
           ffn2_w_gate, ffn2_w_up, ffn2_w_down, ln3_g, ln3_b):
    batch, seq, _ = x.shape
    m = batch * seq
    xr = x.reshape(m, D_MODEL)
    bf = lambda w: w.astype(BF16)
    row = lambda v: v.reshape(1, -1)

    w_in_t = w_in[0].T
    w_idx = bf(jnp.pad(w_in_t[IN_OFFS[4]:IN_OFFS[6]].T, ((0, 0), (0, LANES - IDX_HEAD_DIM - N_IDX_HEADS))))
    idx_g = jnp.pad(idx_k_norm_g[0], (0, LANES - IDX_HEAD_DIM)).reshape(1, LANES)
    idx_b = jnp.pad(idx_k_norm_b[0], (0, LANES - IDX_HEAD_DIM)).reshape(1, LANES)
    lb = jnp.cumsum(jax.nn.softmax(hg_lb_logits.astype(F32), axis=0), axis=0)[0].reshape(1, HG_W)
    ng = hg_norm_g[0].reshape(1, HG_W)

    f1 = (ffn1_w_gate[0], ffn1_w_up[0], ffn1_w_down[0], row(ln1_g[0]), row(ln1_b[0]))
    f2 = (ffn2_w_gate[0], ffn2_w_up[0], ffn2_w_down[0], row(ln3_g[0]), row(ln3_b[0]))

    hm = _ffn_ln(meta.astype(F32), *f1, tm=N_META, tf=FFN_TF)
    pm = _proj_all(hm, w_in_t, tm=N_META)
    kv0 = PROJ_COL["kv"]
    km = jnp.pad(pm[:, kv0:kv0 + ATT_KV_W], ((0, LANES - N_META), (0, 0)))
    vm = jnp.pad(pm[:, kv0 + ATT_KV_W:kv0 + 2 * ATT_KV_W], ((0, LANES - N_META), (0, 0)))
    s_zero = jnp.zeros((HG_HEADS, HG_DIM, HG_DIM), F32)
    _, s_meta = _hgrn(pm, lb, ng, s_zero, batch=1, seq=N_META, chunk=N_META, sub=N_META,
                      heads=HG_HEADS, n_inner=1)

    h1 = _ffn_ln(xr, *f1, tm=FFN_TM, tf=FFN_TF)
    p = _proj_all(h1, w_in_t, tm=1024)
    p_idx = _proj_idx(h1, w_idx, idx_g, idx_b, tm=1024)

    ikn = p_idx[:, :IDX_HEAD_DIM].astype(BF16)
    ik_lo = jnp.pad(ikn, ((0, 0), (0, LANES - IDX_HEAD_DIM)))
    ik_hi = jnp.pad(ikn, ((0, 0), (LANES - IDX_HEAD_DIM, 0)))
    iw_t = p_idx[:, IDX_HEAD_DIM:IDX_HEAD_DIM + N_IDX_HEADS].T
    y_att = _attention(p, iw_t, ik_lo, ik_hi, km, vm, batch=batch, seq=seq, tq=256, kc=512)
    y_hg, _ = _hgrn(p, lb, ng, s_meta[0], batch=batch, seq=seq, chunk=64, sub=16,
                    heads=HG_HEADS, n_inner=4)

    h2 = _merge(y_att, y_hg, p, h1, bf(w_branch_att[0]), bf(w_branch_hg[0]), bf(w_out[0]),
                row(ln2_g[0]), row(ln2_b[0]), tm=512, tn=512)
    out = _ffn_ln(h2, *f2, tm=FFN_TM, tf=FFN_TF)
    return out.reshape(batch, seq, D_MODEL)
```

```python
import functools
import math

import jax
import jax.numpy as jnp
import numpy as np
from jax import lax
from jax.experimental import pallas as pl
from jax.experimental.pallas import tpu as pltpu

D_MODEL = 2048
N_META = 16
N_ATT_HEADS = 8
N_KV_HEADS = 2
ATT_GROUP = N_ATT_HEADS // N_KV_HEADS
ATT_HEAD_DIM = 128
N_IDX_HEADS = 16
IDX_HEAD_DIM = 64
TOPK = 256
HG_HEADS = 8
HG_DIM = 128
D_FF = 5632
LN_EPS = 1e-5
RMS_EPS = 1e-6
ALPHA = 2.0 ** 0.25

ATT_Q_W = N_ATT_HEADS * ATT_HEAD_DIM
ATT_KV_W = N_KV_HEADS * ATT_HEAD_DIM
IDX_Q_W = N_IDX_HEADS * IDX_HEAD_DIM
HG_W = HG_HEADS * HG_DIM

LANES = 128
SUBLANES = 8
VMEM_LIMIT = 56 * 1024 * 1024

F32 = jnp.float32
BF16 = jnp.bfloat16
INT_MIN = -2 ** 31
F32_LOWEST = float(np.finfo(np.float32).min)
LOG2E = math.log2(math.e)
ATT_Q_SCALE = ATT_HEAD_DIM ** -0.5 * LOG2E
POS_SHIFT = 6
POS_SPLIT = 1 << POS_SHIFT


def _dot(a, b):
    return jnp.dot(a, b, preferred_element_type=F32)


def _dot_nt(a, b):
    return lax.dot_general(a, b, (((1,), (1,)), ((), ())), preferred_element_type=F32)


def _dot_tn(a, b):
    return lax.dot_general(a, b, (((0,), (0,)), ((), ())), preferred_element_type=F32)


def _layer_norm(y, g, b):
    mu = jnp.mean(y, axis=-1, keepdims=True)
    d = y - mu
    var = jnp.mean(d * d, axis=-1, keepdims=True)
    return d * lax.rsqrt(var + LN_EPS) * g + b


def _silu(x):
    return x * jax.nn.sigmoid(x)


def _bf16_parts(c, n=3):
    parts = []
    for _ in range(n):
        p = float(np.asarray(c, np.float32).astype(jnp.bfloat16).astype(np.float32))
        parts.append(p)
        c = c - p
    return parts


FFN_TM = 1024
FFN_TF = 256
LN_ROWS = 128


def _ffn_ln_kernel(x_ref, wg_ref, wu_ref, wd_ref, g_ref, b_ref, o_ref, xb_scr):
    j = pl.program_id(1)

    @pl.when(j == 0)
    def _():
        xb_scr[...] = x_ref[...].astype(BF16)
        o_ref[...] = jnp.zeros_like(o_ref)

    xb = xb_scr[...]
    gate = _dot(xb, wg_ref[...].astype(BF16))
    up = _dot(xb, wu_ref[...].astype(BF16))
    a = (_silu(gate) * up).astype(BF16)
    o_ref[...] += _dot(a, wd_ref[...].astype(BF16))

    @pl.when(j == pl.num_programs(1) - 1)
    def _():
        n_rows = min(LN_ROWS, o_ref.shape[0])

        def ln_rows(r, carry):
            rows = pl.ds(pl.multiple_of(r * n_rows, n_rows), n_rows)
            y = ALPHA * x_ref[rows, :] + 0.5 * o_ref[rows, :]
            o_ref[rows, :] = _layer_norm(y, g_ref[...], b_ref[...])
            return carry

        lax.fori_loop(0, o_ref.shape[0] // n_rows, ln_rows, 0)


def _ffn_ln(x, wg, wu, wd, g, b, *, tm, tf):
    m = x.shape[0]
    grid = (m // tm, D_FF // tf)
    return pl.pallas_call(
        _ffn_ln_kernel,
        out_shape=jax.ShapeDtypeStruct((m, D_MODEL), F32),
        grid=grid,
        in_specs=[
            pl.BlockSpec((tm, D_MODEL), lambda i, j: (i, 0)),
            pl.BlockSpec((D_MODEL, tf), lambda i, j: (0, j)),
            pl.BlockSpec((D_MODEL, tf), lambda i, j: (0, j)),
            pl.BlockSpec((tf, D_MODEL), lambda i, j: (j, 0)),
            pl.BlockSpec((1, D_MODEL), lambda i, j: (0, 0)),
            pl.BlockSpec((1, D_MODEL), lambda i, j: (0, 0)),
        ],
        out_specs=pl.BlockSpec((tm, D_MODEL), lambda i, j: (i, 0)),
        scratch_shapes=[pltpu.VMEM((tm, D_MODEL), BF16)],
        compiler_params=pltpu.CompilerParams(
            dimension_semantics=("parallel", "arbitrary"), vmem_limit_bytes=VMEM_LIMIT),
        name="ffn_ln",
    )(x, wg, wu, wd, g, b)


IN_SPLITS = (ATT_Q_W, ATT_KV_W, ATT_KV_W, IDX_Q_W, IDX_HEAD_DIM, N_IDX_HEADS, HG_W, HG_W, HG_W, HG_W,
             D_MODEL, D_MODEL)
IN_OFFS = tuple(int(v) for v in np.cumsum((0,) + IN_SPLITS))
PROJ_TN = 512
PROJ_LAYOUT = (("aq", IN_OFFS[0], ATT_Q_W), ("iq", IN_OFFS[3], IDX_Q_W), ("hq", IN_OFFS[6], HG_W),
               ("hf", IN_OFFS[7], HG_W), ("hi", IN_OFFS[8], HG_W), ("hg", IN_OFFS[9], HG_W),
               ("ga", IN_OFFS[10], D_MODEL), ("gb", IN_OFFS[11], D_MODEL), ("kv", IN_OFFS[1], 2 * ATT_KV_W))
PROJ_COL = {}
PROJ_TILE_SRC = []
for _name, _src, _width in PROJ_LAYOUT:
    PROJ_COL[_name] = len(PROJ_TILE_SRC) * PROJ_TN
    PROJ_TILE_SRC += [_src + t * PROJ_TN for t in range(_width // PROJ_TN)]
PROJ_W = len(PROJ_TILE_SRC) * PROJ_TN
PROJ_SRC_ALIGN = 16
assert all(s % PROJ_SRC_ALIGN == 0 for s in PROJ_TILE_SRC)


def _proj_all_kernel(src_ref, x_ref, wt_ref, o_ref, xb_scr, *, scaled_tiles, scale):
    @pl.when(pl.program_id(1) == 0)
    def _():
        xb_scr[...] = x_ref[...].astype(BF16)

    acc = _dot_nt(xb_scr[...], wt_ref[...].astype(BF16))
    acc = acc * jnp.where(pl.program_id(1) < scaled_tiles, scale, 1.0)
    o_ref[...] = acc.astype(o_ref.dtype)


def _idx_kernel(x_ref, w_ref, g_ref, b_ref, o_ref):
    p = _dot(x_ref[...].astype(BF16), w_ref[...])
    lane = lax.broadcasted_iota(jnp.int32, p.shape, 1)
    is_k = lane < IDX_HEAD_DIM
    mu = jnp.sum(jnp.where(is_k, p, 0.0), axis=-1, keepdims=True) * (1.0 / IDX_HEAD_DIM)
    d = jnp.where(is_k, p - mu, 0.0)
    var = jnp.sum(d * d, axis=-1, keepdims=True) * (1.0 / IDX_HEAD_DIM)
    kn = d * lax.rsqrt(var + LN_EPS) * g_ref[...] + b_ref[...]
    w_scale = (N_IDX_HEADS ** -0.5) * (IDX_HEAD_DIM ** -0.5)
    o_ref[...] = jnp.where(is_k, kn, p * w_scale)


def _proj_all(xb, w_in_t, *, tm):
    m = xb.shape[0]
    n_tiles = len(PROJ_TILE_SRC)
    return pl.pallas_call(
        functools.partial(_proj_all_kernel, scaled_tiles=ATT_Q_W // PROJ_TN, scale=ATT_Q_SCALE),
        out_shape=jax.ShapeDtypeStruct((m, PROJ_W), BF16),
        grid_spec=pltpu.PrefetchScalarGridSpec(
            num_scalar_prefetch=1,
            grid=(m // tm, n_tiles),
            in_specs=[pl.BlockSpec((tm, D_MODEL), lambda i, j, src: (i, 0)),
                      pl.BlockSpec((pl.Element(PROJ_TN), pl.Element(D_MODEL)),
                                   lambda i, j, src: (src[j] * PROJ_SRC_ALIGN, 0))],
            out_specs=pl.BlockSpec((tm, PROJ_TN), lambda i, j, src: (i, j)),
            scratch_shapes=[pltpu.VMEM((tm, D_MODEL), BF16)],
        ),
        compiler_params=pltpu.CompilerParams(
            dimension_semantics=("parallel", "arbitrary"), vmem_limit_bytes=VMEM_LIMIT),
        name="proj_all",
    )(jnp.asarray([s // PROJ_SRC_ALIGN for s in PROJ_TILE_SRC], jnp.int32), xb, w_in_t)


def _proj_idx(xb, w, g, b, *, tm):
    m = xb.shape[0]
    return pl.pallas_call(
        _idx_kernel,
        out_shape=jax.ShapeDtypeStruct((m, LANES), F32),
        grid=(m // tm,),
        in_specs=[pl.BlockSpec((tm, D_MODEL), lambda i: (i, 0)),
                  pl.BlockSpec((D_MODEL, LANES), lambda i: (0, 0)),
                  pl.BlockSpec((1, LANES), lambda i: (0, 0)),
                  pl.BlockSpec((1, LANES), lambda i: (0, 0))],
        out_specs=pl.BlockSpec((tm, LANES), lambda i: (i, 0)),
        compiler_params=pltpu.CompilerParams(
            dimension_semantics=("parallel",), vmem_limit_bytes=VMEM_LIMIT),
        name="proj_idx",
    )(xb, w, g, b)


ATT_ROW_TILE = 32
ATT_BOUND_SLACK = 45.0


def _key_to_f32(key):
    return pltpu.bitcast(key ^ ((key >> 31) & 0x7FFFFFFF), F32)


def _attn_kernel(aq_ref, iq_ref, iwt_ref, iklo_ref, ikhi_ref, k_ref, v_ref, km_ref, vm_ref,
                 o_ref,
                 score_scr, mb_scr, kaug_scr, vaug_scr, qaug_scr, s0_scr, s1_scr, p_scr,
                 macc_scr, m_scr, acc_scr, kn_scr, bound_scr, *, tq, kc, seq):
    qi = pl.program_id(1)
    n_kc = ((qi + 1) * tq + kc - 1) // kc
    kf = float(TOPK)

    @pl.when(qi == 0)
    def _():
        r = lax.broadcasted_iota(jnp.int32, (LANES + seq, LANES), 0)
        lane = lax.broadcasted_iota(jnp.int32, (LANES + seq, LANES), 1)
        pos = jnp.where(r < LANES, r, r - LANES + N_META)
        feat = jnp.where(lane < 3, pos >> POS_SHIFT, jnp.where(lane < 6, pos & (POS_SPLIT - 1), 0))
        feat = feat.astype(F32).astype(BF16)
        ones = jnp.ones((LANES + seq, LANES), BF16)
        for kvh in range(N_KV_HEADS):
            cs = slice(kvh * LANES, (kvh + 1) * LANES)
            kaug_scr[kvh, 0:LANES, 0:LANES] = km_ref[:, cs]
            kaug_scr[kvh, LANES:, 0:LANES] = k_ref[:, cs]
            kaug_scr[kvh, :, LANES:] = feat
            vaug_scr[kvh, 0:LANES, 0:LANES] = vm_ref[:, cs]
            vaug_scr[kvh, LANES:, 0:LANES] = v_ref[:, cs]
            vaug_scr[kvh, :, LANES:] = ones
            kf32 = kaug_scr[kvh, :, 0:LANES].astype(F32)
            k_norm2 = jnp.max(jnp.sum(kf32 * kf32, axis=1, keepdims=True))
            kn_scr[kvh] = jnp.broadcast_to(k_norm2, kn_scr.shape[1:])

    qcol = qi * tq + lax.broadcasted_iota(jnp.int32, (1, tq), 1)
    iwt = iwt_ref[...]

    def score_body(j, carry):
        off = pl.multiple_of(j * kc, kc)
        klo = iklo_ref[pl.ds(off, kc), :]
        khi = ikhi_ref[pl.ds(off, kc), :]
        acc = jnp.zeros((kc, tq), F32)
        for p in range(N_IDX_HEADS // 2):
            q2 = iq_ref[:, p * LANES:(p + 1) * LANES]
            acc = acc + jnp.maximum(_dot_nt(klo, q2), 0.0) * iwt[2 * p:2 * p + 1, :]
            acc = acc + jnp.maximum(_dot_nt(khi, q2), 0.0) * iwt[2 * p + 1:2 * p + 2, :]
        krow = off + lax.broadcasted_iota(jnp.int32, (kc, tq), 0)
        score_scr[pl.ds(off, kc), :] = jnp.where(krow <= qcol, acc, -jnp.inf)
        return carry

    lax.fori_loop(0, n_kc, score_body, 0)

    n_acc = 8
    acc_rows = n_acc * SUBLANES

    def count_ge(cand):
        def body(j, acc):
            off = pl.multiple_of(j * kc, kc)
            w = jnp.where(score_scr[pl.ds(off, kc), :] >= cand, 1.0, 0.0)
            return acc + jnp.sum(w.reshape(kc // acc_rows, acc_rows, tq), axis=0)

        acc = lax.fori_loop(0, n_kc, body, jnp.zeros((acc_rows, tq), F32))
        return jnp.sum(acc, axis=0, keepdims=True)

    c0 = count_ge(jnp.zeros((1, tq), F32))
    ok0 = c0 >= kf
    thr0 = jnp.where(ok0, 0, INT_MIN).astype(jnp.int32)
    cnt0 = jnp.where(ok0, c0, 0.0)

    short_row = qcol + 1 < TOPK

    def unsettled(cnt):
        return jnp.sum(jnp.where((cnt == kf) | short_row, 0.0, 1.0))

    n_bits = 31
    group = 4

    def bit_cond(carry):
        i, _, _, pending = carry
        return jnp.logical_and(i < n_bits, pending > 0.0)

    def bit_body(carry):
        i, thr, cnt, _ = carry
        for b in range(group):
            shift = jnp.maximum(n_bits - 1 - i - b, 0)
            bit = jnp.where(i + b < n_bits, jnp.left_shift(jnp.int32(1), shift), 0)
            cand = thr | bit
            c = count_ge(_key_to_f32(cand))
            ok = c >= kf
            thr, cnt = jnp.where(ok, cand, thr), jnp.where(ok, c, cnt)
        return i + group, thr, cnt, unsettled(cnt)

    _, thr, cnt, _ = lax.while_loop(bit_cond, bit_body, (jnp.int32(0), thr0, cnt0, unsettled(cnt0)))
    t_f = jnp.where(thr == INT_MIN, F32_LOWEST, _key_to_f32(thr))

    @pl.when(jnp.max(cnt) > kf)
    def _():
        def gt_body(j, acc):
            off = pl.multiple_of(j * kc, kc)
            w = jnp.where(score_scr[pl.ds(off, kc), :] > t_f, 1.0, 0.0)
            return acc + jnp.sum(w, axis=0, keepdims=True)

        need = kf - lax.fori_loop(0, n_kc, gt_body, jnp.zeros((1, tq), F32))
        lower = (lax.broadcasted_iota(jnp.int32, (kc, kc), 0)
                 >= lax.broadcasted_iota(jnp.int32, (kc, kc), 1)).astype(BF16)

        def tie_body(j, before):
            off = pl.multiple_of(j * kc, kc)
            sc = score_scr[pl.ds(off, kc), :]
            eq = sc == t_f
            eqf = jnp.where(eq, 1.0, 0.0)
            rank = before + _dot(lower, eqf.astype(BF16))
            score_scr[pl.ds(off, kc), :] = jnp.where(eq & (rank > need), -jnp.inf, sc)
            return before + jnp.sum(eqf, axis=0, keepdims=True)

        lax.fori_loop(0, n_kc, tie_body, jnp.zeros((1, tq), F32))

    eye = (lax.broadcasted_iota(jnp.int32, (tq, tq), 0)
           == lax.broadcasted_iota(jnp.int32, (tq, tq), 1)).astype(BF16)

    def mask_body(j, last_sel):
        off = pl.multiple_of(j * kc, kc)
        picked = score_scr[pl.ds(off, kc), :] >= t_f
        sel = _dot_nt(eye, jnp.where(picked, 1.0, 0.0).astype(BF16))
        mb_scr[:, pl.ds(off, kc)] = jnp.where(sel > 0.5, 0.0, -jnp.inf)
        krow = (off + lax.broadcasted_iota(jnp.int32, (kc, tq), 0)).astype(F32)
        hit = jnp.where(picked, krow, -1.0)
        return jnp.maximum(last_sel, jnp.max(hit.reshape(kc // SUBLANES, SUBLANES, tq), axis=0))

    last_sel = lax.fori_loop(0, n_kc, mask_body, jnp.full((SUBLANES, tq), -1.0, F32))
    last_pos = jnp.maximum(jnp.max(last_sel, axis=0, keepdims=True) + N_META, N_META - 1.0)

    rt = ATT_ROW_TILE
    lane_q = lax.broadcasted_iota(jnp.int32, (1, LANES), 1)
    meta_mask = jnp.where(lax.broadcasted_iota(jnp.int32, (rt, LANES), 1) < N_META, 0.0, -jnp.inf)

    s_slots = (s0_scr, s1_scr)

    def logits(kvh, krow0, width, slot):
        s_slots[slot][:, 0:width] = _dot_nt(qaug_scr[kvh], kaug_scr[kvh, pl.ds(krow0, width), :])

    def max_tiles(kvh, slot, width, mask_fn):
        s_ref = s_slots[slot]
        for r0 in range(0, tq, rt):
            mb = mask_fn(r0)
            for g in range(ATT_GROUP):
                rows = slice(g * tq + r0, g * tq + r0 + rt)
                mx = s_ref[rows, 0:LANES] + mb[:, 0:LANES]
                for c in range(1, width // LANES):
                    cols = slice(c * LANES, (c + 1) * LANES)
                    mx = jnp.maximum(mx, s_ref[rows, cols] + mb[:, cols])
                macc_scr[rows, :] = jnp.maximum(macc_scr[rows, :], mx)

    def exp_tiles(kvh, slot, width, mask_fn):
        s_ref = s_slots[slot]
        for r0 in range(0, tq, rt):
            mb = mask_fn(r0)
            for g in range(ATT_GROUP):
                rows = slice(g * tq + r0, g * tq + r0 + rt)
                m_row = m_scr[rows, :]
                for c in range(width // LANES):
                    cols = slice(c * LANES, (c + 1) * LANES)
                    p_scr[rows, cols] = jnp.exp2(s_ref[rows, cols] + mb[:, cols] - m_row).astype(BF16)

    def add_pv(kvh, krow0, width):
        vblk = vaug_scr[kvh, pl.ds(krow0, width), :]
        half = ATT_GROUP * tq // 2
        for r0 in (0, half):
            acc_scr[kvh, r0:r0 + half, :] += _dot(p_scr[r0:r0 + half, 0:width], vblk)

    def run_pass(kvh, tiles_fn, with_pv):
        def krow(j):
            return pl.multiple_of(LANES + j * kc, LANES)

        def step(j, slot, lookahead):
            off = pl.multiple_of(j * kc, kc)
            if lookahead:
                logits(kvh, krow(j + 1), kc, 1 - slot)
            tiles_fn(kvh, slot, kc, lambda r0: mb_scr[r0:r0 + rt, pl.ds(off, kc)])
            if with_pv:
                add_pv(kvh, krow(j), kc)

        logits(kvh, 0, LANES, 0)
        tiles_fn(kvh, 0, LANES, lambda r0: meta_mask)
        if with_pv:
            add_pv(kvh, 0, LANES)

            def serial_body(j, carry):
                off = pl.multiple_of(j * kc, kc)
                logits(kvh, krow(j), kc, 0)
                tiles_fn(kvh, 0, kc, lambda r0: mb_scr[r0:r0 + rt, pl.ds(off, kc)])
                add_pv(kvh, krow(j), kc)
                return carry

            lax.fori_loop(0, n_kc, serial_body, 0)
            return
        logits(kvh, krow(0), kc, 0)
        n_pairs = (n_kc - 1) // 2

        def pair_body(i, carry):
            step(2 * i, 0, True)
            step(2 * i + 1, 1, True)
            return carry

        lax.fori_loop(0, n_pairs, pair_body, 0)
        j0 = 2 * n_pairs
        two_left = n_kc - j0 == 2

        @pl.when(two_left)
        def _():
            step(j0, 0, True)
            step(j0 + 1, 1, False)

        @pl.when(jnp.logical_not(two_left))
        def _():
            step(j0, 0, False)

    for h in range(N_ATT_HEADS):
        kvh, g = divmod(h, ATT_GROUP)
        parts = _bf16_parts(2.0 ** (-8.0 * (h + 1) / N_ATT_HEADS) * LOG2E)
        vals = [POS_SPLIT * p for p in parts] + parts
        qfeat = jnp.zeros((1, LANES), F32)
        for i, val in enumerate(vals):
            qfeat = jnp.where(lane_q == i, val, qfeat)
        qaug_scr[kvh, g * tq:(g + 1) * tq, 0:LANES] = aq_ref[:, h * LANES:(h + 1) * LANES]
        qaug_scr[kvh, g * tq:(g + 1) * tq, LANES:] = jnp.broadcast_to(qfeat, (tq, LANES)).astype(BF16)

    a_pos = jnp.floor(last_pos * (1.0 / POS_SPLIT))
    b_pos = last_pos - a_pos * POS_SPLIT
    to_rows = lambda v: _dot_nt(eye, jnp.broadcast_to(v, (LANES, tq)).astype(BF16))
    last_pos_rows = to_rows(a_pos) * POS_SPLIT + to_rows(b_pos)
    slack = jnp.float32(0.0)
    for kvh in range(N_KV_HEADS):
        qf = qaug_scr[kvh, :, 0:LANES].astype(F32)
        qk_max = jnp.sqrt(jnp.sum(qf * qf, axis=1, keepdims=True) * kn_scr[kvh, 0:1, 0:1])
        slack = jnp.maximum(slack, jnp.max(qk_max))
        for g in range(ATT_GROUP):
            rows = slice(g * tq, (g + 1) * tq)
            c = 2.0 ** (-8.0 * (kvh * ATT_GROUP + g + 1) / N_ATT_HEADS) * LOG2E
            bound_scr[kvh, rows, :] = c * last_pos_rows + qk_max[rows]
    bound_ok = slack <= ATT_BOUND_SLACK

    for kvh in range(N_KV_HEADS):
        @pl.when(bound_ok)
        def _():
            m_scr[...] = bound_scr[kvh]

        @pl.when(jnp.logical_not(bound_ok))
        def _():
            macc_scr[...] = jnp.full_like(macc_scr, -jnp.inf)
            run_pass(kvh, max_tiles, with_pv=False)
            m_scr[...] = jnp.broadcast_to(jnp.max(macc_scr[...], axis=1, keepdims=True), m_scr.shape)

        acc_scr[kvh] = jnp.zeros(acc_scr.shape[1:], F32)
        run_pass(kvh, exp_tiles, with_pv=True)

    for h in range(N_ATT_HEADS):
        kvh, g = divmod(h, ATT_GROUP)
        acc = acc_scr[kvh, g * tq:(g + 1) * tq, :]
        o_ref[:, h * LANES:(h + 1) * LANES] = (acc[:, 0:LANES] / acc[:, LANES:LANES + 1]).astype(o_ref.dtype)


def _attention(p_att, iw_t, ik_lo, ik_hi, km, vm, *, batch, seq, tq, kc):
    m = batch * seq
    nq = seq // tq
    qcol, icol, kcol = PROJ_COL["aq"] // ATT_Q_W, PROJ_COL["iq"] // IDX_Q_W, PROJ_COL["kv"] // ATT_KV_W
    rows4 = ATT_GROUP * tq
    return pl.pallas_call(
        functools.partial(_attn_kernel, tq=tq, kc=kc, seq=seq),
        out_shape=jax.ShapeDtypeStruct((m, ATT_Q_W), BF16),
        grid=(batch, nq),
        in_specs=[
            pl.BlockSpec((tq, ATT_Q_W), lambda b, q: (b * nq + q, qcol)),
            pl.BlockSpec((tq, IDX_Q_W), lambda b, q: (b * nq + q, icol)),
            pl.BlockSpec((N_IDX_HEADS, tq), lambda b, q: (0, b * nq + q)),
            pl.BlockSpec((seq, LANES), lambda b, q: (b, 0)),
            pl.BlockSpec((seq, LANES), lambda b, q: (b, 0)),
            pl.BlockSpec((seq, ATT_KV_W), lambda b, q: (b, kcol)),
            pl.BlockSpec((seq, ATT_KV_W), lambda b, q: (b, kcol + 1)),
            pl.BlockSpec((LANES, ATT_KV_W), lambda b, q: (0, 0)),
            pl.BlockSpec((LANES, ATT_KV_W), lambda b, q: (0, 0)),
        ],
        out_specs=pl.BlockSpec((tq, ATT_Q_W), lambda b, q: (b * nq + q, 0)),
        scratch_shapes=[
            pltpu.VMEM((seq, tq), F32),
            pltpu.VMEM((tq, seq), F32),
            pltpu.VMEM((N_KV_HEADS, LANES + seq, 2 * LANES), BF16),
            pltpu.VMEM((N_KV_HEADS, LANES + seq, 2 * LANES), BF16),
            pltpu.VMEM((N_KV_HEADS, rows4, 2 * LANES), BF16),
            pltpu.VMEM((rows4, kc), F32),
            pltpu.VMEM((rows4, kc), F32),
            pltpu.VMEM((rows4, kc), BF16),
            pltpu.VMEM((rows4, LANES), F32),
            pltpu.VMEM((rows4, LANES), F32),
            pltpu.VMEM((N_KV_HEADS, rows4, 2 * LANES), F32),
            pltpu.VMEM((N_KV_HEADS, SUBLANES, LANES), F32),
            pltpu.VMEM((N_KV_HEADS, rows4, LANES), F32),
        ],
        compiler_params=pltpu.CompilerParams(
            dimension_semantics=("parallel", "arbitrary"), vmem_limit_bytes=VMEM_LIMIT),
        name="dsa_attention",
    )(p_att, p_att, iw_t, ik_lo, ik_hi, p_att, p_att, km, vm)


def _split3(x):
    hi = x.astype(BF16)
    r = x - hi.astype(F32)
    mid = r.astype(BF16)
    lo = (r - mid.astype(F32)).astype(BF16)
    return hi, mid, lo


HG_SAFE_EXPONENT = 80.0


def _hgrn_kernel(hf_ref, hr_q_ref, hr_i_ref, hr_g_ref, lb_ref, ng_ref, s0_ref,
                 y_ref, sT_out_ref, sT_scr, *, chunk, sub, heads, n_inner):
    c_idx = pl.program_id(2)

    @pl.when(c_idx == 0)
    def _():
        sT_scr[...] = s0_ref[...]

    n_sub = chunk // sub
    causal = (lax.broadcasted_iota(jnp.int32, (chunk, chunk), 0)
              >= lax.broadcasted_iota(jnp.int32, (chunk, chunk), 1))
    tri = causal.astype(BF16)
    t_iota = lax.broadcasted_iota(jnp.int32, (sub, LANES), 0)

    def head_inputs(r0, g):
        rows, cs = pl.ds(r0, chunk), slice(g * HG_DIM, (g + 1) * HG_DIM)
        lb = lb_ref[:, cs]
        q = _silu(hr_q_ref[rows, cs].astype(F32))
        fg = lb + (1.0 - lb) * jax.nn.sigmoid(hf_ref[rows, cs].astype(F32))
        v = hr_i_ref[rows, cs].astype(F32)
        l_hi, l_mid, l_lo = _split3(jnp.log(fg))
        b = _dot(tri, l_hi) + _dot(tri, l_mid) + _dot(tri, l_lo)
        return q, 1.0 - fg, v, b

    def carry_state(g, q, kk, vb, b):
        sT = sT_scr[g]
        b_last = b[chunk - 1:chunk, :]
        qe = (q * jnp.exp(b)).astype(BF16)
        o_state = _dot_nt(qe, sT.astype(BF16))
        khat = (kk * jnp.exp(b_last - b)).astype(BF16)
        sT_scr[g] = sT * jnp.exp(b_last) + _dot_tn(vb, khat)
        return qe, o_state

    def finish(r0, g, o):
        rows, cs = pl.ds(r0, chunk), slice(g * HG_DIM, (g + 1) * HG_DIM)
        o = o * lax.rsqrt(jnp.mean(o * o, axis=-1, keepdims=True) + RMS_EPS) * ng_ref[:, cs]
        o = o * _silu(hr_g_ref[rows, cs].astype(F32))
        y_ref[rows, cs] = o.astype(y_ref.dtype)

    def factored_chunk(c, carry):
        rows = pl.ds(pl.multiple_of(c * chunk, chunk), chunk)
        hs = [slice(g * HG_DIM, (g + 1) * HG_DIM) for g in range(heads)]
        lb = lb_ref[...]
        q = _silu(hr_q_ref[rows, :].astype(F32))
        fg = lb + (1.0 - lb) * jax.nn.sigmoid(hf_ref[rows, :].astype(F32))
        kk = 1.0 - fg
        vb = hr_i_ref[rows, :]
        l_hi, l_mid, l_lo = _split3(jnp.log(fg))
        b = _dot(tri, l_hi) + _dot(tri, l_mid) + _dot(tri, l_lo)
        b_last = b[chunk - 1:chunk, :]
        qe = (q * jnp.exp(b)).astype(BF16)
        ke = (kk * jnp.exp(-b)).astype(BF16)
        khat = (kk * jnp.exp(b_last - b)).astype(BF16)
        keep = jnp.exp(b_last)
        s_old = [sT_scr[g] for g in range(heads)]
        a = [_dot_nt(qe[:, cs], ke[:, cs]) for cs in hs]
        o_state = [_dot_nt(qe[:, cs], s_old[g].astype(BF16)) for g, cs in enumerate(hs)]
        s_add = [_dot_tn(vb[:, cs], khat[:, cs]) for cs in hs]
        a = [jnp.where(causal, x, 0.0).astype(BF16) for x in a]
        o = [o_state[g] + _dot(a[g], vb[:, cs]) for g, cs in enumerate(hs)]
        for g, cs in enumerate(hs):
            sT_scr[g] = s_old[g] * keep[:, cs] + s_add[g]
        o = [x * lax.rsqrt(jnp.mean(x * x, axis=-1, keepdims=True) + RMS_EPS) for x in o]
        o = jnp.concatenate(o, axis=1) * ng_ref[...] * _silu(hr_g_ref[rows, :].astype(F32))
        y_ref[rows, :] = o.astype(y_ref.dtype)
        return carry

    def guarded_chunk(c, carry):
        r0 = pl.multiple_of(c * chunk, chunk)
        for g in range(heads):
            q, kk, v, b = head_inputs(r0, g)
            vb = v.astype(BF16)
            _, o_state = carry_state(g, q, kk, vb, b)
            rows = []
            for i in range(n_sub):
                s0, s1 = i * sub, (i + 1) * sub
                bs, qs, ks, vs = b[s0:s1], q[s0:s1], kk[s0:s1], v[s0:s1]
                o_i = o_state[s0:s1]
                if i > 0:
                    r_i = b[s0 - 1:s0, :]
                    qt = (qs * jnp.exp(bs - r_i)).astype(BF16)
                    kt = (kk[:s0] * jnp.exp(r_i - b[:s0])).astype(BF16)
                    a_off = _dot_nt(qt, kt)
                    o_i = o_i + _dot(a_off.astype(BF16), vb[:s0])
                for s in range(sub):
                    e = jnp.exp(jnp.where(t_iota >= s, bs - bs[s:s + 1], -jnp.inf))
                    a_col = jnp.sum(qs * e * ks[s:s + 1], axis=1, keepdims=True)
                    o_i = o_i + a_col * vs[s:s + 1]
                rows.append(o_i)
            finish(r0, g, jnp.concatenate(rows, axis=0) if n_sub > 1 else rows[0])
        return carry

    factoring_safe = jnp.max(-jnp.log(lb_ref[...])) * chunk <= HG_SAFE_EXPONENT

    @pl.when(factoring_safe)
    def _():
        lax.fori_loop(0, n_inner, factored_chunk, 0)

    @pl.when(jnp.logical_not(factoring_safe))
    def _():
        lax.fori_loop(0, n_inner, guarded_chunk, 0)

    @pl.when(c_idx == pl.num_programs(2) - 1)
    def _():
        sT_out_ref[0] = sT_scr[...]


def _hgrn(p, lb, ng, s0, *, batch, seq, chunk, sub, heads, n_inner):
    m = batch * seq
    rows = chunk * n_inner
    nc = seq // rows
    hw = heads * HG_DIM
    nhg = HG_HEADS // heads
    cf, cq, ci, cg = (PROJ_COL[k] // hw for k in ("hf", "hq", "hi", "hg"))
    return pl.pallas_call(
        functools.partial(_hgrn_kernel, chunk=chunk, sub=sub, heads=heads, n_inner=n_inner),
        out_shape=(jax.ShapeDtypeStruct((m, HG_W), BF16),
                   jax.ShapeDtypeStruct((batch, HG_HEADS, HG_DIM, HG_DIM), F32)),
        grid=(batch, nhg, nc),
        in_specs=[
            pl.BlockSpec((rows, hw), lambda b, h, c: (b * nc + c, cf + h)),
            pl.BlockSpec((rows, hw), lambda b, h, c: (b * nc + c, cq + h)),
            pl.BlockSpec((rows, hw), lambda b, h, c: (b * nc + c, ci + h)),
            pl.BlockSpec((rows, hw), lambda b, h, c: (b * nc + c, cg + h)),
            pl.BlockSpec((1, hw), lambda b, h, c: (0, h)),
            pl.BlockSpec((1, hw), lambda b, h, c: (0, h)),
            pl.BlockSpec((heads, HG_DIM, HG_DIM), lambda b, h, c: (h, 0, 0)),
        ],
        out_specs=(
            pl.BlockSpec((rows, hw), lambda b, h, c: (b * nc + c, h)),
            pl.BlockSpec((1, heads, HG_DIM, HG_DIM), lambda b, h, c: (b, h, 0, 0)),
        ),
        scratch_shapes=[pltpu.VMEM((heads, HG_DIM, HG_DIM), F32)],
        compiler_params=pltpu.CompilerParams(
            dimension_semantics=("parallel", "parallel", "arbitrary"), vmem_limit_bytes=VMEM_LIMIT),
        name="hgrn2",
    )(p, p, p, p, lb, ng, s0)


def _merge_kernel(ya_ref, yh_ref, gts_a_ref, gts_h_ref, h_ref, wa_ref, wh_ref, wo_ref, g_ref, b_ref,
                  o_ref, acc_scr):
    j = pl.program_id(1)

    @pl.when(j == 0)
    def _():
        acc_scr[...] = jnp.zeros_like(acc_scr)

    ga = jax.nn.sigmoid(gts_a_ref[...].astype(F32))
    gh = jax.nn.sigmoid(gts_h_ref[...].astype(F32))
    merged = ga * _dot(ya_ref[...], wa_ref[...]) + gh * _dot(yh_ref[...], wh_ref[...])
    acc_scr[...] += _dot(merged.astype(BF16), wo_ref[...])

    @pl.when(j == pl.num_programs(1) - 1)
    def _():
        o_ref[...] = _layer_norm(ALPHA * h_ref[...] + acc_scr[...], g_ref[...], b_ref[...])


def _merge(ya, yh, gates, h1, wa, wh, wo, g, b, *, tm, tn):
    m = ya.shape[0]
    nj = D_MODEL // tn
    ca, cb = PROJ_COL["ga"] // tn, PROJ_COL["gb"] // tn
    return pl.pallas_call(
        _merge_kernel,
        out_shape=jax.ShapeDtypeStruct((m, D_MODEL), F32),
        grid=(m // tm, nj),
        in_specs=[
            pl.BlockSpec((tm, ATT_Q_W), lambda i, j: (i, 0)),
            pl.BlockSpec((tm, HG_W), lambda i, j: (i, 0)),
            pl.BlockSpec((tm, tn), lambda i, j: (i, ca + j)),
            pl.BlockSpec((tm, tn), lambda i, j: (i, cb + j)),
            pl.BlockSpec((tm, D_MODEL), lambda i, j: (i, 0)),
            pl.BlockSpec((ATT_Q_W, tn), lambda i, j: (0, j)),
            pl.BlockSpec((HG_W, tn), lambda i, j: (0, j)),
            pl.BlockSpec((tn, D_MODEL), lambda i, j: (j, 0)),
            pl.BlockSpec((1, D_MODEL), lambda i, j: (0, 0)),
            pl.BlockSpec((1, D_MODEL), lambda i, j: (0, 0)),
        ],
        out_specs=pl.BlockSpec((tm, D_MODEL), lambda i, j: (i, 0)),
        scratch_shapes=[pltpu.VMEM((tm, D_MODEL), F32)],
        compiler_params=pltpu.CompilerParams(
            dimension_semantics=("parallel", "arbitrary"), vmem_limit_bytes=VMEM_LIMIT),
        name="merge_ln",
    )(ya, yh, gates, gates, h1, wa, wh, wo, g, b)


def kernel(x, meta, ffn1_w_gate, ffn1_w_up, ffn1_w_down, ln1_g, ln1_b, w_in, idx_k_norm_g, idx_k_norm_b,
           hg_lb_logits, hg_norm_g, w_branch_att, w_branch_hg, w_out, ln2_g, ln2_b,
           ffn2_w_gate, ffn2_w_up, ffn2_w_down, ln3_g, ln3_b):
    batch, seq, _ = x.shape
    m = batch * seq
    xr = x.reshape(m, D_MODEL)
    bf = lambda w: w.astype(BF16)
    row = lambda v: v.reshape(1, -1)

    w_in_t = w_in[0].T
    w_idx = bf(jnp.pad(w_in_t[IN_OFFS[4]:IN_OFFS[6]].T, ((0, 0), (0, LANES - IDX_HEAD_DIM - N_IDX_HEADS))))
    idx_g = jnp.pad(idx_k_norm_g[0], (0, LANES - IDX_HEAD_DIM)).reshape(1, LANES)
    idx_b = jnp.pad(idx_k_norm_b[0], (0, LANES - IDX_HEAD_DIM)).reshape(1, LANES)
    lb = jnp.cumsum(jax.nn.softmax(hg_lb_logits.astype(F32), axis=0), axis=0)[0].reshape(1, HG_W)
    ng = hg_norm_g[0].reshape(1, HG_W)

    f1 = (ffn1_w_gate[0], ffn1_w_up[0], ffn1_w_down[0], row(ln1_g[0]), row(ln1_b[0]))
    f2 = (ffn2_w_gate[0], ffn2_w_up[0], ffn2_w_down[0], row(ln3_g[0]), row(ln3_b[0]))

    hm = _ffn_ln(meta.astype(F32), *f1, tm=N_META, tf=FFN_TF)
    pm = _proj_all(hm, w_in_t, tm=N_META)
    kv0 = PROJ_COL["kv"]
    km = jnp.pad(pm[:, kv0:kv0 + ATT_KV_W], ((0, LANES - N_META), (0, 0)))
    vm = jnp.pad(pm[:, kv0 + ATT_KV_W:kv0 + 2 * ATT_KV_W], ((0, LANES - N_META), (0, 0)))
    s_zero = jnp.zeros((HG_HEADS, HG_DIM, HG_DIM), F32)
    _, s_meta = _hgrn(pm, lb, ng, s_zero, batch=1, seq=N_META, chunk=N_META, sub=N_META,
                      heads=HG_HEADS, n_inner=1)

    h1 = _ffn_ln(xr, *f1, tm=FFN_TM, tf=FFN_TF)
    p = _proj_all(h1, w_in_t, tm=1024)
    p_idx = _proj_idx(h1, w_idx, idx_g, idx_b, tm=1024)

    ikn = p_idx[:, :IDX_HEAD_DIM].astype(BF16)
    ik_lo = jnp.pad(ikn, ((0, 0), (0, LANES - IDX_HEAD_DIM)))
    ik_hi = jnp.pad(ikn, ((0, 0), (LANES - IDX_HEAD_DIM, 0)))
    iw_t = p_idx[:, IDX_HEAD_DIM:IDX_HEAD_DIM + N_IDX_HEADS].T
    y_att = _attention(p, iw_t, ik_lo, ik_hi, km, vm, batch=batch, seq=seq, tq=256, kc=512)
    y_hg, _ = _hgrn(p, lb, ng, s_meta[0], batch=batch, seq=seq, chunk=64, sub=16,
                    heads=HG_HEADS, n_inner=4)

    h2 = _merge(y_att, y_hg, p, h1, bf(w_branch_att[0]), bf(w_branch_hg[0]), bf(w_out[0]),
                row(ln2_g[0]), row(ln2_b[0]), tm=512, tn=512)
    out = _ffn_ln(h2, *f2, tm=FFN_TM, tf=FFN_TF)
    return out.reshape(batch, seq, D_MODEL)
```

```python
import functools
import math

import jax
import jax.numpy as jnp
import numpy as np
from jax import lax
from jax.experimental import pallas as pl
from jax.experimental.pallas import tpu as pltpu

D_MODEL = 2048
N_META = 16
N_ATT_HEADS = 8
N_KV_HEADS = 2
ATT_GROUP = N_ATT_HEADS // N_KV_HEADS
ATT_HEAD_DIM = 128
N_IDX_HEADS = 16
IDX_HEAD_DIM = 64
TOPK = 256
HG_HEADS = 8
HG_DIM = 128
D_FF = 5632
LN_EPS = 1e-5
RMS_EPS = 1e-6
ALPHA = 2.0 ** 0.25

ATT_Q_W = N_ATT_HEADS * ATT_HEAD_DIM
ATT_KV_W = N_KV_HEADS * ATT_HEAD_DIM
IDX_Q_W = N_IDX_HEADS * IDX_HEAD_DIM
HG_W = HG_HEADS * HG_DIM

LANES = 128
SUBLANES = 8
VMEM_LIMIT = 56 * 1024 * 1024

F32 = jnp.float32
BF16 = jnp.bfloat16
INT_MIN = -2 ** 31
F32_LOWEST = float(np.finfo(np.float32).min)
LOG2E = math.log2(math.e)
ATT_Q_SCALE = ATT_HEAD_DIM ** -0.5 * LOG2E
POS_SHIFT = 6
POS_SPLIT = 1 << POS_SHIFT


def _dot(a, b):
    return jnp.dot(a, b, preferred_element_type=F32)


def _dot_nt(a, b):
    return lax.dot_general(a, b, (((1,), (1,)), ((), ())), preferred_element_type=F32)


def _dot_tn(a, b):
    return lax.dot_general(a, b, (((0,), (0,)), ((), ())), preferred_element_type=F32)


def _layer_norm(y, g, b):
    mu = jnp.mean(y, axis=-1, keepdims=True)
    d = y - mu
    var = jnp.mean(d * d, axis=-1, keepdims=True)
    return d * lax.rsqrt(var + LN_EPS) * g + b


def _silu(x):
    return x * jax.nn.sigmoid(x)


def _bf16_parts(c, n=3):
    parts = []
    for _ in range(n):
        p = float(np.asarray(c, np.float32).astype(jnp.bfloat16).astype(np.float32))
        parts.append(p)
        c = c - p
    return parts


FFN_TM = 1024
FFN_TF = 256
LN_ROWS = 128


def _ffn_ln_kernel(x_ref, wg_ref, wu_ref, wd_ref, g_ref, b_ref, o_ref, xb_scr):
    j = pl.program_id(1)

    @pl.when(j == 0)
    def _():
        xb_scr[...] = x_ref[...].astype(BF16)
        o_ref[...] = jnp.zeros_like(o_ref)

    xb = xb_scr[...]
    gate = _dot(xb, wg_ref[...].astype(BF16))
    up = _dot(xb, wu_ref[...].astype(BF16))
    a = (_silu(gate) * up).astype(BF16)
    o_ref[...] += _dot(a, wd_ref[...].astype(BF16))

    @pl.when(j == pl.num_programs(1) - 1)
    def _():
        n_rows = min(LN_ROWS, o_ref.shape[0])

        def ln_rows(r, carry):
            rows = pl.ds(pl.multiple_of(r * n_rows, n_rows), n_rows)
            y = ALPHA * x_ref[rows, :] + 0.5 * o_ref[rows, :]
            o_ref[rows, :] = _layer_norm(y, g_ref[...], b_ref[...])
            return carry

        lax.fori_loop(0, o_ref.shape[0] // n_rows, ln_rows, 0)


def _ffn_ln(x, wg, wu, wd, g, b, *, tm, tf):
    m = x.shape[0]
    grid = (m // tm, D_FF // tf)
    return pl.pallas_call(
        _ffn_ln_kernel,
        out_shape=jax.ShapeDtypeStruct((m, D_MODEL), F32),
        grid=grid,
        in_specs=[
            pl.BlockSpec((tm, D_MODEL), lambda i, j: (i, 0)),
            pl.BlockSpec((D_MODEL, tf), lambda i, j: (0, j)),
            pl.BlockSpec((D_MODEL, tf), lambda i, j: (0, j)),
            pl.BlockSpec((tf, D_MODEL), lambda i, j: (j, 0)),
            pl.BlockSpec((1, D_MODEL), lambda i, j: (0, 0)),
            pl.BlockSpec((1, D_MODEL), lambda i, j: (0, 0)),
        ],
        out_specs=pl.BlockSpec((tm, D_MODEL), lambda i, j: (i, 0)),
        scratch_shapes=[pltpu.VMEM((tm, D_MODEL), BF16)],
        compiler_params=pltpu.CompilerParams(
            dimension_semantics=("parallel", "arbitrary"), vmem_limit_bytes=VMEM_LIMIT),
        name="ffn_ln",
    )(x, wg, wu, wd, g, b)


IN_SPLITS = (ATT_Q_W, ATT_KV_W, ATT_KV_W, IDX_Q_W, IDX_HEAD_DIM, N_IDX_HEADS, HG_W, HG_W, HG_W, HG_W,
             D_MODEL, D_MODEL)
IN_OFFS = tuple(int(v) for v in np.cumsum((0,) + IN_SPLITS))
PROJ_TN = 1024
PROJ_GROUPS = {"aq": (IN_OFFS[0], ATT_Q_W), "iq": (IN_OFFS[3], IDX_Q_W), "hq": (IN_OFFS[6], HG_W),
               "hf": (IN_OFFS[7], HG_W), "hi": (IN_OFFS[8], HG_W), "hg": (IN_OFFS[9], HG_W),
               "ga": (IN_OFFS[10], D_MODEL), "gb": (IN_OFFS[11], D_MODEL), "kv": (IN_OFFS[1], 2 * ATT_KV_W)}
PROJ_SRC_ALIGN = 16


def _proj_layout(names):
    col, src = {}, []
    for name in names:
        start, width = PROJ_GROUPS[name]
        col[name] = len(src) * PROJ_TN
        src += [start + t * PROJ_TN for t in range(-(-width // PROJ_TN))]
    assert all(s % PROJ_SRC_ALIGN == 0 and s + PROJ_TN <= IN_OFFS[-1] for s in src)
    return col, tuple(src)


PROJ_COL, PROJ_TILE_SRC = _proj_layout(("aq", "iq", "hq", "hf", "hi", "hg", "ga", "gb", "kv"))
META_COL, META_TILE_SRC = _proj_layout(("hf", "hi", "kv"))


def _proj_all_kernel(src_ref, x_ref, wt_ref, o_ref, xb_scr, *, scaled_tiles, scale):
    @pl.when(pl.program_id(1) == 0)
    def _():
        xb_scr[...] = x_ref[...].astype(BF16)

    acc = _dot_nt(xb_scr[...], wt_ref[...].astype(BF16))
    acc = acc * jnp.where(pl.program_id(1) < scaled_tiles, scale, 1.0)
    o_ref[...] = acc.astype(o_ref.dtype)


def _idx_kernel(x_ref, w_ref, g_ref, b_ref, o_ref):
    p = _dot(x_ref[...].astype(BF16), w_ref[...])
    lane = lax.broadcasted_iota(jnp.int32, p.shape, 1)
    is_k = lane < IDX_HEAD_DIM
    mu = jnp.sum(jnp.where(is_k, p, 0.0), axis=-1, keepdims=True) * (1.0 / IDX_HEAD_DIM)
    d = jnp.where(is_k, p - mu, 0.0)
    var = jnp.sum(d * d, axis=-1, keepdims=True) * (1.0 / IDX_HEAD_DIM)
    kn = d * lax.rsqrt(var + LN_EPS) * g_ref[...] + b_ref[...]
    w_scale = (N_IDX_HEADS ** -0.5) * (IDX_HEAD_DIM ** -0.5)
    o_ref[...] = jnp.where(is_k, kn, p * w_scale)


def _proj_all(xb, w_in_t, tile_src, *, tm, scaled_tiles):
    m = xb.shape[0]
    n_tiles = len(tile_src)
    return pl.pallas_call(
        functools.partial(_proj_all_kernel, scaled_tiles=scaled_tiles, scale=ATT_Q_SCALE),
        out_shape=jax.ShapeDtypeStruct((m, n_tiles * PROJ_TN), BF16),
        grid_spec=pltpu.PrefetchScalarGridSpec(
            num_scalar_prefetch=1,
            grid=(m // tm, n_tiles),
            in_specs=[pl.BlockSpec((tm, D_MODEL), lambda i, j, src: (i, 0)),
                      pl.BlockSpec((pl.Element(PROJ_TN), pl.Element(D_MODEL)),
                                   lambda i, j, src: (src[j] * PROJ_SRC_ALIGN, 0))],
            out_specs=pl.BlockSpec((tm, PROJ_TN), lambda i, j, src: (i, j)),
            scratch_shapes=[pltpu.VMEM((tm, D_MODEL), BF16)],
        ),
        compiler_params=pltpu.CompilerParams(
            dimension_semantics=("parallel", "arbitrary"), vmem_limit_bytes=VMEM_LIMIT),
        name="proj_all",
    )(jnp.asarray([s // PROJ_SRC_ALIGN for s in tile_src], jnp.int32), xb, w_in_t)


def _proj_idx(xb, w, g, b, *, tm):
    m = xb.shape[0]
    return pl.pallas_call(
        _idx_kernel,
        out_shape=jax.ShapeDtypeStruct((m, LANES), F32),
        grid=(m // tm,),
        in_specs=[pl.BlockSpec((tm, D_MODEL), lambda i: (i, 0)),
                  pl.BlockSpec((D_MODEL, LANES), lambda i: (0, 0)),
                  pl.BlockSpec((1, LANES), lambda i: (0, 0)),
                  pl.BlockSpec((1, LANES), lambda i: (0, 0))],
        out_specs=pl.BlockSpec((tm, LANES), lambda i: (i, 0)),
        compiler_params=pltpu.CompilerParams(
            dimension_semantics=("parallel",), vmem_limit_bytes=VMEM_LIMIT),
        name="proj_idx",
    )(xb, w, g, b)


ATT_ROW_TILE = 32
ATT_BOUND_SLACK = 45.0


def _key_to_f32(key):
    return pltpu.bitcast(key ^ ((key >> 31) & 0x7FFFFFFF), F32)


def _attn_kernel(aq_ref, iq_ref, iwt_ref, iklo_ref, ikhi_ref, k_ref, v_ref, km_ref, vm_ref,
                 o_ref,
                 score_scr, mb_scr, kaug_scr, vaug_scr, qaug_scr, s0_scr, s1_scr, p_scr,
                 macc_scr, m_scr, acc_scr, kn_scr, bound_scr, eye_scr, *, tq, kc, seq):
    qi = pl.program_id(1)
    n_kc = ((qi + 1) * tq + kc - 1) // kc
    kf = float(TOPK)

    @pl.when(qi == 0)
    def _():
        r = lax.broadcasted_iota(jnp.int32, (LANES + seq, LANES), 0)
        lane = lax.broadcasted_iota(jnp.int32, (LANES + seq, LANES), 1)
        pos = jnp.where(r < LANES, r, r - LANES + N_META)
        feat = jnp.where(lane < 3, pos >> POS_SHIFT, jnp.where(lane < 6, pos & (POS_SPLIT - 1), 0))
        feat = feat.astype(F32).astype(BF16)
        ones = jnp.ones((LANES + seq, LANES), BF16)
        eye_scr[...] = (lax.broadcasted_iota(jnp.int32, (tq, tq), 0)
                        == lax.broadcasted_iota(jnp.int32, (tq, tq), 1)).astype(BF16)
        for kvh in range(N_KV_HEADS):
            cs = slice(kvh * LANES, (kvh + 1) * LANES)
            kaug_scr[kvh, 0:LANES, 0:LANES] = km_ref[:, cs]
            kaug_scr[kvh, LANES:, 0:LANES] = k_ref[:, cs]
            kaug_scr[kvh, :, LANES:] = feat
            vaug_scr[kvh, 0:LANES, 0:LANES] = vm_ref[:, cs]
            vaug_scr[kvh, LANES:, 0:LANES] = v_ref[:, cs]
            vaug_scr[kvh, :, LANES:] = ones
            kf32 = kaug_scr[kvh, :, 0:LANES].astype(F32)
            k_norm2 = jnp.max(jnp.sum(kf32 * kf32, axis=1, keepdims=True))
            kn_scr[kvh] = jnp.broadcast_to(k_norm2, kn_scr.shape[1:])

    qcol = qi * tq + lax.broadcasted_iota(jnp.int32, (1, tq), 1)
    iwt = iwt_ref[...]

    def score_body(j, carry):
        off = pl.multiple_of(j * kc, kc)
        klo = iklo_ref[pl.ds(off, kc), :]
        khi = ikhi_ref[pl.ds(off, kc), :]
        acc = jnp.zeros((kc, tq), F32)
        for p in range(N_IDX_HEADS // 2):
            q2 = iq_ref[:, p * LANES:(p + 1) * LANES]
            acc = acc + jnp.maximum(_dot_nt(klo, q2), 0.0) * iwt[2 * p:2 * p + 1, :]
            acc = acc + jnp.maximum(_dot_nt(khi, q2), 0.0) * iwt[2 * p + 1:2 * p + 2, :]
        krow = off + lax.broadcasted_iota(jnp.int32, (kc, tq), 0)
        score_scr[pl.ds(off, kc), :] = jnp.where(krow <= qcol, acc, -jnp.inf)
        return carry

    lax.fori_loop(0, n_kc, score_body, 0)

    n_acc = 8
    acc_rows = n_acc * SUBLANES

    def count_ge(cand):
        def body(j, acc):
            off = pl.multiple_of(j * kc, kc)
            w = jnp.where(score_scr[pl.ds(off, kc), :] >= cand, 1.0, 0.0)
            return acc + jnp.sum(w.reshape(kc // acc_rows, acc_rows, tq), axis=0)

        acc = lax.fori_loop(0, n_kc, body, jnp.zeros((acc_rows, tq), F32))
        return jnp.sum(acc, axis=0, keepdims=True)

    c0 = count_ge(jnp.zeros((1, tq), F32))
    ok0 = c0 >= kf
    thr0 = jnp.where(ok0, 0, INT_MIN).astype(jnp.int32)
    cnt0 = jnp.where(ok0, c0, 0.0)

    short_row = qcol + 1 < TOPK

    def unsettled(cnt):
        return jnp.sum(jnp.where((cnt == kf) | short_row, 0.0, 1.0))

    n_bits = 31
    group = 4

    def bit_cond(carry):
        i, _, _, pending = carry
        return jnp.logical_and(i < n_bits, pending > 0.0)

    def bit_body(carry):
        i, thr, cnt, _ = carry
        for b in range(group):
            shift = jnp.maximum(n_bits - 1 - i - b, 0)
            bit = jnp.where(i + b < n_bits, jnp.left_shift(jnp.int32(1), shift), 0)
            cand = thr | bit
            c = count_ge(_key_to_f32(cand))
            ok = c >= kf
            thr, cnt = jnp.where(ok, cand, thr), jnp.where(ok, c, cnt)
        return i + group, thr, cnt, unsettled(cnt)

    _, thr, cnt, _ = lax.while_loop(bit_cond, bit_body, (jnp.int32(0), thr0, cnt0, unsettled(cnt0)))
    t_f = jnp.where(thr == INT_MIN, F32_LOWEST, _key_to_f32(thr))

    @pl.when(jnp.max(cnt) > kf)
    def _():
        def gt_body(j, acc):
            off = pl.multiple_of(j * kc, kc)
            w = jnp.where(score_scr[pl.ds(off, kc), :] > t_f, 1.0, 0.0)
            return acc + jnp.sum(w, axis=0, keepdims=True)

        need = kf - lax.fori_loop(0, n_kc, gt_body, jnp.zeros((1, tq), F32))
        lower = (lax.broadcasted_iota(jnp.int32, (kc, kc), 0)
                 >= lax.broadcasted_iota(jnp.int32, (kc, kc), 1)).astype(BF16)

        def tie_body(j, before):
            off = pl.multiple_of(j * kc, kc)
            sc = score_scr[pl.ds(off, kc), :]
            eq = sc == t_f
            eqf = jnp.where(eq, 1.0, 0.0)
            rank = before + _dot(lower, eqf.astype(BF16))
            score_scr[pl.ds(off, kc), :] = jnp.where(eq & (rank > need), -jnp.inf, sc)
            return before + jnp.sum(eqf, axis=0, keepdims=True)

        lax.fori_loop(0, n_kc, tie_body, jnp.zeros((1, tq), F32))

    eye = eye_scr[...]

    def mask_body(j, last_sel):
        off = pl.multiple_of(j * kc, kc)
        picked = score_scr[pl.ds(off, kc), :] >= t_f
        sel = _dot_nt(eye, jnp.where(picked, 1.0, 0.0).astype(BF16))
        mb_scr[:, pl.ds(off, kc)] = jnp.where(sel > 0.5, 0.0, -jnp.inf)
        krow = (off + lax.broadcasted_iota(jnp.int32, (kc, tq), 0)).astype(F32)
        hit = jnp.where(picked, krow, -1.0)
        return jnp.maximum(last_sel, jnp.max(hit.reshape(kc // SUBLANES, SUBLANES, tq), axis=0))

    last_sel = lax.fori_loop(0, n_kc, mask_body, jnp.full((SUBLANES, tq), -1.0, F32))
    last_pos = jnp.maximum(jnp.max(last_sel, axis=0, keepdims=True) + N_META, N_META - 1.0)

    rt = ATT_ROW_TILE
    lane_q = lax.broadcasted_iota(jnp.int32, (1, LANES), 1)
    meta_mask = jnp.where(lax.broadcasted_iota(jnp.int32, (rt, LANES), 1) < N_META, 0.0, -jnp.inf)

    s_slots = (s0_scr, s1_scr)

    def logits(kvh, krow0, width, slot):
        s_slots[slot][:, 0:width] = _dot_nt(qaug_scr[kvh], kaug_scr[kvh, pl.ds(krow0, width), :])

    def max_tiles(kvh, slot, width, mask_fn):
        s_ref = s_slots[slot]
        for r0 in range(0, tq, rt):
            mb = mask_fn(r0)
            for g in range(ATT_GROUP):
                rows = slice(g * tq + r0, g * tq + r0 + rt)
                mx = s_ref[rows, 0:LANES] + mb[:, 0:LANES]
                for c in range(1, width // LANES):
                    cols = slice(c * LANES, (c + 1) * LANES)
                    mx = jnp.maximum(mx, s_ref[rows, cols] + mb[:, cols])
                macc_scr[rows, :] = jnp.maximum(macc_scr[rows, :], mx)

    def exp_tiles(kvh, slot, width, mask_fn):
        s_ref = s_slots[slot]
        for r0 in range(0, tq, rt):
            mb = mask_fn(r0)
            for g in range(ATT_GROUP):
                rows = slice(g * tq + r0, g * tq + r0 + rt)
                m_row = m_scr[rows, :]
                for c in range(width // LANES):
                    cols = slice(c * LANES, (c + 1) * LANES)
                    p_scr[rows, cols] = jnp.exp2(s_ref[rows, cols] + mb[:, cols] - m_row).astype(BF16)

    def add_pv(kvh, krow0, width):
        vblk = vaug_scr[kvh, pl.ds(krow0, width), :]
        half = ATT_GROUP * tq // 2
        for r0 in (0, half):
            acc_scr[kvh, r0:r0 + half, :] += _dot(p_scr[r0:r0 + half, 0:width], vblk)

    def run_pass(kvh, tiles_fn, with_pv):
        def krow(j):
            return pl.multiple_of(LANES + j * kc, LANES)

        def step(j, slot, lookahead):
            off = pl.multiple_of(j * kc, kc)
            if lookahead:
                logits(kvh, krow(j + 1), kc, 1 - slot)
            tiles_fn(kvh, slot, kc, lambda r0: mb_scr[r0:r0 + rt, pl.ds(off, kc)])
            if with_pv:
                add_pv(kvh, krow(j), kc)

        logits(kvh, 0, LANES, 0)
        tiles_fn(kvh, 0, LANES, lambda r0: meta_mask)
        if with_pv:
            add_pv(kvh, 0, LANES)

            def serial_body(j, carry):
                off = pl.multiple_of(j * kc, kc)
                logits(kvh, krow(j), kc, 0)
                tiles_fn(kvh, 0, kc, lambda r0: mb_scr[r0:r0 + rt, pl.ds(off, kc)])
                add_pv(kvh, krow(j), kc)
                return carry

            lax.fori_loop(0, n_kc, serial_body, 0)
            return
        logits(kvh, krow(0), kc, 0)
        n_pairs = (n_kc - 1) // 2

        def pair_body(i, carry):
            step(2 * i, 0, True)
            step(2 * i + 1, 1, True)
            return carry

        lax.fori_loop(0, n_pairs, pair_body, 0)
        j0 = 2 * n_pairs
        two_left = n_kc - j0 == 2

        @pl.when(two_left)
        def _():
            step(j0, 0, True)
            step(j0 + 1, 1, False)

        @pl.when(jnp.logical_not(two_left))
        def _():
            step(j0, 0, False)

    for h in range(N_ATT_HEADS):
        kvh, g = divmod(h, ATT_GROUP)
        parts = _bf16_parts(2.0 ** (-8.0 * (h + 1) / N_ATT_HEADS) * LOG2E)
        vals = [POS_SPLIT * p for p in parts] + parts
        qfeat = jnp.zeros((1, LANES), F32)
        for i, val in enumerate(vals):
            qfeat = jnp.where(lane_q == i, val, qfeat)
        qaug_scr[kvh, g * tq:(g + 1) * tq, 0:LANES] = aq_ref[:, h * LANES:(h + 1) * LANES]
        qaug_scr[kvh, g * tq:(g + 1) * tq, LANES:] = jnp.broadcast_to(qfeat, (tq, LANES)).astype(BF16)

    a_pos = jnp.floor(last_pos * (1.0 / POS_SPLIT))
    b_pos = last_pos - a_pos * POS_SPLIT
    to_rows = lambda v: _dot_nt(eye, jnp.broadcast_to(v, (LANES, tq)).astype(BF16))
    last_pos_rows = to_rows(a_pos) * POS_SPLIT + to_rows(b_pos)
    slack = jnp.float32(0.0)
    for kvh in range(N_KV_HEADS):
        qf = qaug_scr[kvh, :, 0:LANES].astype(F32)
        qk_max = jnp.sqrt(jnp.sum(qf * qf, axis=1, keepdims=True) * kn_scr[kvh, 0:1, 0:1])
        slack = jnp.maximum(slack, jnp.max(qk_max))
        for g in range(ATT_GROUP):
            rows = slice(g * tq, (g + 1) * tq)
            c = 2.0 ** (-8.0 * (kvh * ATT_GROUP + g + 1) / N_ATT_HEADS) * LOG2E
            bound_scr[kvh, rows, :] = c * last_pos_rows + qk_max[rows]
    bound_ok = slack <= ATT_BOUND_SLACK

    for kvh in range(N_KV_HEADS):
        @pl.when(bound_ok)
        def _():
            m_scr[...] = bound_scr[kvh]

        @pl.when(jnp.logical_not(bound_ok))
        def _():
            macc_scr[...] = jnp.full_like(macc_scr, -jnp.inf)
            run_pass(kvh, max_tiles, with_pv=False)
            m_scr[...] = jnp.broadcast_to(jnp.max(macc_scr[...], axis=1, keepdims=True), m_scr.shape)

        acc_scr[kvh] = jnp.zeros(acc_scr.shape[1:], F32)
        run_pass(kvh, exp_tiles, with_pv=True)

    for h in range(N_ATT_HEADS):
        kvh, g = divmod(h, ATT_GROUP)
        acc = acc_scr[kvh, g * tq:(g + 1) * tq, :]
        o_ref[:, h * LANES:(h + 1) * LANES] = (acc[:, 0:LANES] / acc[:, LANES:LANES + 1]).astype(o_ref.dtype)


def _attention(p_att, iw_t, ik_lo, ik_hi, km, vm, *, batch, seq, tq, kc):
    m = batch * seq
    nq = seq // tq
    qcol, icol, kcol = PROJ_COL["aq"] // ATT_Q_W, PROJ_COL["iq"] // IDX_Q_W, PROJ_COL["kv"] // ATT_KV_W
    rows4 = ATT_GROUP * tq
    return pl.pallas_call(
        functools.partial(_attn_kernel, tq=tq, kc=kc, seq=seq),
        out_shape=jax.ShapeDtypeStruct((m, ATT_Q_W), BF16),
        grid=(batch, nq),
        in_specs=[
            pl.BlockSpec((tq, ATT_Q_W), lambda b, q: (b * nq + q, qcol)),
            pl.BlockSpec((tq, IDX_Q_W), lambda b, q: (b * nq + q, icol)),
            pl.BlockSpec((N_IDX_HEADS, tq), lambda b, q: (0, b * nq + q)),
            pl.BlockSpec((seq, LANES), lambda b, q: (b, 0)),
            pl.BlockSpec((seq, LANES), lambda b, q: (b, 0)),
            pl.BlockSpec((seq, ATT_KV_W), lambda b, q: (b, kcol)),
            pl.BlockSpec((seq, ATT_KV_W), lambda b, q: (b, kcol + 1)),
            pl.BlockSpec((LANES, ATT_KV_W), lambda b, q: (0, 0)),
            pl.BlockSpec((LANES, ATT_KV_W), lambda b, q: (0, 0)),
        ],
        out_specs=pl.BlockSpec((tq, ATT_Q_W), lambda b, q: (b * nq + q, 0)),
        scratch_shapes=[
            pltpu.VMEM((seq, tq), F32),
            pltpu.VMEM((tq, seq), F32),
            pltpu.VMEM((N_KV_HEADS, LANES + seq, 2 * LANES), BF16),
            pltpu.VMEM((N_KV_HEADS, LANES + seq, 2 * LANES), BF16),
            pltpu.VMEM((N_KV_HEADS, rows4, 2 * LANES), BF16),
            pltpu.VMEM((rows4, kc), F32),
            pltpu.VMEM((rows4, kc), F32),
            pltpu.VMEM((rows4, kc), BF16),
            pltpu.VMEM((rows4, LANES), F32),
            pltpu.VMEM((rows4, LANES), F32),
            pltpu.VMEM((N_KV_HEADS, rows4, 2 * LANES), F32),
            pltpu.VMEM((N_KV_HEADS, SUBLANES, LANES), F32),
            pltpu.VMEM((N_KV_HEADS, rows4, LANES), F32),
            pltpu.VMEM((tq, tq), BF16),
        ],
        compiler_params=pltpu.CompilerParams(
            dimension_semantics=("parallel", "arbitrary"), vmem_limit_bytes=VMEM_LIMIT),
        name="dsa_attention",
    )(p_att, p_att, iw_t, ik_lo, ik_hi, p_att, p_att, km, vm)


def _split3(x):
    hi = x.astype(BF16)
    r = x - hi.astype(F32)
    mid = r.astype(BF16)
    lo = (r - mid.astype(F32)).astype(BF16)
    return hi, mid, lo


HG_SAFE_EXPONENT = 80.0


def _hgrn_kernel(hf_ref, hr_q_ref, hr_i_ref, hr_g_ref, lb_ref, ng_ref, s0_ref,
                 y_ref, sT_out_ref, sT_scr, *, chunk, sub, heads, n_inner):
    c_idx = pl.program_id(2)

    @pl.when(c_idx == 0)
    def _():
        sT_scr[...] = s0_ref[...]

    n_sub = chunk // sub
    causal = (lax.broadcasted_iota(jnp.int32, (chunk, chunk), 0)
              >= lax.broadcasted_iota(jnp.int32, (chunk, chunk), 1))
    tri = causal.astype(BF16)
    t_iota = lax.broadcasted_iota(jnp.int32, (sub, LANES), 0)

    def head_inputs(r0, g):
        rows, cs = pl.ds(r0, chunk), slice(g * HG_DIM, (g + 1) * HG_DIM)
        lb = lb_ref[:, cs]
        q = _silu(hr_q_ref[rows, cs].astype(F32))
        fg = lb + (1.0 - lb) * jax.nn.sigmoid(hf_ref[rows, cs].astype(F32))
        v = hr_i_ref[rows, cs].astype(F32)
        l_hi, l_mid, l_lo = _split3(jnp.log(fg))
        b = _dot(tri, l_hi) + _dot(tri, l_mid) + _dot(tri, l_lo)
        return q, 1.0 - fg, v, b

    def carry_state(g, q, kk, vb, b):
        sT = sT_scr[g]
        b_last = b[chunk - 1:chunk, :]
        qe = (q * jnp.exp(b)).astype(BF16)
        o_state = _dot_nt(qe, sT.astype(BF16))
        khat = (kk * jnp.exp(b_last - b)).astype(BF16)
        sT_scr[g] = sT * jnp.exp(b_last) + _dot_tn(vb, khat)
        return qe, o_state

    def finish(r0, g, o):
        rows, cs = pl.ds(r0, chunk), slice(g * HG_DIM, (g + 1) * HG_DIM)
        o = o * lax.rsqrt(jnp.mean(o * o, axis=-1, keepdims=True) + RMS_EPS) * ng_ref[:, cs]
        o = o * _silu(hr_g_ref[rows, cs].astype(F32))
        y_ref[rows, cs] = o.astype(y_ref.dtype)

    def factored_chunk(c, carry):
        rows = pl.ds(pl.multiple_of(c * chunk, chunk), chunk)
        hs = [slice(g * HG_DIM, (g + 1) * HG_DIM) for g in range(heads)]
        lb = lb_ref[...]
        q = _silu(hr_q_ref[rows, :].astype(F32))
        fg = lb + (1.0 - lb) * jax.nn.sigmoid(hf_ref[rows, :].astype(F32))
        kk = 1.0 - fg
        vb = hr_i_ref[rows, :]
        l_hi, l_mid, l_lo = _split3(jnp.log(fg))
        b = _dot(tri, l_hi) + _dot(tri, l_mid) + _dot(tri, l_lo)
        b_last = b[chunk - 1:chunk, :]
        qe = (q * jnp.exp(b)).astype(BF16)
        ke = (kk * jnp.exp(-b)).astype(BF16)
        khat = (kk * jnp.exp(b_last - b)).astype(BF16)
        keep = jnp.exp(b_last)
        s_old = [sT_scr[g] for g in range(heads)]
        a = [_dot_nt(qe[:, cs], ke[:, cs]) for cs in hs]
        o_state = [_dot_nt(qe[:, cs], s_old[g].astype(BF16)) for g, cs in enumerate(hs)]
        s_add = [_dot_tn(vb[:, cs], khat[:, cs]) for cs in hs]
        a = [jnp.where(causal, x, 0.0).astype(BF16) for x in a]
        o = [o_state[g] + _dot(a[g], vb[:, cs]) for g, cs in enumerate(hs)]
        for g, cs in enumerate(hs):
            sT_scr[g] = s_old[g] * keep[:, cs] + s_add[g]
        o = [x * lax.rsqrt(jnp.mean(x * x, axis=-1, keepdims=True) + RMS_EPS) for x in o]
        o = jnp.concatenate(o, axis=1) * ng_ref[...] * _silu(hr_g_ref[rows, :].astype(F32))
        y_ref[rows, :] = o.astype(y_ref.dtype)
        return carry

    def guarded_chunk(c, carry):
        r0 = pl.multiple_of(c * chunk, chunk)
        for g in range(heads):
            q, kk, v, b = head_inputs(r0, g)
            vb = v.astype(BF16)
            _, o_state = carry_state(g, q, kk, vb, b)
            rows = []
            for i in range(n_sub):
                s0, s1 = i * sub, (i + 1) * sub
                bs, qs, ks, vs = b[s0:s1], q[s0:s1], kk[s0:s1], v[s0:s1]
                o_i = o_state[s0:s1]
                if i > 0:
                    r_i = b[s0 - 1:s0, :]
                    qt = (qs * jnp.exp(bs - r_i)).astype(BF16)
                    kt = (kk[:s0] * jnp.exp(r_i - b[:s0])).astype(BF16)
                    a_off = _dot_nt(qt, kt)
                    o_i = o_i + _dot(a_off.astype(BF16), vb[:s0])
                for s in range(sub):
                    e = jnp.exp(jnp.where(t_iota >= s, bs - bs[s:s + 1], -jnp.inf))
                    a_col = jnp.sum(qs * e * ks[s:s + 1], axis=1, keepdims=True)
                    o_i = o_i + a_col * vs[s:s + 1]
                rows.append(o_i)
            finish(r0, g, jnp.concatenate(rows, axis=0) if n_sub > 1 else rows[0])
        return carry

    factoring_safe = jnp.max(-jnp.log(lb_ref[...])) * chunk <= HG_SAFE_EXPONENT

    @pl.when(factoring_safe)
    def _():
        lax.fori_loop(0, n_inner, factored_chunk, 0)

    @pl.when(jnp.logical_not(factoring_safe))
    def _():
        lax.fori_loop(0, n_inner, guarded_chunk, 0)

    @pl.when(c_idx == pl.num_programs(2) - 1)
    def _():
        sT_out_ref[0] = sT_scr[...]


def _hgrn(p, cols, lb, ng, s0, *, batch, seq, chunk, sub, heads, n_inner):
    m = batch * seq
    rows = chunk * n_inner
    nc = seq // rows
    hw = heads * HG_DIM
    nhg = HG_HEADS // heads
    cf, cq, ci, cg = (cols[k] // hw for k in ("hf", "hq", "hi", "hg"))
    return pl.pallas_call(
        functools.partial(_hgrn_kernel, chunk=chunk, sub=sub, heads=heads, n_inner=n_inner),
        out_shape=(jax.ShapeDtypeStruct((m, HG_W), BF16),
                   jax.ShapeDtypeStruct((batch, HG_HEADS, HG_DIM, HG_DIM), F32)),
        grid=(batch, nhg, nc),
        in_specs=[
            pl.BlockSpec((rows, hw), lambda b, h, c: (b * nc + c, cf + h)),
            pl.BlockSpec((rows, hw), lambda b, h, c: (b * nc + c, cq + h)),
            pl.BlockSpec((rows, hw), lambda b, h, c: (b * nc + c, ci + h)),
            pl.BlockSpec((rows, hw), lambda b, h, c: (b * nc + c, cg + h)),
            pl.BlockSpec((1, hw), lambda b, h, c: (0, h)),
            pl.BlockSpec((1, hw), lambda b, h, c: (0, h)),
            pl.BlockSpec((heads, HG_DIM, HG_DIM), lambda b, h, c: (h, 0, 0)),
        ],
        out_specs=(
            pl.BlockSpec((rows, hw), lambda b, h, c: (b * nc + c, h)),
            pl.BlockSpec((1, heads, HG_DIM, HG_DIM), lambda b, h, c: (b, h, 0, 0)),
        ),
        scratch_shapes=[pltpu.VMEM((heads, HG_DIM, HG_DIM), F32)],
        compiler_params=pltpu.CompilerParams(
            dimension_semantics=("parallel", "parallel", "arbitrary"), vmem_limit_bytes=VMEM_LIMIT),
        name="hgrn2",
    )(p, p, p, p, lb, ng, s0)


def _merge_kernel(ya_ref, yh_ref, gts_a_ref, gts_h_ref, h_ref, wa_ref, wh_ref, wo_ref, g_ref, b_ref,
                  o_ref, acc_scr):
    j = pl.program_id(1)

    @pl.when(j == 0)
    def _():
        acc_scr[...] = jnp.zeros_like(acc_scr)

    ga = jax.nn.sigmoid(gts_a_ref[...].astype(F32))
    gh = jax.nn.sigmoid(gts_h_ref[...].astype(F32))
    merged = ga * _dot(ya_ref[...], wa_ref[...]) + gh * _dot(yh_ref[...], wh_ref[...])
    acc_scr[...] += _dot(merged.astype(BF16), wo_ref[...])

    @pl.when(j == pl.num_programs(1) - 1)
    def _():
        o_ref[...] = _layer_norm(ALPHA * h_ref[...] + acc_scr[...], g_ref[...], b_ref[...])


def _merge(ya, yh, gates, h1, wa, wh, wo, g, b, *, tm, tn):
    m = ya.shape[0]
    nj = D_MODEL // tn
    ca, cb = PROJ_COL["ga"] // tn, PROJ_COL["gb"] // tn
    return pl.pallas_call(
        _merge_kernel,
        out_shape=jax.ShapeDtypeStruct((m, D_MODEL), F32),
        grid=(m // tm, nj),
        in_specs=[
            pl.BlockSpec((tm, ATT_Q_W), lambda i, j: (i, 0)),
            pl.BlockSpec((tm, HG_W), lambda i, j: (i, 0)),
            pl.BlockSpec((tm, tn), lambda i, j: (i, ca + j)),
            pl.BlockSpec((tm, tn), lambda i, j: (i, cb + j)),
            pl.BlockSpec((tm, D_MODEL), lambda i, j: (i, 0)),
            pl.BlockSpec((ATT_Q_W, tn), lambda i, j: (0, j)),
            pl.BlockSpec((HG_W, tn), lambda i, j: (0, j)),
            pl.BlockSpec((tn, D_MODEL), lambda i, j: (j, 0)),
            pl.BlockSpec((1, D_MODEL), lambda i, j: (0, 0)),
            pl.BlockSpec((1, D_MODEL), lambda i, j: (0, 0)),
        ],
        out_specs=pl.BlockSpec((tm, D_MODEL), lambda i, j: (i, 0)),
        scratch_shapes=[pltpu.VMEM((tm, D_MODEL), F32)],
        compiler_params=pltpu.CompilerParams(
            dimension_semantics=("parallel", "arbitrary"), vmem_limit_bytes=VMEM_LIMIT),
        name="merge_ln",
    )(ya, yh, gates, gates, h1, wa, wh, wo, g, b)


def kernel(x, meta, ffn1_w_gate, ffn1_w_up, ffn1_w_down, ln1_g, ln1_b, w_in, idx_k_norm_g, idx_k_norm_b,
           hg_lb_logits, hg_norm_g, w_branch_att, w_branch_hg, w_out, ln2_g, ln2_b,
           ffn2_w_gate, ffn2_w_up, ffn2_w_down, ln3_g, ln3_b):
    batch, seq, _ = x.shape
    m = batch * seq
    xr = x.reshape(m, D_MODEL)
    bf = lambda w: w.astype(BF16)
    row = lambda v: v.reshape(1, -1)

    w_in_t = w_in[0].T
    w_idx = bf(jnp.pad(w_in_t[IN_OFFS[4]:IN_OFFS[6]].T, ((0, 0), (0, LANES - IDX_HEAD_DIM - N_IDX_HEADS))))
    idx_g = jnp.pad(idx_k_norm_g[0], (0, LANES - IDX_HEAD_DIM)).reshape(1, LANES)
    idx_b = jnp.pad(idx_k_norm_b[0], (0, LANES - IDX_HEAD_DIM)).reshape(1, LANES)
    lb = jnp.cumsum(jax.nn.softmax(hg_lb_logits.astype(F32), axis=0), axis=0)[0].reshape(1, HG_W)
    ng = hg_norm_g[0].reshape(1, HG_W)

    f1 = (ffn1_w_gate[0], ffn1_w_up[0], ffn1_w_down[0], row(ln1_g[0]), row(ln1_b[0]))
    f2 = (ffn2_w_gate[0], ffn2_w_up[0], ffn2_w_down[0], row(ln3_g[0]), row(ln3_b[0]))

    hm = _ffn_ln(meta.astype(F32), *f1, tm=N_META, tf=FFN_TF)
    pm = _proj_all(hm, w_in_t, META_TILE_SRC, tm=N_META, scaled_tiles=0)
    kv0 = META_COL["kv"]
    km = jnp.pad(pm[:, kv0:kv0 + ATT_KV_W], ((0, LANES - N_META), (0, 0)))
    vm = jnp.pad(pm[:, kv0 + ATT_KV_W:kv0 + 2 * ATT_KV_W], ((0, LANES - N_META), (0, 0)))
    s_zero = jnp.zeros((HG_HEADS, HG_DIM, HG_DIM), F32)
    meta_cols = {"hf": META_COL["hf"], "hi": META_COL["hi"], "hq": META_COL["hf"], "hg": META_COL["hi"]}
    _, s_meta = _hgrn(pm, meta_cols, lb, ng, s_zero, batch=1, seq=N_META, chunk=N_META, sub=N_META,
                      heads=HG_HEADS, n_inner=1)

    h1 = _ffn_ln(xr, *f1, tm=FFN_TM, tf=FFN_TF)
    p = _proj_all(h1, w_in_t, PROJ_TILE_SRC, tm=1024, scaled_tiles=ATT_Q_W // PROJ_TN)
    p_idx = _proj_idx(h1, w_idx, idx_g, idx_b, tm=1024)

    ikn = p_idx[:, :IDX_HEAD_DIM].astype(BF16)
    ik_lo = jnp.pad(ikn, ((0, 0), (0, LANES - IDX_HEAD_DIM)))
    ik_hi = jnp.pad(ikn, ((0, 0), (LANES - IDX_HEAD_DIM, 0)))
    iw_t = p_idx[:, IDX_HEAD_DIM:IDX_HEAD_DIM + N_IDX_HEADS].T
    y_att = _attention(p, iw_t, ik_lo, ik_hi, km, vm, batch=batch, seq=seq, tq=256, kc=512)
    y_hg, _ = _hgrn(p, PROJ_COL, lb, ng, s_meta[0], batch=batch, seq=seq, chunk=64, sub=16,
                    heads=HG_HEADS, n_inner=4)

    h2 = _merge(y_att, y_hg, p, h1, bf(w_branch_att[0]), bf(w_branch_hg[0]), bf(w_out[0]),
                row(ln2_g[0]), row(ln2_b[0]), tm=512, tn=512)
    out = _ffn_ln(h2, *f2, tm=FFN_TM, tf=FFN_TF)
    return out.reshape(batch, seq, D_MODEL)
```

```python
import functools
import math

import jax
import jax.numpy as jnp
import numpy as np
from jax import lax
from jax.experimental import pallas as pl
from jax.experimental.pallas import tpu as pltpu

D_MODEL = 2048
N_META = 16
N_ATT_HEADS = 8
N_KV_HEADS = 2
ATT_GROUP = N_ATT_HEADS // N_KV_HEADS
ATT_HEAD_DIM = 128
N_IDX_HEADS = 16
IDX_HEAD_DIM = 64
TOPK = 256
HG_HEADS = 8
HG_DIM = 128
D_FF = 5632
LN_EPS = 1e-5
RMS_EPS = 1e-6
ALPHA = 2.0 ** 0.25

ATT_Q_W = N_ATT_HEADS * ATT_HEAD_DIM
ATT_KV_W = N_KV_HEADS * ATT_HEAD_DIM
IDX_Q_W = N_IDX_HEADS * IDX_HEAD_DIM
HG_W = HG_HEADS * HG_DIM

LANES = 128
SUBLANES = 8
VMEM_LIMIT = 56 * 1024 * 1024

F32 = jnp.float32
BF16 = jnp.bfloat16
INT_MIN = -2 ** 31
F32_LOWEST = float(np.finfo(np.float32).min)
LOG2E = math.log2(math.e)
ATT_Q_SCALE = ATT_HEAD_DIM ** -0.5 * LOG2E
POS_SHIFT = 6
POS_SPLIT = 1 << POS_SHIFT


def _dot(a, b):
    return jnp.dot(a, b, preferred_element_type=F32)


def _dot_nt(a, b):
    return lax.dot_general(a, b, (((1,), (1,)), ((), ())), preferred_element_type=F32)


def _dot_tn(a, b):
    return lax.dot_general(a, b, (((0,), (0,)), ((), ())), preferred_element_type=F32)


def _layer_norm(y, g, b):
    mu = jnp.mean(y, axis=-1, keepdims=True)
    d = y - mu
    var = jnp.mean(d * d, axis=-1, keepdims=True)
    return d * lax.rsqrt(var + LN_EPS) * g + b


def _silu(x):
    return x * jax.nn.sigmoid(x)


def _bf16_parts(c, n=3):
    parts = []
    for _ in range(n):
        p = float(np.asarray(c, np.float32).astype(jnp.bfloat16).astype(np.float32))
        parts.append(p)
        c = c - p
    return parts


FFN_TM = 1024
FFN_TF = 256
LN_ROWS = 128


def _residual_ln_in_place(x_ref, o_ref, g_ref, b_ref, *, branch_scale):
    n_rows = min(LN_ROWS, o_ref.shape[0])

    def ln_rows(r, carry):
        rows = pl.ds(pl.multiple_of(r * n_rows, n_rows), n_rows)
        y = ALPHA * x_ref[rows, :] + branch_scale * o_ref[rows, :]
        o_ref[rows, :] = _layer_norm(y, g_ref[...], b_ref[...])
        return carry

    lax.fori_loop(0, o_ref.shape[0] // n_rows, ln_rows, 0)


def _ffn_ln_kernel(x_ref, wg_ref, wu_ref, wd_ref, g_ref, b_ref, o_ref, xb_scr):
    j = pl.program_id(1)

    @pl.when(j == 0)
    def _():
        xb_scr[...] = x_ref[...].astype(BF16)
        o_ref[...] = jnp.zeros_like(o_ref)

    xb = xb_scr[...]
    gate = _dot(xb, wg_ref[...].astype(BF16))
    up = _dot(xb, wu_ref[...].astype(BF16))
    a = (_silu(gate) * up).astype(BF16)
    o_ref[...] += _dot(a, wd_ref[...].astype(BF16))

    @pl.when(j == pl.num_programs(1) - 1)
    def _():
        _residual_ln_in_place(x_ref, o_ref, g_ref, b_ref, branch_scale=0.5)


def _ffn_ln(x, wg, wu, wd, g, b, *, tm, tf):
    m = x.shape[0]
    grid = (m // tm, D_FF // tf)
    return pl.pallas_call(
        _ffn_ln_kernel,
        out_shape=jax.ShapeDtypeStruct((m, D_MODEL), F32),
        grid=grid,
        in_specs=[
            pl.BlockSpec((tm, D_MODEL), lambda i, j: (i, 0)),
            pl.BlockSpec((D_MODEL, tf), lambda i, j: (0, j)),
            pl.BlockSpec((D_MODEL, tf), lambda i, j: (0, j)),
            pl.BlockSpec((tf, D_MODEL), lambda i, j: (j, 0)),
            pl.BlockSpec((1, D_MODEL), lambda i, j: (0, 0)),
            pl.BlockSpec((1, D_MODEL), lambda i, j: (0, 0)),
        ],
        out_specs=pl.BlockSpec((tm, D_MODEL), lambda i, j: (i, 0)),
        scratch_shapes=[pltpu.VMEM((tm, D_MODEL), BF16)],
        compiler_params=pltpu.CompilerParams(
            dimension_semantics=("parallel", "arbitrary"), vmem_limit_bytes=VMEM_LIMIT),
        name="ffn_ln",
    )(x, wg, wu, wd, g, b)


IN_SPLITS = (ATT_Q_W, ATT_KV_W, ATT_KV_W, IDX_Q_W, IDX_HEAD_DIM, N_IDX_HEADS, HG_W, HG_W, HG_W, HG_W,
             D_MODEL, D_MODEL)
IN_OFFS = tuple(int(v) for v in np.cumsum((0,) + IN_SPLITS))
PROJ_TN = 1024
PROJ_GROUPS = {"aq": (IN_OFFS[0], ATT_Q_W), "iq": (IN_OFFS[3], IDX_Q_W), "hq": (IN_OFFS[6], HG_W),
               "hf": (IN_OFFS[7], HG_W), "hi": (IN_OFFS[8], HG_W), "hg": (IN_OFFS[9], HG_W),
               "ga": (IN_OFFS[10], D_MODEL), "gb": (IN_OFFS[11], D_MODEL), "kv": (IN_OFFS[1], 2 * ATT_KV_W)}
PROJ_SRC_ALIGN = 16


def _proj_layout(names):
    col, src = {}, []
    for name in names:
        start, width = PROJ_GROUPS[name]
        col[name] = len(src) * PROJ_TN
        src += [start + t * PROJ_TN for t in range(-(-width // PROJ_TN))]
    assert all(s % PROJ_SRC_ALIGN == 0 and s + PROJ_TN <= IN_OFFS[-1] for s in src)
    return col, tuple(src)


PROJ_COL, PROJ_TILE_SRC = _proj_layout(("aq", "iq", "hq", "hf", "hi", "hg", "ga", "gb", "kv"))
META_COL, META_TILE_SRC = _proj_layout(("hf", "hi", "kv"))


def _proj_all_kernel(src_ref, x_ref, wt_ref, o_ref, xb_scr, *, scaled_tiles, scale):
    @pl.when(pl.program_id(1) == 0)
    def _():
        xb_scr[...] = x_ref[...].astype(BF16)

    acc = _dot_nt(xb_scr[...], wt_ref[...].astype(BF16))
    acc = acc * jnp.where(pl.program_id(1) < scaled_tiles, scale, 1.0)
    o_ref[...] = acc.astype(o_ref.dtype)


def _idx_kernel(x_ref, w_ref, g_ref, b_ref, o_ref):
    p = _dot(x_ref[...].astype(BF16), w_ref[...])
    lane = lax.broadcasted_iota(jnp.int32, p.shape, 1)
    is_k = lane < IDX_HEAD_DIM
    mu = jnp.sum(jnp.where(is_k, p, 0.0), axis=-1, keepdims=True) * (1.0 / IDX_HEAD_DIM)
    d = jnp.where(is_k, p - mu, 0.0)
    var = jnp.sum(d * d, axis=-1, keepdims=True) * (1.0 / IDX_HEAD_DIM)
    kn = d * lax.rsqrt(var + LN_EPS) * g_ref[...] + b_ref[...]
    w_scale = (N_IDX_HEADS ** -0.5) * (IDX_HEAD_DIM ** -0.5)
    o_ref[...] = jnp.where(is_k, kn, p * w_scale)


def _proj_all(xb, w_in_t, tile_src, *, tm, scaled_tiles):
    m = xb.shape[0]
    n_tiles = len(tile_src)
    return pl.pallas_call(
        functools.partial(_proj_all_kernel, scaled_tiles=scaled_tiles, scale=ATT_Q_SCALE),
        out_shape=jax.ShapeDtypeStruct((m, n_tiles * PROJ_TN), BF16),
        grid_spec=pltpu.PrefetchScalarGridSpec(
            num_scalar_prefetch=1,
            grid=(m // tm, n_tiles),
            in_specs=[pl.BlockSpec((tm, D_MODEL), lambda i, j, src: (i, 0)),
                      pl.BlockSpec((pl.Element(PROJ_TN), pl.Element(D_MODEL)),
                                   lambda i, j, src: (src[j] * PROJ_SRC_ALIGN, 0))],
            out_specs=pl.BlockSpec((tm, PROJ_TN), lambda i, j, src: (i, j)),
            scratch_shapes=[pltpu.VMEM((tm, D_MODEL), BF16)],
        ),
        compiler_params=pltpu.CompilerParams(
            dimension_semantics=("parallel", "arbitrary"), vmem_limit_bytes=VMEM_LIMIT),
        name="proj_all",
    )(jnp.asarray([s // PROJ_SRC_ALIGN for s in tile_src], jnp.int32), xb, w_in_t)


def _proj_idx(xb, w, g, b, *, tm):
    m = xb.shape[0]
    return pl.pallas_call(
        _idx_kernel,
        out_shape=jax.ShapeDtypeStruct((m, LANES), F32),
        grid=(m // tm,),
        in_specs=[pl.BlockSpec((tm, D_MODEL), lambda i: (i, 0)),
                  pl.BlockSpec((D_MODEL, LANES), lambda i: (0, 0)),
                  pl.BlockSpec((1, LANES), lambda i: (0, 0)),
                  pl.BlockSpec((1, LANES), lambda i: (0, 0))],
        out_specs=pl.BlockSpec((tm, LANES), lambda i: (i, 0)),
        compiler_params=pltpu.CompilerParams(
            dimension_semantics=("parallel",), vmem_limit_bytes=VMEM_LIMIT),
        name="proj_idx",
    )(xb, w, g, b)


ATT_ROW_TILE = 32
ATT_BOUND_SLACK = 45.0


def _key_to_f32(key):
    return pltpu.bitcast(key ^ ((key >> 31) & 0x7FFFFFFF), F32)


def _attn_kernel(aq_ref, iq_ref, iwt_ref, iklo_ref, ikhi_ref, k_ref, v_ref, km_ref, vm_ref,
                 o_ref,
                 score_scr, mb_scr, kaug_scr, vaug_scr, qaug_scr, s0_scr, s1_scr, p_scr,
                 macc_scr, m_scr, acc_scr, kn_scr, bound_scr, eye_scr, *, tq, kc, seq):
    qi = pl.program_id(1)
    n_kc = ((qi + 1) * tq + kc - 1) // kc
    kf = float(TOPK)

    @pl.when(qi == 0)
    def _():
        r = lax.broadcasted_iota(jnp.int32, (LANES + seq, LANES), 0)
        lane = lax.broadcasted_iota(jnp.int32, (LANES + seq, LANES), 1)
        pos = jnp.where(r < LANES, r, r - LANES + N_META)
        feat = jnp.where(lane < 3, pos >> POS_SHIFT, jnp.where(lane < 6, pos & (POS_SPLIT - 1), 0))
        feat = feat.astype(F32).astype(BF16)
        ones = jnp.ones((LANES + seq, LANES), BF16)
        eye_scr[...] = (lax.broadcasted_iota(jnp.int32, (tq, tq), 0)
                        == lax.broadcasted_iota(jnp.int32, (tq, tq), 1)).astype(BF16)
        for kvh in range(N_KV_HEADS):
            cs = slice(kvh * LANES, (kvh + 1) * LANES)
            kaug_scr[kvh, 0:LANES, 0:LANES] = km_ref[:, cs]
            kaug_scr[kvh, LANES:, 0:LANES] = k_ref[:, cs]
            kaug_scr[kvh, :, LANES:] = feat
            vaug_scr[kvh, 0:LANES, 0:LANES] = vm_ref[:, cs]
            vaug_scr[kvh, LANES:, 0:LANES] = v_ref[:, cs]
            vaug_scr[kvh, :, LANES:] = ones
            kf32 = kaug_scr[kvh, :, 0:LANES].astype(F32)
            k_norm2 = jnp.max(jnp.sum(kf32 * kf32, axis=1, keepdims=True))
            kn_scr[kvh] = jnp.broadcast_to(k_norm2, kn_scr.shape[1:])

    qcol = qi * tq + lax.broadcasted_iota(jnp.int32, (1, tq), 1)
    iwt = iwt_ref[...]

    def score_body(j, carry):
        off = pl.multiple_of(j * kc, kc)
        klo = iklo_ref[pl.ds(off, kc), :]
        khi = ikhi_ref[pl.ds(off, kc), :]
        acc = jnp.zeros((kc, tq), F32)
        for p in range(N_IDX_HEADS // 2):
            q2 = iq_ref[:, p * LANES:(p + 1) * LANES]
            acc = acc + jnp.maximum(_dot_nt(klo, q2), 0.0) * iwt[2 * p:2 * p + 1, :]
            acc = acc + jnp.maximum(_dot_nt(khi, q2), 0.0) * iwt[2 * p + 1:2 * p + 2, :]
        krow = off + lax.broadcasted_iota(jnp.int32, (kc, tq), 0)
        score_scr[pl.ds(off, kc), :] = jnp.where(krow <= qcol, acc, -jnp.inf)
        return carry

    lax.fori_loop(0, n_kc, score_body, 0)

    n_acc = 8
    acc_rows = n_acc * SUBLANES

    def count_ge(cand):
        def body(j, acc):
            off = pl.multiple_of(j * kc, kc)
            w = jnp.where(score_scr[pl.ds(off, kc), :] >= cand, 1.0, 0.0)
            return acc + jnp.sum(w.reshape(kc // acc_rows, acc_rows, tq), axis=0)

        acc = lax.fori_loop(0, n_kc, body, jnp.zeros((acc_rows, tq), F32))
        return jnp.sum(acc, axis=0, keepdims=True)

    c0 = count_ge(jnp.zeros((1, tq), F32))
    ok0 = c0 >= kf
    thr0 = jnp.where(ok0, 0, INT_MIN).astype(jnp.int32)
    cnt0 = jnp.where(ok0, c0, 0.0)

    short_row = qcol + 1 < TOPK

    def unsettled(cnt):
        return jnp.sum(jnp.where((cnt == kf) | short_row, 0.0, 1.0))

    n_bits = 31
    group = 4

    def bit_cond(carry):
        i, _, _, pending = carry
        return jnp.logical_and(i < n_bits, pending > 0.0)

    def bit_body(carry):
        i, thr, cnt, _ = carry
        for b in range(group):
            shift = jnp.maximum(n_bits - 1 - i - b, 0)
            bit = jnp.where(i + b < n_bits, jnp.left_shift(jnp.int32(1), shift), 0)
            cand = thr | bit
            c = count_ge(_key_to_f32(cand))
            ok = c >= kf
            thr, cnt = jnp.where(ok, cand, thr), jnp.where(ok, c, cnt)
        return i + group, thr, cnt, unsettled(cnt)

    _, thr, cnt, _ = lax.while_loop(bit_cond, bit_body, (jnp.int32(0), thr0, cnt0, unsettled(cnt0)))
    t_f = jnp.where(thr == INT_MIN, F32_LOWEST, _key_to_f32(thr))

    @pl.when(jnp.max(cnt) > kf)
    def _():
        def gt_body(j, acc):
            off = pl.multiple_of(j * kc, kc)
            w = jnp.where(score_scr[pl.ds(off, kc), :] > t_f, 1.0, 0.0)
            return acc + jnp.sum(w, axis=0, keepdims=True)

        need = kf - lax.fori_loop(0, n_kc, gt_body, jnp.zeros((1, tq), F32))
        lower = (lax.broadcasted_iota(jnp.int32, (kc, kc), 0)
                 >= lax.broadcasted_iota(jnp.int32, (kc, kc), 1)).astype(BF16)

        def tie_body(j, before):
            off = pl.multiple_of(j * kc, kc)
            sc = score_scr[pl.ds(off, kc), :]
            eq = sc == t_f
            eqf = jnp.where(eq, 1.0, 0.0)
            rank = before + _dot(lower, eqf.astype(BF16))
            score_scr[pl.ds(off, kc), :] = jnp.where(eq & (rank > need), -jnp.inf, sc)
            return before + jnp.sum(eqf, axis=0, keepdims=True)

        lax.fori_loop(0, n_kc, tie_body, jnp.zeros((1, tq), F32))

    eye = eye_scr[...]

    def mask_body(j, last_sel):
        off = pl.multiple_of(j * kc, kc)
        picked = score_scr[pl.ds(off, kc), :] >= t_f
        sel = _dot_nt(eye, jnp.where(picked, 1.0, 0.0).astype(BF16))
        mb_scr[:, pl.ds(off, kc)] = jnp.where(sel > 0.5, 0.0, -jnp.inf)
        krow = (off + lax.broadcasted_iota(jnp.int32, (kc, tq), 0)).astype(F32)
        hit = jnp.where(picked, krow, -1.0)
        return jnp.maximum(last_sel, jnp.max(hit.reshape(kc // SUBLANES, SUBLANES, tq), axis=0))

    last_sel = lax.fori_loop(0, n_kc, mask_body, jnp.full((SUBLANES, tq), -1.0, F32))
    last_pos = jnp.maximum(jnp.max(last_sel, axis=0, keepdims=True) + N_META, N_META - 1.0)

    rt = ATT_ROW_TILE
    lane_q = lax.broadcasted_iota(jnp.int32, (1, LANES), 1)
    meta_mask = jnp.where(lax.broadcasted_iota(jnp.int32, (rt, LANES), 1) < N_META, 0.0, -jnp.inf)

    s_slots = (s0_scr, s1_scr)

    def logits(kvh, krow0, width, slot):
        s_slots[slot][:, 0:width] = _dot_nt(qaug_scr[kvh], kaug_scr[kvh, pl.ds(krow0, width), :])

    def max_tiles(kvh, slot, width, mask_fn):
        s_ref = s_slots[slot]
        for r0 in range(0, tq, rt):
            mb = mask_fn(r0)
            for g in range(ATT_GROUP):
                rows = slice(g * tq + r0, g * tq + r0 + rt)
                mx = s_ref[rows, 0:LANES] + mb[:, 0:LANES]
                for c in range(1, width // LANES):
                    cols = slice(c * LANES, (c + 1) * LANES)
                    mx = jnp.maximum(mx, s_ref[rows, cols] + mb[:, cols])
                macc_scr[rows, :] = jnp.maximum(macc_scr[rows, :], mx)

    def exp_tiles(kvh, slot, width, mask_fn):
        s_ref = s_slots[slot]
        for r0 in range(0, tq, rt):
            mb = mask_fn(r0)
            for g in range(ATT_GROUP):
                rows = slice(g * tq + r0, g * tq + r0 + rt)
                m_row = m_scr[rows, :]
                for c in range(width // LANES):
                    cols = slice(c * LANES, (c + 1) * LANES)
                    p_scr[rows, cols] = jnp.exp2(s_ref[rows, cols] + mb[:, cols] - m_row).astype(BF16)

    def add_pv(kvh, krow0, width):
        vblk = vaug_scr[kvh, pl.ds(krow0, width), :]
        half = ATT_GROUP * tq // 2
        for r0 in (0, half):
            acc_scr[kvh, r0:r0 + half, :] += _dot(p_scr[r0:r0 + half, 0:width], vblk)

    def run_pass(kvh, tiles_fn, with_pv):
        def krow(j):
            return pl.multiple_of(LANES + j * kc, LANES)

        def step(j, slot, lookahead):
            off = pl.multiple_of(j * kc, kc)
            if lookahead:
                logits(kvh, krow(j + 1), kc, 1 - slot)
            tiles_fn(kvh, slot, kc, lambda r0: mb_scr[r0:r0 + rt, pl.ds(off, kc)])
            if with_pv:
                add_pv(kvh, krow(j), kc)

        logits(kvh, 0, LANES, 0)
        tiles_fn(kvh, 0, LANES, lambda r0: meta_mask)
        if with_pv:
            add_pv(kvh, 0, LANES)

            def serial_body(j, carry):
                off = pl.multiple_of(j * kc, kc)
                logits(kvh, krow(j), kc, 0)
                tiles_fn(kvh, 0, kc, lambda r0: mb_scr[r0:r0 + rt, pl.ds(off, kc)])
                add_pv(kvh, krow(j), kc)
                return carry

            lax.fori_loop(0, n_kc, serial_body, 0)
            return
        logits(kvh, krow(0), kc, 0)
        n_pairs = (n_kc - 1) // 2

        def pair_body(i, carry):
            step(2 * i, 0, True)
            step(2 * i + 1, 1, True)
            return carry

        lax.fori_loop(0, n_pairs, pair_body, 0)
        j0 = 2 * n_pairs
        two_left = n_kc - j0 == 2

        @pl.when(two_left)
        def _():
            step(j0, 0, True)
            step(j0 + 1, 1, False)

        @pl.when(jnp.logical_not(two_left))
        def _():
            step(j0, 0, False)

    for h in range(N_ATT_HEADS):
        kvh, g = divmod(h, ATT_GROUP)
        parts = _bf16_parts(2.0 ** (-8.0 * (h + 1) / N_ATT_HEADS) * LOG2E)
        vals = [POS_SPLIT * p for p in parts] + parts
        qfeat = jnp.zeros((1, LANES), F32)
        for i, val in enumerate(vals):
            qfeat = jnp.where(lane_q == i, val, qfeat)
        qaug_scr[kvh, g * tq:(g + 1) * tq, 0:LANES] = aq_ref[:, h * LANES:(h + 1) * LANES]
        qaug_scr[kvh, g * tq:(g + 1) * tq, LANES:] = jnp.broadcast_to(qfeat, (tq, LANES)).astype(BF16)

    a_pos = jnp.floor(last_pos * (1.0 / POS_SPLIT))
    b_pos = last_pos - a_pos * POS_SPLIT
    to_rows = lambda v: _dot_nt(eye, jnp.broadcast_to(v, (LANES, tq)).astype(BF16))
    last_pos_rows = to_rows(a_pos) * POS_SPLIT + to_rows(b_pos)
    slack = jnp.float32(0.0)
    for kvh in range(N_KV_HEADS):
        qf = qaug_scr[kvh, :, 0:LANES].astype(F32)
        qk_max = jnp.sqrt(jnp.sum(qf * qf, axis=1, keepdims=True) * kn_scr[kvh, 0:1, 0:1])
        slack = jnp.maximum(slack, jnp.max(qk_max))
        for g in range(ATT_GROUP):
            rows = slice(g * tq, (g + 1) * tq)
            c = 2.0 ** (-8.0 * (kvh * ATT_GROUP + g + 1) / N_ATT_HEADS) * LOG2E
            bound_scr[kvh, rows, :] = c * last_pos_rows + qk_max[rows]
    bound_ok = slack <= ATT_BOUND_SLACK

    for kvh in range(N_KV_HEADS):
        @pl.when(bound_ok)
        def _():
            m_scr[...] = bound_scr[kvh]

        @pl.when(jnp.logical_not(bound_ok))
        def _():
            macc_scr[...] = jnp.full_like(macc_scr, -jnp.inf)
            run_pass(kvh, max_tiles, with_pv=False)
            m_scr[...] = jnp.broadcast_to(jnp.max(macc_scr[...], axis=1, keepdims=True), m_scr.shape)

        acc_scr[kvh] = jnp.zeros(acc_scr.shape[1:], F32)
        run_pass(kvh, exp_tiles, with_pv=True)

    for h in range(N_ATT_HEADS):
        kvh, g = divmod(h, ATT_GROUP)
        acc = acc_scr[kvh, g * tq:(g + 1) * tq, :]
        o_ref[:, h * LANES:(h + 1) * LANES] = (acc[:, 0:LANES] / acc[:, LANES:LANES + 1]).astype(o_ref.dtype)


def _attention(p_att, iw_t, ik_lo, ik_hi, km, vm, *, batch, seq, tq, kc):
    m = batch * seq
    nq = seq // tq
    qcol, icol, kcol = PROJ_COL["aq"] // ATT_Q_W, PROJ_COL["iq"] // IDX_Q_W, PROJ_COL["kv"] // ATT_KV_W
    rows4 = ATT_GROUP * tq
    return pl.pallas_call(
        functools.partial(_attn_kernel, tq=tq, kc=kc, seq=seq),
        out_shape=jax.ShapeDtypeStruct((m, ATT_Q_W), BF16),
        grid=(batch, nq),
        in_specs=[
            pl.BlockSpec((tq, ATT_Q_W), lambda b, q: (b * nq + q, qcol)),
            pl.BlockSpec((tq, IDX_Q_W), lambda b, q: (b * nq + q, icol)),
            pl.BlockSpec((N_IDX_HEADS, tq), lambda b, q: (0, b * nq + q)),
            pl.BlockSpec((seq, LANES), lambda b, q: (b, 0)),
            pl.BlockSpec((seq, LANES), lambda b, q: (b, 0)),
            pl.BlockSpec((seq, ATT_KV_W), lambda b, q: (b, kcol)),
            pl.BlockSpec((seq, ATT_KV_W), lambda b, q: (b, kcol + 1)),
            pl.BlockSpec((LANES, ATT_KV_W), lambda b, q: (0, 0)),
            pl.BlockSpec((LANES, ATT_KV_W), lambda b, q: (0, 0)),
        ],
        out_specs=pl.BlockSpec((tq, ATT_Q_W), lambda b, q: (b * nq + q, 0)),
        scratch_shapes=[
            pltpu.VMEM((seq, tq), F32),
            pltpu.VMEM((tq, seq), F32),
            pltpu.VMEM((N_KV_HEADS, LANES + seq, 2 * LANES), BF16),
            pltpu.VMEM((N_KV_HEADS, LANES + seq, 2 * LANES), BF16),
            pltpu.VMEM((N_KV_HEADS, rows4, 2 * LANES), BF16),
            pltpu.VMEM((rows4, kc), F32),
            pltpu.VMEM((rows4, kc), F32),
            pltpu.VMEM((rows4, kc), BF16),
            pltpu.VMEM((rows4, LANES), F32),
            pltpu.VMEM((rows4, LANES), F32),
            pltpu.VMEM((N_KV_HEADS, rows4, 2 * LANES), F32),
            pltpu.VMEM((N_KV_HEADS, SUBLANES, LANES), F32),
            pltpu.VMEM((N_KV_HEADS, rows4, LANES), F32),
            pltpu.VMEM((tq, tq), BF16),
        ],
        compiler_params=pltpu.CompilerParams(
            dimension_semantics=("parallel", "arbitrary"), vmem_limit_bytes=VMEM_LIMIT),
        name="dsa_attention",
    )(p_att, p_att, iw_t, ik_lo, ik_hi, p_att, p_att, km, vm)


def _split3(x):
    hi = x.astype(BF16)
    r = x - hi.astype(F32)
    mid = r.astype(BF16)
    lo = (r - mid.astype(F32)).astype(BF16)
    return hi, mid, lo


HG_SAFE_EXPONENT = 80.0


def _hgrn_kernel(hf_ref, hr_q_ref, hr_i_ref, hr_g_ref, lb_ref, ng_ref, s0_ref,
                 y_ref, sT_out_ref, sT_scr, *, chunk, sub, heads, n_inner):
    c_idx = pl.program_id(2)

    @pl.when(c_idx == 0)
    def _():
        sT_scr[...] = s0_ref[...]

    n_sub = chunk // sub
    causal = (lax.broadcasted_iota(jnp.int32, (chunk, chunk), 0)
              >= lax.broadcasted_iota(jnp.int32, (chunk, chunk), 1))
    tri = causal.astype(BF16)
    t_iota = lax.broadcasted_iota(jnp.int32, (sub, LANES), 0)

    def head_inputs(r0, g):
        rows, cs = pl.ds(r0, chunk), slice(g * HG_DIM, (g + 1) * HG_DIM)
        lb = lb_ref[:, cs]
        q = _silu(hr_q_ref[rows, cs].astype(F32))
        fg = lb + (1.0 - lb) * jax.nn.sigmoid(hf_ref[rows, cs].astype(F32))
        v = hr_i_ref[rows, cs].astype(F32)
        l_hi, l_mid, l_lo = _split3(jnp.log(fg))
        b = _dot(tri, l_hi) + _dot(tri, l_mid) + _dot(tri, l_lo)
        return q, 1.0 - fg, v, b

    def carry_state(g, q, kk, vb, b):
        sT = sT_scr[g]
        b_last = b[chunk - 1:chunk, :]
        qe = (q * jnp.exp(b)).astype(BF16)
        o_state = _dot_nt(qe, sT.astype(BF16))
        khat = (kk * jnp.exp(b_last - b)).astype(BF16)
        sT_scr[g] = sT * jnp.exp(b_last) + _dot_tn(vb, khat)
        return qe, o_state

    def finish(r0, g, o):
        rows, cs = pl.ds(r0, chunk), slice(g * HG_DIM, (g + 1) * HG_DIM)
        o = o * lax.rsqrt(jnp.mean(o * o, axis=-1, keepdims=True) + RMS_EPS) * ng_ref[:, cs]
        o = o * _silu(hr_g_ref[rows, cs].astype(F32))
        y_ref[rows, cs] = o.astype(y_ref.dtype)

    def factored_chunk(c, carry):
        rows = pl.ds(pl.multiple_of(c * chunk, chunk), chunk)
        hs = [slice(g * HG_DIM, (g + 1) * HG_DIM) for g in range(heads)]
        lb = lb_ref[...]
        q = _silu(hr_q_ref[rows, :].astype(F32))
        fg = lb + (1.0 - lb) * jax.nn.sigmoid(hf_ref[rows, :].astype(F32))
        kk = 1.0 - fg
        vb = hr_i_ref[rows, :]
        l_hi, l_mid, l_lo = _split3(jnp.log(fg))
        b = _dot(tri, l_hi) + _dot(tri, l_mid) + _dot(tri, l_lo)
        b_last = b[chunk - 1:chunk, :]
        qe = (q * jnp.exp(b)).astype(BF16)
        ke = (kk * jnp.exp(-b)).astype(BF16)
        khat = (kk * jnp.exp(b_last - b)).astype(BF16)
        keep = jnp.exp(b_last)
        s_old = [sT_scr[g] for g in range(heads)]
        a = [_dot_nt(qe[:, cs], ke[:, cs]) for cs in hs]
        o_state = [_dot_nt(qe[:, cs], s_old[g].astype(BF16)) for g, cs in enumerate(hs)]
        s_add = [_dot_tn(vb[:, cs], khat[:, cs]) for cs in hs]
        a = [jnp.where(causal, x, 0.0).astype(BF16) for x in a]
        o = [o_state[g] + _dot(a[g], vb[:, cs]) for g, cs in enumerate(hs)]
        for g, cs in enumerate(hs):
            sT_scr[g] = s_old[g] * keep[:, cs] + s_add[g]
        o = [x * lax.rsqrt(jnp.mean(x * x, axis=-1, keepdims=True) + RMS_EPS) for x in o]
        o = jnp.concatenate(o, axis=1) * ng_ref[...] * _silu(hr_g_ref[rows, :].astype(F32))
        y_ref[rows, :] = o.astype(y_ref.dtype)
        return carry

    def guarded_chunk(c, carry):
        r0 = pl.multiple_of(c * chunk, chunk)
        for g in range(heads):
            q, kk, v, b = head_inputs(r0, g)
            vb = v.astype(BF16)
            _, o_state = carry_state(g, q, kk, vb, b)
            rows = []
            for i in range(n_sub):
                s0, s1 = i * sub, (i + 1) * sub
                bs, qs, ks, vs = b[s0:s1], q[s0:s1], kk[s0:s1], v[s0:s1]
                o_i = o_state[s0:s1]
                if i > 0:
                    r_i = b[s0 - 1:s0, :]
                    qt = (qs * jnp.exp(bs - r_i)).astype(BF16)
                    kt = (kk[:s0] * jnp.exp(r_i - b[:s0])).astype(BF16)
                    a_off = _dot_nt(qt, kt)
                    o_i = o_i + _dot(a_off.astype(BF16), vb[:s0])
                for s in range(sub):
                    e = jnp.exp(jnp.where(t_iota >= s, bs - bs[s:s + 1], -jnp.inf))
                    a_col = jnp.sum(qs * e * ks[s:s + 1], axis=1, keepdims=True)
                    o_i = o_i + a_col * vs[s:s + 1]
                rows.append(o_i)
            finish(r0, g, jnp.concatenate(rows, axis=0) if n_sub > 1 else rows[0])
        return carry

    factoring_safe = jnp.max(-jnp.log(lb_ref[...])) * chunk <= HG_SAFE_EXPONENT

    @pl.when(factoring_safe)
    def _():
        lax.fori_loop(0, n_inner, factored_chunk, 0)

    @pl.when(jnp.logical_not(factoring_safe))
    def _():
        lax.fori_loop(0, n_inner, guarded_chunk, 0)

    @pl.when(c_idx == pl.num_programs(2) - 1)
    def _():
        sT_out_ref[0] = sT_scr[...]


def _hgrn(p, cols, lb, ng, s0, *, batch, seq, chunk, sub, heads, n_inner):
    m = batch * seq
    rows = chunk * n_inner
    nc = seq // rows
    hw = heads * HG_DIM
    nhg = HG_HEADS // heads
    cf, cq, ci, cg = (cols[k] // hw for k in ("hf", "hq", "hi", "hg"))
    return pl.pallas_call(
        functools.partial(_hgrn_kernel, chunk=chunk, sub=sub, heads=heads, n_inner=n_inner),
        out_shape=(jax.ShapeDtypeStruct((m, HG_W), BF16),
                   jax.ShapeDtypeStruct((batch, HG_HEADS, HG_DIM, HG_DIM), F32)),
        grid=(batch, nhg, nc),
        in_specs=[
            pl.BlockSpec((rows, hw), lambda b, h, c: (b * nc + c, cf + h)),
            pl.BlockSpec((rows, hw), lambda b, h, c: (b * nc + c, cq + h)),
            pl.BlockSpec((rows, hw), lambda b, h, c: (b * nc + c, ci + h)),
            pl.BlockSpec((rows, hw), lambda b, h, c: (b * nc + c, cg + h)),
            pl.BlockSpec((1, hw), lambda b, h, c: (0, h)),
            pl.BlockSpec((1, hw), lambda b, h, c: (0, h)),
            pl.BlockSpec((heads, HG_DIM, HG_DIM), lambda b, h, c: (h, 0, 0)),
        ],
        out_specs=(
            pl.BlockSpec((rows, hw), lambda b, h, c: (b * nc + c, h)),
            pl.BlockSpec((1, heads, HG_DIM, HG_DIM), lambda b, h, c: (b, h, 0, 0)),
        ),
        scratch_shapes=[pltpu.VMEM((heads, HG_DIM, HG_DIM), F32)],
        compiler_params=pltpu.CompilerParams(
            dimension_semantics=("parallel", "parallel", "arbitrary"), vmem_limit_bytes=VMEM_LIMIT),
        name="hgrn2",
    )(p, p, p, p, lb, ng, s0)


def _merge_kernel(ya_ref, yh_ref, gts_a_ref, gts_h_ref, h_ref, wa_ref, wh_ref, wo_ref, g_ref, b_ref,
                  o_ref):
    j = pl.program_id(1)

    @pl.when(j == 0)
    def _():
        o_ref[...] = jnp.zeros_like(o_ref)

    ga = jax.nn.sigmoid(gts_a_ref[...].astype(F32))
    gh = jax.nn.sigmoid(gts_h_ref[...].astype(F32))
    merged = (ga * _dot(ya_ref[...], wa_ref[...].astype(BF16))
              + gh * _dot(yh_ref[...], wh_ref[...].astype(BF16)))
    o_ref[...] += _dot(merged.astype(BF16), wo_ref[...].astype(BF16))

    @pl.when(j == pl.num_programs(1) - 1)
    def _():
        _residual_ln_in_place(h_ref, o_ref, g_ref, b_ref, branch_scale=1.0)


def _merge(ya, yh, gates, h1, wa, wh, wo, g, b, *, tm, tn):
    m = ya.shape[0]
    nj = D_MODEL // tn
    ca, cb = PROJ_COL["ga"] // tn, PROJ_COL["gb"] // tn
    return pl.pallas_call(
        _merge_kernel,
        out_shape=jax.ShapeDtypeStruct((m, D_MODEL), F32),
        grid=(m // tm, nj),
        in_specs=[
            pl.BlockSpec((tm, ATT_Q_W), lambda i, j: (i, 0)),
            pl.BlockSpec((tm, HG_W), lambda i, j: (i, 0)),
            pl.BlockSpec((tm, tn), lambda i, j: (i, ca + j)),
            pl.BlockSpec((tm, tn), lambda i, j: (i, cb + j)),
            pl.BlockSpec((tm, D_MODEL), lambda i, j: (i, 0)),
            pl.BlockSpec((ATT_Q_W, tn), lambda i, j: (0, j)),
            pl.BlockSpec((HG_W, tn), lambda i, j: (0, j)),
            pl.BlockSpec((tn, D_MODEL), lambda i, j: (j, 0)),
            pl.BlockSpec((1, D_MODEL), lambda i, j: (0, 0)),
            pl.BlockSpec((1, D_MODEL), lambda i, j: (0, 0)),
        ],
        out_specs=pl.BlockSpec((tm, D_MODEL), lambda i, j: (i, 0)),
        compiler_params=pltpu.CompilerParams(
            dimension_semantics=("parallel", "arbitrary"), vmem_limit_bytes=VMEM_LIMIT),
        name="merge_ln",
    )(ya, yh, gates, gates, h1, wa, wh, wo, g, b)


def kernel(x, meta, ffn1_w_gate, ffn1_w_up, ffn1_w_down, ln1_g, ln1_b, w_in, idx_k_norm_g, idx_k_norm_b,
           hg_lb_logits, hg_norm_g, w_branch_att, w_branch_hg, w_out, ln2_g, ln2_b,
           ffn2_w_gate, ffn2_w_up, ffn2_w_down, ln3_g, ln3_b):
    batch, seq, _ = x.shape
    m = batch * seq
    xr = x.reshape(m, D_MODEL)
    bf = lambda w: w.astype(BF16)
    row = lambda v: v.reshape(1, -1)

    w_in_t = w_in[0].T
    w_idx = bf(jnp.pad(w_in_t[IN_OFFS[4]:IN_OFFS[6]].T, ((0, 0), (0, LANES - IDX_HEAD_DIM - N_IDX_HEADS))))
    idx_g = jnp.pad(idx_k_norm_g[0], (0, LANES - IDX_HEAD_DIM)).reshape(1, LANES)
    idx_b = jnp.pad(idx_k_norm_b[0], (0, LANES - IDX_HEAD_DIM)).reshape(1, LANES)
    lb = jnp.cumsum(jax.nn.softmax(hg_lb_logits.astype(F32), axis=0), axis=0)[0].reshape(1, HG_W)
    ng = hg_norm_g[0].reshape(1, HG_W)

    f1 = (ffn1_w_gate[0], ffn1_w_up[0], ffn1_w_down[0], row(ln1_g[0]), row(ln1_b[0]))
    f2 = (ffn2_w_gate[0], ffn2_w_up[0], ffn2_w_down[0], row(ln3_g[0]), row(ln3_b[0]))

    hm = _ffn_ln(meta.astype(F32), *f1, tm=N_META, tf=FFN_TF)
    pm = _proj_all(hm, w_in_t, META_TILE_SRC, tm=N_META, scaled_tiles=0)
    kv0 = META_COL["kv"]
    km = jnp.pad(pm[:, kv0:kv0 + ATT_KV_W], ((0, LANES - N_META), (0, 0)))
    vm = jnp.pad(pm[:, kv0 + ATT_KV_W:kv0 + 2 * ATT_KV_W], ((0, LANES - N_META), (0, 0)))
    s_zero = jnp.zeros((HG_HEADS, HG_DIM, HG_DIM), F32)
    meta_cols = {"hf": META_COL["hf"], "hi": META_COL["hi"], "hq": META_COL["hf"], "hg": META_COL["hi"]}
    _, s_meta = _hgrn(pm, meta_cols, lb, ng, s_zero, batch=1, seq=N_META, chunk=N_META, sub=N_META,
                      heads=HG_HEADS, n_inner=1)

    h1 = _ffn_ln(xr, *f1, tm=FFN_TM, tf=FFN_TF)
    p = _proj_all(h1, w_in_t, PROJ_TILE_SRC, tm=1024, scaled_tiles=ATT_Q_W // PROJ_TN)
    p_idx = _proj_idx(h1, w_idx, idx_g, idx_b, tm=1024)

    ikn = p_idx[:, :IDX_HEAD_DIM].astype(BF16)
    ik_lo = jnp.pad(ikn, ((0, 0), (0, LANES - IDX_HEAD_DIM)))
    ik_hi = jnp.pad(ikn, ((0, 0), (LANES - IDX_HEAD_DIM, 0)))
    iw_t = p_idx[:, IDX_HEAD_DIM:IDX_HEAD_DIM + N_IDX_HEADS].T
    y_att = _attention(p, iw_t, ik_lo, ik_hi, km, vm, batch=batch, seq=seq, tq=256, kc=512)
    y_hg, _ = _hgrn(p, PROJ_COL, lb, ng, s_meta[0], batch=batch, seq=seq, chunk=64, sub=16,
                    heads=HG_HEADS, n_inner=4)

    h2 = _merge(y_att, y_hg, p, h1, w_branch_att[0], w_branch_hg[0], w_out[0],
                row(ln2_g[0]), row(ln2_b[0]), tm=512, tn=512)
    out = _ffn_ln(h2, *f2, tm=FFN_TM, tf=FFN_TF)
    return out.reshape(batch, seq, D_MODEL)
```

```python
import functools
import math

import jax
import jax.numpy as jnp
import numpy as np
from jax import lax
from jax.experimental import pallas as pl
from jax.experimental.pallas import tpu as pltpu

D_MODEL = 2048
N_META = 16
N_ATT_HEADS = 8
N_KV_HEADS = 2
ATT_GROUP = N_ATT_HEADS // N_KV_HEADS
ATT_HEAD_DIM = 128
N_IDX_HEADS = 16
IDX_HEAD_DIM = 64
TOPK = 256
HG_HEADS = 8
HG_DIM = 128
D_FF = 5632
LN_EPS = 1e-5
RMS_EPS = 1e-6
ALPHA = 2.0 ** 0.25

ATT_Q_W = N_ATT_HEADS * ATT_HEAD_DIM
ATT_KV_W = N_KV_HEADS * ATT_HEAD_DIM
IDX_Q_W = N_IDX_HEADS * IDX_HEAD_DIM
HG_W = HG_HEADS * HG_DIM

LANES = 128
SUBLANES = 8
VMEM_LIMIT = 56 * 1024 * 1024

F32 = jnp.float32
BF16 = jnp.bfloat16
INT_MIN = -2 ** 31
F32_LOWEST = float(np.finfo(np.float32).min)
LOG2E = math.log2(math.e)
ATT_Q_SCALE = ATT_HEAD_DIM ** -0.5 * LOG2E
POS_SHIFT = 6
POS_SPLIT = 1 << POS_SHIFT


def _dot(a, b):
    return jnp.dot(a, b, preferred_element_type=F32)


def _dot_nt(a, b):
    return lax.dot_general(a, b, (((1,), (1,)), ((), ())), preferred_element_type=F32)


def _dot_tn(a, b):
    return lax.dot_general(a, b, (((0,), (0,)), ((), ())), preferred_element_type=F32)


def _layer_norm(y, g, b):
    mu = jnp.mean(y, axis=-1, keepdims=True)
    d = y - mu
    var = jnp.mean(d * d, axis=-1, keepdims=True)
    return d * lax.rsqrt(var + LN_EPS) * g + b


def _silu(x):
    return x * jax.nn.sigmoid(x)


def _bf16_parts(c, n=3):
    parts = []
    for _ in range(n):
        p = float(np.asarray(c, np.float32).astype(jnp.bfloat16).astype(np.float32))
        parts.append(p)
        c = c - p
    return parts


FFN_TM = 1024
FFN_TF = 256
LN_ROWS = 128


def _ffn_ln_kernel(x_ref, wg_ref, wu_ref, wd_ref, g_ref, b_ref, o_ref, xb_scr):
    j = pl.program_id(1)

    @pl.when(j == 0)
    def _():
        xb_scr[...] = x_ref[...].astype(BF16)
        o_ref[...] = jnp.zeros_like(o_ref)

    xb = xb_scr[...]
    gate = _dot(xb, wg_ref[...].astype(BF16))
    up = _dot(xb, wu_ref[...].astype(BF16))
    a = (_silu(gate) * up).astype(BF16)
    o_ref[...] += _dot(a, wd_ref[...].astype(BF16))

    @pl.when(j == pl.num_programs(1) - 1)
    def _():
        n_rows = min(LN_ROWS, o_ref.shape[0])

        def ln_rows(r, carry):
            rows = pl.ds(pl.multiple_of(r * n_rows, n_rows), n_rows)
            y = ALPHA * x_ref[rows, :] + 0.5 * o_ref[rows, :]
            o_ref[rows, :] = _layer_norm(y, g_ref[...], b_ref[...])
            return carry

        lax.fori_loop(0, o_ref.shape[0] // n_rows, ln_rows, 0)


def _ffn_ln(x, wg, wu, wd, g, b, *, tm, tf):
    m = x.shape[0]
    grid = (m // tm, D_FF // tf)
    return pl.pallas_call(
        _ffn_ln_kernel,
        out_shape=jax.ShapeDtypeStruct((m, D_MODEL), F32),
        grid=grid,
        in_specs=[
            pl.BlockSpec((tm, D_MODEL), lambda i, j: (i, 0)),
            pl.BlockSpec((D_MODEL, tf), lambda i, j: (0, j)),
            pl.BlockSpec((D_MODEL, tf), lambda i, j: (0, j)),
            pl.BlockSpec((tf, D_MODEL), lambda i, j: (j, 0)),
            pl.BlockSpec((1, D_MODEL), lambda i, j: (0, 0)),
            pl.BlockSpec((1, D_MODEL), lambda i, j: (0, 0)),
        ],
        out_specs=pl.BlockSpec((tm, D_MODEL), lambda i, j: (i, 0)),
        scratch_shapes=[pltpu.VMEM((tm, D_MODEL), BF16)],
        compiler_params=pltpu.CompilerParams(
            dimension_semantics=("parallel", "arbitrary"), vmem_limit_bytes=VMEM_LIMIT),
        name="ffn_ln",
    )(x, wg, wu, wd, g, b)


IN_SPLITS = (ATT_Q_W, ATT_KV_W, ATT_KV_W, IDX_Q_W, IDX_HEAD_DIM, N_IDX_HEADS, HG_W, HG_W, HG_W, HG_W,
             D_MODEL, D_MODEL)
IN_OFFS = tuple(int(v) for v in np.cumsum((0,) + IN_SPLITS))
PROJ_TN = 1024
PROJ_GROUPS = {"aq": (IN_OFFS[0], ATT_Q_W), "iq": (IN_OFFS[3], IDX_Q_W), "hq": (IN_OFFS[6], HG_W),
               "hf": (IN_OFFS[7], HG_W), "hi": (IN_OFFS[8], HG_W), "hg": (IN_OFFS[9], HG_W),
               "ga": (IN_OFFS[10], D_MODEL), "gb": (IN_OFFS[11], D_MODEL), "kv": (IN_OFFS[1], 2 * ATT_KV_W)}
PROJ_SRC_ALIGN = 16


def _proj_layout(names):
    col, src = {}, []
    for name in names:
        start, width = PROJ_GROUPS[name]
        col[name] = len(src) * PROJ_TN
        src += [start + t * PROJ_TN for t in range(-(-width // PROJ_TN))]
    assert all(s % PROJ_SRC_ALIGN == 0 and s + PROJ_TN <= IN_OFFS[-1] for s in src)
    return col, tuple(src)


PROJ_COL, PROJ_TILE_SRC = _proj_layout(("aq", "iq", "hq", "hf", "hi", "hg", "ga", "gb", "kv"))
META_COL, META_TILE_SRC = _proj_layout(("hf", "hi", "kv"))


def _proj_all_kernel(src_ref, x_ref, wt_ref, o_ref, xb_scr, *, scaled_tiles, scale):
    @pl.when(pl.program_id(1) == 0)
    def _():
        xb_scr[...] = x_ref[...].astype(BF16)

    acc = _dot_nt(xb_scr[...], wt_ref[...].astype(BF16))
    acc = acc * jnp.where(pl.program_id(1) < scaled_tiles, scale, 1.0)
    o_ref[...] = acc.astype(o_ref.dtype)


def _idx_kernel(x_ref, w_ref, g_ref, b_ref, o_ref):
    p = _dot(x_ref[...].astype(BF16), w_ref[...])
    lane = lax.broadcasted_iota(jnp.int32, p.shape, 1)
    is_k = lane < IDX_HEAD_DIM
    mu = jnp.sum(jnp.where(is_k, p, 0.0), axis=-1, keepdims=True) * (1.0 / IDX_HEAD_DIM)
    d = jnp.where(is_k, p - mu, 0.0)
    var = jnp.sum(d * d, axis=-1, keepdims=True) * (1.0 / IDX_HEAD_DIM)
    kn = d * lax.rsqrt(var + LN_EPS) * g_ref[...] + b_ref[...]
    w_scale = (N_IDX_HEADS ** -0.5) * (IDX_HEAD_DIM ** -0.5)
    o_ref[...] = jnp.where(is_k, kn, p * w_scale)


def _proj_all(xb, w_in_t, tile_src, *, tm, scaled_tiles):
    m = xb.shape[0]
    n_tiles = len(tile_src)
    return pl.pallas_call(
        functools.partial(_proj_all_kernel, scaled_tiles=scaled_tiles, scale=ATT_Q_SCALE),
        out_shape=jax.ShapeDtypeStruct((m, n_tiles * PROJ_TN), BF16),
        grid_spec=pltpu.PrefetchScalarGridSpec(
            num_scalar_prefetch=1,
            grid=(m // tm, n_tiles),
            in_specs=[pl.BlockSpec((tm, D_MODEL), lambda i, j, src: (i, 0)),
                      pl.BlockSpec((pl.Element(PROJ_TN), pl.Element(D_MODEL)),
                                   lambda i, j, src: (src[j] * PROJ_SRC_ALIGN, 0))],
            out_specs=pl.BlockSpec((tm, PROJ_TN), lambda i, j, src: (i, j)),
            scratch_shapes=[pltpu.VMEM((tm, D_MODEL), BF16)],
        ),
        compiler_params=pltpu.CompilerParams(
            dimension_semantics=("parallel", "arbitrary"), vmem_limit_bytes=VMEM_LIMIT),
        name="proj_all",
    )(jnp.asarray([s // PROJ_SRC_ALIGN for s in tile_src], jnp.int32), xb, w_in_t)


def _proj_idx(xb, w, g, b, *, tm):
    m = xb.shape[0]
    return pl.pallas_call(
        _idx_kernel,
        out_shape=jax.ShapeDtypeStruct((m, LANES), F32),
        grid=(m // tm,),
        in_specs=[pl.BlockSpec((tm, D_MODEL), lambda i: (i, 0)),
                  pl.BlockSpec((D_MODEL, LANES), lambda i: (0, 0)),
                  pl.BlockSpec((1, LANES), lambda i: (0, 0)),
                  pl.BlockSpec((1, LANES), lambda i: (0, 0))],
        out_specs=pl.BlockSpec((tm, LANES), lambda i: (i, 0)),
        compiler_params=pltpu.CompilerParams(
            dimension_semantics=("parallel",), vmem_limit_bytes=VMEM_LIMIT),
        name="proj_idx",
    )(xb, w, g, b)


ATT_ROW_TILE = 32
ATT_BOUND_SLACK = 45.0


def _key_to_f32(key):
    return pltpu.bitcast(key ^ ((key >> 31) & 0x7FFFFFFF), F32)


def _attn_kernel(aq_ref, iq_ref, iwt_ref, iklo_ref, ikhi_ref, k_ref, v_ref, km_ref, vm_ref,
                 o_ref,
                 score_scr, mb_scr, kaug_scr, vaug_scr, qaug_scr, s0_scr, s1_scr, p_scr,
                 macc_scr, m_scr, acc_scr, kn_scr, bound_scr, eye_scr, *, tq, kc, seq):
    qi = pl.program_id(1)
    n_kc = ((qi + 1) * tq + kc - 1) // kc
    kf = float(TOPK)

    @pl.when(qi == 0)
    def _():
        r = lax.broadcasted_iota(jnp.int32, (LANES + seq, LANES), 0)
        lane = lax.broadcasted_iota(jnp.int32, (LANES + seq, LANES), 1)
        pos = jnp.where(r < LANES, r, r - LANES + N_META)
        feat = jnp.where(lane < 3, pos >> POS_SHIFT, jnp.where(lane < 6, pos & (POS_SPLIT - 1), 0))
        feat = feat.astype(F32).astype(BF16)
        ones = jnp.ones((LANES + seq, LANES), BF16)
        eye_scr[...] = (lax.broadcasted_iota(jnp.int32, (tq, tq), 0)
                        == lax.broadcasted_iota(jnp.int32, (tq, tq), 1)).astype(BF16)
        for kvh in range(N_KV_HEADS):
            cs = slice(kvh * LANES, (kvh + 1) * LANES)
            kaug_scr[kvh, 0:LANES, 0:LANES] = km_ref[:, cs]
            kaug_scr[kvh, LANES:, 0:LANES] = k_ref[:, cs]
            kaug_scr[kvh, :, LANES:] = feat
            vaug_scr[kvh, 0:LANES, 0:LANES] = vm_ref[:, cs]
            vaug_scr[kvh, LANES:, 0:LANES] = v_ref[:, cs]
            vaug_scr[kvh, :, LANES:] = ones
            kf32 = kaug_scr[kvh, :, 0:LANES].astype(F32)
            k_norm2 = jnp.max(jnp.sum(kf32 * kf32, axis=1, keepdims=True))
            kn_scr[kvh] = jnp.broadcast_to(k_norm2, kn_scr.shape[1:])

    qcol = qi * tq + lax.broadcasted_iota(jnp.int32, (1, tq), 1)
    iwt = iwt_ref[...]

    def score_body(j, carry):
        off = pl.multiple_of(j * kc, kc)
        klo = iklo_ref[pl.ds(off, kc), :]
        khi = ikhi_ref[pl.ds(off, kc), :]
        acc = jnp.zeros((kc, tq), F32)
        for p in range(N_IDX_HEADS // 2):
            q2 = iq_ref[:, p * LANES:(p + 1) * LANES]
            acc = acc + jnp.maximum(_dot_nt(klo, q2), 0.0) * iwt[2 * p:2 * p + 1, :]
            acc = acc + jnp.maximum(_dot_nt(khi, q2), 0.0) * iwt[2 * p + 1:2 * p + 2, :]
        krow = off + lax.broadcasted_iota(jnp.int32, (kc, tq), 0)
        score_scr[pl.ds(off, kc), :] = jnp.where(krow <= qcol, acc, -jnp.inf)
        return carry

    lax.fori_loop(0, n_kc, score_body, 0)

    n_acc = 8
    acc_rows = n_acc * SUBLANES

    def count_ge(cand):
        def body(j, acc):
            off = pl.multiple_of(j * kc, kc)
            w = jnp.where(score_scr[pl.ds(off, kc), :] >= cand, 1.0, 0.0)
            return acc + jnp.sum(w.reshape(kc // acc_rows, acc_rows, tq), axis=0)

        acc = lax.fori_loop(0, n_kc, body, jnp.zeros((acc_rows, tq), F32))
        return jnp.sum(acc, axis=0, keepdims=True)

    c0 = count_ge(jnp.zeros((1, tq), F32))
    ok0 = c0 >= kf
    thr0 = jnp.where(ok0, 0, INT_MIN).astype(jnp.int32)
    cnt0 = jnp.where(ok0, c0, 0.0)

    short_row = qcol + 1 < TOPK

    def unsettled(cnt):
        return jnp.sum(jnp.where((cnt == kf) | short_row, 0.0, 1.0))

    n_bits = 31
    group = 4

    def bit_cond(carry):
        i, _, _, pending = carry
        return jnp.logical_and(i < n_bits, pending > 0.0)

    def bit_body(carry):
        i, thr, cnt, _ = carry
        for b in range(group):
            shift = jnp.maximum(n_bits - 1 - i - b, 0)
            bit = jnp.where(i + b < n_bits, jnp.left_shift(jnp.int32(1), shift), 0)
            cand = thr | bit
            c = count_ge(_key_to_f32(cand))
            ok = c >= kf
            thr, cnt = jnp.where(ok, cand, thr), jnp.where(ok, c, cnt)
        return i + group, thr, cnt, unsettled(cnt)

    _, thr, cnt, _ = lax.while_loop(bit_cond, bit_body, (jnp.int32(0), thr0, cnt0, unsettled(cnt0)))
    t_f = jnp.where(thr == INT_MIN, F32_LOWEST, _key_to_f32(thr))

    @pl.when(jnp.max(cnt) > kf)
    def _():
        def gt_body(j, acc):
            off = pl.multiple_of(j * kc, kc)
            w = jnp.where(score_scr[pl.ds(off, kc), :] > t_f, 1.0, 0.0)
            return acc + jnp.sum(w, axis=0, keepdims=True)

        need = kf - lax.fori_loop(0, n_kc, gt_body, jnp.zeros((1, tq), F32))
        lower = (lax.broadcasted_iota(jnp.int32, (kc, kc), 0)
                 >= lax.broadcasted_iota(jnp.int32, (kc, kc), 1)).astype(BF16)

        def tie_body(j, before):
            off = pl.multiple_of(j * kc, kc)
            sc = score_scr[pl.ds(off, kc), :]
            eq = sc == t_f
            eqf = jnp.where(eq, 1.0, 0.0)
            rank = before + _dot(lower, eqf.astype(BF16))
            score_scr[pl.ds(off, kc), :] = jnp.where(eq & (rank > need), -jnp.inf, sc)
            return before + jnp.sum(eqf, axis=0, keepdims=True)

        lax.fori_loop(0, n_kc, tie_body, jnp.zeros((1, tq), F32))

    eye = eye_scr[...]

    def mask_body(j, last_sel):
        off = pl.multiple_of(j * kc, kc)
        picked = score_scr[pl.ds(off, kc), :] >= t_f
        sel = _dot_nt(eye, jnp.where(picked, 1.0, 0.0).astype(BF16))
        mb_scr[:, pl.ds(off, kc)] = jnp.where(sel > 0.5, 0.0, -jnp.inf)
        krow = (off + lax.broadcasted_iota(jnp.int32, (kc, tq), 0)).astype(F32)
        hit = jnp.where(picked, krow, -1.0)
        return jnp.maximum(last_sel, jnp.max(hit.reshape(kc // SUBLANES, SUBLANES, tq), axis=0))

    last_sel = lax.fori_loop(0, n_kc, mask_body, jnp.full((SUBLANES, tq), -1.0, F32))
    last_pos = jnp.maximum(jnp.max(last_sel, axis=0, keepdims=True) + N_META, N_META - 1.0)

    rt = ATT_ROW_TILE
    lane_q = lax.broadcasted_iota(jnp.int32, (1, LANES), 1)
    meta_mask = jnp.where(lax.broadcasted_iota(jnp.int32, (rt, LANES), 1) < N_META, 0.0, -jnp.inf)

    s_slots = (s0_scr, s1_scr)

    def logits(kvh, krow0, width, slot):
        s_slots[slot][:, 0:width] = _dot_nt(qaug_scr[kvh], kaug_scr[kvh, pl.ds(krow0, width), :])

    def max_tiles(kvh, slot, width, mask_fn):
        s_ref = s_slots[slot]
        for r0 in range(0, tq, rt):
            mb = mask_fn(r0)
            for g in range(ATT_GROUP):
                rows = slice(g * tq + r0, g * tq + r0 + rt)
                mx = s_ref[rows, 0:LANES] + mb[:, 0:LANES]
                for c in range(1, width // LANES):
                    cols = slice(c * LANES, (c + 1) * LANES)
                    mx = jnp.maximum(mx, s_ref[rows, cols] + mb[:, cols])
                macc_scr[rows, :] = jnp.maximum(macc_scr[rows, :], mx)

    def exp_tiles(kvh, slot, width, mask_fn):
        s_ref = s_slots[slot]
        for r0 in range(0, tq, rt):
            mb = mask_fn(r0)
            for g in range(ATT_GROUP):
                rows = slice(g * tq + r0, g * tq + r0 + rt)
                m_row = m_scr[rows, :]
                for c in range(width // LANES):
                    cols = slice(c * LANES, (c + 1) * LANES)
                    p_scr[rows, cols] = jnp.exp2(s_ref[rows, cols] + mb[:, cols] - m_row).astype(BF16)

    def add_pv(kvh, krow0, width):
        vblk = vaug_scr[kvh, pl.ds(krow0, width), :]
        half = ATT_GROUP * tq // 2
        for r0 in (0, half):
            acc_scr[kvh, r0:r0 + half, :] += _dot(p_scr[r0:r0 + half, 0:width], vblk)

    def run_pass(kvh, tiles_fn, with_pv):
        def krow(j):
            return pl.multiple_of(LANES + j * kc, LANES)

        def step(j, slot, lookahead):
            off = pl.multiple_of(j * kc, kc)
            if lookahead:
                logits(kvh, krow(j + 1), kc, 1 - slot)
            tiles_fn(kvh, slot, kc, lambda r0: mb_scr[r0:r0 + rt, pl.ds(off, kc)])
            if with_pv:
                add_pv(kvh, krow(j), kc)

        logits(kvh, 0, LANES, 0)
        tiles_fn(kvh, 0, LANES, lambda r0: meta_mask)
        if with_pv:
            add_pv(kvh, 0, LANES)

            def serial_body(j, carry):
                off = pl.multiple_of(j * kc, kc)
                logits(kvh, krow(j), kc, 0)
                tiles_fn(kvh, 0, kc, lambda r0: mb_scr[r0:r0 + rt, pl.ds(off, kc)])
                add_pv(kvh, krow(j), kc)
                return carry

            lax.fori_loop(0, n_kc, serial_body, 0)
            return
        logits(kvh, krow(0), kc, 0)
        n_pairs = (n_kc - 1) // 2

        def pair_body(i, carry):
            step(2 * i, 0, True)
            step(2 * i + 1, 1, True)
            return carry

        lax.fori_loop(0, n_pairs, pair_body, 0)
        j0 = 2 * n_pairs
        two_left = n_kc - j0 == 2

        @pl.when(two_left)
        def _():
            step(j0, 0, True)
            step(j0 + 1, 1, False)

        @pl.when(jnp.logical_not(two_left))
        def _():
            step(j0, 0, False)

    for h in range(N_ATT_HEADS):
        kvh, g = divmod(h, ATT_GROUP)
        parts = _bf16_parts(2.0 ** (-8.0 * (h + 1) / N_ATT_HEADS) * LOG2E)
        vals = [POS_SPLIT * p for p in parts] + parts
        qfeat = jnp.zeros((1, LANES), F32)
        for i, val in enumerate(vals):
            qfeat = jnp.where(lane_q == i, val, qfeat)
        qaug_scr[kvh, g * tq:(g + 1) * tq, 0:LANES] = aq_ref[:, h * LANES:(h + 1) * LANES]
        qaug_scr[kvh, g * tq:(g + 1) * tq, LANES:] = jnp.broadcast_to(qfeat, (tq, LANES)).astype(BF16)

    a_pos = jnp.floor(last_pos * (1.0 / POS_SPLIT))
    b_pos = last_pos - a_pos * POS_SPLIT
    to_rows = lambda v: _dot_nt(eye, jnp.broadcast_to(v, (LANES, tq)).astype(BF16))
    last_pos_rows = to_rows(a_pos) * POS_SPLIT + to_rows(b_pos)
    slack = jnp.float32(0.0)
    for kvh in range(N_KV_HEADS):
        qf = qaug_scr[kvh, :, 0:LANES].astype(F32)
        qk_max = jnp.sqrt(jnp.sum(qf * qf, axis=1, keepdims=True) * kn_scr[kvh, 0:1, 0:1])
        slack = jnp.maximum(slack, jnp.max(qk_max))
        for g in range(ATT_GROUP):
            rows = slice(g * tq, (g + 1) * tq)
            c = 2.0 ** (-8.0 * (kvh * ATT_GROUP + g + 1) / N_ATT_HEADS) * LOG2E
            bound_scr[kvh, rows, :] = c * last_pos_rows + qk_max[rows]
    bound_ok = slack <= ATT_BOUND_SLACK

    for kvh in range(N_KV_HEADS):
        @pl.when(bound_ok)
        def _():
            m_scr[...] = bound_scr[kvh]

        @pl.when(jnp.logical_not(bound_ok))
        def _():
            macc_scr[...] = jnp.full_like(macc_scr, -jnp.inf)
            run_pass(kvh, max_tiles, with_pv=False)
            m_scr[...] = jnp.broadcast_to(jnp.max(macc_scr[...], axis=1, keepdims=True), m_scr.shape)

        acc_scr[kvh] = jnp.zeros(acc_scr.shape[1:], F32)
        run_pass(kvh, exp_tiles, with_pv=True)

    for h in range(N_ATT_HEADS):
        kvh, g = divmod(h, ATT_GROUP)
        acc = acc_scr[kvh, g * tq:(g + 1) * tq, :]
        o_ref[:, h * LANES:(h + 1) * LANES] = (acc[:, 0:LANES] / acc[:, LANES:LANES + 1]).astype(o_ref.dtype)


def _attention(p_att, iw_t, ik_lo, ik_hi, km, vm, *, batch, seq, tq, kc):
    m = batch * seq
    nq = seq // tq
    qcol, icol, kcol = PROJ_COL["aq"] // ATT_Q_W, PROJ_COL["iq"] // IDX_Q_W, PROJ_COL["kv"] // ATT_KV_W
    rows4 = ATT_GROUP * tq
    return pl.pallas_call(
        functools.partial(_attn_kernel, tq=tq, kc=kc, seq=seq),
        out_shape=jax.ShapeDtypeStruct((m, ATT_Q_W), BF16),
        grid=(batch, nq),
        in_specs=[
            pl.BlockSpec((tq, ATT_Q_W), lambda b, q: (b * nq + q, qcol)),
            pl.BlockSpec((tq, IDX_Q_W), lambda b, q: (b * nq + q, icol)),
            pl.BlockSpec((N_IDX_HEADS, tq), lambda b, q: (0, b * nq + q)),
            pl.BlockSpec((seq, LANES), lambda b, q: (b, 0)),
            pl.BlockSpec((seq, LANES), lambda b, q: (b, 0)),
            pl.BlockSpec((seq, ATT_KV_W), lambda b, q: (b, kcol)),
            pl.BlockSpec((seq, ATT_KV_W), lambda b, q: (b, kcol + 1)),
            pl.BlockSpec((LANES, ATT_KV_W), lambda b, q: (0, 0)),
            pl.BlockSpec((LANES, ATT_KV_W), lambda b, q: (0, 0)),
        ],
        out_specs=pl.BlockSpec((tq, ATT_Q_W), lambda b, q: (b * nq + q, 0)),
        scratch_shapes=[
            pltpu.VMEM((seq, tq), F32),
            pltpu.VMEM((tq, seq), F32),
            pltpu.VMEM((N_KV_HEADS, LANES + seq, 2 * LANES), BF16),
            pltpu.VMEM((N_KV_HEADS, LANES + seq, 2 * LANES), BF16),
            pltpu.VMEM((N_KV_HEADS, rows4, 2 * LANES), BF16),
            pltpu.VMEM((rows4, kc), F32),
            pltpu.VMEM((rows4, kc), F32),
            pltpu.VMEM((rows4, kc), BF16),
            pltpu.VMEM((rows4, LANES), F32),
            pltpu.VMEM((rows4, LANES), F32),
            pltpu.VMEM((N_KV_HEADS, rows4, 2 * LANES), F32),
            pltpu.VMEM((N_KV_HEADS, SUBLANES, LANES), F32),
            pltpu.VMEM((N_KV_HEADS, rows4, LANES), F32),
            pltpu.VMEM((tq, tq), BF16),
        ],
        compiler_params=pltpu.CompilerParams(
            dimension_semantics=("parallel", "arbitrary"), vmem_limit_bytes=VMEM_LIMIT),
        name="dsa_attention",
    )(p_att, p_att, iw_t, ik_lo, ik_hi, p_att, p_att, km, vm)


def _split3(x):
    hi = x.astype(BF16)
    r = x - hi.astype(F32)
    mid = r.astype(BF16)
    lo = (r - mid.astype(F32)).astype(BF16)
    return hi, mid, lo


HG_SAFE_EXPONENT = 80.0


def _hgrn_kernel(hf_ref, hr_q_ref, hr_i_ref, hr_g_ref, lb_ref, ng_ref, s0_ref,
                 y_ref, sT_out_ref, sT_scr, *, chunk, sub, heads, n_inner):
    c_idx = pl.program_id(2)

    @pl.when(c_idx == 0)
    def _():
        sT_scr[...] = s0_ref[...]

    n_sub = chunk // sub
    causal = (lax.broadcasted_iota(jnp.int32, (chunk, chunk), 0)
              >= lax.broadcasted_iota(jnp.int32, (chunk, chunk), 1))
    tri = causal.astype(BF16)
    t_iota = lax.broadcasted_iota(jnp.int32, (sub, LANES), 0)

    def head_inputs(r0, g):
        rows, cs = pl.ds(r0, chunk), slice(g * HG_DIM, (g + 1) * HG_DIM)
        lb = lb_ref[:, cs]
        q = _silu(hr_q_ref[rows, cs].astype(F32))
        fg = lb + (1.0 - lb) * jax.nn.sigmoid(hf_ref[rows, cs].astype(F32))
        v = hr_i_ref[rows, cs].astype(F32)
        l_hi, l_mid, l_lo = _split3(jnp.log(fg))
        b = _dot(tri, l_hi) + _dot(tri, l_mid) + _dot(tri, l_lo)
        return q, 1.0 - fg, v, b

    def carry_state(g, q, kk, vb, b):
        sT = sT_scr[g]
        b_last = b[chunk - 1:chunk, :]
        qe = (q * jnp.exp(b)).astype(BF16)
        o_state = _dot_nt(qe, sT.astype(BF16))
        khat = (kk * jnp.exp(b_last - b)).astype(BF16)
        sT_scr[g] = sT * jnp.exp(b_last) + _dot_tn(vb, khat)
        return qe, o_state

    def finish(r0, g, o):
        rows, cs = pl.ds(r0, chunk), slice(g * HG_DIM, (g + 1) * HG_DIM)
        o = o * lax.rsqrt(jnp.mean(o * o, axis=-1, keepdims=True) + RMS_EPS) * ng_ref[:, cs]
        o = o * _silu(hr_g_ref[rows, cs].astype(F32))
        y_ref[rows, cs] = o.astype(y_ref.dtype)

    def factored_chunk(c, carry):
        rows = pl.ds(pl.multiple_of(c * chunk, chunk), chunk)
        hs = [slice(g * HG_DIM, (g + 1) * HG_DIM) for g in range(heads)]
        lb = lb_ref[...]
        q = _silu(hr_q_ref[rows, :].astype(F32))
        fg = lb + (1.0 - lb) * jax.nn.sigmoid(hf_ref[rows, :].astype(F32))
        kk = 1.0 - fg
        vb = hr_i_ref[rows, :]
        l_hi, l_mid, l_lo = _split3(jnp.log(fg))
        b = _dot(tri, l_hi) + _dot(tri, l_mid) + _dot(tri, l_lo)
        b_last = b[chunk - 1:chunk, :]
        qe = (q * jnp.exp(b)).astype(BF16)
        ke = (kk * jnp.exp(-b)).astype(BF16)
        khat = (kk * jnp.exp(b_last - b)).astype(BF16)
        keep = jnp.exp(b_last)
        s_old = [sT_scr[g] for g in range(heads)]
        a = [_dot_nt(qe[:, cs], ke[:, cs]) for cs in hs]
        o_state = [_dot_nt(qe[:, cs], s_old[g].astype(BF16)) for g, cs in enumerate(hs)]
        s_add = [_dot_tn(vb[:, cs], khat[:, cs]) for cs in hs]
        a = [jnp.where(causal, x, 0.0).astype(BF16) for x in a]
        o = [o_state[g] + _dot(a[g], vb[:, cs]) for g, cs in enumerate(hs)]
        for g, cs in enumerate(hs):
            sT_scr[g] = s_old[g] * keep[:, cs] + s_add[g]
        o = [x * lax.rsqrt(jnp.mean(x * x, axis=-1, keepdims=True) + RMS_EPS) for x in o]
        o = jnp.concatenate(o, axis=1) * ng_ref[...] * _silu(hr_g_ref[rows, :].astype(F32))
        y_ref[rows, :] = o.astype(y_ref.dtype)
        return carry

    def guarded_chunk(c, carry):
        r0 = pl.multiple_of(c * chunk, chunk)
        for g in range(heads):
            q, kk, v, b = head_inputs(r0, g)
            vb = v.astype(BF16)
            _, o_state = carry_state(g, q, kk, vb, b)
            rows = []
            for i in range(n_sub):
                s0, s1 = i * sub, (i + 1) * sub
                bs, qs, ks, vs = b[s0:s1], q[s0:s1], kk[s0:s1], v[s0:s1]
                o_i = o_state[s0:s1]
                if i > 0:
                    r_i = b[s0 - 1:s0, :]
                    qt = (qs * jnp.exp(bs - r_i)).astype(BF16)
                    kt = (kk[:s0] * jnp.exp(r_i - b[:s0])).astype(BF16)
                    a_off = _dot_nt(qt, kt)
                    o_i = o_i + _dot(a_off.astype(BF16), vb[:s0])
                for s in range(sub):
                    e = jnp.exp(jnp.where(t_iota >= s, bs - bs[s:s + 1], -jnp.inf))
                    a_col = jnp.sum(qs * e * ks[s:s + 1], axis=1, keepdims=True)
                    o_i = o_i + a_col * vs[s:s + 1]
                rows.append(o_i)
            finish(r0, g, jnp.concatenate(rows, axis=0) if n_sub > 1 else rows[0])
        return carry

    factoring_safe = jnp.max(-jnp.log(lb_ref[...])) * chunk <= HG_SAFE_EXPONENT

    @pl.when(factoring_safe)
    def _():
        for c in range(n_inner):
            factored_chunk(c, 0)

    @pl.when(jnp.logical_not(factoring_safe))
    def _():
        lax.fori_loop(0, n_inner, guarded_chunk, 0)

    @pl.when(c_idx == pl.num_programs(2) - 1)
    def _():
        sT_out_ref[0] = sT_scr[...]


def _hgrn(p, cols, lb, ng, s0, *, batch, seq, chunk, sub, heads, n_inner):
    m = batch * seq
    rows = chunk * n_inner
    nc = seq // rows
    hw = heads * HG_DIM
    nhg = HG_HEADS // heads
    cf, cq, ci, cg = (cols[k] // hw for k in ("hf", "hq", "hi", "hg"))
    return pl.pallas_call(
        functools.partial(_hgrn_kernel, chunk=chunk, sub=sub, heads=heads, n_inner=n_inner),
        out_shape=(jax.ShapeDtypeStruct((m, HG_W), BF16),
                   jax.ShapeDtypeStruct((batch, HG_HEADS, HG_DIM, HG_DIM), F32)),
        grid=(batch, nhg, nc),
        in_specs=[
            pl.BlockSpec((rows, hw), lambda b, h, c: (b * nc + c, cf + h)),
            pl.BlockSpec((rows, hw), lambda b, h, c: (b * nc + c, cq + h)),
            pl.BlockSpec((rows, hw), lambda b, h, c: (b * nc + c, ci + h)),
            pl.BlockSpec((rows, hw), lambda b, h, c: (b * nc + c, cg + h)),
            pl.BlockSpec((1, hw), lambda b, h, c: (0, h)),
            pl.BlockSpec((1, hw), lambda b, h, c: (0, h)),
            pl.BlockSpec((heads, HG_DIM, HG_DIM), lambda b, h, c: (h, 0, 0)),
        ],
        out_specs=(
            pl.BlockSpec((rows, hw), lambda b, h, c: (b * nc + c, h)),
            pl.BlockSpec((1, heads, HG_DIM, HG_DIM), lambda b, h, c: (b, h, 0, 0)),
        ),
        scratch_shapes=[pltpu.VMEM((heads, HG_DIM, HG_DIM), F32)],
        compiler_params=pltpu.CompilerParams(
            dimension_semantics=("parallel", "parallel", "arbitrary"), vmem_limit_bytes=VMEM_LIMIT),
        name="hgrn2",
    )(p, p, p, p, lb, ng, s0)


def _merge_kernel(ya_ref, yh_ref, gts_a_ref, gts_h_ref, h_ref, wa_ref, wh_ref, wo_ref, g_ref, b_ref,
                  o_ref, acc_scr):
    j = pl.program_id(1)

    @pl.when(j == 0)
    def _():
        acc_scr[...] = jnp.zeros_like(acc_scr)

    ga = jax.nn.sigmoid(gts_a_ref[...].astype(F32))
    gh = jax.nn.sigmoid(gts_h_ref[...].astype(F32))
    merged = ga * _dot(ya_ref[...], wa_ref[...]) + gh * _dot(yh_ref[...], wh_ref[...])
    acc_scr[...] += _dot(merged.astype(BF16), wo_ref[...])

    @pl.when(j == pl.num_programs(1) - 1)
    def _():
        o_ref[...] = _layer_norm(ALPHA * h_ref[...] + acc_scr[...], g_ref[...], b_ref[...])


def _merge(ya, yh, gates, h1, wa, wh, wo, g, b, *, tm, tn):
    m = ya.shape[0]
    nj = D_MODEL // tn
    ca, cb = PROJ_COL["ga"] // tn, PROJ_COL["gb"] // tn
    return pl.pallas_call(
        _merge_kernel,
        out_shape=jax.ShapeDtypeStruct((m, D_MODEL), F32),
        grid=(m // tm, nj),
        in_specs=[
            pl.BlockSpec((tm, ATT_Q_W), lambda i, j: (i, 0)),
            pl.BlockSpec((tm, HG_W), lambda i, j: (i, 0)),
            pl.BlockSpec((tm, tn), lambda i, j: (i, ca + j)),
            pl.BlockSpec((tm, tn), lambda i, j: (i, cb + j)),
            pl.BlockSpec((tm, D_MODEL), lambda i, j: (i, 0)),
            pl.BlockSpec((ATT_Q_W, tn), lambda i, j: (0, j)),
            pl.BlockSpec((HG_W, tn), lambda i, j: (0, j)),
            pl.BlockSpec((tn, D_MODEL), lambda i, j: (j, 0)),
            pl.BlockSpec((1, D_MODEL), lambda i, j: (0, 0)),
            pl.BlockSpec((1, D_MODEL), lambda i, j: (0, 0)),
        ],
        out_specs=pl.BlockSpec((tm, D_MODEL), lambda i, j: (i, 0)),
        scratch_shapes=[pltpu.VMEM((tm, D_MODEL), F32)],
        compiler_params=pltpu.CompilerParams(
            dimension_semantics=("parallel", "arbitrary"), vmem_limit_bytes=VMEM_LIMIT),
        name="merge_ln",
    )(ya, yh, gates, gates, h1, wa, wh, wo, g, b)


def kernel(x, meta, ffn1_w_gate, ffn1_w_up, ffn1_w_down, ln1_g, ln1_b, w_in, idx_k_norm_g, idx_k_norm_b,
           hg_lb_logits, hg_norm_g, w_branch_att, w_branch_hg, w_out, ln2_g, ln2_b,
           ffn2_w_gate, ffn2_w_up, ffn2_w_down, ln3_g, ln3_b):
    batch, seq, _ = x.shape
    m = batch * seq
    xr = x.reshape(m, D_MODEL)
    bf = lambda w: w.astype(BF16)
    row = lambda v: v.reshape(1, -1)

    w_in_t = w_in[0].T
    w_idx = bf(jnp.pad(w_in_t[IN_OFFS[4]:IN_OFFS[6]].T, ((0, 0), (0, LANES - IDX_HEAD_DIM - N_IDX_HEADS))))
    idx_g = jnp.pad(idx_k_norm_g[0], (0, LANES - IDX_HEAD_DIM)).reshape(1, LANES)
    idx_b = jnp.pad(idx_k_norm_b[0], (0, LANES - IDX_HEAD_DIM)).reshape(1, LANES)
    lb = jnp.cumsum(jax.nn.softmax(hg_lb_logits.astype(F32), axis=0), axis=0)[0].reshape(1, HG_W)
    ng = hg_norm_g[0].reshape(1, HG_W)

    f1 = (ffn1_w_gate[0], ffn1_w_up[0], ffn1_w_down[0], row(ln1_g[0]), row(ln1_b[0]))
    f2 = (ffn2_w_gate[0], ffn2_w_up[0], ffn2_w_down[0], row(ln3_g[0]), row(ln3_b[0]))

    hm = _ffn_ln(meta.astype(F32), *f1, tm=N_META, tf=FFN_TF)
    pm = _proj_all(hm, w_in_t, META_TILE_SRC, tm=N_META, scaled_tiles=0)
    kv0 = META_COL["kv"]
    km = jnp.pad(pm[:, kv0:kv0 + ATT_KV_W], ((0, LANES - N_META), (0, 0)))
    vm = jnp.pad(pm[:, kv0 + ATT_KV_W:kv0 + 2 * ATT_KV_W], ((0, LANES - N_META), (0, 0)))
    s_zero = jnp.zeros((HG_HEADS, HG_DIM, HG_DIM), F32)
    meta_cols = {"hf": META_COL["hf"], "hi": META_COL["hi"], "hq": META_COL["hf"], "hg": META_COL["hi"]}
    _, s_meta = _hgrn(pm, meta_cols, lb, ng, s_zero, batch=1, seq=N_META, chunk=N_META, sub=N_META,
                      heads=HG_HEADS, n_inner=1)

    h1 = _ffn_ln(xr, *f1, tm=FFN_TM, tf=FFN_TF)
    p = _proj_all(h1, w_in_t, PROJ_TILE_SRC, tm=1024, scaled_tiles=ATT_Q_W // PROJ_TN)
    p_idx = _proj_idx(h1, w_idx, idx_g, idx_b, tm=1024)

    ikn = p_idx[:, :IDX_HEAD_DIM].astype(BF16)
    ik_lo = jnp.pad(ikn, ((0, 0), (0, LANES - IDX_HEAD_DIM)))
    ik_hi = jnp.pad(ikn, ((0, 0), (LANES - IDX_HEAD_DIM, 0)))
    iw_t = p_idx[:, IDX_HEAD_DIM:IDX_HEAD_DIM + N_IDX_HEADS].T
    y_att = _attention(p, iw_t, ik_lo, ik_hi, km, vm, batch=batch, seq=seq, tq=256, kc=512)
    y_hg, _ = _hgrn(p, PROJ_COL, lb, ng, s_meta[0], batch=batch, seq=seq, chunk=64, sub=16,
                    heads=HG_HEADS, n_inner=8)

    h2 = _merge(y_att, y_hg, p, h1, bf(w_branch_att[0]), bf(w_branch_hg[0]), bf(w_out[0]),
                row(ln2_g[0]), row(ln2_b[0]), tm=512, tn=512)
    out = _ffn_ln(h2, *f2, tm=FFN_TM, tf=FFN_TF)
    return out.reshape(batch, seq, D_MODEL)
```

```python
import functools
import math

import jax
import jax.numpy as jnp
import numpy as np
from jax import lax
from jax.experimental import pallas as pl
from jax.experimental.pallas import tpu as pltpu

D_MODEL = 2048
N_META = 16
N_ATT_HEADS = 8
N_KV_HEADS = 2
ATT_GROUP = N_ATT_HEADS // N_KV_HEADS
ATT_HEAD_DIM = 128
N_IDX_HEADS = 16
IDX_HEAD_DIM = 64
TOPK = 256
HG_HEADS = 8
HG_DIM = 128
D_FF = 5632
LN_EPS = 1e-5
RMS_EPS = 1e-6
ALPHA = 2.0 ** 0.25

ATT_Q_W = N_ATT_HEADS * ATT_HEAD_DIM
ATT_KV_W = N_KV_HEADS * ATT_HEAD_DIM
IDX_Q_W = N_IDX_HEADS * IDX_HEAD_DIM
HG_W = HG_HEADS * HG_DIM

LANES = 128
SUBLANES = 8
VMEM_LIMIT = 56 * 1024 * 1024

F32 = jnp.float32
BF16 = jnp.bfloat16
INT_MIN = -2 ** 31
F32_LOWEST = float(np.finfo(np.float32).min)
LOG2E = math.log2(math.e)
ATT_Q_SCALE = ATT_HEAD_DIM ** -0.5 * LOG2E
POS_SHIFT = 6
POS_SPLIT = 1 << POS_SHIFT


def _dot(a, b):
    return jnp.dot(a, b, preferred_element_type=F32)


def _dot_nt(a, b):
    return lax.dot_general(a, b, (((1,), (1,)), ((), ())), preferred_element_type=F32)


def _dot_tn(a, b):
    return lax.dot_general(a, b, (((0,), (0,)), ((), ())), preferred_element_type=F32)


def _layer_norm(y, g, b):
    mu = jnp.mean(y, axis=-1, keepdims=True)
    d = y - mu
    var = jnp.mean(d * d, axis=-1, keepdims=True)
    return d * lax.rsqrt(var + LN_EPS) * g + b


def _silu(x):
    return x * jax.nn.sigmoid(x)


def _bf16_parts(c, n=3):
    parts = []
    for _ in range(n):
        p = float(np.asarray(c, np.float32).astype(jnp.bfloat16).astype(np.float32))
        parts.append(p)
        c = c - p
    return parts


FFN_TM = 1024
FFN_TF = 256
LN_ROWS = 128


def _ffn_ln_kernel(x_ref, wg_ref, wu_ref, wd_ref, g_ref, b_ref, o_ref, xb_scr):
    j = pl.program_id(1)

    @pl.when(j == 0)
    def _():
        xb_scr[...] = x_ref[...].astype(BF16)
        o_ref[...] = jnp.zeros_like(o_ref)

    xb = xb_scr[...]
    gate = _dot(xb, wg_ref[...].astype(BF16))
    up = _dot(xb, wu_ref[...].astype(BF16))
    a = (_silu(gate) * up).astype(BF16)
    o_ref[...] += _dot(a, wd_ref[...].astype(BF16))

    @pl.when(j == pl.num_programs(1) - 1)
    def _():
        n_rows = min(LN_ROWS, o_ref.shape[0])

        def ln_rows(r, carry):
            rows = pl.ds(pl.multiple_of(r * n_rows, n_rows), n_rows)
            y = ALPHA * x_ref[rows, :] + 0.5 * o_ref[rows, :]
            o_ref[rows, :] = _layer_norm(y, g_ref[...], b_ref[...])
            return carry

        lax.fori_loop(0, o_ref.shape[0] // n_rows, ln_rows, 0)


def _ffn_ln(x, wg, wu, wd, g, b, *, tm, tf):
    m = x.shape[0]
    grid = (m // tm, D_FF // tf)
    return pl.pallas_call(
        _ffn_ln_kernel,
        out_shape=jax.ShapeDtypeStruct((m, D_MODEL), F32),
        grid=grid,
        in_specs=[
            pl.BlockSpec((tm, D_MODEL), lambda i, j: (i, 0)),
            pl.BlockSpec((D_MODEL, tf), lambda i, j: (0, j)),
            pl.BlockSpec((D_MODEL, tf), lambda i, j: (0, j)),
            pl.BlockSpec((tf, D_MODEL), lambda i, j: (j, 0)),
            pl.BlockSpec((1, D_MODEL), lambda i, j: (0, 0)),
            pl.BlockSpec((1, D_MODEL), lambda i, j: (0, 0)),
        ],
        out_specs=pl.BlockSpec((tm, D_MODEL), lambda i, j: (i, 0)),
        scratch_shapes=[pltpu.VMEM((tm, D_MODEL), BF16)],
        compiler_params=pltpu.CompilerParams(
            dimension_semantics=("parallel", "arbitrary"), vmem_limit_bytes=VMEM_LIMIT),
        name="ffn_ln",
    )(x, wg, wu, wd, g, b)


IN_SPLITS = (ATT_Q_W, ATT_KV_W, ATT_KV_W, IDX_Q_W, IDX_HEAD_DIM, N_IDX_HEADS, HG_W, HG_W, HG_W, HG_W,
             D_MODEL, D_MODEL)
IN_OFFS = tuple(int(v) for v in np.cumsum((0,) + IN_SPLITS))
PROJ_TN = 1024
PROJ_GROUPS = {"aq": (IN_OFFS[0], ATT_Q_W), "iq": (IN_OFFS[3], IDX_Q_W), "hq": (IN_OFFS[6], HG_W),
               "hf": (IN_OFFS[7], HG_W), "hi": (IN_OFFS[8], HG_W), "hg": (IN_OFFS[9], HG_W),
               "ga": (IN_OFFS[10], D_MODEL), "gb": (IN_OFFS[11], D_MODEL), "kv": (IN_OFFS[1], 2 * ATT_KV_W)}
PROJ_SRC_ALIGN = 16


def _proj_layout(names):
    col, src = {}, []
    for name in names:
        start, width = PROJ_GROUPS[name]
        col[name] = len(src) * PROJ_TN
        src += [start + t * PROJ_TN for t in range(-(-width // PROJ_TN))]
    assert all(s % PROJ_SRC_ALIGN == 0 and s + PROJ_TN <= IN_OFFS[-1] for s in src)
    return col, tuple(src)


PROJ_COL, PROJ_TILE_SRC = _proj_layout(("aq", "iq", "hq", "hf", "hi", "hg", "ga", "gb", "kv"))
META_COL, META_TILE_SRC = _proj_layout(("hf", "hi", "kv"))


def _proj_all_kernel(src_ref, x_ref, wt_ref, o_ref, xb_scr, *, scaled_tiles, scale):
    @pl.when(pl.program_id(1) == 0)
    def _():
        xb_scr[...] = x_ref[...].astype(BF16)

    acc = _dot_nt(xb_scr[...], wt_ref[...].astype(BF16))
    acc = acc * jnp.where(pl.program_id(1) < scaled_tiles, scale, 1.0)
    o_ref[...] = acc.astype(o_ref.dtype)


def _idx_kernel(x_ref, w_ref, g_ref, b_ref, o_ref):
    p = _dot(x_ref[...].astype(BF16), w_ref[...])
    lane = lax.broadcasted_iota(jnp.int32, p.shape, 1)
    is_k = lane < IDX_HEAD_DIM
    mu = jnp.sum(jnp.where(is_k, p, 0.0), axis=-1, keepdims=True) * (1.0 / IDX_HEAD_DIM)
    d = jnp.where(is_k, p - mu, 0.0)
    var = jnp.sum(d * d, axis=-1, keepdims=True) * (1.0 / IDX_HEAD_DIM)
    kn = d * lax.rsqrt(var + LN_EPS) * g_ref[...] + b_ref[...]
    w_scale = (N_IDX_HEADS ** -0.5) * (IDX_HEAD_DIM ** -0.5)
    o_ref[...] = jnp.where(is_k, kn, p * w_scale)


def _proj_all(xb, w_in_t, tile_src, *, tm, scaled_tiles):
    m = xb.shape[0]
    n_tiles = len(tile_src)
    return pl.pallas_call(
        functools.partial(_proj_all_kernel, scaled_tiles=scaled_tiles, scale=ATT_Q_SCALE),
        out_shape=jax.ShapeDtypeStruct((m, n_tiles * PROJ_TN), BF16),
        grid_spec=pltpu.PrefetchScalarGridSpec(
            num_scalar_prefetch=1,
            grid=(m // tm, n_tiles),
            in_specs=[pl.BlockSpec((tm, D_MODEL), lambda i, j, src: (i, 0)),
                      pl.BlockSpec((pl.Element(PROJ_TN), pl.Element(D_MODEL)),
                                   lambda i, j, src: (src[j] * PROJ_SRC_ALIGN, 0))],
            out_specs=pl.BlockSpec((tm, PROJ_TN), lambda i, j, src: (i, j)),
            scratch_shapes=[pltpu.VMEM((tm, D_MODEL), BF16)],
        ),
        compiler_params=pltpu.CompilerParams(
            dimension_semantics=("parallel", "arbitrary"), vmem_limit_bytes=VMEM_LIMIT),
        name="proj_all",
    )(jnp.asarray([s // PROJ_SRC_ALIGN for s in tile_src], jnp.int32), xb, w_in_t)


def _proj_idx(xb, w, g, b, *, tm):
    m = xb.shape[0]
    return pl.pallas_call(
        _idx_kernel,
        out_shape=jax.ShapeDtypeStruct((m, LANES), F32),
        grid=(m // tm,),
        in_specs=[pl.BlockSpec((tm, D_MODEL), lambda i: (i, 0)),
                  pl.BlockSpec((D_MODEL, LANES), lambda i: (0, 0)),
                  pl.BlockSpec((1, LANES), lambda i: (0, 0)),
                  pl.BlockSpec((1, LANES), lambda i: (0, 0))],
        out_specs=pl.BlockSpec((tm, LANES), lambda i: (i, 0)),
        compiler_params=pltpu.CompilerParams(
            dimension_semantics=("parallel",), vmem_limit_bytes=VMEM_LIMIT),
        name="proj_idx",
    )(xb, w, g, b)


ATT_ROW_TILE = 32
ATT_BOUND_SLACK = 45.0
UNCHECKED_BITS = 16


def _key_to_f32(key):
    return pltpu.bitcast(key ^ ((key >> 31) & 0x7FFFFFFF), F32)


def _attn_kernel(aq_ref, iq_ref, iwt_ref, iklo_ref, ikhi_ref, k_ref, v_ref, km_ref, vm_ref,
                 o_ref,
                 score_scr, mb_scr, kaug_scr, vaug_scr, qaug_scr, s0_scr, s1_scr, p_scr,
                 macc_scr, m_scr, acc_scr, kn_scr, bound_scr, eye_scr, *, tq, kc, seq):
    qi = pl.program_id(1)
    n_kc = ((qi + 1) * tq + kc - 1) // kc
    kf = float(TOPK)

    @pl.when(qi == 0)
    def _():
        r = lax.broadcasted_iota(jnp.int32, (LANES + seq, LANES), 0)
        lane = lax.broadcasted_iota(jnp.int32, (LANES + seq, LANES), 1)
        pos = jnp.where(r < LANES, r, r - LANES + N_META)
        feat = jnp.where(lane < 3, pos >> POS_SHIFT, jnp.where(lane < 6, pos & (POS_SPLIT - 1), 0))
        feat = feat.astype(F32).astype(BF16)
        ones = jnp.ones((LANES + seq, LANES), BF16)
        eye_scr[...] = (lax.broadcasted_iota(jnp.int32, (tq, tq), 0)
                        == lax.broadcasted_iota(jnp.int32, (tq, tq), 1)).astype(BF16)
        for kvh in range(N_KV_HEADS):
            cs = slice(kvh * LANES, (kvh + 1) * LANES)
            kaug_scr[kvh, 0:LANES, 0:LANES] = km_ref[:, cs]
            kaug_scr[kvh, LANES:, 0:LANES] = k_ref[:, cs]
            kaug_scr[kvh, :, LANES:] = feat
            vaug_scr[kvh, 0:LANES, 0:LANES] = vm_ref[:, cs]
            vaug_scr[kvh, LANES:, 0:LANES] = v_ref[:, cs]
            vaug_scr[kvh, :, LANES:] = ones
            kf32 = kaug_scr[kvh, :, 0:LANES].astype(F32)
            k_norm2 = jnp.max(jnp.sum(kf32 * kf32, axis=1, keepdims=True))
            kn_scr[kvh] = jnp.broadcast_to(k_norm2, kn_scr.shape[1:])

    qcol = qi * tq + lax.broadcasted_iota(jnp.int32, (1, tq), 1)
    iwt = iwt_ref[...]

    def score_body(j, carry):
        off = pl.multiple_of(j * kc, kc)
        klo = iklo_ref[pl.ds(off, kc), :]
        khi = ikhi_ref[pl.ds(off, kc), :]
        acc = jnp.zeros((kc, tq), F32)
        for p in range(N_IDX_HEADS // 2):
            q2 = iq_ref[:, p * LANES:(p + 1) * LANES]
            acc = acc + jnp.maximum(_dot_nt(klo, q2), 0.0) * iwt[2 * p:2 * p + 1, :]
            acc = acc + jnp.maximum(_dot_nt(khi, q2), 0.0) * iwt[2 * p + 1:2 * p + 2, :]
        krow = off + lax.broadcasted_iota(jnp.int32, (kc, tq), 0)
        score_scr[pl.ds(off, kc), :] = jnp.where(krow <= qcol, acc, -jnp.inf)
        return carry

    lax.fori_loop(0, n_kc, score_body, 0)

    n_acc = 8
    acc_rows = n_acc * SUBLANES

    def count_ge(cand):
        def body(j, acc):
            off = pl.multiple_of(j * kc, kc)
            w = jnp.where(score_scr[pl.ds(off, kc), :] >= cand, 1.0, 0.0)
            return acc + jnp.sum(w.reshape(kc // acc_rows, acc_rows, tq), axis=0)

        acc = lax.fori_loop(0, n_kc, body, jnp.zeros((acc_rows, tq), F32))
        return jnp.sum(acc, axis=0, keepdims=True)

    c0 = count_ge(jnp.zeros((1, tq), F32))
    ok0 = c0 >= kf
    thr0 = jnp.where(ok0, 0, INT_MIN).astype(jnp.int32)
    cnt0 = jnp.where(ok0, c0, 0.0)

    short_row = qcol + 1 < TOPK

    def unsettled(cnt):
        return jnp.sum(jnp.where((cnt == kf) | short_row, 0.0, 1.0))

    n_bits = 31
    group = 4

    def bit_cond(carry):
        i, _, _, pending = carry
        return jnp.logical_and(i < n_bits, pending > 0.0)

    def resolve_bit(i, thr, cnt):
        shift = jnp.maximum(n_bits - 1 - i, 0)
        cand = thr | jnp.where(i < n_bits, jnp.left_shift(jnp.int32(1), shift), 0)
        c = count_ge(_key_to_f32(cand))
        ok = c >= kf
        return jnp.where(ok, cand, thr), jnp.where(ok, c, cnt)

    def bit_body(carry):
        i, thr, cnt, _ = carry
        for b in range(group):
            thr, cnt = resolve_bit(i + b, thr, cnt)
        return i + group, thr, cnt, unsettled(cnt)

    thr1, cnt1 = lax.fori_loop(0, UNCHECKED_BITS, lambda i, c: resolve_bit(i, *c), (thr0, cnt0))
    _, thr, cnt, _ = lax.while_loop(
        bit_cond, bit_body, (jnp.int32(UNCHECKED_BITS), thr1, cnt1, unsettled(cnt1)))
    t_f = jnp.where(thr == INT_MIN, F32_LOWEST, _key_to_f32(thr))

    @pl.when(jnp.max(cnt) > kf)
    def _():
        def gt_body(j, acc):
            off = pl.multiple_of(j * kc, kc)
            w = jnp.where(score_scr[pl.ds(off, kc), :] > t_f, 1.0, 0.0)
            return acc + jnp.sum(w, axis=0, keepdims=True)

        need = kf - lax.fori_loop(0, n_kc, gt_body, jnp.zeros((1, tq), F32))
        lower = (lax.broadcasted_iota(jnp.int32, (kc, kc), 0)
                 >= lax.broadcasted_iota(jnp.int32, (kc, kc), 1)).astype(BF16)

        def tie_body(j, before):
            off = pl.multiple_of(j * kc, kc)
            sc = score_scr[pl.ds(off, kc), :]
            eq = sc == t_f
            eqf = jnp.where(eq, 1.0, 0.0)
            rank = before + _dot(lower, eqf.astype(BF16))
            score_scr[pl.ds(off, kc), :] = jnp.where(eq & (rank > need), -jnp.inf, sc)
            return before + jnp.sum(eqf, axis=0, keepdims=True)

        lax.fori_loop(0, n_kc, tie_body, jnp.zeros((1, tq), F32))

    eye = eye_scr[...]

    def mask_body(j, last_sel):
        off = pl.multiple_of(j * kc, kc)
        picked = score_scr[pl.ds(off, kc), :] >= t_f
        sel = _dot_nt(eye, jnp.where(picked, 1.0, 0.0).astype(BF16))
        mb_scr[:, pl.ds(off, kc)] = jnp.where(sel > 0.5, 0.0, -jnp.inf)
        krow = (off + lax.broadcasted_iota(jnp.int32, (kc, tq), 0)).astype(F32)
        hit = jnp.where(picked, krow, -1.0)
        return jnp.maximum(last_sel, jnp.max(hit.reshape(kc // SUBLANES, SUBLANES, tq), axis=0))

    last_sel = lax.fori_loop(0, n_kc, mask_body, jnp.full((SUBLANES, tq), -1.0, F32))
    last_pos = jnp.maximum(jnp.max(last_sel, axis=0, keepdims=True) + N_META, N_META - 1.0)

    rt = ATT_ROW_TILE
    lane_q = lax.broadcasted_iota(jnp.int32, (1, LANES), 1)
    meta_mask = jnp.where(lax.broadcasted_iota(jnp.int32, (rt, LANES), 1) < N_META, 0.0, -jnp.inf)

    s_slots = (s0_scr, s1_scr)

    def logits(kvh, krow0, width, slot):
        s_slots[slot][:, 0:width] = _dot_nt(qaug_scr[kvh], kaug_scr[kvh, pl.ds(krow0, width), :])

    def max_tiles(kvh, slot, width, mask_fn):
        s_ref = s_slots[slot]
        for r0 in range(0, tq, rt):
            mb = mask_fn(r0)
            for g in range(ATT_GROUP):
                rows = slice(g * tq + r0, g * tq + r0 + rt)
                mx = s_ref[rows, 0:LANES] + mb[:, 0:LANES]
                for c in range(1, width // LANES):
                    cols = slice(c * LANES, (c + 1) * LANES)
                    mx = jnp.maximum(mx, s_ref[rows, cols] + mb[:, cols])
                macc_scr[rows, :] = jnp.maximum(macc_scr[rows, :], mx)

    def exp_tiles(kvh, slot, width, mask_fn):
        s_ref = s_slots[slot]
        for r0 in range(0, tq, rt):
            mb = mask_fn(r0)
            for g in range(ATT_GROUP):
                rows = slice(g * tq + r0, g * tq + r0 + rt)
                m_row = m_scr[rows, :]
                for c in range(width // LANES):
                    cols = slice(c * LANES, (c + 1) * LANES)
                    p_scr[rows, cols] = jnp.exp2(s_ref[rows, cols] + mb[:, cols] - m_row).astype(BF16)

    def add_pv(kvh, krow0, width):
        vblk = vaug_scr[kvh, pl.ds(krow0, width), :]
        half = ATT_GROUP * tq // 2
        for r0 in (0, half):
            acc_scr[kvh, r0:r0 + half, :] += _dot(p_scr[r0:r0 + half, 0:width], vblk)

    def run_pass(kvh, tiles_fn, with_pv):
        def krow(j):
            return pl.multiple_of(LANES + j * kc, LANES)

        def step(j, slot, lookahead):
            off = pl.multiple_of(j * kc, kc)
            if lookahead:
                logits(kvh, krow(j + 1), kc, 1 - slot)
            tiles_fn(kvh, slot, kc, lambda r0: mb_scr[r0:r0 + rt, pl.ds(off, kc)])
            if with_pv:
                add_pv(kvh, krow(j), kc)

        logits(kvh, 0, LANES, 0)
        tiles_fn(kvh, 0, LANES, lambda r0: meta_mask)
        if with_pv:
            add_pv(kvh, 0, LANES)

            def serial_body(j, carry):
                off = pl.multiple_of(j * kc, kc)
                logits(kvh, krow(j), kc, 0)
                tiles_fn(kvh, 0, kc, lambda r0: mb_scr[r0:r0 + rt, pl.ds(off, kc)])
                add_pv(kvh, krow(j), kc)
                return carry

            lax.fori_loop(0, n_kc, serial_body, 0)
            return
        logits(kvh, krow(0), kc, 0)
        n_pairs = (n_kc - 1) // 2

        def pair_body(i, carry):
            step(2 * i, 0, True)
            step(2 * i + 1, 1, True)
            return carry

        lax.fori_loop(0, n_pairs, pair_body, 0)
        j0 = 2 * n_pairs
        two_left = n_kc - j0 == 2

        @pl.when(two_left)
        def _():
            step(j0, 0, True)
            step(j0 + 1, 1, False)

        @pl.when(jnp.logical_not(two_left))
        def _():
            step(j0, 0, False)

    for h in range(N_ATT_HEADS):
        kvh, g = divmod(h, ATT_GROUP)
        parts = _bf16_parts(2.0 ** (-8.0 * (h + 1) / N_ATT_HEADS) * LOG2E)
        vals = [POS_SPLIT * p for p in parts] + parts
        qfeat = jnp.zeros((1, LANES), F32)
        for i, val in enumerate(vals):
            qfeat = jnp.where(lane_q == i, val, qfeat)
        qaug_scr[kvh, g * tq:(g + 1) * tq, 0:LANES] = aq_ref[:, h * LANES:(h + 1) * LANES]
        qaug_scr[kvh, g * tq:(g + 1) * tq, LANES:] = jnp.broadcast_to(qfeat, (tq, LANES)).astype(BF16)

    a_pos = jnp.floor(last_pos * (1.0 / POS_SPLIT))
    b_pos = last_pos - a_pos * POS_SPLIT
    to_rows = lambda v: _dot_nt(eye, jnp.broadcast_to(v, (LANES, tq)).astype(BF16))
    last_pos_rows = to_rows(a_pos) * POS_SPLIT + to_rows(b_pos)
    slack = jnp.float32(0.0)
    for kvh in range(N_KV_HEADS):
        qf = qaug_scr[kvh, :, 0:LANES].astype(F32)
        qk_max = jnp.sqrt(jnp.sum(qf * qf, axis=1, keepdims=True) * kn_scr[kvh, 0:1, 0:1])
        slack = jnp.maximum(slack, jnp.max(qk_max))
        for g in range(ATT_GROUP):
            rows = slice(g * tq, (g + 1) * tq)
            c = 2.0 ** (-8.0 * (kvh * ATT_GROUP + g + 1) / N_ATT_HEADS) * LOG2E
            bound_scr[kvh, rows, :] = c * last_pos_rows + qk_max[rows]
    bound_ok = slack <= ATT_BOUND_SLACK

    for kvh in range(N_KV_HEADS):
        @pl.when(bound_ok)
        def _():
            m_scr[...] = bound_scr[kvh]

        @pl.when(jnp.logical_not(bound_ok))
        def _():
            macc_scr[...] = jnp.full_like(macc_scr, -jnp.inf)
            run_pass(kvh, max_tiles, with_pv=False)
            m_scr[...] = jnp.broadcast_to(jnp.max(macc_scr[...], axis=1, keepdims=True), m_scr.shape)

        acc_scr[kvh] = jnp.zeros(acc_scr.shape[1:], F32)
        run_pass(kvh, exp_tiles, with_pv=True)

    for h in range(N_ATT_HEADS):
        kvh, g = divmod(h, ATT_GROUP)
        acc = acc_scr[kvh, g * tq:(g + 1) * tq, :]
        o_ref[:, h * LANES:(h + 1) * LANES] = (acc[:, 0:LANES] / acc[:, LANES:LANES + 1]).astype(o_ref.dtype)


def _attention(p_att, iw_t, ik_lo, ik_hi, km, vm, *, batch, seq, tq, kc):
    m = batch * seq
    nq = seq // tq
    qcol, icol, kcol = PROJ_COL["aq"] // ATT_Q_W, PROJ_COL["iq"] // IDX_Q_W, PROJ_COL["kv"] // ATT_KV_W
    rows4 = ATT_GROUP * tq
    return pl.pallas_call(
        functools.partial(_attn_kernel, tq=tq, kc=kc, seq=seq),
        out_shape=jax.ShapeDtypeStruct((m, ATT_Q_W), BF16),
        grid=(batch, nq),
        in_specs=[
            pl.BlockSpec((tq, ATT_Q_W), lambda b, q: (b * nq + q, qcol)),
            pl.BlockSpec((tq, IDX_Q_W), lambda b, q: (b * nq + q, icol)),
            pl.BlockSpec((N_IDX_HEADS, tq), lambda b, q: (0, b * nq + q)),
            pl.BlockSpec((seq, LANES), lambda b, q: (b, 0)),
            pl.BlockSpec((seq, LANES), lambda b, q: (b, 0)),
            pl.BlockSpec((seq, ATT_KV_W), lambda b, q: (b, kcol)),
            pl.BlockSpec((seq, ATT_KV_W), lambda b, q: (b, kcol + 1)),
            pl.BlockSpec((LANES, ATT_KV_W), lambda b, q: (0, 0)),
            pl.BlockSpec((LANES, ATT_KV_W), lambda b, q: (0, 0)),
        ],
        out_specs=pl.BlockSpec((tq, ATT_Q_W), lambda b, q: (b * nq + q, 0)),
        scratch_shapes=[
            pltpu.VMEM((seq, tq), F32),
            pltpu.VMEM((tq, seq), F32),
            pltpu.VMEM((N_KV_HEADS, LANES + seq, 2 * LANES), BF16),
            pltpu.VMEM((N_KV_HEADS, LANES + seq, 2 * LANES), BF16),
            pltpu.VMEM((N_KV_HEADS, rows4, 2 * LANES), BF16),
            pltpu.VMEM((rows4, kc), F32),
            pltpu.VMEM((rows4, kc), F32),
            pltpu.VMEM((rows4, kc), BF16),
            pltpu.VMEM((rows4, LANES), F32),
            pltpu.VMEM((rows4, LANES), F32),
            pltpu.VMEM((N_KV_HEADS, rows4, 2 * LANES), F32),
            pltpu.VMEM((N_KV_HEADS, SUBLANES, LANES), F32),
            pltpu.VMEM((N_KV_HEADS, rows4, LANES), F32),
            pltpu.VMEM((tq, tq), BF16),
        ],
        compiler_params=pltpu.CompilerParams(
            dimension_semantics=("parallel", "arbitrary"), vmem_limit_bytes=VMEM_LIMIT),
        name="dsa_attention",
    )(p_att, p_att, iw_t, ik_lo, ik_hi, p_att, p_att, km, vm)


def _split3(x):
    hi = x.astype(BF16)
    r = x - hi.astype(F32)
    mid = r.astype(BF16)
    lo = (r - mid.astype(F32)).astype(BF16)
    return hi, mid, lo


HG_SAFE_EXPONENT = 80.0


def _hgrn_kernel(hf_ref, hr_q_ref, hr_i_ref, hr_g_ref, lb_ref, ng_ref, s0_ref,
                 y_ref, sT_out_ref, sT_scr, *, chunk, sub, heads, n_inner):
    c_idx = pl.program_id(2)

    @pl.when(c_idx == 0)
    def _():
        sT_scr[...] = s0_ref[...]

    n_sub = chunk // sub
    causal = (lax.broadcasted_iota(jnp.int32, (chunk, chunk), 0)
              >= lax.broadcasted_iota(jnp.int32, (chunk, chunk), 1))
    tri = causal.astype(BF16)
    t_iota = lax.broadcasted_iota(jnp.int32, (sub, LANES), 0)

    def head_inputs(r0, g):
        rows, cs = pl.ds(r0, chunk), slice(g * HG_DIM, (g + 1) * HG_DIM)
        lb = lb_ref[:, cs]
        q = _silu(hr_q_ref[rows, cs].astype(F32))
        fg = lb + (1.0 - lb) * jax.nn.sigmoid(hf_ref[rows, cs].astype(F32))
        v = hr_i_ref[rows, cs].astype(F32)
        l_hi, l_mid, l_lo = _split3(jnp.log(fg))
        b = _dot(tri, l_hi) + _dot(tri, l_mid) + _dot(tri, l_lo)
        return q, 1.0 - fg, v, b

    def carry_state(g, q, kk, vb, b):
        sT = sT_scr[g]
        b_last = b[chunk - 1:chunk, :]
        qe = (q * jnp.exp(b)).astype(BF16)
        o_state = _dot_nt(qe, sT.astype(BF16))
        khat = (kk * jnp.exp(b_last - b)).astype(BF16)
        sT_scr[g] = sT * jnp.exp(b_last) + _dot_tn(vb, khat)
        return qe, o_state

    def finish(r0, g, o):
        rows, cs = pl.ds(r0, chunk), slice(g * HG_DIM, (g + 1) * HG_DIM)
        o = o * lax.rsqrt(jnp.mean(o * o, axis=-1, keepdims=True) + RMS_EPS) * ng_ref[:, cs]
        o = o * _silu(hr_g_ref[rows, cs].astype(F32))
        y_ref[rows, cs] = o.astype(y_ref.dtype)

    def factored_chunk(c, carry):
        rows = pl.ds(pl.multiple_of(c * chunk, chunk), chunk)
        hs = [slice(g * HG_DIM, (g + 1) * HG_DIM) for g in range(heads)]
        lb = lb_ref[...]
        q = _silu(hr_q_ref[rows, :].astype(F32))
        fg = lb + (1.0 - lb) * jax.nn.sigmoid(hf_ref[rows, :].astype(F32))
        kk = 1.0 - fg
        vb = hr_i_ref[rows, :]
        l_hi, l_mid, l_lo = _split3(jnp.log(fg))
        b = _dot(tri, l_hi) + _dot(tri, l_mid) + _dot(tri, l_lo)
        b_last = b[chunk - 1:chunk, :]
        qe = (q * jnp.exp(b)).astype(BF16)
        ke = (kk * jnp.exp(-b)).astype(BF16)
        khat = (kk * jnp.exp(b_last - b)).astype(BF16)
        keep = jnp.exp(b_last)
        s_old = [sT_scr[g] for g in range(heads)]
        a = [_dot_nt(qe[:, cs], ke[:, cs]) for cs in hs]
        o_state = [_dot_nt(qe[:, cs], s_old[g].astype(BF16)) for g, cs in enumerate(hs)]
        s_add = [_dot_tn(vb[:, cs], khat[:, cs]) for cs in hs]
        a = [jnp.where(causal, x, 0.0).astype(BF16) for x in a]
        o = [o_state[g] + _dot(a[g], vb[:, cs]) for g, cs in enumerate(hs)]
        for g, cs in enumerate(hs):
            sT_scr[g] = s_old[g] * keep[:, cs] + s_add[g]
        o = [x * lax.rsqrt(jnp.mean(x * x, axis=-1, keepdims=True) + RMS_EPS) for x in o]
        o = jnp.concatenate(o, axis=1) * ng_ref[...] * _silu(hr_g_ref[rows, :].astype(F32))
        y_ref[rows, :] = o.astype(y_ref.dtype)
        return carry

    def guarded_chunk(c, carry):
        r0 = pl.multiple_of(c * chunk, chunk)
        for g in range(heads):
            q, kk, v, b = head_inputs(r0, g)
            vb = v.astype(BF16)
            _, o_state = carry_state(g, q, kk, vb, b)
            rows = []
            for i in range(n_sub):
                s0, s1 = i * sub, (i + 1) * sub
                bs, qs, ks, vs = b[s0:s1], q[s0:s1], kk[s0:s1], v[s0:s1]
                o_i = o_state[s0:s1]
                if i > 0:
                    r_i = b[s0 - 1:s0, :]
                    qt = (qs * jnp.exp(bs - r_i)).astype(BF16)
                    kt = (kk[:s0] * jnp.exp(r_i - b[:s0])).astype(BF16)
                    a_off = _dot_nt(qt, kt)
                    o_i = o_i + _dot(a_off.astype(BF16), vb[:s0])
                for s in range(sub):
                    e = jnp.exp(jnp.where(t_iota >= s, bs - bs[s:s + 1], -jnp.inf))
                    a_col = jnp.sum(qs * e * ks[s:s + 1], axis=1, keepdims=True)
                    o_i = o_i + a_col * vs[s:s + 1]
                rows.append(o_i)
            finish(r0, g, jnp.concatenate(rows, axis=0) if n_sub > 1 else rows[0])
        return carry

    factoring_safe = jnp.max(-jnp.log(lb_ref[...])) * chunk <= HG_SAFE_EXPONENT

    @pl.when(factoring_safe)
    def _():
        for c in range(n_inner):
            factored_chunk(c, 0)

    @pl.when(jnp.logical_not(factoring_safe))
    def _():
        lax.fori_loop(0, n_inner, guarded_chunk, 0)

    @pl.when(c_idx == pl.num_programs(2) - 1)
    def _():
        sT_out_ref[0] = sT_scr[...]


def _hgrn(p, cols, lb, ng, s0, *, batch, seq, chunk, sub, heads, n_inner):
    m = batch * seq
    rows = chunk * n_inner
    nc = seq // rows
    hw = heads * HG_DIM
    nhg = HG_HEADS // heads
    cf, cq, ci, cg = (cols[k] // hw for k in ("hf", "hq", "hi", "hg"))
    return pl.pallas_call(
        functools.partial(_hgrn_kernel, chunk=chunk, sub=sub, heads=heads, n_inner=n_inner),
        out_shape=(jax.ShapeDtypeStruct((m, HG_W), BF16),
                   jax.ShapeDtypeStruct((batch, HG_HEADS, HG_DIM, HG_DIM), F32)),
        grid=(batch, nhg, nc),
        in_specs=[
            pl.BlockSpec((rows, hw), lambda b, h, c: (b * nc + c, cf + h)),
            pl.BlockSpec((rows, hw), lambda b, h, c: (b * nc + c, cq + h)),
            pl.BlockSpec((rows, hw), lambda b, h, c: (b * nc + c, ci + h)),
            pl.BlockSpec((rows, hw), lambda b, h, c: (b * nc + c, cg + h)),
            pl.BlockSpec((1, hw), lambda b, h, c: (0, h)),
            pl.BlockSpec((1, hw), lambda b, h, c: (0, h)),
            pl.BlockSpec((heads, HG_DIM, HG_DIM), lambda b, h, c: (h, 0, 0)),
        ],
        out_specs=(
            pl.BlockSpec((rows, hw), lambda b, h, c: (b * nc + c, h)),
            pl.BlockSpec((1, heads, HG_DIM, HG_DIM), lambda b, h, c: (b, h, 0, 0)),
        ),
        scratch_shapes=[pltpu.VMEM((heads, HG_DIM, HG_DIM), F32)],
        compiler_params=pltpu.CompilerParams(
            dimension_semantics=("parallel", "parallel", "arbitrary"), vmem_limit_bytes=VMEM_LIMIT),
        name="hgrn2",
    )(p, p, p, p, lb, ng, s0)


def _merge_kernel(ya_ref, yh_ref, gts_a_ref, gts_h_ref, h_ref, wa_ref, wh_ref, wo_ref, g_ref, b_ref,
                  o_ref, acc_scr):
    j = pl.program_id(1)

    @pl.when(j == 0)
    def _():
        acc_scr[...] = jnp.zeros_like(acc_scr)

    ga = jax.nn.sigmoid(gts_a_ref[...].astype(F32))
    gh = jax.nn.sigmoid(gts_h_ref[...].astype(F32))
    merged = ga * _dot(ya_ref[...], wa_ref[...]) + gh * _dot(yh_ref[...], wh_ref[...])
    acc_scr[...] += _dot(merged.astype(BF16), wo_ref[...])

    @pl.when(j == pl.num_programs(1) - 1)
    def _():
        o_ref[...] = _layer_norm(ALPHA * h_ref[...] + acc_scr[...], g_ref[...], b_ref[...])


def _merge(ya, yh, gates, h1, wa, wh, wo, g, b, *, tm, tn):
    m = ya.shape[0]
    nj = D_MODEL // tn
    ca, cb = PROJ_COL["ga"] // tn, PROJ_COL["gb"] // tn
    return pl.pallas_call(
        _merge_kernel,
        out_shape=jax.ShapeDtypeStruct((m, D_MODEL), F32),
        grid=(m // tm, nj),
        in_specs=[
            pl.BlockSpec((tm, ATT_Q_W), lambda i, j: (i, 0)),
            pl.BlockSpec((tm, HG_W), lambda i, j: (i, 0)),
            pl.BlockSpec((tm, tn), lambda i, j: (i, ca + j)),
            pl.BlockSpec((tm, tn), lambda i, j: (i, cb + j)),
            pl.BlockSpec((tm, D_MODEL), lambda i, j: (i, 0)),
            pl.BlockSpec((ATT_Q_W, tn), lambda i, j: (0, j)),
            pl.BlockSpec((HG_W, tn), lambda i, j: (0, j)),
            pl.BlockSpec((tn, D_MODEL), lambda i, j: (j, 0)),
            pl.BlockSpec((1, D_MODEL), lambda i, j: (0, 0)),
            pl.BlockSpec((1, D_MODEL), lambda i, j: (0, 0)),
        ],
        out_specs=pl.BlockSpec((tm, D_MODEL), lambda i, j: (i, 0)),
        scratch_shapes=[pltpu.VMEM((tm, D_MODEL), F32)],
        compiler_params=pltpu.CompilerParams(
            dimension_semantics=("parallel", "arbitrary"), vmem_limit_bytes=VMEM_LIMIT),
        name="merge_ln",
    )(ya, yh, gates, gates, h1, wa, wh, wo, g, b)


def kernel(x, meta, ffn1_w_gate, ffn1_w_up, ffn1_w_down, ln1_g, ln1_b, w_in, idx_k_norm_g, idx_k_norm_b,
           hg_lb_logits, hg_norm_g, w_branch_att, w_branch_hg, w_out, ln2_g, ln2_b,
           ffn2_w_gate, ffn2_w_up, ffn2_w_down, ln3_g, ln3_b):
    batch, seq, _ = x.shape
    m = batch * seq
    xr = x.reshape(m, D_MODEL)
    bf = lambda w: w.astype(BF16)
    row = lambda v: v.reshape(1, -1)

    w_in_t = w_in[0].T
    w_idx = bf(jnp.pad(w_in_t[IN_OFFS[4]:IN_OFFS[6]].T, ((0, 0), (0, LANES - IDX_HEAD_DIM - N_IDX_HEADS))))
    idx_g = jnp.pad(idx_k_norm_g[0], (0, LANES - IDX_HEAD_DIM)).reshape(1, LANES)
    idx_b = jnp.pad(idx_k_norm_b[0], (0, LANES - IDX_HEAD_DIM)).reshape(1, LANES)
    lb = jnp.cumsum(jax.nn.softmax(hg_lb_logits.astype(F32), axis=0), axis=0)[0].reshape(1, HG_W)
    ng = hg_norm_g[0].reshape(1, HG_W)

    f1 = (ffn1_w_gate[0], ffn1_w_up[0], ffn1_w_down[0], row(ln1_g[0]), row(ln1_b[0]))
    f2 = (ffn2_w_gate[0], ffn2_w_up[0], ffn2_w_down[0], row(ln3_g[0]), row(ln3_b[0]))

    hm = _ffn_ln(meta.astype(F32), *f1, tm=N_META, tf=FFN_TF)
    pm = _proj_all(hm, w_in_t, META_TILE_SRC, tm=N_META, scaled_tiles=0)
    kv0 = META_COL["kv"]
    km = jnp.pad(pm[:, kv0:kv0 + ATT_KV_W], ((0, LANES - N_META), (0, 0)))
    vm = jnp.pad(pm[:, kv0 + ATT_KV_W:kv0 + 2 * ATT_KV_W], ((0, LANES - N_META), (0, 0)))
    s_zero = jnp.zeros((HG_HEADS, HG_DIM, HG_DIM), F32)
    meta_cols = {"hf": META_COL["hf"], "hi": META_COL["hi"], "hq": META_COL["hf"], "hg": META_COL["hi"]}
    _, s_meta = _hgrn(pm, meta_cols, lb, ng, s_zero, batch=1, seq=N_META, chunk=N_META, sub=N_META,
                      heads=HG_HEADS, n_inner=1)

    h1 = _ffn_ln(xr, *f1, tm=FFN_TM, tf=FFN_TF)
    p = _proj_all(h1, w_in_t, PROJ_TILE_SRC, tm=1024, scaled_tiles=ATT_Q_W // PROJ_TN)
    p_idx = _proj_idx(h1, w_idx, idx_g, idx_b, tm=1024)

    ikn = p_idx[:, :IDX_HEAD_DIM].astype(BF16)
    ik_lo = jnp.pad(ikn, ((0, 0), (0, LANES - IDX_HEAD_DIM)))
    ik_hi = jnp.pad(ikn, ((0, 0), (LANES - IDX_HEAD_DIM, 0)))
    iw_t = p_idx[:, IDX_HEAD_DIM:IDX_HEAD_DIM + N_IDX_HEADS].T
    y_att = _attention(p, iw_t, ik_lo, ik_hi, km, vm, batch=batch, seq=seq, tq=256, kc=512)
    y_hg, _ = _hgrn(p, PROJ_COL, lb, ng, s_meta[0], batch=batch, seq=seq, chunk=64, sub=16,
                    heads=HG_HEADS, n_inner=8)

    h2 = _merge(y_att, y_hg, p, h1, bf(w_branch_att[0]), bf(w_branch_hg[0]), bf(w_out[0]),
                row(ln2_g[0]), row(ln2_b[0]), tm=512, tn=512)
    out = _ffn_ln(h2, *f2, tm=FFN_TM, tf=FFN_TF)
    return out.reshape(batch, seq, D_MODEL)
```

```python
import functools
import math

import jax
import jax.numpy as jnp
import numpy as np
from jax import lax
from jax.experimental import pallas as pl
from jax.experimental.pallas import tpu as pltpu

D_MODEL = 2048
N_META = 16
N_ATT_HEADS = 8
N_KV_HEADS = 2
ATT_GROUP = N_ATT_HEADS // N_KV_HEADS
ATT_HEAD_DIM = 128
N_IDX_HEADS = 16
IDX_HEAD_DIM = 64
TOPK = 256
HG_HEADS = 8
HG_DIM = 128
D_FF = 5632
LN_EPS = 1e-5
RMS_EPS = 1e-6
ALPHA = 2.0 ** 0.25

ATT_Q_W = N_ATT_HEADS * ATT_HEAD_DIM
ATT_KV_W = N_KV_HEADS * ATT_HEAD_DIM
IDX_Q_W = N_IDX_HEADS * IDX_HEAD_DIM
HG_W = HG_HEADS * HG_DIM

LANES = 128
SUBLANES = 8
VMEM_LIMIT = 56 * 1024 * 1024

F32 = jnp.float32
BF16 = jnp.bfloat16
INT_MIN = -2 ** 31
F32_LOWEST = float(np.finfo(np.float32).min)
LOG2E = math.log2(math.e)
ATT_Q_SCALE = ATT_HEAD_DIM ** -0.5 * LOG2E
POS_SHIFT = 6
POS_SPLIT = 1 << POS_SHIFT


def _dot(a, b):
    return jnp.dot(a, b, preferred_element_type=F32)


def _dot_nt(a, b):
    return lax.dot_general(a, b, (((1,), (1,)), ((), ())), preferred_element_type=F32)


def _dot_tn(a, b):
    return lax.dot_general(a, b, (((0,), (0,)), ((), ())), preferred_element_type=F32)


def _layer_norm(y, g, b):
    mu = jnp.mean(y, axis=-1, keepdims=True)
    d = y - mu
    var = jnp.mean(d * d, axis=-1, keepdims=True)
    return d * lax.rsqrt(var + LN_EPS) * g + b


def _silu(x):
    return x * jax.nn.sigmoid(x)


def _bf16_parts(c, n=3):
    parts = []
    for _ in range(n):
        p = float(np.asarray(c, np.float32).astype(jnp.bfloat16).astype(np.float32))
        parts.append(p)
        c = c - p
    return parts


FFN_TM = 1024
FFN_TF = 256
LN_ROWS = 128


def _ffn_ln_kernel(x_ref, wg_ref, wu_ref, wd_ref, g_ref, b_ref, o_ref, xb_scr):
    j = pl.program_id(1)

    @pl.when(j == 0)
    def _():
        xb_scr[...] = x_ref[...].astype(BF16)
        o_ref[...] = jnp.zeros_like(o_ref)

    xb = xb_scr[...]
    gate = _dot(xb, wg_ref[...].astype(BF16))
    up = _dot(xb, wu_ref[...].astype(BF16))
    a = (_silu(gate) * up).astype(BF16)
    o_ref[...] += _dot(a, wd_ref[...].astype(BF16))

    @pl.when(j == pl.num_programs(1) - 1)
    def _():
        n_rows = min(LN_ROWS, o_ref.shape[0])

        def ln_rows(r, carry):
            rows = pl.ds(pl.multiple_of(r * n_rows, n_rows), n_rows)
            y = ALPHA * x_ref[rows, :] + 0.5 * o_ref[rows, :]
            o_ref[rows, :] = _layer_norm(y, g_ref[...], b_ref[...])
            return carry

        lax.fori_loop(0, o_ref.shape[0] // n_rows, ln_rows, 0)


def _ffn_ln(x, wg, wu, wd, g, b, *, tm, tf):
    m = x.shape[0]
    grid = (m // tm, D_FF // tf)
    return pl.pallas_call(
        _ffn_ln_kernel,
        out_shape=jax.ShapeDtypeStruct((m, D_MODEL), F32),
        grid=grid,
        in_specs=[
            pl.BlockSpec((tm, D_MODEL), lambda i, j: (i, 0)),
            pl.BlockSpec((D_MODEL, tf), lambda i, j: (0, j)),
            pl.BlockSpec((D_MODEL, tf), lambda i, j: (0, j)),
            pl.BlockSpec((tf, D_MODEL), lambda i, j: (j, 0)),
            pl.BlockSpec((1, D_MODEL), lambda i, j: (0, 0)),
            pl.BlockSpec((1, D_MODEL), lambda i, j: (0, 0)),
        ],
        out_specs=pl.BlockSpec((tm, D_MODEL), lambda i, j: (i, 0)),
        scratch_shapes=[pltpu.VMEM((tm, D_MODEL), BF16)],
        compiler_params=pltpu.CompilerParams(
            dimension_semantics=("parallel", "arbitrary"), vmem_limit_bytes=VMEM_LIMIT),
        name="ffn_ln",
    )(x, wg, wu, wd, g, b)


IN_SPLITS = (ATT_Q_W, ATT_KV_W, ATT_KV_W, IDX_Q_W, IDX_HEAD_DIM, N_IDX_HEADS, HG_W, HG_W, HG_W, HG_W,
             D_MODEL, D_MODEL)
IN_OFFS = tuple(int(v) for v in np.cumsum((0,) + IN_SPLITS))
PROJ_TN = 1024
PROJ_GROUPS = {"aq": (IN_OFFS[0], ATT_Q_W), "iq": (IN_OFFS[3], IDX_Q_W), "hq": (IN_OFFS[6], HG_W),
               "hf": (IN_OFFS[7], HG_W), "hi": (IN_OFFS[8], HG_W), "hg": (IN_OFFS[9], HG_W),
               "ga": (IN_OFFS[10], D_MODEL), "gb": (IN_OFFS[11], D_MODEL), "kv": (IN_OFFS[1], 2 * ATT_KV_W)}
PROJ_SRC_ALIGN = 16


def _proj_layout(names):
    col, src = {}, []
    for name in names:
        start, width = PROJ_GROUPS[name]
        col[name] = len(src) * PROJ_TN
        src += [start + t * PROJ_TN for t in range(-(-width // PROJ_TN))]
    assert all(s % PROJ_SRC_ALIGN == 0 and s + PROJ_TN <= IN_OFFS[-1] for s in src)
    return col, tuple(src)


PROJ_COL, PROJ_TILE_SRC = _proj_layout(("aq", "iq", "hq", "hf", "hi", "hg", "ga", "gb", "kv"))
META_COL, META_TILE_SRC = _proj_layout(("hf", "hi", "kv"))


def _proj_all_kernel(src_ref, x_ref, wt_ref, o_ref, xb_scr, *, scaled_tiles, scale):
    @pl.when(pl.program_id(1) == 0)
    def _():
        xb_scr[...] = x_ref[...].astype(BF16)

    acc = _dot_nt(xb_scr[...], wt_ref[...].astype(BF16))
    acc = acc * jnp.where(pl.program_id(1) < scaled_tiles, scale, 1.0)
    o_ref[...] = acc.astype(o_ref.dtype)


def _idx_kernel(x_ref, w_ref, g_ref, b_ref, o_ref):
    p = _dot(x_ref[...].astype(BF16), w_ref[...])
    lane = lax.broadcasted_iota(jnp.int32, p.shape, 1)
    is_k = lane < IDX_HEAD_DIM
    mu = jnp.sum(jnp.where(is_k, p, 0.0), axis=-1, keepdims=True) * (1.0 / IDX_HEAD_DIM)
    d = jnp.where(is_k, p - mu, 0.0)
    var = jnp.sum(d * d, axis=-1, keepdims=True) * (1.0 / IDX_HEAD_DIM)
    kn = d * lax.rsqrt(var + LN_EPS) * g_ref[...] + b_ref[...]
    w_scale = (N_IDX_HEADS ** -0.5) * (IDX_HEAD_DIM ** -0.5)
    o_ref[...] = jnp.where(is_k, kn, p * w_scale)


def _proj_all(xb, w_in_t, tile_src, *, tm, scaled_tiles):
    m = xb.shape[0]
    n_tiles = len(tile_src)
    return pl.pallas_call(
        functools.partial(_proj_all_kernel, scaled_tiles=scaled_tiles, scale=ATT_Q_SCALE),
        out_shape=jax.ShapeDtypeStruct((m, n_tiles * PROJ_TN), BF16),
        grid_spec=pltpu.PrefetchScalarGridSpec(
            num_scalar_prefetch=1,
            grid=(m // tm, n_tiles),
            in_specs=[pl.BlockSpec((tm, D_MODEL), lambda i, j, src: (i, 0)),
                      pl.BlockSpec((pl.Element(PROJ_TN), pl.Element(D_MODEL)),
                                   lambda i, j, src: (src[j] * PROJ_SRC_ALIGN, 0))],
            out_specs=pl.BlockSpec((tm, PROJ_TN), lambda i, j, src: (i, j)),
            scratch_shapes=[pltpu.VMEM((tm, D_MODEL), BF16)],
        ),
        compiler_params=pltpu.CompilerParams(
            dimension_semantics=("parallel", "arbitrary"), vmem_limit_bytes=VMEM_LIMIT),
        name="proj_all",
    )(jnp.asarray([s // PROJ_SRC_ALIGN for s in tile_src], jnp.int32), xb, w_in_t)


def _proj_idx(xb, w, g, b, *, tm):
    m = xb.shape[0]
    return pl.pallas_call(
        _idx_kernel,
        out_shape=jax.ShapeDtypeStruct((m, LANES), F32),
        grid=(m // tm,),
        in_specs=[pl.BlockSpec((tm, D_MODEL), lambda i: (i, 0)),
                  pl.BlockSpec((D_MODEL, LANES), lambda i: (0, 0)),
                  pl.BlockSpec((1, LANES), lambda i: (0, 0)),
                  pl.BlockSpec((1, LANES), lambda i: (0, 0))],
        out_specs=pl.BlockSpec((tm, LANES), lambda i: (i, 0)),
        compiler_params=pltpu.CompilerParams(
            dimension_semantics=("parallel",), vmem_limit_bytes=VMEM_LIMIT),
        name="proj_idx",
    )(xb, w, g, b)


ATT_ROW_TILE = 32
ATT_BOUND_SLACK = 45.0
UNCHECKED_BITS = 16


def _key_to_f32(key):
    return pltpu.bitcast(key ^ ((key >> 31) & 0x7FFFFFFF), F32)


def _attn_kernel(aq_ref, iq_ref, iwt_ref, iklo_ref, ikhi_ref, k_ref, v_ref, km_ref, vm_ref,
                 o_ref,
                 score_scr, mb_scr, kaug_scr, vaug_scr, qaug_scr, s0_scr, s1_scr, p_scr,
                 macc_scr, m_scr, acc_scr, kn_scr, bound_scr, eye_scr, *, tq, kc, seq):
    qi = pl.program_id(1)
    n_kc = ((qi + 1) * tq + kc - 1) // kc
    kf = float(TOPK)

    @pl.when(qi == 0)
    def _():
        r = lax.broadcasted_iota(jnp.int32, (LANES + seq, LANES), 0)
        lane = lax.broadcasted_iota(jnp.int32, (LANES + seq, LANES), 1)
        pos = jnp.where(r < LANES, r, r - LANES + N_META)
        feat = jnp.where(lane < 3, pos >> POS_SHIFT, jnp.where(lane < 6, pos & (POS_SPLIT - 1), 0))
        feat = feat.astype(F32).astype(BF16)
        ones = jnp.ones((LANES + seq, LANES), BF16)
        eye_scr[...] = (lax.broadcasted_iota(jnp.int32, (tq, tq), 0)
                        == lax.broadcasted_iota(jnp.int32, (tq, tq), 1)).astype(BF16)
        for kvh in range(N_KV_HEADS):
            cs = slice(kvh * LANES, (kvh + 1) * LANES)
            kaug_scr[kvh, 0:LANES, 0:LANES] = km_ref[:, cs]
            kaug_scr[kvh, LANES:, 0:LANES] = k_ref[:, cs]
            kaug_scr[kvh, :, LANES:] = feat
            vaug_scr[kvh, 0:LANES, 0:LANES] = vm_ref[:, cs]
            vaug_scr[kvh, LANES:, 0:LANES] = v_ref[:, cs]
            vaug_scr[kvh, :, LANES:] = ones
            kf32 = kaug_scr[kvh, :, 0:LANES].astype(F32)
            k_norm2 = jnp.max(jnp.sum(kf32 * kf32, axis=1, keepdims=True))
            kn_scr[kvh] = jnp.broadcast_to(k_norm2, kn_scr.shape[1:])

    qcol = qi * tq + lax.broadcasted_iota(jnp.int32, (1, tq), 1)
    iwt = iwt_ref[...]

    def score_body(j, carry):
        off = pl.multiple_of(j * kc, kc)
        klo = iklo_ref[pl.ds(off, kc), :]
        khi = ikhi_ref[pl.ds(off, kc), :]
        acc = jnp.zeros((kc, tq), F32)
        for p in range(N_IDX_HEADS // 2):
            q2 = iq_ref[:, p * LANES:(p + 1) * LANES]
            acc = acc + jnp.maximum(_dot_nt(klo, q2), 0.0) * iwt[2 * p:2 * p + 1, :]
            acc = acc + jnp.maximum(_dot_nt(khi, q2), 0.0) * iwt[2 * p + 1:2 * p + 2, :]
        krow = off + lax.broadcasted_iota(jnp.int32, (kc, tq), 0)
        score_scr[pl.ds(off, kc), :] = jnp.where(krow <= qcol, acc, -jnp.inf)
        return carry

    lax.fori_loop(0, n_kc, score_body, 0)

    n_acc = 8
    acc_rows = n_acc * SUBLANES

    def count_ge(cand):
        def body(j, acc):
            off = pl.multiple_of(j * kc, kc)
            w = jnp.where(score_scr[pl.ds(off, kc), :] >= cand, 1.0, 0.0)
            return acc + jnp.sum(w.reshape(kc // acc_rows, acc_rows, tq), axis=0)

        acc = lax.fori_loop(0, n_kc, body, jnp.zeros((acc_rows, tq), F32))
        return jnp.sum(acc, axis=0, keepdims=True)

    c0 = count_ge(jnp.zeros((1, tq), F32))
    ok0 = c0 >= kf
    thr0 = jnp.where(ok0, 0, INT_MIN).astype(jnp.int32)
    cnt0 = jnp.where(ok0, c0, 0.0)

    short_row = qcol + 1 < TOPK

    def unsettled(cnt):
        return jnp.sum(jnp.where((cnt == kf) | short_row, 0.0, 1.0))

    n_bits = 31
    group = 4

    def bit_cond(carry):
        i, _, _, pending = carry
        return jnp.logical_and(i < n_bits, pending > 0.0)

    def resolve_bit(i, thr, cnt):
        shift = jnp.maximum(n_bits - 1 - i, 0)
        cand = thr | jnp.where(i < n_bits, jnp.left_shift(jnp.int32(1), shift), 0)
        c = count_ge(_key_to_f32(cand))
        ok = c >= kf
        return jnp.where(ok, cand, thr), jnp.where(ok, c, cnt)

    def bit_body(carry):
        i, thr, cnt, _ = carry
        for b in range(group):
            thr, cnt = resolve_bit(i + b, thr, cnt)
        return i + group, thr, cnt, unsettled(cnt)

    thr1, cnt1 = lax.fori_loop(0, UNCHECKED_BITS, lambda i, c: resolve_bit(i, *c), (thr0, cnt0))
    _, thr, cnt, _ = lax.while_loop(
        bit_cond, bit_body, (jnp.int32(UNCHECKED_BITS), thr1, cnt1, unsettled(cnt1)))
    t_f = jnp.where(thr == INT_MIN, F32_LOWEST, _key_to_f32(thr))

    @pl.when(jnp.max(cnt) > kf)
    def _():
        def gt_body(j, acc):
            off = pl.multiple_of(j * kc, kc)
            w = jnp.where(score_scr[pl.ds(off, kc), :] > t_f, 1.0, 0.0)
            return acc + jnp.sum(w, axis=0, keepdims=True)

        need = kf - lax.fori_loop(0, n_kc, gt_body, jnp.zeros((1, tq), F32))
        lower = (lax.broadcasted_iota(jnp.int32, (kc, kc), 0)
                 >= lax.broadcasted_iota(jnp.int32, (kc, kc), 1)).astype(BF16)

        def tie_body(j, before):
            off = pl.multiple_of(j * kc, kc)
            sc = score_scr[pl.ds(off, kc), :]
            eq = sc == t_f
            eqf = jnp.where(eq, 1.0, 0.0)
            rank = before + _dot(lower, eqf.astype(BF16))
            score_scr[pl.ds(off, kc), :] = jnp.where(eq & (rank > need), -jnp.inf, sc)
            return before + jnp.sum(eqf, axis=0, keepdims=True)

        lax.fori_loop(0, n_kc, tie_body, jnp.zeros((1, tq), F32))

    eye = eye_scr[...]

    def mask_body(j, last_sel):
        off = pl.multiple_of(j * kc, kc)
        picked = score_scr[pl.ds(off, kc), :] >= t_f
        sel = _dot_nt(eye, jnp.where(picked, 1.0, 0.0).astype(BF16))
        mb_scr[:, pl.ds(off, kc)] = jnp.where(sel > 0.5, 0.0, -jnp.inf)
        krow = (off + lax.broadcasted_iota(jnp.int32, (kc, tq), 0)).astype(F32)
        hit = jnp.where(picked, krow, -1.0)
        return jnp.maximum(last_sel, jnp.max(hit.reshape(kc // SUBLANES, SUBLANES, tq), axis=0))

    last_sel = lax.fori_loop(0, n_kc, mask_body, jnp.full((SUBLANES, tq), -1.0, F32))
    last_pos = jnp.maximum(jnp.max(last_sel, axis=0, keepdims=True) + N_META, N_META - 1.0)

    rt = ATT_ROW_TILE
    lane_q = lax.broadcasted_iota(jnp.int32, (1, LANES), 1)
    meta_mask = jnp.where(lax.broadcasted_iota(jnp.int32, (rt, LANES), 1) < N_META, 0.0, -jnp.inf)

    s_slots = (s0_scr, s1_scr)

    def logits(kvh, krow0, width, slot):
        s_slots[slot][:, 0:width] = _dot_nt(qaug_scr[kvh], kaug_scr[kvh, pl.ds(krow0, width), :])

    def max_tiles(kvh, slot, width, mask_fn):
        s_ref = s_slots[slot]
        for r0 in range(0, tq, rt):
            mb = mask_fn(r0)
            for g in range(ATT_GROUP):
                rows = slice(g * tq + r0, g * tq + r0 + rt)
                mx = s_ref[rows, 0:LANES] + mb[:, 0:LANES]
                for c in range(1, width // LANES):
                    cols = slice(c * LANES, (c + 1) * LANES)
                    mx = jnp.maximum(mx, s_ref[rows, cols] + mb[:, cols])
                macc_scr[rows, :] = jnp.maximum(macc_scr[rows, :], mx)

    def exp_tiles(kvh, slot, width, mask_fn):
        s_ref = s_slots[slot]
        for r0 in range(0, tq, rt):
            mb = mask_fn(r0)
            for g in range(ATT_GROUP):
                rows = slice(g * tq + r0, g * tq + r0 + rt)
                m_row = m_scr[rows, :]
                for c in range(width // LANES):
                    cols = slice(c * LANES, (c + 1) * LANES)
                    p_scr[rows, cols] = jnp.exp2(s_ref[rows, cols] + mb[:, cols] - m_row).astype(BF16)

    def add_pv(kvh, krow0, width):
        vblk = vaug_scr[kvh, pl.ds(krow0, width), :]
        half = ATT_GROUP * tq // 2
        for r0 in (0, half):
            acc_scr[kvh, r0:r0 + half, :] += _dot(p_scr[r0:r0 + half, 0:width], vblk)

    def run_pass(kvh, tiles_fn, with_pv):
        def krow(j):
            return pl.multiple_of(LANES + j * kc, LANES)

        def step(j, slot, lookahead):
            off = pl.multiple_of(j * kc, kc)
            if lookahead:
                logits(kvh, krow(j + 1), kc, 1 - slot)
            tiles_fn(kvh, slot, kc, lambda r0: mb_scr[r0:r0 + rt, pl.ds(off, kc)])
            if with_pv:
                add_pv(kvh, krow(j), kc)

        logits(kvh, 0, LANES, 0)
        tiles_fn(kvh, 0, LANES, lambda r0: meta_mask)
        if with_pv:
            add_pv(kvh, 0, LANES)

            def serial_body(j, carry):
                off = pl.multiple_of(j * kc, kc)
                logits(kvh, krow(j), kc, 0)
                tiles_fn(kvh, 0, kc, lambda r0: mb_scr[r0:r0 + rt, pl.ds(off, kc)])
                add_pv(kvh, krow(j), kc)
                return carry

            lax.fori_loop(0, n_kc, serial_body, 0)
            return
        logits(kvh, krow(0), kc, 0)
        n_pairs = (n_kc - 1) // 2

        def pair_body(i, carry):
            step(2 * i, 0, True)
            step(2 * i + 1, 1, True)
            return carry

        lax.fori_loop(0, n_pairs, pair_body, 0)
        j0 = 2 * n_pairs
        two_left = n_kc - j0 == 2

        @pl.when(two_left)
        def _():
            step(j0, 0, True)
            step(j0 + 1, 1, False)

        @pl.when(jnp.logical_not(two_left))
        def _():
            step(j0, 0, False)

    for h in range(N_ATT_HEADS):
        kvh, g = divmod(h, ATT_GROUP)
        parts = _bf16_parts(2.0 ** (-8.0 * (h + 1) / N_ATT_HEADS) * LOG2E)
        vals = [POS_SPLIT * p for p in parts] + parts
        qfeat = jnp.zeros((1, LANES), F32)
        for i, val in enumerate(vals):
            qfeat = jnp.where(lane_q == i, val, qfeat)
        qaug_scr[kvh, g * tq:(g + 1) * tq, 0:LANES] = aq_ref[:, h * LANES:(h + 1) * LANES]
        qaug_scr[kvh, g * tq:(g + 1) * tq, LANES:] = jnp.broadcast_to(qfeat, (tq, LANES)).astype(BF16)

    a_pos = jnp.floor(last_pos * (1.0 / POS_SPLIT))
    b_pos = last_pos - a_pos * POS_SPLIT
    to_rows = lambda v: _dot_nt(eye, jnp.broadcast_to(v, (LANES, tq)).astype(BF16))
    last_pos_rows = to_rows(a_pos) * POS_SPLIT + to_rows(b_pos)
    slack = jnp.float32(0.0)
    for kvh in range(N_KV_HEADS):
        qf = qaug_scr[kvh, :, 0:LANES].astype(F32)
        qk_max = jnp.sqrt(jnp.sum(qf * qf, axis=1, keepdims=True) * kn_scr[kvh, 0:1, 0:1])
        slack = jnp.maximum(slack, jnp.max(qk_max))
        for g in range(ATT_GROUP):
            rows = slice(g * tq, (g + 1) * tq)
            c = 2.0 ** (-8.0 * (kvh * ATT_GROUP + g + 1) / N_ATT_HEADS) * LOG2E
            bound_scr[kvh, rows, :] = c * last_pos_rows + qk_max[rows]
    bound_ok = slack <= ATT_BOUND_SLACK

    for kvh in range(N_KV_HEADS):
        @pl.when(bound_ok)
        def _():
            m_scr[...] = bound_scr[kvh]

        @pl.when(jnp.logical_not(bound_ok))
        def _():
            macc_scr[...] = jnp.full_like(macc_scr, -jnp.inf)
            run_pass(kvh, max_tiles, with_pv=False)
            m_scr[...] = jnp.broadcast_to(jnp.max(macc_scr[...], axis=1, keepdims=True), m_scr.shape)

        acc_scr[kvh] = jnp.zeros(acc_scr.shape[1:], F32)
        run_pass(kvh, exp_tiles, with_pv=True)

    for h in range(N_ATT_HEADS):
        kvh, g = divmod(h, ATT_GROUP)
        acc = acc_scr[kvh, g * tq:(g + 1) * tq, :]
        o_ref[:, h * LANES:(h + 1) * LANES] = (acc[:, 0:LANES] / acc[:, LANES:LANES + 1]).astype(o_ref.dtype)


def _attention(p_att, iw_t, ik_lo, ik_hi, km, vm, *, batch, seq, tq, kc):
    m = batch * seq
    nq = seq // tq
    qcol, icol, kcol = PROJ_COL["aq"] // ATT_Q_W, PROJ_COL["iq"] // IDX_Q_W, PROJ_COL["kv"] // ATT_KV_W
    rows4 = ATT_GROUP * tq
    return pl.pallas_call(
        functools.partial(_attn_kernel, tq=tq, kc=kc, seq=seq),
        out_shape=jax.ShapeDtypeStruct((m, ATT_Q_W), BF16),
        grid=(batch, nq),
        in_specs=[
            pl.BlockSpec((tq, ATT_Q_W), lambda b, q: (b * nq + q, qcol)),
            pl.BlockSpec((tq, IDX_Q_W), lambda b, q: (b * nq + q, icol)),
            pl.BlockSpec((N_IDX_HEADS, tq), lambda b, q: (0, b * nq + q)),
            pl.BlockSpec((seq, LANES), lambda b, q: (b, 0)),
            pl.BlockSpec((seq, LANES), lambda b, q: (b, 0)),
            pl.BlockSpec((seq, ATT_KV_W), lambda b, q: (b, kcol)),
            pl.BlockSpec((seq, ATT_KV_W), lambda b, q: (b, kcol + 1)),
            pl.BlockSpec((LANES, ATT_KV_W), lambda b, q: (0, 0)),
            pl.BlockSpec((LANES, ATT_KV_W), lambda b, q: (0, 0)),
        ],
        out_specs=pl.BlockSpec((tq, ATT_Q_W), lambda b, q: (b * nq + q, 0)),
        scratch_shapes=[
            pltpu.VMEM((seq, tq), F32),
            pltpu.VMEM((tq, seq), F32),
            pltpu.VMEM((N_KV_HEADS, LANES + seq, 2 * LANES), BF16),
            pltpu.VMEM((N_KV_HEADS, LANES + seq, 2 * LANES), BF16),
            pltpu.VMEM((N_KV_HEADS, rows4, 2 * LANES), BF16),
            pltpu.VMEM((rows4, kc), F32),
            pltpu.VMEM((rows4, kc), F32),
            pltpu.VMEM((rows4, kc), BF16),
            pltpu.VMEM((rows4, LANES), F32),
            pltpu.VMEM((rows4, LANES), F32),
            pltpu.VMEM((N_KV_HEADS, rows4, 2 * LANES), F32),
            pltpu.VMEM((N_KV_HEADS, SUBLANES, LANES), F32),
            pltpu.VMEM((N_KV_HEADS, rows4, LANES), F32),
            pltpu.VMEM((tq, tq), BF16),
        ],
        compiler_params=pltpu.CompilerParams(
            dimension_semantics=("parallel", "arbitrary"), vmem_limit_bytes=VMEM_LIMIT),
        name="dsa_attention",
    )(p_att, p_att, iw_t, ik_lo, ik_hi, p_att, p_att, km, vm)


def _split3(x):
    hi = x.astype(BF16)
    r = x - hi.astype(F32)
    mid = r.astype(BF16)
    lo = (r - mid.astype(F32)).astype(BF16)
    return hi, mid, lo


HG_SAFE_EXPONENT = 80.0


def _hgrn_kernel(hf_ref, hr_q_ref, hr_i_ref, hr_g_ref, lb_ref, ng_ref, s0_ref,
                 y_ref, sT_out_ref, sT_scr, *, chunk, sub, heads, n_inner):
    c_idx = pl.program_id(2)

    @pl.when(c_idx == 0)
    def _():
        sT_scr[...] = s0_ref[...]

    n_sub = chunk // sub
    causal = (lax.broadcasted_iota(jnp.int32, (chunk, chunk), 0)
              >= lax.broadcasted_iota(jnp.int32, (chunk, chunk), 1))
    tri = causal.astype(BF16)
    t_iota = lax.broadcasted_iota(jnp.int32, (sub, LANES), 0)

    def head_inputs(r0, g):
        rows, cs = pl.ds(r0, chunk), slice(g * HG_DIM, (g + 1) * HG_DIM)
        lb = lb_ref[:, cs]
        q = _silu(hr_q_ref[rows, cs].astype(F32))
        fg = lb + (1.0 - lb) * jax.nn.sigmoid(hf_ref[rows, cs].astype(F32))
        v = hr_i_ref[rows, cs].astype(F32)
        l_hi, l_mid, l_lo = _split3(jnp.log(fg))
        b = _dot(tri, l_hi) + _dot(tri, l_mid) + _dot(tri, l_lo)
        return q, 1.0 - fg, v, b

    def carry_state(g, q, kk, vb, b):
        sT = sT_scr[g]
        b_last = b[chunk - 1:chunk, :]
        qe = (q * jnp.exp(b)).astype(BF16)
        o_state = _dot_nt(qe, sT.astype(BF16))
        khat = (kk * jnp.exp(b_last - b)).astype(BF16)
        sT_scr[g] = sT * jnp.exp(b_last) + _dot_tn(vb, khat)
        return qe, o_state

    def finish(r0, g, o):
        rows, cs = pl.ds(r0, chunk), slice(g * HG_DIM, (g + 1) * HG_DIM)
        o = o * lax.rsqrt(jnp.mean(o * o, axis=-1, keepdims=True) + RMS_EPS) * ng_ref[:, cs]
        o = o * _silu(hr_g_ref[rows, cs].astype(F32))
        y_ref[rows, cs] = o.astype(y_ref.dtype)

    def factored_chunk(c, carry):
        rows = pl.ds(pl.multiple_of(c * chunk, chunk), chunk)
        hs = [slice(g * HG_DIM, (g + 1) * HG_DIM) for g in range(heads)]
        lb = lb_ref[...]
        q = _silu(hr_q_ref[rows, :].astype(F32))
        fg = lb + (1.0 - lb) * jax.nn.sigmoid(hf_ref[rows, :].astype(F32))
        kk = 1.0 - fg
        vb = hr_i_ref[rows, :]
        l_hi, l_mid, l_lo = _split3(jnp.log(fg))
        b = _dot(tri, l_hi) + _dot(tri, l_mid) + _dot(tri, l_lo)
        b_last = b[chunk - 1:chunk, :]
        qe = (q * jnp.exp(b)).astype(BF16)
        ke = (kk * jnp.exp(-b)).astype(BF16)
        khat = (kk * jnp.exp(b_last - b)).astype(BF16)
        keep = jnp.exp(b_last)
        s_old = [sT_scr[g] for g in range(heads)]
        a = [_dot_nt(qe[:, cs], ke[:, cs]) for cs in hs]
        o_state = [_dot_nt(qe[:, cs], s_old[g].astype(BF16)) for g, cs in enumerate(hs)]
        s_add = [_dot_tn(vb[:, cs], khat[:, cs]) for cs in hs]
        a = [jnp.where(causal, x, 0.0).astype(BF16) for x in a]
        o = [o_state[g] + _dot(a[g], vb[:, cs]) for g, cs in enumerate(hs)]
        for g, cs in enumerate(hs):
            sT_scr[g] = s_old[g] * keep[:, cs] + s_add[g]
        o = [x * lax.rsqrt(jnp.mean(x * x, axis=-1, keepdims=True) + RMS_EPS) for x in o]
        o = jnp.concatenate(o, axis=1) * ng_ref[...] * _silu(hr_g_ref[rows, :].astype(F32))
        y_ref[rows, :] = o.astype(y_ref.dtype)
        return carry

    def guarded_chunk(c, carry):
        r0 = pl.multiple_of(c * chunk, chunk)
        for g in range(heads):
            q, kk, v, b = head_inputs(r0, g)
            vb = v.astype(BF16)
            _, o_state = carry_state(g, q, kk, vb, b)
            rows = []
            for i in range(n_sub):
                s0, s1 = i * sub, (i + 1) * sub
                bs, qs, ks, vs = b[s0:s1], q[s0:s1], kk[s0:s1], v[s0:s1]
                o_i = o_state[s0:s1]
                if i > 0:
                    r_i = b[s0 - 1:s0, :]
                    qt = (qs * jnp.exp(bs - r_i)).astype(BF16)
                    kt = (kk[:s0] * jnp.exp(r_i - b[:s0])).astype(BF16)
                    a_off = _dot_nt(qt, kt)
                    o_i = o_i + _dot(a_off.astype(BF16), vb[:s0])
                for s in range(sub):
                    e = jnp.exp(jnp.where(t_iota >= s, bs - bs[s:s + 1], -jnp.inf))
                    a_col = jnp.sum(qs * e * ks[s:s + 1], axis=1, keepdims=True)
                    o_i = o_i + a_col * vs[s:s + 1]
                rows.append(o_i)
            finish(r0, g, jnp.concatenate(rows, axis=0) if n_sub > 1 else rows[0])
        return carry

    factoring_safe = jnp.max(-jnp.log(lb_ref[...])) * chunk <= HG_SAFE_EXPONENT

    @pl.when(factoring_safe)
    def _():
        for c in range(n_inner):
            factored_chunk(c, 0)

    @pl.when(jnp.logical_not(factoring_safe))
    def _():
        lax.fori_loop(0, n_inner, guarded_chunk, 0)

    @pl.when(c_idx == pl.num_programs(2) - 1)
    def _():
        sT_out_ref[0] = sT_scr[...]


def _hgrn(p, cols, lb, ng, s0, *, batch, seq, chunk, sub, heads, n_inner):
    m = batch * seq
    rows = chunk * n_inner
    nc = seq // rows
    hw = heads * HG_DIM
    nhg = HG_HEADS // heads
    cf, cq, ci, cg = (cols[k] // hw for k in ("hf", "hq", "hi", "hg"))
    return pl.pallas_call(
        functools.partial(_hgrn_kernel, chunk=chunk, sub=sub, heads=heads, n_inner=n_inner),
        out_shape=(jax.ShapeDtypeStruct((m, HG_W), BF16),
                   jax.ShapeDtypeStruct((batch, HG_HEADS, HG_DIM, HG_DIM), F32)),
        grid=(batch, nhg, nc),
        in_specs=[
            pl.BlockSpec((rows, hw), lambda b, h, c: (b * nc + c, cf + h)),
            pl.BlockSpec((rows, hw), lambda b, h, c: (b * nc + c, cq + h)),
            pl.BlockSpec((rows, hw), lambda b, h, c: (b * nc + c, ci + h)),
            pl.BlockSpec((rows, hw), lambda b, h, c: (b * nc + c, cg + h)),
            pl.BlockSpec((1, hw), lambda b, h, c: (0, h)),
            pl.BlockSpec((1, hw), lambda b, h, c: (0, h)),
            pl.BlockSpec((heads, HG_DIM, HG_DIM), lambda b, h, c: (h, 0, 0)),
        ],
        out_specs=(
            pl.BlockSpec((rows, hw), lambda b, h, c: (b * nc + c, h)),
            pl.BlockSpec((1, heads, HG_DIM, HG_DIM), lambda b, h, c: (b, h, 0, 0)),
        ),
        scratch_shapes=[pltpu.VMEM((heads, HG_DIM, HG_DIM), F32)],
        compiler_params=pltpu.CompilerParams(
            dimension_semantics=("parallel", "parallel", "arbitrary"), vmem_limit_bytes=VMEM_LIMIT),
        name="hgrn2",
    )(p, p, p, p, lb, ng, s0)


def _merge_kernel(ya_ref, yh_ref, gts_a_ref, gts_h_ref, h_ref, wa_ref, wh_ref, wo_ref, g_ref, b_ref,
                  o_ref, acc_scr):
    j = pl.program_id(1)

    @pl.when(j == 0)
    def _():
        acc_scr[...] = jnp.zeros_like(acc_scr)

    ga = jax.nn.sigmoid(gts_a_ref[...].astype(F32))
    gh = jax.nn.sigmoid(gts_h_ref[...].astype(F32))
    merged = ga * _dot(ya_ref[...], wa_ref[...]) + gh * _dot(yh_ref[...], wh_ref[...])
    acc_scr[...] += _dot(merged.astype(BF16), wo_ref[...])

    @pl.when(j == pl.num_programs(1) - 1)
    def _():
        o_ref[...] = _layer_norm(ALPHA * h_ref[...] + acc_scr[...], g_ref[...], b_ref[...])


def _merge(ya, yh, gates, h1, wa, wh, wo, g, b, *, tm, tn):
    m = ya.shape[0]
    nj = D_MODEL // tn
    ca, cb = PROJ_COL["ga"] // tn, PROJ_COL["gb"] // tn
    return pl.pallas_call(
        _merge_kernel,
        out_shape=jax.ShapeDtypeStruct((m, D_MODEL), F32),
        grid=(m // tm, nj),
        in_specs=[
            pl.BlockSpec((tm, ATT_Q_W), lambda i, j: (i, 0)),
            pl.BlockSpec((tm, HG_W), lambda i, j: (i, 0)),
            pl.BlockSpec((tm, tn), lambda i, j: (i, ca + j)),
            pl.BlockSpec((tm, tn), lambda i, j: (i, cb + j)),
            pl.BlockSpec((tm, D_MODEL), lambda i, j: (i, 0)),
            pl.BlockSpec((ATT_Q_W, tn), lambda i, j: (0, j)),
            pl.BlockSpec((HG_W, tn), lambda i, j: (0, j)),
            pl.BlockSpec((tn, D_MODEL), lambda i, j: (j, 0)),
            pl.BlockSpec((1, D_MODEL), lambda i, j: (0, 0)),
            pl.BlockSpec((1, D_MODEL), lambda i, j: (0, 0)),
        ],
        out_specs=pl.BlockSpec((tm, D_MODEL), lambda i, j: (i, 0)),
        scratch_shapes=[pltpu.VMEM((tm, D_MODEL), F32)],
        compiler_params=pltpu.CompilerParams(
            dimension_semantics=("parallel", "arbitrary"), vmem_limit_bytes=VMEM_LIMIT),
        name="merge_ln",
    )(ya, yh, gates, gates, h1, wa, wh, wo, g, b)


def kernel(x, meta, ffn1_w_gate, ffn1_w_up, ffn1_w_down, ln1_g, ln1_b, w_in, idx_k_norm_g, idx_k_norm_b,
           hg_lb_logits, hg_norm_g, w_branch_att, w_branch_hg, w_out, ln2_g, ln2_b,
           ffn2_w_gate, ffn2_w_up, ffn2_w_down, ln3_g, ln3_b):
    batch, seq, _ = x.shape
    m = batch * seq
    xr = x.reshape(m, D_MODEL)
    bf = lambda w: w.astype(BF16)
    row = lambda v: v.reshape(1, -1)

    w_in_t = w_in[0].T
    w_idx = bf(jnp.pad(w_in_t[IN_OFFS[4]:IN_OFFS[6]].T, ((0, 0), (0, LANES - IDX_HEAD_DIM - N_IDX_HEADS))))
    idx_g = jnp.pad(idx_k_norm_g[0], (0, LANES - IDX_HEAD_DIM)).reshape(1, LANES)
    idx_b = jnp.pad(idx_k_norm_b[0], (0, LANES - IDX_HEAD_DIM)).reshape(1, LANES)
    lb = jnp.cumsum(jax.nn.softmax(hg_lb_logits.astype(F32), axis=0), axis=0)[0].reshape(1, HG_W)
    ng = hg_norm_g[0].reshape(1, HG_W)

    f1 = (ffn1_w_gate[0], ffn1_w_up[0], ffn1_w_down[0], row(ln1_g[0]), row(ln1_b[0]))
    f2 = (ffn2_w_gate[0], ffn2_w_up[0], ffn2_w_down[0], row(ln3_g[0]), row(ln3_b[0]))

    hm = _ffn_ln(meta.astype(F32), *f1, tm=N_META, tf=FFN_TF)
    pm = _proj_all(hm, w_in_t, META_TILE_SRC, tm=N_META, scaled_tiles=0)
    kv0 = META_COL["kv"]
    km = jnp.pad(pm[:, kv0:kv0 + ATT_KV_W], ((0, LANES - N_META), (0, 0)))
    vm = jnp.pad(pm[:, kv0 + ATT_KV_W:kv0 + 2 * ATT_KV_W], ((0, LANES - N_META), (0, 0)))
    s_zero = jnp.zeros((HG_HEADS, HG_DIM, HG_DIM), F32)
    meta_cols = {"hf": META_COL["hf"], "hi": META_COL["hi"], "hq": META_COL["hf"], "hg": META_COL["hi"]}
    _, s_meta = _hgrn(pm, meta_cols, lb, ng, s_zero, batch=1, seq=N_META, chunk=N_META, sub=N_META,
                      heads=HG_HEADS, n_inner=1)

    h1 = _ffn_ln(xr, *f1, tm=FFN_TM, tf=FFN_TF)
    p = _proj_all(h1, w_in_t, PROJ_TILE_SRC, tm=1024, scaled_tiles=ATT_Q_W // PROJ_TN)
    p_idx = _proj_idx(h1, w_idx, idx_g, idx_b, tm=1024)

    ikn = p_idx[:, :IDX_HEAD_DIM].astype(BF16)
    ik_lo = jnp.pad(ikn, ((0, 0), (0, LANES - IDX_HEAD_DIM)))
    ik_hi = jnp.pad(ikn, ((0, 0), (LANES - IDX_HEAD_DIM, 0)))
    iw_t = p_idx[:, IDX_HEAD_DIM:IDX_HEAD_DIM + N_IDX_HEADS].T
    y_att = _attention(p, iw_t, ik_lo, ik_hi, km, vm, batch=batch, seq=seq, tq=256, kc=512)
    y_hg, _ = _hgrn(p, PROJ_COL, lb, ng, s_meta[0], batch=batch, seq=seq, chunk=64, sub=16,
                    heads=HG_HEADS, n_inner=8)

    h2 = _merge(y_att, y_hg, p, h1, bf(w_branch_att[0]), bf(w_branch_hg[0]), bf(w_out[0]),
                row(ln2_g[0]), row(ln2_b[0]), tm=512, tn=1024)
    out = _ffn_ln(h2, *f2, tm=FFN_TM, tf=FFN_TF)
    return out.reshape(batch, seq, D_MODEL)
```

```python
import functools
import math

import jax
import jax.numpy as jnp
import numpy as np
from jax import lax
from jax.experimental import pallas as pl
from jax.experimental.pallas import tpu as pltpu

D_MODEL = 2048
N_META = 16
N_ATT_HEADS = 8
N_KV_HEADS = 2
ATT_GROUP = N_ATT_HEADS // N_KV_HEADS
ATT_HEAD_DIM = 128
N_IDX_HEADS = 16
IDX_HEAD_DIM = 64
TOPK = 256
HG_HEADS = 8
HG_DIM = 128
D_FF = 5632
LN_EPS = 1e-5
RMS_EPS = 1e-6
ALPHA = 2.0 ** 0.25

ATT_Q_W = N_ATT_HEADS * ATT_HEAD_DIM
ATT_KV_W = N_KV_HEADS * ATT_HEAD_DIM
IDX_Q_W = N_IDX_HEADS * IDX_HEAD_DIM
HG_W = HG_HEADS * HG_DIM

LANES = 128
SUBLANES = 8
VMEM_LIMIT = 56 * 1024 * 1024

F32 = jnp.float32
BF16 = jnp.bfloat16
INT_MIN = -2 ** 31
F32_LOWEST = float(np.finfo(np.float32).min)
LOG2E = math.log2(math.e)
ATT_Q_SCALE = ATT_HEAD_DIM ** -0.5 * LOG2E
POS_SHIFT = 6
POS_SPLIT = 1 << POS_SHIFT


def _dot(a, b):
    return jnp.dot(a, b, preferred_element_type=F32)


def _dot_nt(a, b):
    return lax.dot_general(a, b, (((1,), (1,)), ((), ())), preferred_element_type=F32)


def _dot_tn(a, b):
    return lax.dot_general(a, b, (((0,), (0,)), ((), ())), preferred_element_type=F32)


def _layer_norm(y, g, b):
    mu = jnp.mean(y, axis=-1, keepdims=True)
    d = y - mu
    var = jnp.mean(d * d, axis=-1, keepdims=True)
    return d * lax.rsqrt(var + LN_EPS) * g + b


def _silu(x):
    return x * jax.nn.sigmoid(x)


def _bf16_parts(c, n=3):
    parts = []
    for _ in range(n):
        p = float(np.asarray(c, np.float32).astype(jnp.bfloat16).astype(np.float32))
        parts.append(p)
        c = c - p
    return parts


FFN_TM = 1024
FFN_TF = 256
LN_ROWS = 128


def _ffn_ln_kernel(x_ref, wg_ref, wu_ref, wd_ref, g_ref, b_ref, o_ref, xb_scr):
    j = pl.program_id(1)

    @pl.when(j == 0)
    def _():
        xb_scr[...] = x_ref[...].astype(BF16)
        o_ref[...] = jnp.zeros_like(o_ref)

    xb = xb_scr[...]
    gate = _dot(xb, wg_ref[...].astype(BF16))
    up = _dot(xb, wu_ref[...].astype(BF16))
    a = (_silu(gate) * up).astype(BF16)
    o_ref[...] += _dot(a, wd_ref[...].astype(BF16))

    @pl.when(j == pl.num_programs(1) - 1)
    def _():
        n_rows = min(LN_ROWS, o_ref.shape[0])

        def ln_rows(r, carry):
            rows = pl.ds(pl.multiple_of(r * n_rows, n_rows), n_rows)
            y = ALPHA * x_ref[rows, :] + 0.5 * o_ref[rows, :]
            o_ref[rows, :] = _layer_norm(y, g_ref[...], b_ref[...])
            return carry

        lax.fori_loop(0, o_ref.shape[0] // n_rows, ln_rows, 0)


def _ffn_ln(x, wg, wu, wd, g, b, *, tm, tf):
    m = x.shape[0]
    grid = (m // tm, D_FF // tf)
    return pl.pallas_call(
        _ffn_ln_kernel,
        out_shape=jax.ShapeDtypeStruct((m, D_MODEL), F32),
        grid=grid,
        in_specs=[
            pl.BlockSpec((tm, D_MODEL), lambda i, j: (i, 0)),
            pl.BlockSpec((D_MODEL, tf), lambda i, j: (0, j)),
            pl.BlockSpec((D_MODEL, tf), lambda i, j: (0, j)),
            pl.BlockSpec((tf, D_MODEL), lambda i, j: (j, 0)),
            pl.BlockSpec((1, D_MODEL), lambda i, j: (0, 0)),
            pl.BlockSpec((1, D_MODEL), lambda i, j: (0, 0)),
        ],
        out_specs=pl.BlockSpec((tm, D_MODEL), lambda i, j: (i, 0)),
        scratch_shapes=[pltpu.VMEM((tm, D_MODEL), BF16)],
        compiler_params=pltpu.CompilerParams(
            dimension_semantics=("parallel", "arbitrary"), vmem_limit_bytes=VMEM_LIMIT),
        name="ffn_ln",
    )(x, wg, wu, wd, g, b)


IN_SPLITS = (ATT_Q_W, ATT_KV_W, ATT_KV_W, IDX_Q_W, IDX_HEAD_DIM, N_IDX_HEADS, HG_W, HG_W, HG_W, HG_W,
             D_MODEL, D_MODEL)
IN_OFFS = tuple(int(v) for v in np.cumsum((0,) + IN_SPLITS))
PROJ_TN = 1024
PROJ_GROUPS = {"aq": (IN_OFFS[0], ATT_Q_W), "iq": (IN_OFFS[3], IDX_Q_W), "hq": (IN_OFFS[6], HG_W),
               "hf": (IN_OFFS[7], HG_W), "hi": (IN_OFFS[8], HG_W), "hg": (IN_OFFS[9], HG_W),
               "ga": (IN_OFFS[10], D_MODEL), "gb": (IN_OFFS[11], D_MODEL), "kv": (IN_OFFS[1], 2 * ATT_KV_W)}
PROJ_SRC_ALIGN = 16


def _proj_layout(names):
    col, src = {}, []
    for name in names:
        start, width = PROJ_GROUPS[name]
        col[name] = len(src) * PROJ_TN
        src += [start + t * PROJ_TN for t in range(-(-width // PROJ_TN))]
    assert all(s % PROJ_SRC_ALIGN == 0 and s + PROJ_TN <= IN_OFFS[-1] for s in src)
    return col, tuple(src)


PROJ_COL, PROJ_TILE_SRC = _proj_layout(("aq", "iq", "hq", "hf", "hi", "hg", "ga", "gb", "kv"))
META_COL, META_TILE_SRC = _proj_layout(("hf", "hi", "kv"))


def _proj_all_kernel(src_ref, x_ref, wt_ref, *rest, scaled_tiles, scale, n_tiles, with_idx):
    if with_idx:
        widx_ref, g_ref, b_ref, o_ref, oidx_ref, xb_scr = rest
    else:
        o_ref, xb_scr = rest
    j = pl.program_id(1)

    @pl.when(j == 0)
    def _():
        xb_scr[...] = x_ref[...].astype(BF16)

    def wide_tile():
        acc = _dot_nt(xb_scr[...], wt_ref[...].astype(BF16))
        acc = acc * jnp.where(j < scaled_tiles, scale, 1.0)
        o_ref[...] = acc.astype(o_ref.dtype)

    if not with_idx:
        wide_tile()
        return
    pl.when(j < n_tiles)(wide_tile)

    @pl.when(j == n_tiles)
    def _():
        p = _dot(xb_scr[...], widx_ref[...])
        lane = lax.broadcasted_iota(jnp.int32, p.shape, 1)
        is_k = lane < IDX_HEAD_DIM
        mu = jnp.sum(jnp.where(is_k, p, 0.0), axis=-1, keepdims=True) * (1.0 / IDX_HEAD_DIM)
        d = jnp.where(is_k, p - mu, 0.0)
        var = jnp.sum(d * d, axis=-1, keepdims=True) * (1.0 / IDX_HEAD_DIM)
        kn = d * lax.rsqrt(var + LN_EPS) * g_ref[...] + b_ref[...]
        w_scale = (N_IDX_HEADS ** -0.5) * (IDX_HEAD_DIM ** -0.5)
        oidx_ref[...] = jnp.where(is_k, kn, p * w_scale)


def _proj_all(xb, w_in_t, tile_src, *, tm, scaled_tiles, idx=None):
    m = xb.shape[0]
    n_tiles = len(tile_src)
    with_idx = idx is not None
    wide = lambda j: jnp.minimum(j, n_tiles - 1)
    in_specs = [pl.BlockSpec((tm, D_MODEL), lambda i, j, src: (i, 0)),
                pl.BlockSpec((pl.Element(PROJ_TN), pl.Element(D_MODEL)),
                             lambda i, j, src: (src[wide(j)] * PROJ_SRC_ALIGN, 0))]
    out_shape = [jax.ShapeDtypeStruct((m, n_tiles * PROJ_TN), BF16)]
    out_specs = [pl.BlockSpec((tm, PROJ_TN), lambda i, j, src: (i, wide(j)))]
    if with_idx:
        in_specs += [pl.BlockSpec((D_MODEL, LANES), lambda i, j, src: (0, 0)),
                     pl.BlockSpec((1, LANES), lambda i, j, src: (0, 0)),
                     pl.BlockSpec((1, LANES), lambda i, j, src: (0, 0))]
        out_shape.append(jax.ShapeDtypeStruct((m, LANES), F32))
        out_specs.append(pl.BlockSpec((tm, LANES), lambda i, j, src: (i, 0)))
    return pl.pallas_call(
        functools.partial(_proj_all_kernel, scaled_tiles=scaled_tiles, scale=ATT_Q_SCALE,
                          n_tiles=n_tiles, with_idx=with_idx),
        out_shape=tuple(out_shape),
        grid_spec=pltpu.PrefetchScalarGridSpec(
            num_scalar_prefetch=1,
            grid=(m // tm, n_tiles + int(with_idx)),
            in_specs=in_specs,
            out_specs=tuple(out_specs),
            scratch_shapes=[pltpu.VMEM((tm, D_MODEL), BF16)],
        ),
        compiler_params=pltpu.CompilerParams(
            dimension_semantics=("parallel", "arbitrary"), vmem_limit_bytes=VMEM_LIMIT),
        name="proj_all",
    )(jnp.asarray([s // PROJ_SRC_ALIGN for s in tile_src], jnp.int32), xb, w_in_t, *(idx or ()))


ATT_ROW_TILE = 32
ATT_BOUND_SLACK = 45.0
UNCHECKED_BITS = 16


def _key_to_f32(key):
    return pltpu.bitcast(key ^ ((key >> 31) & 0x7FFFFFFF), F32)


def _attn_kernel(aq_ref, iq_ref, iwt_ref, iklo_ref, ikhi_ref, k_ref, v_ref, km_ref, vm_ref,
                 o_ref,
                 score_scr, mb_scr, kaug_scr, vaug_scr, qaug_scr, s0_scr, s1_scr, p_scr,
                 macc_scr, m_scr, acc_scr, kn_scr, bound_scr, eye_scr, *, tq, kc, seq):
    qi = pl.program_id(1)
    n_kc = ((qi + 1) * tq + kc - 1) // kc
    kf = float(TOPK)

    @pl.when(qi == 0)
    def _():
        r = lax.broadcasted_iota(jnp.int32, (LANES + seq, LANES), 0)
        lane = lax.broadcasted_iota(jnp.int32, (LANES + seq, LANES), 1)
        pos = jnp.where(r < LANES, r, r - LANES + N_META)
        feat = jnp.where(lane < 3, pos >> POS_SHIFT, jnp.where(lane < 6, pos & (POS_SPLIT - 1), 0))
        feat = feat.astype(F32).astype(BF16)
        ones = jnp.ones((LANES + seq, LANES), BF16)
        eye_scr[...] = (lax.broadcasted_iota(jnp.int32, (tq, tq), 0)
                        == lax.broadcasted_iota(jnp.int32, (tq, tq), 1)).astype(BF16)
        for kvh in range(N_KV_HEADS):
            cs = slice(kvh * LANES, (kvh + 1) * LANES)
            kaug_scr[kvh, 0:LANES, 0:LANES] = km_ref[:, cs]
            kaug_scr[kvh, LANES:, 0:LANES] = k_ref[:, cs]
            kaug_scr[kvh, :, LANES:] = feat
            vaug_scr[kvh, 0:LANES, 0:LANES] = vm_ref[:, cs]
            vaug_scr[kvh, LANES:, 0:LANES] = v_ref[:, cs]
            vaug_scr[kvh, :, LANES:] = ones
            kf32 = kaug_scr[kvh, :, 0:LANES].astype(F32)
            k_norm2 = jnp.max(jnp.sum(kf32 * kf32, axis=1, keepdims=True))
            kn_scr[kvh] = jnp.broadcast_to(k_norm2, kn_scr.shape[1:])

    qcol = qi * tq + lax.broadcasted_iota(jnp.int32, (1, tq), 1)
    iwt = iwt_ref[...]

    def score_body(j, carry):
        off = pl.multiple_of(j * kc, kc)
        klo = iklo_ref[pl.ds(off, kc), :]
        khi = ikhi_ref[pl.ds(off, kc), :]
        acc = jnp.zeros((kc, tq), F32)
        for p in range(N_IDX_HEADS // 2):
            q2 = iq_ref[:, p * LANES:(p + 1) * LANES]
            acc = acc + jnp.maximum(_dot_nt(klo, q2), 0.0) * iwt[2 * p:2 * p + 1, :]
            acc = acc + jnp.maximum(_dot_nt(khi, q2), 0.0) * iwt[2 * p + 1:2 * p + 2, :]
        krow = off + lax.broadcasted_iota(jnp.int32, (kc, tq), 0)
        score_scr[pl.ds(off, kc), :] = jnp.where(krow <= qcol, acc, -jnp.inf)
        return carry

    lax.fori_loop(0, n_kc, score_body, 0)

    n_acc = 8
    acc_rows = n_acc * SUBLANES

    def count_ge(cand):
        def body(j, acc):
            off = pl.multiple_of(j * kc, kc)
            w = jnp.where(score_scr[pl.ds(off, kc), :] >= cand, 1.0, 0.0)
            return acc + jnp.sum(w.reshape(kc // acc_rows, acc_rows, tq), axis=0)

        acc = lax.fori_loop(0, n_kc, body, jnp.zeros((acc_rows, tq), F32))
        return jnp.sum(acc, axis=0, keepdims=True)

    c0 = count_ge(jnp.zeros((1, tq), F32))
    ok0 = c0 >= kf
    thr0 = jnp.where(ok0, 0, INT_MIN).astype(jnp.int32)
    cnt0 = jnp.where(ok0, c0, 0.0)

    short_row = qcol + 1 < TOPK

    def unsettled(cnt):
        return jnp.sum(jnp.where((cnt == kf) | short_row, 0.0, 1.0))

    n_bits = 31
    group = 4

    def bit_cond(carry):
        i, _, _, pending = carry
        return jnp.logical_and(i < n_bits, pending > 0.0)

    def resolve_bit(i, thr, cnt):
        shift = jnp.maximum(n_bits - 1 - i, 0)
        cand = thr | jnp.where(i < n_bits, jnp.left_shift(jnp.int32(1), shift), 0)
        c = count_ge(_key_to_f32(cand))
        ok = c >= kf
        return jnp.where(ok, cand, thr), jnp.where(ok, c, cnt)

    def bit_body(carry):
        i, thr, cnt, _ = carry
        for b in range(group):
            thr, cnt = resolve_bit(i + b, thr, cnt)
        return i + group, thr, cnt, unsettled(cnt)

    thr1, cnt1 = lax.fori_loop(0, UNCHECKED_BITS, lambda i, c: resolve_bit(i, *c), (thr0, cnt0))
    _, thr, cnt, _ = lax.while_loop(
        bit_cond, bit_body, (jnp.int32(UNCHECKED_BITS), thr1, cnt1, unsettled(cnt1)))
    t_f = jnp.where(thr == INT_MIN, F32_LOWEST, _key_to_f32(thr))

    @pl.when(jnp.max(cnt) > kf)
    def _():
        def gt_body(j, acc):
            off = pl.multiple_of(j * kc, kc)
            w = jnp.where(score_scr[pl.ds(off, kc), :] > t_f, 1.0, 0.0)
            return acc + jnp.sum(w, axis=0, keepdims=True)

        need = kf - lax.fori_loop(0, n_kc, gt_body, jnp.zeros((1, tq), F32))
        lower = (lax.broadcasted_iota(jnp.int32, (kc, kc), 0)
                 >= lax.broadcasted_iota(jnp.int32, (kc, kc), 1)).astype(BF16)

        def tie_body(j, before):
            off = pl.multiple_of(j * kc, kc)
            sc = score_scr[pl.ds(off, kc), :]
            eq = sc == t_f
            eqf = jnp.where(eq, 1.0, 0.0)
            rank = before + _dot(lower, eqf.astype(BF16))
            score_scr[pl.ds(off, kc), :] = jnp.where(eq & (rank > need), -jnp.inf, sc)
            return before + jnp.sum(eqf, axis=0, keepdims=True)

        lax.fori_loop(0, n_kc, tie_body, jnp.zeros((1, tq), F32))

    eye = eye_scr[...]

    def mask_body(j, last_sel):
        off = pl.multiple_of(j * kc, kc)
        picked = score_scr[pl.ds(off, kc), :] >= t_f
        sel = _dot_nt(eye, jnp.where(picked, 1.0, 0.0).astype(BF16))
        mb_scr[:, pl.ds(off, kc)] = jnp.where(sel > 0.5, 0.0, -jnp.inf)
        krow = (off + lax.broadcasted_iota(jnp.int32, (kc, tq), 0)).astype(F32)
        hit = jnp.where(picked, krow, -1.0)
        return jnp.maximum(last_sel, jnp.max(hit.reshape(kc // SUBLANES, SUBLANES, tq), axis=0))

    last_sel = lax.fori_loop(0, n_kc, mask_body, jnp.full((SUBLANES, tq), -1.0, F32))
    last_pos = jnp.maximum(jnp.max(last_sel, axis=0, keepdims=True) + N_META, N_META - 1.0)

    rt = ATT_ROW_TILE
    lane_q = lax.broadcasted_iota(jnp.int32, (1, LANES), 1)
    meta_mask = jnp.where(lax.broadcasted_iota(jnp.int32, (rt, LANES), 1) < N_META, 0.0, -jnp.inf)

    s_slots = (s0_scr, s1_scr)

    def logits(kvh, krow0, width, slot):
        s_slots[slot][:, 0:width] = _dot_nt(qaug_scr[kvh], kaug_scr[kvh, pl.ds(krow0, width), :])

    def max_tiles(kvh, slot, width, mask_fn):
        s_ref = s_slots[slot]
        for r0 in range(0, tq, rt):
            mb = mask_fn(r0)
            for g in range(ATT_GROUP):
                rows = slice(g * tq + r0, g * tq + r0 + rt)
                mx = s_ref[rows, 0:LANES] + mb[:, 0:LANES]
                for c in range(1, width // LANES):
                    cols = slice(c * LANES, (c + 1) * LANES)
                    mx = jnp.maximum(mx, s_ref[rows, cols] + mb[:, cols])
                macc_scr[rows, :] = jnp.maximum(macc_scr[rows, :], mx)

    def exp_tiles(kvh, slot, width, mask_fn):
        s_ref = s_slots[slot]
        for r0 in range(0, tq, rt):
            mb = mask_fn(r0)
            for g in range(ATT_GROUP):
                rows = slice(g * tq + r0, g * tq + r0 + rt)
                m_row = m_scr[rows, :]
                for c in range(width // LANES):
                    cols = slice(c * LANES, (c + 1) * LANES)
                    p_scr[rows, cols] = jnp.exp2(s_ref[rows, cols] + mb[:, cols] - m_row).astype(BF16)

    def add_pv(kvh, krow0, width):
        vblk = vaug_scr[kvh, pl.ds(krow0, width), :]
        half = ATT_GROUP * tq // 2
        for r0 in (0, half):
            acc_scr[kvh, r0:r0 + half, :] += _dot(p_scr[r0:r0 + half, 0:width], vblk)

    def run_pass(kvh, tiles_fn, with_pv):
        def krow(j):
            return pl.multiple_of(LANES + j * kc, LANES)

        def step(j, slot, lookahead):
            off = pl.multiple_of(j * kc, kc)
            if lookahead:
                logits(kvh, krow(j + 1), kc, 1 - slot)
            tiles_fn(kvh, slot, kc, lambda r0: mb_scr[r0:r0 + rt, pl.ds(off, kc)])
            if with_pv:
                add_pv(kvh, krow(j), kc)

        logits(kvh, 0, LANES, 0)
        tiles_fn(kvh, 0, LANES, lambda r0: meta_mask)
        if with_pv:
            add_pv(kvh, 0, LANES)

            def serial_body(j, carry):
                off = pl.multiple_of(j * kc, kc)
                logits(kvh, krow(j), kc, 0)
                tiles_fn(kvh, 0, kc, lambda r0: mb_scr[r0:r0 + rt, pl.ds(off, kc)])
                add_pv(kvh, krow(j), kc)
                return carry

            lax.fori_loop(0, n_kc, serial_body, 0)
            return
        logits(kvh, krow(0), kc, 0)
        n_pairs = (n_kc - 1) // 2

        def pair_body(i, carry):
            step(2 * i, 0, True)
            step(2 * i + 1, 1, True)
            return carry

        lax.fori_loop(0, n_pairs, pair_body, 0)
        j0 = 2 * n_pairs
        two_left = n_kc - j0 == 2

        @pl.when(two_left)
        def _():
            step(j0, 0, True)
            step(j0 + 1, 1, False)

        @pl.when(jnp.logical_not(two_left))
        def _():
            step(j0, 0, False)

    for h in range(N_ATT_HEADS):
        kvh, g = divmod(h, ATT_GROUP)
        parts = _bf16_parts(2.0 ** (-8.0 * (h + 1) / N_ATT_HEADS) * LOG2E)
        vals = [POS_SPLIT * p for p in parts] + parts
        qfeat = jnp.zeros((1, LANES), F32)
        for i, val in enumerate(vals):
            qfeat = jnp.where(lane_q == i, val, qfeat)
        qaug_scr[kvh, g * tq:(g + 1) * tq, 0:LANES] = aq_ref[:, h * LANES:(h + 1) * LANES]
        qaug_scr[kvh, g * tq:(g + 1) * tq, LANES:] = jnp.broadcast_to(qfeat, (tq, LANES)).astype(BF16)

    a_pos = jnp.floor(last_pos * (1.0 / POS_SPLIT))
    b_pos = last_pos - a_pos * POS_SPLIT
    to_rows = lambda v: _dot_nt(eye, jnp.broadcast_to(v, (LANES, tq)).astype(BF16))
    last_pos_rows = to_rows(a_pos) * POS_SPLIT + to_rows(b_pos)
    slack = jnp.float32(0.0)
    for kvh in range(N_KV_HEADS):
        qf = qaug_scr[kvh, :, 0:LANES].astype(F32)
        qk_max = jnp.sqrt(jnp.sum(qf * qf, axis=1, keepdims=True) * kn_scr[kvh, 0:1, 0:1])
        slack = jnp.maximum(slack, jnp.max(qk_max))
        for g in range(ATT_GROUP):
            rows = slice(g * tq, (g + 1) * tq)
            c = 2.0 ** (-8.0 * (kvh * ATT_GROUP + g + 1) / N_ATT_HEADS) * LOG2E
            bound_scr[kvh, rows, :] = c * last_pos_rows + qk_max[rows]
    bound_ok = slack <= ATT_BOUND_SLACK

    for kvh in range(N_KV_HEADS):
        @pl.when(bound_ok)
        def _():
            m_scr[...] = bound_scr[kvh]

        @pl.when(jnp.logical_not(bound_ok))
        def _():
            macc_scr[...] = jnp.full_like(macc_scr, -jnp.inf)
            run_pass(kvh, max_tiles, with_pv=False)
            m_scr[...] = jnp.broadcast_to(jnp.max(macc_scr[...], axis=1, keepdims=True), m_scr.shape)

        acc_scr[kvh] = jnp.zeros(acc_scr.shape[1:], F32)
        run_pass(kvh, exp_tiles, with_pv=True)

    for h in range(N_ATT_HEADS):
        kvh, g = divmod(h, ATT_GROUP)
        acc = acc_scr[kvh, g * tq:(g + 1) * tq, :]
        o_ref[:, h * LANES:(h + 1) * LANES] = (acc[:, 0:LANES] / acc[:, LANES:LANES + 1]).astype(o_ref.dtype)


def _attention(p_att, iw_t, ik_lo, ik_hi, km, vm, *, batch, seq, tq, kc):
    m = batch * seq
    nq = seq // tq
    qcol, icol, kcol = PROJ_COL["aq"] // ATT_Q_W, PROJ_COL["iq"] // IDX_Q_W, PROJ_COL["kv"] // ATT_KV_W
    rows4 = ATT_GROUP * tq
    return pl.pallas_call(
        functools.partial(_attn_kernel, tq=tq, kc=kc, seq=seq),
        out_shape=jax.ShapeDtypeStruct((m, ATT_Q_W), BF16),
        grid=(batch, nq),
        in_specs=[
            pl.BlockSpec((tq, ATT_Q_W), lambda b, q: (b * nq + q, qcol)),
            pl.BlockSpec((tq, IDX_Q_W), lambda b, q: (b * nq + q, icol)),
            pl.BlockSpec((N_IDX_HEADS, tq), lambda b, q: (0, b * nq + q)),
            pl.BlockSpec((seq, LANES), lambda b, q: (b, 0)),
            pl.BlockSpec((seq, LANES), lambda b, q: (b, 0)),
            pl.BlockSpec((seq, ATT_KV_W), lambda b, q: (b, kcol)),
            pl.BlockSpec((seq, ATT_KV_W), lambda b, q: (b, kcol + 1)),
            pl.BlockSpec((LANES, ATT_KV_W), lambda b, q: (0, 0)),
            pl.BlockSpec((LANES, ATT_KV_W), lambda b, q: (0, 0)),
        ],
        out_specs=pl.BlockSpec((tq, ATT_Q_W), lambda b, q: (b * nq + q, 0)),
        scratch_shapes=[
            pltpu.VMEM((seq, tq), F32),
            pltpu.VMEM((tq, seq), F32),
            pltpu.VMEM((N_KV_HEADS, LANES + seq, 2 * LANES), BF16),
            pltpu.VMEM((N_KV_HEADS, LANES + seq, 2 * LANES), BF16),
            pltpu.VMEM((N_KV_HEADS, rows4, 2 * LANES), BF16),
            pltpu.VMEM((rows4, kc), F32),
            pltpu.VMEM((rows4, kc), F32),
            pltpu.VMEM((rows4, kc), BF16),
            pltpu.VMEM((rows4, LANES), F32),
            pltpu.VMEM((rows4, LANES), F32),
            pltpu.VMEM((N_KV_HEADS, rows4, 2 * LANES), F32),
            pltpu.VMEM((N_KV_HEADS, SUBLANES, LANES), F32),
            pltpu.VMEM((N_KV_HEADS, rows4, LANES), F32),
            pltpu.VMEM((tq, tq), BF16),
        ],
        compiler_params=pltpu.CompilerParams(
            dimension_semantics=("parallel", "arbitrary"), vmem_limit_bytes=VMEM_LIMIT),
        name="dsa_attention",
    )(p_att, p_att, iw_t, ik_lo, ik_hi, p_att, p_att, km, vm)


def _split3(x):
    hi = x.astype(BF16)
    r = x - hi.astype(F32)
    mid = r.astype(BF16)
    lo = (r - mid.astype(F32)).astype(BF16)
    return hi, mid, lo


HG_SAFE_EXPONENT = 80.0


def _hgrn_kernel(hf_ref, hr_q_ref, hr_i_ref, hr_g_ref, lb_ref, ng_ref, s0_ref,
                 y_ref, sT_out_ref, sT_scr, *, chunk, sub, heads, n_inner):
    c_idx = pl.program_id(2)

    @pl.when(c_idx == 0)
    def _():
        sT_scr[...] = s0_ref[...]

    n_sub = chunk // sub
    causal = (lax.broadcasted_iota(jnp.int32, (chunk, chunk), 0)
              >= lax.broadcasted_iota(jnp.int32, (chunk, chunk), 1))
    tri = causal.astype(BF16)
    t_iota = lax.broadcasted_iota(jnp.int32, (sub, LANES), 0)

    def head_inputs(r0, g):
        rows, cs = pl.ds(r0, chunk), slice(g * HG_DIM, (g + 1) * HG_DIM)
        lb = lb_ref[:, cs]
        q = _silu(hr_q_ref[rows, cs].astype(F32))
        fg = lb + (1.0 - lb) * jax.nn.sigmoid(hf_ref[rows, cs].astype(F32))
        v = hr_i_ref[rows, cs].astype(F32)
        l_hi, l_mid, l_lo = _split3(jnp.log(fg))
        b = _dot(tri, l_hi) + _dot(tri, l_mid) + _dot(tri, l_lo)
        return q, 1.0 - fg, v, b

    def carry_state(g, q, kk, vb, b):
        sT = sT_scr[g]
        b_last = b[chunk - 1:chunk, :]
        qe = (q * jnp.exp(b)).astype(BF16)
        o_state = _dot_nt(qe, sT.astype(BF16))
        khat = (kk * jnp.exp(b_last - b)).astype(BF16)
        sT_scr[g] = sT * jnp.exp(b_last) + _dot_tn(vb, khat)
        return qe, o_state

    def finish(r0, g, o):
        rows, cs = pl.ds(r0, chunk), slice(g * HG_DIM, (g + 1) * HG_DIM)
        o = o * lax.rsqrt(jnp.mean(o * o, axis=-1, keepdims=True) + RMS_EPS) * ng_ref[:, cs]
        o = o * _silu(hr_g_ref[rows, cs].astype(F32))
        y_ref[rows, cs] = o.astype(y_ref.dtype)

    def factored_chunk(c, carry):
        rows = pl.ds(pl.multiple_of(c * chunk, chunk), chunk)
        hs = [slice(g * HG_DIM, (g + 1) * HG_DIM) for g in range(heads)]
        lb = lb_ref[...]
        q = _silu(hr_q_ref[rows, :].astype(F32))
        fg = lb + (1.0 - lb) * jax.nn.sigmoid(hf_ref[rows, :].astype(F32))
        kk = 1.0 - fg
        vb = hr_i_ref[rows, :]
        l_hi, l_mid, l_lo = _split3(jnp.log(fg))
        b = _dot(tri, l_hi) + _dot(tri, l_mid) + _dot(tri, l_lo)
        b_last = b[chunk - 1:chunk, :]
        qe = (q * jnp.exp(b)).astype(BF16)
        ke = (kk * jnp.exp(-b)).astype(BF16)
        khat = (kk * jnp.exp(b_last - b)).astype(BF16)
        keep = jnp.exp(b_last)
        s_old = [sT_scr[g] for g in range(heads)]
        a = [_dot_nt(qe[:, cs], ke[:, cs]) for cs in hs]
        o_state = [_dot_nt(qe[:, cs], s_old[g].astype(BF16)) for g, cs in enumerate(hs)]
        s_add = [_dot_tn(vb[:, cs], khat[:, cs]) for cs in hs]
        a = [jnp.where(causal, x, 0.0).astype(BF16) for x in a]
        o = [o_state[g] + _dot(a[g], vb[:, cs]) for g, cs in enumerate(hs)]
        for g, cs in enumerate(hs):
            sT_scr[g] = s_old[g] * keep[:, cs] + s_add[g]
        o = [x * lax.rsqrt(jnp.mean(x * x, axis=-1, keepdims=True) + RMS_EPS) for x in o]
        o = jnp.concatenate(o, axis=1) * ng_ref[...] * _silu(hr_g_ref[rows, :].astype(F32))
        y_ref[rows, :] = o.astype(y_ref.dtype)
        return carry

    def guarded_chunk(c, carry):
        r0 = pl.multiple_of(c * chunk, chunk)
        for g in range(heads):
            q, kk, v, b = head_inputs(r0, g)
            vb = v.astype(BF16)
            _, o_state = carry_state(g, q, kk, vb, b)
            rows = []
            for i in range(n_sub):
                s0, s1 = i * sub, (i + 1) * sub
                bs, qs, ks, vs = b[s0:s1], q[s0:s1], kk[s0:s1], v[s0:s1]
                o_i = o_state[s0:s1]
                if i > 0:
                    r_i = b[s0 - 1:s0, :]
                    qt = (qs * jnp.exp(bs - r_i)).astype(BF16)
                    kt = (kk[:s0] * jnp.exp(r_i - b[:s0])).astype(BF16)
                    a_off = _dot_nt(qt, kt)
                    o_i = o_i + _dot(a_off.astype(BF16), vb[:s0])
                for s in range(sub):
                    e = jnp.exp(jnp.where(t_iota >= s, bs - bs[s:s + 1], -jnp.inf))
                    a_col = jnp.sum(qs * e * ks[s:s + 1], axis=1, keepdims=True)
                    o_i = o_i + a_col * vs[s:s + 1]
                rows.append(o_i)
            finish(r0, g, jnp.concatenate(rows, axis=0) if n_sub > 1 else rows[0])
        return carry

    factoring_safe = jnp.max(-jnp.log(lb_ref[...])) * chunk <= HG_SAFE_EXPONENT

    @pl.when(factoring_safe)
    def _():
        for c in range(n_inner):
            factored_chunk(c, 0)

    @pl.when(jnp.logical_not(factoring_safe))
    def _():
        lax.fori_loop(0, n_inner, guarded_chunk, 0)

    @pl.when(c_idx == pl.num_programs(2) - 1)
    def _():
        sT_out_ref[0] = sT_scr[...]


def _hgrn(p, cols, lb, ng, s0, *, batch, seq, chunk, sub, heads, n_inner):
    m = batch * seq
    rows = chunk * n_inner
    nc = seq // rows
    hw = heads * HG_DIM
    nhg = HG_HEADS // heads
    cf, cq, ci, cg = (cols[k] // hw for k in ("hf", "hq", "hi", "hg"))
    return pl.pallas_call(
        functools.partial(_hgrn_kernel, chunk=chunk, sub=sub, heads=heads, n_inner=n_inner),
        out_shape=(jax.ShapeDtypeStruct((m, HG_W), BF16),
                   jax.ShapeDtypeStruct((batch, HG_HEADS, HG_DIM, HG_DIM), F32)),
        grid=(batch, nhg, nc),
        in_specs=[
            pl.BlockSpec((rows, hw), lambda b, h, c: (b * nc + c, cf + h)),
            pl.BlockSpec((rows, hw), lambda b, h, c: (b * nc + c, cq + h)),
            pl.BlockSpec((rows, hw), lambda b, h, c: (b * nc + c, ci + h)),
            pl.BlockSpec((rows, hw), lambda b, h, c: (b * nc + c, cg + h)),
            pl.BlockSpec((1, hw), lambda b, h, c: (0, h)),
            pl.BlockSpec((1, hw), lambda b, h, c: (0, h)),
            pl.BlockSpec((heads, HG_DIM, HG_DIM), lambda b, h, c: (h, 0, 0)),
        ],
        out_specs=(
            pl.BlockSpec((rows, hw), lambda b, h, c: (b * nc + c, h)),
            pl.BlockSpec((1, heads, HG_DIM, HG_DIM), lambda b, h, c: (b, h, 0, 0)),
        ),
        scratch_shapes=[pltpu.VMEM((heads, HG_DIM, HG_DIM), F32)],
        compiler_params=pltpu.CompilerParams(
            dimension_semantics=("parallel", "parallel", "arbitrary"), vmem_limit_bytes=VMEM_LIMIT),
        name="hgrn2",
    )(p, p, p, p, lb, ng, s0)


def _merge_kernel(ya_ref, yh_ref, gts_a_ref, gts_h_ref, h_ref, wa_ref, wh_ref, wo_ref, g_ref, b_ref,
                  o_ref, acc_scr):
    j = pl.program_id(1)

    @pl.when(j == 0)
    def _():
        acc_scr[...] = jnp.zeros_like(acc_scr)

    ga = jax.nn.sigmoid(gts_a_ref[...].astype(F32))
    gh = jax.nn.sigmoid(gts_h_ref[...].astype(F32))
    merged = ga * _dot(ya_ref[...], wa_ref[...]) + gh * _dot(yh_ref[...], wh_ref[...])
    acc_scr[...] += _dot(merged.astype(BF16), wo_ref[...])

    @pl.when(j == pl.num_programs(1) - 1)
    def _():
        o_ref[...] = _layer_norm(ALPHA * h_ref[...] + acc_scr[...], g_ref[...], b_ref[...])


def _merge(ya, yh, gates, h1, wa, wh, wo, g, b, *, tm, tn):
    m = ya.shape[0]
    nj = D_MODEL // tn
    ca, cb = PROJ_COL["ga"] // tn, PROJ_COL["gb"] // tn
    return pl.pallas_call(
        _merge_kernel,
        out_shape=jax.ShapeDtypeStruct((m, D_MODEL), F32),
        grid=(m // tm, nj),
        in_specs=[
            pl.BlockSpec((tm, ATT_Q_W), lambda i, j: (i, 0)),
            pl.BlockSpec((tm, HG_W), lambda i, j: (i, 0)),
            pl.BlockSpec((tm, tn), lambda i, j: (i, ca + j)),
            pl.BlockSpec((tm, tn), lambda i, j: (i, cb + j)),
            pl.BlockSpec((tm, D_MODEL), lambda i, j: (i, 0)),
            pl.BlockSpec((ATT_Q_W, tn), lambda i, j: (0, j)),
            pl.BlockSpec((HG_W, tn), lambda i, j: (0, j)),
            pl.BlockSpec((tn, D_MODEL), lambda i, j: (j, 0)),
            pl.BlockSpec((1, D_MODEL), lambda i, j: (0, 0)),
            pl.BlockSpec((1, D_MODEL), lambda i, j: (0, 0)),
        ],
        out_specs=pl.BlockSpec((tm, D_MODEL), lambda i, j: (i, 0)),
        scratch_shapes=[pltpu.VMEM((tm, D_MODEL), F32)],
        compiler_params=pltpu.CompilerParams(
            dimension_semantics=("parallel", "arbitrary"), vmem_limit_bytes=VMEM_LIMIT),
        name="merge_ln",
    )(ya, yh, gates, gates, h1, wa, wh, wo, g, b)


def kernel(x, meta, ffn1_w_gate, ffn1_w_up, ffn1_w_down, ln1_g, ln1_b, w_in, idx_k_norm_g, idx_k_norm_b,
           hg_lb_logits, hg_norm_g, w_branch_att, w_branch_hg, w_out, ln2_g, ln2_b,
           ffn2_w_gate, ffn2_w_up, ffn2_w_down, ln3_g, ln3_b):
    batch, seq, _ = x.shape
    m = batch * seq
    xr = x.reshape(m, D_MODEL)
    bf = lambda w: w.astype(BF16)
    row = lambda v: v.reshape(1, -1)

    w_in_t = w_in[0].T
    w_idx = bf(jnp.pad(w_in_t[IN_OFFS[4]:IN_OFFS[6]].T, ((0, 0), (0, LANES - IDX_HEAD_DIM - N_IDX_HEADS))))
    idx_g = jnp.pad(idx_k_norm_g[0], (0, LANES - IDX_HEAD_DIM)).reshape(1, LANES)
    idx_b = jnp.pad(idx_k_norm_b[0], (0, LANES - IDX_HEAD_DIM)).reshape(1, LANES)
    lb = jnp.cumsum(jax.nn.softmax(hg_lb_logits.astype(F32), axis=0), axis=0)[0].reshape(1, HG_W)
    ng = hg_norm_g[0].reshape(1, HG_W)

    f1 = (ffn1_w_gate[0], ffn1_w_up[0], ffn1_w_down[0], row(ln1_g[0]), row(ln1_b[0]))
    f2 = (ffn2_w_gate[0], ffn2_w_up[0], ffn2_w_down[0], row(ln3_g[0]), row(ln3_b[0]))

    hm = _ffn_ln(meta.astype(F32), *f1, tm=N_META, tf=FFN_TF)
    (pm,) = _proj_all(hm, w_in_t, META_TILE_SRC, tm=N_META, scaled_tiles=0)
    kv0 = META_COL["kv"]
    km = jnp.pad(pm[:, kv0:kv0 + ATT_KV_W], ((0, LANES - N_META), (0, 0)))
    vm = jnp.pad(pm[:, kv0 + ATT_KV_W:kv0 + 2 * ATT_KV_W], ((0, LANES - N_META), (0, 0)))
    s_zero = jnp.zeros((HG_HEADS, HG_DIM, HG_DIM), F32)
    meta_cols = {"hf": META_COL["hf"], "hi": META_COL["hi"], "hq": META_COL["hf"], "hg": META_COL["hi"]}
    _, s_meta = _hgrn(pm, meta_cols, lb, ng, s_zero, batch=1, seq=N_META, chunk=N_META, sub=N_META,
                      heads=HG_HEADS, n_inner=1)

    h1 = _ffn_ln(xr, *f1, tm=FFN_TM, tf=FFN_TF)
    p, p_idx = _proj_all(h1, w_in_t, PROJ_TILE_SRC, tm=1024, scaled_tiles=ATT_Q_W // PROJ_TN,
                         idx=(w_idx, idx_g, idx_b))

    ikn = p_idx[:, :IDX_HEAD_DIM].astype(BF16)
    ik_lo = jnp.pad(ikn, ((0, 0), (0, LANES - IDX_HEAD_DIM)))
    ik_hi = jnp.pad(ikn, ((0, 0), (LANES - IDX_HEAD_DIM, 0)))
    iw_t = p_idx[:, IDX_HEAD_DIM:IDX_HEAD_DIM + N_IDX_HEADS].T
    y_att = _attention(p, iw_t, ik_lo, ik_hi, km, vm, batch=batch, seq=seq, tq=256, kc=512)
    y_hg, _ = _hgrn(p, PROJ_COL, lb, ng, s_meta[0], batch=batch, seq=seq, chunk=64, sub=16,
                    heads=HG_HEADS, n_inner=8)

    h2 = _merge(y_att, y_hg, p, h1, bf(w_branch_att[0]), bf(w_branch_hg[0]), bf(w_out[0]),
                row(ln2_g[0]), row(ln2_b[0]), tm=512, tn=1024)
    out = _ffn_ln(h2, *f2, tm=FFN_TM, tf=FFN_TF)
    return out.reshape(batch, seq, D_MODEL)
```

```python
import functools
import math

import jax
import jax.numpy as jnp
import numpy as np
from jax import lax
from jax.experimental import pallas as pl
from jax.experimental.pallas import tpu as pltpu

D_MODEL = 2048
N_META = 16
N_ATT_HEADS = 8
N_KV_HEADS = 2
ATT_GROUP = N_ATT_HEADS // N_KV_HEADS
ATT_HEAD_DIM = 128
N_IDX_HEADS = 16
IDX_HEAD_DIM = 64
TOPK = 256
HG_HEADS = 8
HG_DIM = 128
D_FF = 5632
LN_EPS = 1e-5
RMS_EPS = 1e-6
ALPHA = 2.0 ** 0.25

ATT_Q_W = N_ATT_HEADS * ATT_HEAD_DIM
ATT_KV_W = N_KV_HEADS * ATT_HEAD_DIM
IDX_Q_W = N_IDX_HEADS * IDX_HEAD_DIM
HG_W = HG_HEADS * HG_DIM

LANES = 128
SUBLANES = 8
VMEM_LIMIT = 56 * 1024 * 1024

F32 = jnp.float32
BF16 = jnp.bfloat16
INT_MIN = -2 ** 31
F32_LOWEST = float(np.finfo(np.float32).min)
LOG2E = math.log2(math.e)
ATT_Q_SCALE = ATT_HEAD_DIM ** -0.5 * LOG2E
POS_SHIFT = 6
POS_SPLIT = 1 << POS_SHIFT


def _dot(a, b):
    return jnp.dot(a, b, preferred_element_type=F32)


def _dot_nt(a, b):
    return lax.dot_general(a, b, (((1,), (1,)), ((), ())), preferred_element_type=F32)


def _dot_tn(a, b):
    return lax.dot_general(a, b, (((0,), (0,)), ((), ())), preferred_element_type=F32)


def _layer_norm(y, g, b):
    mu = jnp.mean(y, axis=-1, keepdims=True)
    d = y - mu
    var = jnp.mean(d * d, axis=-1, keepdims=True)
    return d * lax.rsqrt(var + LN_EPS) * g + b


def _silu(x):
    return x * jax.nn.sigmoid(x)


def _bf16_parts(c, n=3):
    parts = []
    for _ in range(n):
        p = float(np.asarray(c, np.float32).astype(jnp.bfloat16).astype(np.float32))
        parts.append(p)
        c = c - p
    return parts


FFN_TM = 1024
FFN_TF = 256
LN_ROWS = 128


def _ffn_ln_kernel(x_ref, wg_ref, wu_ref, wd_ref, g_ref, b_ref, o_ref, xb_scr):
    j = pl.program_id(1)

    @pl.when(j == 0)
    def _():
        xb_scr[...] = x_ref[...].astype(BF16)
        o_ref[...] = jnp.zeros_like(o_ref)

    xb = xb_scr[...]
    gate = _dot(xb, wg_ref[...].astype(BF16))
    up = _dot(xb, wu_ref[...].astype(BF16))
    a = (_silu(gate) * up).astype(BF16)
    o_ref[...] += _dot(a, wd_ref[...].astype(BF16))

    @pl.when(j == pl.num_programs(1) - 1)
    def _():
        n_rows = min(LN_ROWS, o_ref.shape[0])

        def ln_rows(r, carry):
            rows = pl.ds(pl.multiple_of(r * n_rows, n_rows), n_rows)
            y = ALPHA * x_ref[rows, :] + 0.5 * o_ref[rows, :]
            o_ref[rows, :] = _layer_norm(y, g_ref[...], b_ref[...])
            return carry

        lax.fori_loop(0, o_ref.shape[0] // n_rows, ln_rows, 0)


def _ffn_ln(x, wg, wu, wd, g, b, *, tm, tf):
    m = x.shape[0]
    grid = (m // tm, D_FF // tf)
    return pl.pallas_call(
        _ffn_ln_kernel,
        out_shape=jax.ShapeDtypeStruct((m, D_MODEL), F32),
        grid=grid,
        in_specs=[
            pl.BlockSpec((tm, D_MODEL), lambda i, j: (i, 0)),
            pl.BlockSpec((D_MODEL, tf), lambda i, j: (0, j)),
            pl.BlockSpec((D_MODEL, tf), lambda i, j: (0, j)),
            pl.BlockSpec((tf, D_MODEL), lambda i, j: (j, 0)),
            pl.BlockSpec((1, D_MODEL), lambda i, j: (0, 0)),
            pl.BlockSpec((1, D_MODEL), lambda i, j: (0, 0)),
        ],
        out_specs=pl.BlockSpec((tm, D_MODEL), lambda i, j: (i, 0)),
        scratch_shapes=[pltpu.VMEM((tm, D_MODEL), BF16)],
        compiler_params=pltpu.CompilerParams(
            dimension_semantics=("parallel", "arbitrary"), vmem_limit_bytes=VMEM_LIMIT),
        name="ffn_ln",
    )(x, wg, wu, wd, g, b)


IN_SPLITS = (ATT_Q_W, ATT_KV_W, ATT_KV_W, IDX_Q_W, IDX_HEAD_DIM, N_IDX_HEADS, HG_W, HG_W, HG_W, HG_W,
             D_MODEL, D_MODEL)
IN_OFFS = tuple(int(v) for v in np.cumsum((0,) + IN_SPLITS))
PROJ_TN = 1024
PROJ_GROUPS = {"aq": (IN_OFFS[0], ATT_Q_W), "iq": (IN_OFFS[3], IDX_Q_W), "hq": (IN_OFFS[6], HG_W),
               "hf": (IN_OFFS[7], HG_W), "hi": (IN_OFFS[8], HG_W), "hg": (IN_OFFS[9], HG_W),
               "ga": (IN_OFFS[10], D_MODEL), "gb": (IN_OFFS[11], D_MODEL), "kv": (IN_OFFS[1], 2 * ATT_KV_W)}
PROJ_SRC_ALIGN = 16


def _proj_layout(names):
    col, src = {}, []
    for name in names:
        start, width = PROJ_GROUPS[name]
        col[name] = len(src) * PROJ_TN
        src += [start + t * PROJ_TN for t in range(-(-width // PROJ_TN))]
    assert all(s % PROJ_SRC_ALIGN == 0 and s + PROJ_TN <= IN_OFFS[-1] for s in src)
    return col, tuple(src)


PROJ_COL, PROJ_TILE_SRC = _proj_layout(("aq", "iq", "hq", "hf", "hi", "hg", "ga", "gb", "kv"))
META_COL, META_TILE_SRC = _proj_layout(("hf", "hi", "kv"))


def _proj_all_kernel(src_ref, x_ref, wt_ref, o_ref, xb_scr, *, scaled_tiles, scale):
    @pl.when(pl.program_id(1) == 0)
    def _():
        xb_scr[...] = x_ref[...].astype(BF16)

    acc = _dot_nt(xb_scr[...], wt_ref[...].astype(BF16))
    acc = acc * jnp.where(pl.program_id(1) < scaled_tiles, scale, 1.0)
    o_ref[...] = acc.astype(o_ref.dtype)


def _idx_kernel(x_ref, w_ref, g_ref, b_ref, o_ref):
    p = _dot(x_ref[...].astype(BF16), w_ref[...])
    lane = lax.broadcasted_iota(jnp.int32, p.shape, 1)
    is_k = lane < IDX_HEAD_DIM
    mu = jnp.sum(jnp.where(is_k, p, 0.0), axis=-1, keepdims=True) * (1.0 / IDX_HEAD_DIM)
    d = jnp.where(is_k, p - mu, 0.0)
    var = jnp.sum(d * d, axis=-1, keepdims=True) * (1.0 / IDX_HEAD_DIM)
    kn = d * lax.rsqrt(var + LN_EPS) * g_ref[...] + b_ref[...]
    w_scale = (N_IDX_HEADS ** -0.5) * (IDX_HEAD_DIM ** -0.5)
    o_ref[...] = jnp.where(is_k, kn, p * w_scale)


def _proj_all(xb, w_in_t, tile_src, *, tm, scaled_tiles):
    m = xb.shape[0]
    n_tiles = len(tile_src)
    return pl.pallas_call(
        functools.partial(_proj_all_kernel, scaled_tiles=scaled_tiles, scale=ATT_Q_SCALE),
        out_shape=jax.ShapeDtypeStruct((m, n_tiles * PROJ_TN), BF16),
        grid_spec=pltpu.PrefetchScalarGridSpec(
            num_scalar_prefetch=1,
            grid=(m // tm, n_tiles),
            in_specs=[pl.BlockSpec((tm, D_MODEL), lambda i, j, src: (i, 0)),
                      pl.BlockSpec((pl.Element(PROJ_TN), pl.Element(D_MODEL)),
                                   lambda i, j, src: (src[j] * PROJ_SRC_ALIGN, 0))],
            out_specs=pl.BlockSpec((tm, PROJ_TN), lambda i, j, src: (i, j)),
            scratch_shapes=[pltpu.VMEM((tm, D_MODEL), BF16)],
        ),
        compiler_params=pltpu.CompilerParams(
            dimension_semantics=("parallel", "arbitrary"), vmem_limit_bytes=VMEM_LIMIT),
        name="proj_all",
    )(jnp.asarray([s // PROJ_SRC_ALIGN for s in tile_src], jnp.int32), xb, w_in_t)


def _proj_idx(xb, w, g, b, *, tm):
    m = xb.shape[0]
    return pl.pallas_call(
        _idx_kernel,
        out_shape=jax.ShapeDtypeStruct((m, LANES), F32),
        grid=(m // tm,),
        in_specs=[pl.BlockSpec((tm, D_MODEL), lambda i: (i, 0)),
                  pl.BlockSpec((D_MODEL, LANES), lambda i: (0, 0)),
                  pl.BlockSpec((1, LANES), lambda i: (0, 0)),
                  pl.BlockSpec((1, LANES), lambda i: (0, 0))],
        out_specs=pl.BlockSpec((tm, LANES), lambda i: (i, 0)),
        compiler_params=pltpu.CompilerParams(
            dimension_semantics=("parallel",), vmem_limit_bytes=VMEM_LIMIT),
        name="proj_idx",
    )(xb, w, g, b)


ATT_ROW_TILE = 32
ATT_BOUND_SLACK = 45.0
UNCHECKED_BITS = 16


def _key_to_f32(key):
    return pltpu.bitcast(key ^ ((key >> 31) & 0x7FFFFFFF), F32)


def _attn_kernel(aq_ref, iq_ref, iwt_ref, iklo_ref, ikhi_ref, k_ref, v_ref, km_ref, vm_ref,
                 o_ref,
                 score_scr, mb_scr, kaug_scr, vaug_scr, qaug_scr, s0_scr, s1_scr, p_scr,
                 macc_scr, m_scr, acc_scr, kn_scr, bound_scr, eye_scr, *, tq, kc, seq):
    qi = pl.program_id(1)
    n_kc = ((qi + 1) * tq + kc - 1) // kc
    kf = float(TOPK)

    @pl.when(qi == 0)
    def _():
        r = lax.broadcasted_iota(jnp.int32, (LANES + seq, LANES), 0)
        lane = lax.broadcasted_iota(jnp.int32, (LANES + seq, LANES), 1)
        pos = jnp.where(r < LANES, r, r - LANES + N_META)
        feat = jnp.where(lane < 3, pos >> POS_SHIFT, jnp.where(lane < 6, pos & (POS_SPLIT - 1), 0))
        feat = feat.astype(F32).astype(BF16)
        ones = jnp.ones((LANES + seq, LANES), BF16)
        eye_scr[...] = (lax.broadcasted_iota(jnp.int32, (tq, tq), 0)
                        == lax.broadcasted_iota(jnp.int32, (tq, tq), 1)).astype(BF16)
        for kvh in range(N_KV_HEADS):
            cs = slice(kvh * LANES, (kvh + 1) * LANES)
            kaug_scr[kvh, 0:LANES, 0:LANES] = km_ref[:, cs]
            kaug_scr[kvh, LANES:, 0:LANES] = k_ref[:, cs]
            kaug_scr[kvh, :, LANES:] = feat
            vaug_scr[kvh, 0:LANES, 0:LANES] = vm_ref[:, cs]
            vaug_scr[kvh, LANES:, 0:LANES] = v_ref[:, cs]
            vaug_scr[kvh, :, LANES:] = ones
            kf32 = kaug_scr[kvh, :, 0:LANES].astype(F32)
            k_norm2 = jnp.max(jnp.sum(kf32 * kf32, axis=1, keepdims=True))
            kn_scr[kvh] = jnp.broadcast_to(k_norm2, kn_scr.shape[1:])

    qcol = qi * tq + lax.broadcasted_iota(jnp.int32, (1, tq), 1)
    iwt = iwt_ref[...]

    def score_body(j, carry):
        off = pl.multiple_of(j * kc, kc)
        klo = iklo_ref[pl.ds(off, kc), :]
        khi = ikhi_ref[pl.ds(off, kc), :]
        acc = jnp.zeros((kc, tq), F32)
        for p in range(N_IDX_HEADS // 2):
            q2 = iq_ref[:, p * LANES:(p + 1) * LANES]
            acc = acc + jnp.maximum(_dot_nt(klo, q2), 0.0) * iwt[2 * p:2 * p + 1, :]
            acc = acc + jnp.maximum(_dot_nt(khi, q2), 0.0) * iwt[2 * p + 1:2 * p + 2, :]
        krow = off + lax.broadcasted_iota(jnp.int32, (kc, tq), 0)
        score_scr[pl.ds(off, kc), :] = jnp.where(krow <= qcol, acc, -jnp.inf)
        return carry

    lax.fori_loop(0, n_kc, score_body, 0)

    n_acc = 8
    acc_rows = n_acc * SUBLANES

    def count_ge(cand):
        def body(j, acc):
            off = pl.multiple_of(j * kc, kc)
            w = jnp.where(score_scr[pl.ds(off, kc), :] >= cand, 1.0, 0.0)
            return acc + jnp.sum(w.reshape(kc // acc_rows, acc_rows, tq), axis=0)

        acc = lax.fori_loop(0, n_kc, body, jnp.zeros((acc_rows, tq), F32))
        return jnp.sum(acc, axis=0, keepdims=True)

    c0 = count_ge(jnp.zeros((1, tq), F32))
    ok0 = c0 >= kf
    thr0 = jnp.where(ok0, 0, INT_MIN).astype(jnp.int32)
    cnt0 = jnp.where(ok0, c0, 0.0)

    short_row = qcol + 1 < TOPK

    def unsettled(cnt):
        return jnp.sum(jnp.where((cnt == kf) | short_row, 0.0, 1.0))

    n_bits = 31
    group = 4

    def bit_cond(carry):
        i, _, _, pending = carry
        return jnp.logical_and(i < n_bits, pending > 0.0)

    def resolve_bit(i, thr, cnt):
        shift = jnp.maximum(n_bits - 1 - i, 0)
        cand = thr | jnp.where(i < n_bits, jnp.left_shift(jnp.int32(1), shift), 0)
        c = count_ge(_key_to_f32(cand))
        ok = c >= kf
        return jnp.where(ok, cand, thr), jnp.where(ok, c, cnt)

    def bit_body(carry):
        i, thr, cnt, _ = carry
        for b in range(group):
            thr, cnt = resolve_bit(i + b, thr, cnt)
        return i + group, thr, cnt, unsettled(cnt)

    thr1, cnt1 = lax.fori_loop(0, UNCHECKED_BITS, lambda i, c: resolve_bit(i, *c), (thr0, cnt0))
    _, thr, cnt, _ = lax.while_loop(
        bit_cond, bit_body, (jnp.int32(UNCHECKED_BITS), thr1, cnt1, unsettled(cnt1)))
    t_f = jnp.where(thr == INT_MIN, F32_LOWEST, _key_to_f32(thr))

    @pl.when(jnp.max(cnt) > kf)
    def _():
        def gt_body(j, acc):
            off = pl.multiple_of(j * kc, kc)
            w = jnp.where(score_scr[pl.ds(off, kc), :] > t_f, 1.0, 0.0)
            return acc + jnp.sum(w, axis=0, keepdims=True)

        need = kf - lax.fori_loop(0, n_kc, gt_body, jnp.zeros((1, tq), F32))
        lower = (lax.broadcasted_iota(jnp.int32, (kc, kc), 0)
                 >= lax.broadcasted_iota(jnp.int32, (kc, kc), 1)).astype(BF16)

        def tie_body(j, before):
            off = pl.multiple_of(j * kc, kc)
            sc = score_scr[pl.ds(off, kc), :]
            eq = sc == t_f
            eqf = jnp.where(eq, 1.0, 0.0)
            rank = before + _dot(lower, eqf.astype(BF16))
            score_scr[pl.ds(off, kc), :] = jnp.where(eq & (rank > need), -jnp.inf, sc)
            return before + jnp.sum(eqf, axis=0, keepdims=True)

        lax.fori_loop(0, n_kc, tie_body, jnp.zeros((1, tq), F32))

    eye = eye_scr[...]

    def mask_body(j, last_sel):
        off = pl.multiple_of(j * kc, kc)
        picked = score_scr[pl.ds(off, kc), :] >= t_f
        sel = _dot_nt(eye, jnp.where(picked, 1.0, 0.0).astype(BF16))
        mb_scr[:, pl.ds(off, kc)] = jnp.where(sel > 0.5, 0.0, -jnp.inf)
        krow = (off + lax.broadcasted_iota(jnp.int32, (kc, tq), 0)).astype(F32)
        hit = jnp.where(picked, krow, -1.0)
        return jnp.maximum(last_sel, jnp.max(hit.reshape(kc // SUBLANES, SUBLANES, tq), axis=0))

    last_sel = lax.fori_loop(0, n_kc, mask_body, jnp.full((SUBLANES, tq), -1.0, F32))
    last_pos = jnp.maximum(jnp.max(last_sel, axis=0, keepdims=True) + N_META, N_META - 1.0)

    rt = ATT_ROW_TILE
    lane_q = lax.broadcasted_iota(jnp.int32, (1, LANES), 1)
    meta_mask = jnp.where(lax.broadcasted_iota(jnp.int32, (rt, LANES), 1) < N_META, 0.0, -jnp.inf)

    s_slots = (s0_scr, s1_scr)

    def logits(kvh, krow0, width, slot):
        s_slots[slot][:, 0:width] = _dot_nt(qaug_scr[kvh], kaug_scr[kvh, pl.ds(krow0, width), :])

    def max_tiles(kvh, slot, width, mask_fn):
        s_ref = s_slots[slot]
        for r0 in range(0, tq, rt):
            mb = mask_fn(r0)
            for g in range(ATT_GROUP):
                rows = slice(g * tq + r0, g * tq + r0 + rt)
                mx = s_ref[rows, 0:LANES] + mb[:, 0:LANES]
                for c in range(1, width // LANES):
                    cols = slice(c * LANES, (c + 1) * LANES)
                    mx = jnp.maximum(mx, s_ref[rows, cols] + mb[:, cols])
                macc_scr[rows, :] = jnp.maximum(macc_scr[rows, :], mx)

    def exp_tiles(kvh, slot, width, mask_fn):
        s_ref = s_slots[slot]
        for r0 in range(0, tq, rt):
            mb = mask_fn(r0)
            for g in range(ATT_GROUP):
                rows = slice(g * tq + r0, g * tq + r0 + rt)
                m_row = m_scr[rows, :]
                for c in range(width // LANES):
                    cols = slice(c * LANES, (c + 1) * LANES)
                    p_scr[rows, cols] = jnp.exp2(s_ref[rows, cols] + mb[:, cols] - m_row).astype(BF16)

    def add_pv(kvh, krow0, width):
        vblk = vaug_scr[kvh, pl.ds(krow0, width), :]
        half = ATT_GROUP * tq // 2
        for r0 in (0, half):
            acc_scr[kvh, r0:r0 + half, :] += _dot(p_scr[r0:r0 + half, 0:width], vblk)

    def krow(j):
        return pl.multiple_of(LANES + j * kc, LANES)

    def chunk_mask(j):
        off = pl.multiple_of(j * kc, kc)
        return lambda r0: mb_scr[r0:r0 + rt, pl.ds(off, kc)]

    def exp_pv_pass(kvh):
        logits(kvh, 0, LANES, 0)
        exp_tiles(kvh, 0, LANES, lambda r0: meta_mask)
        add_pv(kvh, 0, LANES)

        def body(j, carry):
            logits(kvh, krow(j), kc, 0)
            exp_tiles(kvh, 0, kc, chunk_mask(j))
            add_pv(kvh, krow(j), kc)
            return carry

        lax.fori_loop(0, n_kc, body, 0)

    def max_pass(kvh):
        def step(j, slot, lookahead):
            if lookahead:
                logits(kvh, krow(j + 1), kc, 1 - slot)
            max_tiles(kvh, slot, kc, chunk_mask(j))

        logits(kvh, 0, LANES, 0)
        max_tiles(kvh, 0, LANES, lambda r0: meta_mask)
        logits(kvh, krow(0), kc, 0)
        n_pairs = (n_kc - 1) // 2

        def pair_body(i, carry):
            step(2 * i, 0, True)
            step(2 * i + 1, 1, True)
            return carry

        lax.fori_loop(0, n_pairs, pair_body, 0)
        j0 = 2 * n_pairs
        two_left = n_kc - j0 == 2

        @pl.when(two_left)
        def _():
            step(j0, 0, True)
            step(j0 + 1, 1, False)

        @pl.when(jnp.logical_not(two_left))
        def _():
            step(j0, 0, False)

    for h in range(N_ATT_HEADS):
        kvh, g = divmod(h, ATT_GROUP)
        parts = _bf16_parts(2.0 ** (-8.0 * (h + 1) / N_ATT_HEADS) * LOG2E)
        vals = [POS_SPLIT * p for p in parts] + parts
        qfeat = jnp.zeros((1, LANES), F32)
        for i, val in enumerate(vals):
            qfeat = jnp.where(lane_q == i, val, qfeat)
        qaug_scr[kvh, g * tq:(g + 1) * tq, 0:LANES] = aq_ref[:, h * LANES:(h + 1) * LANES]
        qaug_scr[kvh, g * tq:(g + 1) * tq, LANES:] = jnp.broadcast_to(qfeat, (tq, LANES)).astype(BF16)

    a_pos = jnp.floor(last_pos * (1.0 / POS_SPLIT))
    b_pos = last_pos - a_pos * POS_SPLIT
    to_rows = lambda v: _dot_nt(eye, jnp.broadcast_to(v, (LANES, tq)).astype(BF16))
    last_pos_rows = to_rows(a_pos) * POS_SPLIT + to_rows(b_pos)
    slack = jnp.float32(0.0)
    for kvh in range(N_KV_HEADS):
        qf = qaug_scr[kvh, :, 0:LANES].astype(F32)
        qk_max = jnp.sqrt(jnp.sum(qf * qf, axis=1, keepdims=True) * kn_scr[kvh, 0:1, 0:1])
        slack = jnp.maximum(slack, jnp.max(qk_max))
        for g in range(ATT_GROUP):
            rows = slice(g * tq, (g + 1) * tq)
            c = 2.0 ** (-8.0 * (kvh * ATT_GROUP + g + 1) / N_ATT_HEADS) * LOG2E
            bound_scr[kvh, rows, :] = c * last_pos_rows + qk_max[rows]
    bound_ok = slack <= ATT_BOUND_SLACK

    for kvh in range(N_KV_HEADS):
        @pl.when(bound_ok)
        def _():
            m_scr[...] = bound_scr[kvh]

        @pl.when(jnp.logical_not(bound_ok))
        def _():
            macc_scr[...] = jnp.full_like(macc_scr, -jnp.inf)
            max_pass(kvh)
            m_scr[...] = jnp.broadcast_to(jnp.max(macc_scr[...], axis=1, keepdims=True), m_scr.shape)

        acc_scr[kvh] = jnp.zeros(acc_scr.shape[1:], F32)
        exp_pv_pass(kvh)

    for h in range(N_ATT_HEADS):
        kvh, g = divmod(h, ATT_GROUP)
        acc = acc_scr[kvh, g * tq:(g + 1) * tq, :]
        o_ref[:, h * LANES:(h + 1) * LANES] = (acc[:, 0:LANES] / acc[:, LANES:LANES + 1]).astype(o_ref.dtype)


def _attention(p_att, iw_t, ik_lo, ik_hi, km, vm, *, batch, seq, tq, kc):
    m = batch * seq
    nq = seq // tq
    qcol, icol, kcol = PROJ_COL["aq"] // ATT_Q_W, PROJ_COL["iq"] // IDX_Q_W, PROJ_COL["kv"] // ATT_KV_W
    rows4 = ATT_GROUP * tq
    return pl.pallas_call(
        functools.partial(_attn_kernel, tq=tq, kc=kc, seq=seq),
        out_shape=jax.ShapeDtypeStruct((m, ATT_Q_W), BF16),
        grid=(batch, nq),
        in_specs=[
            pl.BlockSpec((tq, ATT_Q_W), lambda b, q: (b * nq + q, qcol)),
            pl.BlockSpec((tq, IDX_Q_W), lambda b, q: (b * nq + q, icol)),
            pl.BlockSpec((N_IDX_HEADS, tq), lambda b, q: (0, b * nq + q)),
            pl.BlockSpec((seq, LANES), lambda b, q: (b, 0)),
            pl.BlockSpec((seq, LANES), lambda b, q: (b, 0)),
            pl.BlockSpec((seq, ATT_KV_W), lambda b, q: (b, kcol)),
            pl.BlockSpec((seq, ATT_KV_W), lambda b, q: (b, kcol + 1)),
            pl.BlockSpec((LANES, ATT_KV_W), lambda b, q: (0, 0)),
            pl.BlockSpec((LANES, ATT_KV_W), lambda b, q: (0, 0)),
        ],
        out_specs=pl.BlockSpec((tq, ATT_Q_W), lambda b, q: (b * nq + q, 0)),
        scratch_shapes=[
            pltpu.VMEM((seq, tq), F32),
            pltpu.VMEM((tq, seq), F32),
            pltpu.VMEM((N_KV_HEADS, LANES + seq, 2 * LANES), BF16),
            pltpu.VMEM((N_KV_HEADS, LANES + seq, 2 * LANES), BF16),
            pltpu.VMEM((N_KV_HEADS, rows4, 2 * LANES), BF16),
            pltpu.VMEM((rows4, kc), F32),
            pltpu.VMEM((rows4, kc), F32),
            pltpu.VMEM((rows4, kc), BF16),
            pltpu.VMEM((rows4, LANES), F32),
            pltpu.VMEM((rows4, LANES), F32),
            pltpu.VMEM((N_KV_HEADS, rows4, 2 * LANES), F32),
            pltpu.VMEM((N_KV_HEADS, SUBLANES, LANES), F32),
            pltpu.VMEM((N_KV_HEADS, rows4, LANES), F32),
            pltpu.VMEM((tq, tq), BF16),
        ],
        compiler_params=pltpu.CompilerParams(
            dimension_semantics=("parallel", "arbitrary"), vmem_limit_bytes=VMEM_LIMIT),
        name="dsa_attention",
    )(p_att, p_att, iw_t, ik_lo, ik_hi, p_att, p_att, km, vm)


def _split3(x):
    hi = x.astype(BF16)
    r = x - hi.astype(F32)
    mid = r.astype(BF16)
    lo = (r - mid.astype(F32)).astype(BF16)
    return hi, mid, lo


HG_SAFE_EXPONENT = 80.0


def _hgrn_kernel(hf_ref, hr_q_ref, hr_i_ref, hr_g_ref, lb_ref, ng_ref, s0_ref,
                 y_ref, sT_out_ref, sT_scr, *, chunk, sub, heads, n_inner):
    c_idx = pl.program_id(2)

    @pl.when(c_idx == 0)
    def _():
        sT_scr[...] = s0_ref[...]

    n_sub = chunk // sub
    causal = (lax.broadcasted_iota(jnp.int32, (chunk, chunk), 0)
              >= lax.broadcasted_iota(jnp.int32, (chunk, chunk), 1))
    tri = causal.astype(BF16)
    t_iota = lax.broadcasted_iota(jnp.int32, (sub, LANES), 0)

    def head_inputs(r0, g):
        rows, cs = pl.ds(r0, chunk), slice(g * HG_DIM, (g + 1) * HG_DIM)
        lb = lb_ref[:, cs]
        q = _silu(hr_q_ref[rows, cs].astype(F32))
        fg = lb + (1.0 - lb) * jax.nn.sigmoid(hf_ref[rows, cs].astype(F32))
        v = hr_i_ref[rows, cs].astype(F32)
        l_hi, l_mid, l_lo = _split3(jnp.log(fg))
        b = _dot(tri, l_hi) + _dot(tri, l_mid) + _dot(tri, l_lo)
        return q, 1.0 - fg, v, b

    def carry_state(g, q, kk, vb, b):
        sT = sT_scr[g]
        b_last = b[chunk - 1:chunk, :]
        qe = (q * jnp.exp(b)).astype(BF16)
        o_state = _dot_nt(qe, sT.astype(BF16))
        khat = (kk * jnp.exp(b_last - b)).astype(BF16)
        sT_scr[g] = sT * jnp.exp(b_last) + _dot_tn(vb, khat)
        return qe, o_state

    def finish(r0, g, o):
        rows, cs = pl.ds(r0, chunk), slice(g * HG_DIM, (g + 1) * HG_DIM)
        o = o * lax.rsqrt(jnp.mean(o * o, axis=-1, keepdims=True) + RMS_EPS) * ng_ref[:, cs]
        o = o * _silu(hr_g_ref[rows, cs].astype(F32))
        y_ref[rows, cs] = o.astype(y_ref.dtype)

    def factored_chunk(c, carry):
        rows = pl.ds(pl.multiple_of(c * chunk, chunk), chunk)
        hs = [slice(g * HG_DIM, (g + 1) * HG_DIM) for g in range(heads)]
        lb = lb_ref[...]
        q = _silu(hr_q_ref[rows, :].astype(F32))
        fg = lb + (1.0 - lb) * jax.nn.sigmoid(hf_ref[rows, :].astype(F32))
        kk = 1.0 - fg
        vb = hr_i_ref[rows, :]
        l_hi, l_mid, l_lo = _split3(jnp.log(fg))
        b = _dot(tri, l_hi) + _dot(tri, l_mid) + _dot(tri, l_lo)
        b_last = b[chunk - 1:chunk, :]
        qe = (q * jnp.exp(b)).astype(BF16)
        ke = (kk * jnp.exp(-b)).astype(BF16)
        khat = (kk * jnp.exp(b_last - b)).astype(BF16)
        keep = jnp.exp(b_last)
        s_old = [sT_scr[g] for g in range(heads)]
        a = [_dot_nt(qe[:, cs], ke[:, cs]) for cs in hs]
        o_state = [_dot_nt(qe[:, cs], s_old[g].astype(BF16)) for g, cs in enumerate(hs)]
        s_add = [_dot_tn(vb[:, cs], khat[:, cs]) for cs in hs]
        a = [jnp.where(causal, x, 0.0).astype(BF16) for x in a]
        o = [o_state[g] + _dot(a[g], vb[:, cs]) for g, cs in enumerate(hs)]
        for g, cs in enumerate(hs):
            sT_scr[g] = s_old[g] * keep[:, cs] + s_add[g]
        o = [x * lax.rsqrt(jnp.mean(x * x, axis=-1, keepdims=True) + RMS_EPS) for x in o]
        o = jnp.concatenate(o, axis=1) * ng_ref[...] * _silu(hr_g_ref[rows, :].astype(F32))
        y_ref[rows, :] = o.astype(y_ref.dtype)
        return carry

    def guarded_chunk(c, carry):
        r0 = pl.multiple_of(c * chunk, chunk)
        for g in range(heads):
            q, kk, v, b = head_inputs(r0, g)
            vb = v.astype(BF16)
            _, o_state = carry_state(g, q, kk, vb, b)
            rows = []
            for i in range(n_sub):
                s0, s1 = i * sub, (i + 1) * sub
                bs, qs, ks, vs = b[s0:s1], q[s0:s1], kk[s0:s1], v[s0:s1]
                o_i = o_state[s0:s1]
                if i > 0:
                    r_i = b[s0 - 1:s0, :]
                    qt = (qs * jnp.exp(bs - r_i)).astype(BF16)
                    kt = (kk[:s0] * jnp.exp(r_i - b[:s0])).astype(BF16)
                    a_off = _dot_nt(qt, kt)
                    o_i = o_i + _dot(a_off.astype(BF16), vb[:s0])
                for s in range(sub):
                    e = jnp.exp(jnp.where(t_iota >= s, bs - bs[s:s + 1], -jnp.inf))
                    a_col = jnp.sum(qs * e * ks[s:s + 1], axis=1, keepdims=True)
                    o_i = o_i + a_col * vs[s:s + 1]
                rows.append(o_i)
            finish(r0, g, jnp.concatenate(rows, axis=0) if n_sub > 1 else rows[0])
        return carry

    factoring_safe = jnp.max(-jnp.log(lb_ref[...])) * chunk <= HG_SAFE_EXPONENT

    @pl.when(factoring_safe)
    def _():
        for c in range(n_inner):
            factored_chunk(c, 0)

    @pl.when(jnp.logical_not(factoring_safe))
    def _():
        lax.fori_loop(0, n_inner, guarded_chunk, 0)

    @pl.when(c_idx == pl.num_programs(2) - 1)
    def _():
        sT_out_ref[0] = sT_scr[...]


def _hgrn(p, cols, lb, ng, s0, *, batch, seq, chunk, sub, heads, n_inner):
    m = batch * seq
    rows = chunk * n_inner
    nc = seq // rows
    hw = heads * HG_DIM
    nhg = HG_HEADS // heads
    cf, cq, ci, cg = (cols[k] // hw for k in ("hf", "hq", "hi", "hg"))
    return pl.pallas_call(
        functools.partial(_hgrn_kernel, chunk=chunk, sub=sub, heads=heads, n_inner=n_inner),
        out_shape=(jax.ShapeDtypeStruct((m, HG_W), BF16),
                   jax.ShapeDtypeStruct((batch, HG_HEADS, HG_DIM, HG_DIM), F32)),
        grid=(batch, nhg, nc),
        in_specs=[
            pl.BlockSpec((rows, hw), lambda b, h, c: (b * nc + c, cf + h)),
            pl.BlockSpec((rows, hw), lambda b, h, c: (b * nc + c, cq + h)),
            pl.BlockSpec((rows, hw), lambda b, h, c: (b * nc + c, ci + h)),
            pl.BlockSpec((rows, hw), lambda b, h, c: (b * nc + c, cg + h)),
            pl.BlockSpec((1, hw), lambda b, h, c: (0, h)),
            pl.BlockSpec((1, hw), lambda b, h, c: (0, h)),
            pl.BlockSpec((heads, HG_DIM, HG_DIM), lambda b, h, c: (h, 0, 0)),
        ],
        out_specs=(
            pl.BlockSpec((rows, hw), lambda b, h, c: (b * nc + c, h)),
            pl.BlockSpec((1, heads, HG_DIM, HG_DIM), lambda b, h, c: (b, h, 0, 0)),
        ),
        scratch_shapes=[pltpu.VMEM((heads, HG_DIM, HG_DIM), F32)],
        compiler_params=pltpu.CompilerParams(
            dimension_semantics=("parallel", "parallel", "arbitrary"), vmem_limit_bytes=VMEM_LIMIT),
        name="hgrn2",
    )(p, p, p, p, lb, ng, s0)


def _merge_kernel(ya_ref, yh_ref, gts_a_ref, gts_h_ref, h_ref, wa_ref, wh_ref, wo_ref, g_ref, b_ref,
                  o_ref, acc_scr):
    j = pl.program_id(1)

    @pl.when(j == 0)
    def _():
        acc_scr[...] = jnp.zeros_like(acc_scr)

    ga = jax.nn.sigmoid(gts_a_ref[...].astype(F32))
    gh = jax.nn.sigmoid(gts_h_ref[...].astype(F32))
    merged = ga * _dot(ya_ref[...], wa_ref[...]) + gh * _dot(yh_ref[...], wh_ref[...])
    acc_scr[...] += _dot(merged.astype(BF16), wo_ref[...])

    @pl.when(j == pl.num_programs(1) - 1)
    def _():
        o_ref[...] = _layer_norm(ALPHA * h_ref[...] + acc_scr[...], g_ref[...], b_ref[...])


def _merge(ya, yh, gates, h1, wa, wh, wo, g, b, *, tm, tn):
    m = ya.shape[0]
    nj = D_MODEL // tn
    ca, cb = PROJ_COL["ga"] // tn, PROJ_COL["gb"] // tn
    return pl.pallas_call(
        _merge_kernel,
        out_shape=jax.ShapeDtypeStruct((m, D_MODEL), F32),
        grid=(m // tm, nj),
        in_specs=[
            pl.BlockSpec((tm, ATT_Q_W), lambda i, j: (i, 0)),
            pl.BlockSpec((tm, HG_W), lambda i, j: (i, 0)),
            pl.BlockSpec((tm, tn), lambda i, j: (i, ca + j)),
            pl.BlockSpec((tm, tn), lambda i, j: (i, cb + j)),
            pl.BlockSpec((tm, D_MODEL), lambda i, j: (i, 0)),
            pl.BlockSpec((ATT_Q_W, tn), lambda i, j: (0, j)),
            pl.BlockSpec((HG_W, tn), lambda i, j: (0, j)),
            pl.BlockSpec((tn, D_MODEL), lambda i, j: (j, 0)),
            pl.BlockSpec((1, D_MODEL), lambda i, j: (0, 0)),
            pl.BlockSpec((1, D_MODEL), lambda i, j: (0, 0)),
        ],
        out_specs=pl.BlockSpec((tm, D_MODEL), lambda i, j: (i, 0)),
        scratch_shapes=[pltpu.VMEM((tm, D_MODEL), F32)],
        compiler_params=pltpu.CompilerParams(
            dimension_semantics=("parallel", "arbitrary"), vmem_limit_bytes=VMEM_LIMIT),
        name="merge_ln",
    )(ya, yh, gates, gates, h1, wa, wh, wo, g, b)


def kernel(x, meta, ffn1_w_gate, ffn1_w_up, ffn1_w_down, ln1_g, ln1_b, w_in, idx_k_norm_g, idx_k_norm_b,
           hg_lb_logits, hg_norm_g, w_branch_att, w_branch_hg, w_out, ln2_g, ln2_b,
           ffn2_w_gate, ffn2_w_up, ffn2_w_down, ln3_g, ln3_b):
    batch, seq, _ = x.shape
    m = batch * seq
    xr = x.reshape(m, D_MODEL)
    bf = lambda w: w.astype(BF16)
    row = lambda v: v.reshape(1, -1)

    w_in_t = w_in[0].T
    w_idx = bf(jnp.pad(w_in_t[IN_OFFS[4]:IN_OFFS[6]].T, ((0, 0), (0, LANES - IDX_HEAD_DIM - N_IDX_HEADS))))
    idx_g = jnp.pad(idx_k_norm_g[0], (0, LANES - IDX_HEAD_DIM)).reshape(1, LANES)
    idx_b = jnp.pad(idx_k_norm_b[0], (0, LANES - IDX_HEAD_DIM)).reshape(1, LANES)
    lb = jnp.cumsum(jax.nn.softmax(hg_lb_logits.astype(F32), axis=0), axis=0)[0].reshape(1, HG_W)
    ng = hg_norm_g[0].reshape(1, HG_W)

    f1 = (ffn1_w_gate[0], ffn1_w_up[0], ffn1_w_down[0], row(ln1_g[0]), row(ln1_b[0]))
    f2 = (ffn2_w_gate[0], ffn2_w_up[0], ffn2_w_down[0], row(ln3_g[0]), row(ln3_b[0]))

    hm = _ffn_ln(meta.astype(F32), *f1, tm=N_META, tf=FFN_TF)
    pm = _proj_all(hm, w_in_t, META_TILE_SRC, tm=N_META, scaled_tiles=0)
    kv0 = META_COL["kv"]
    km = jnp.pad(pm[:, kv0:kv0 + ATT_KV_W], ((0, LANES - N_META), (0, 0)))
    vm = jnp.pad(pm[:, kv0 + ATT_KV_W:kv0 + 2 * ATT_KV_W], ((0, LANES - N_META), (0, 0)))
    s_zero = jnp.zeros((HG_HEADS, HG_DIM, HG_DIM), F32)
    meta_cols = {"hf": META_COL["hf"], "hi": META_COL["hi"], "hq": META_COL["hf"], "hg": META_COL["hi"]}
    _, s_meta = _hgrn(pm, meta_cols, lb, ng, s_zero, batch=1, seq=N_META, chunk=N_META, sub=N_META,
                      heads=HG_HEADS, n_inner=1)

    h1 = _ffn_ln(xr, *f1, tm=FFN_TM, tf=FFN_TF)
    p = _proj_all(h1, w_in_t, PROJ_TILE_SRC, tm=1024, scaled_tiles=ATT_Q_W // PROJ_TN)
    p_idx = _proj_idx(h1, w_idx, idx_g, idx_b, tm=1024)

    ikn = p_idx[:, :IDX_HEAD_DIM].astype(BF16)
    ik_lo = jnp.pad(ikn, ((0, 0), (0, LANES - IDX_HEAD_DIM)))
    ik_hi = jnp.pad(ikn, ((0, 0), (LANES - IDX_HEAD_DIM, 0)))
    iw_t = p_idx[:, IDX_HEAD_DIM:IDX_HEAD_DIM + N_IDX_HEADS].T
    y_att = _attention(p, iw_t, ik_lo, ik_hi, km, vm, batch=batch, seq=seq, tq=256, kc=512)
    y_hg, _ = _hgrn(p, PROJ_COL, lb, ng, s_meta[0], batch=batch, seq=seq, chunk=64, sub=16,
                    heads=HG_HEADS, n_inner=8)

    h2 = _merge(y_att, y_hg, p, h1, bf(w_branch_att[0]), bf(w_branch_hg[0]), bf(w_out[0]),
                row(ln2_g[0]), row(ln2_b[0]), tm=512, tn=1024)
    out = _ffn_ln(h2, *f2, tm=FFN_TM, tf=FFN_TF)
    return out.reshape(batch, seq, D_MODEL)
```

```python
import functools
import math

import jax
import jax.numpy as jnp
import numpy as np
from jax import lax
from jax.experimental import pallas as pl
from jax.experimental.pallas import tpu as pltpu

D_MODEL = 2048
N_META = 16
N_ATT_HEADS = 8
N_KV_HEADS = 2
ATT_GROUP = N_ATT_HEADS // N_KV_HEADS
ATT_HEAD_DIM = 128
N_IDX_HEADS = 16
IDX_HEAD_DIM = 64
TOPK = 256
HG_HEADS = 8
HG_DIM = 128
D_FF = 5632
LN_EPS = 1e-5
RMS_EPS = 1e-6
ALPHA = 2.0 ** 0.25

ATT_Q_W = N_ATT_HEADS * ATT_HEAD_DIM
ATT_KV_W = N_KV_HEADS * ATT_HEAD_DIM
IDX_Q_W = N_IDX_HEADS * IDX_HEAD_DIM
HG_W = HG_HEADS * HG_DIM

LANES = 128
SUBLANES = 8
VMEM_LIMIT = 56 * 1024 * 1024

F32 = jnp.float32
BF16 = jnp.bfloat16
INT_MIN = -2 ** 31
F32_LOWEST = float(np.finfo(np.float32).min)
LOG2E = math.log2(math.e)
ATT_Q_SCALE = ATT_HEAD_DIM ** -0.5 * LOG2E
POS_SHIFT = 6
POS_SPLIT = 1 << POS_SHIFT


def _dot(a, b):
    return jnp.dot(a, b, preferred_element_type=F32)


def _dot_nt(a, b):
    return lax.dot_general(a, b, (((1,), (1,)), ((), ())), preferred_element_type=F32)


def _dot_tn(a, b):
    return lax.dot_general(a, b, (((0,), (0,)), ((), ())), preferred_element_type=F32)


def _layer_norm(y, g, b):
    mu = jnp.mean(y, axis=-1, keepdims=True)
    d = y - mu
    var = jnp.mean(d * d, axis=-1, keepdims=True)
    return d * lax.rsqrt(var + LN_EPS) * g + b


def _silu(x):
    return x * jax.nn.sigmoid(x)


def _bf16_parts(c, n=3):
    parts = []
    for _ in range(n):
        p = float(np.asarray(c, np.float32).astype(jnp.bfloat16).astype(np.float32))
        parts.append(p)
        c = c - p
    return parts


FFN_TM = 1024
FFN_TF = 256
LN_ROWS = 128


def _ffn_ln_kernel(x_ref, wg_ref, wu_ref, wd_ref, g_ref, b_ref, o_ref, xb_scr):
    j = pl.program_id(1)

    @pl.when(j == 0)
    def _():
        xb_scr[...] = x_ref[...].astype(BF16)
        o_ref[...] = jnp.zeros_like(o_ref)

    xb = xb_scr[...]
    gate = _dot(xb, wg_ref[...].astype(BF16))
    up = _dot(xb, wu_ref[...].astype(BF16))
    a = (_silu(gate) * up).astype(BF16)
    o_ref[...] += _dot(a, wd_ref[...].astype(BF16))

    @pl.when(j == pl.num_programs(1) - 1)
    def _():
        n_rows = min(LN_ROWS, o_ref.shape[0])

        def ln_rows(r, carry):
            rows = pl.ds(pl.multiple_of(r * n_rows, n_rows), n_rows)
            y = ALPHA * x_ref[rows, :] + 0.5 * o_ref[rows, :]
            o_ref[rows, :] = _layer_norm(y, g_ref[...], b_ref[...])
            return carry

        lax.fori_loop(0, o_ref.shape[0] // n_rows, ln_rows, 0)


def _ffn_ln(x, wg, wu, wd, g, b, *, tm, tf):
    m = x.shape[0]
    grid = (m // tm, D_FF // tf)
    return pl.pallas_call(
        _ffn_ln_kernel,
        out_shape=jax.ShapeDtypeStruct((m, D_MODEL), F32),
        grid=grid,
        in_specs=[
            pl.BlockSpec((tm, D_MODEL), lambda i, j: (i, 0)),
            pl.BlockSpec((D_MODEL, tf), lambda i, j: (0, j)),
            pl.BlockSpec((D_MODEL, tf), lambda i, j: (0, j)),
            pl.BlockSpec((tf, D_MODEL), lambda i, j: (j, 0)),
            pl.BlockSpec((1, D_MODEL), lambda i, j: (0, 0)),
            pl.BlockSpec((1, D_MODEL), lambda i, j: (0, 0)),
        ],
        out_specs=pl.BlockSpec((tm, D_MODEL), lambda i, j: (i, 0)),
        scratch_shapes=[pltpu.VMEM((tm, D_MODEL), BF16)],
        compiler_params=pltpu.CompilerParams(
            dimension_semantics=("parallel", "arbitrary"), vmem_limit_bytes=VMEM_LIMIT),
        name="ffn_ln",
    )(x, wg, wu, wd, g, b)


IN_SPLITS = (ATT_Q_W, ATT_KV_W, ATT_KV_W, IDX_Q_W, IDX_HEAD_DIM, N_IDX_HEADS, HG_W, HG_W, HG_W, HG_W,
             D_MODEL, D_MODEL)
IN_OFFS = tuple(int(v) for v in np.cumsum((0,) + IN_SPLITS))
PROJ_TN = 1024
PROJ_GROUPS = {"aq": (IN_OFFS[0], ATT_Q_W), "iq": (IN_OFFS[3], IDX_Q_W), "hq": (IN_OFFS[6], HG_W),
               "hf": (IN_OFFS[7], HG_W), "hi": (IN_OFFS[8], HG_W), "hg": (IN_OFFS[9], HG_W),
               "ga": (IN_OFFS[10], D_MODEL), "gb": (IN_OFFS[11], D_MODEL), "kv": (IN_OFFS[1], 2 * ATT_KV_W)}
PROJ_SRC_ALIGN = 16


def _proj_layout(names):
    col, src = {}, []
    for name in names:
        start, width = PROJ_GROUPS[name]
        col[name] = len(src) * PROJ_TN
        src += [start + t * PROJ_TN for t in range(-(-width // PROJ_TN))]
    assert all(s % PROJ_SRC_ALIGN == 0 and s + PROJ_TN <= IN_OFFS[-1] for s in src)
    return col, tuple(src)


PROJ_COL, PROJ_TILE_SRC = _proj_layout(("aq", "iq", "hq", "hf", "hi", "hg", "ga", "gb", "kv"))
META_COL, META_TILE_SRC = _proj_layout(("hf", "hi", "kv"))


def _proj_all_kernel(src_ref, x_ref, wt_ref, o_ref, xb_scr, *, scaled_tiles, scale):
    @pl.when(pl.program_id(1) == 0)
    def _():
        xb_scr[...] = x_ref[...].astype(BF16)

    acc = _dot_nt(xb_scr[...], wt_ref[...].astype(BF16))
    acc = acc * jnp.where(pl.program_id(1) < scaled_tiles, scale, 1.0)
    o_ref[...] = acc.astype(o_ref.dtype)


def _idx_kernel(x_ref, w_ref, g_ref, b_ref, o_ref):
    p = _dot(x_ref[...].astype(BF16), w_ref[...])
    lane = lax.broadcasted_iota(jnp.int32, p.shape, 1)
    is_k = lane < IDX_HEAD_DIM
    mu = jnp.sum(jnp.where(is_k, p, 0.0), axis=-1, keepdims=True) * (1.0 / IDX_HEAD_DIM)
    d = jnp.where(is_k, p - mu, 0.0)
    var = jnp.sum(d * d, axis=-1, keepdims=True) * (1.0 / IDX_HEAD_DIM)
    kn = d * lax.rsqrt(var + LN_EPS) * g_ref[...] + b_ref[...]
    w_scale = (N_IDX_HEADS ** -0.5) * (IDX_HEAD_DIM ** -0.5)
    o_ref[...] = jnp.where(is_k, kn, p * w_scale)


def _proj_all(xb, w_in_t, tile_src, *, tm, scaled_tiles):
    m = xb.shape[0]
    n_tiles = len(tile_src)
    return pl.pallas_call(
        functools.partial(_proj_all_kernel, scaled_tiles=scaled_tiles, scale=ATT_Q_SCALE),
        out_shape=jax.ShapeDtypeStruct((m, n_tiles * PROJ_TN), BF16),
        grid_spec=pltpu.PrefetchScalarGridSpec(
            num_scalar_prefetch=1,
            grid=(m // tm, n_tiles),
            in_specs=[pl.BlockSpec((tm, D_MODEL), lambda i, j, src: (i, 0)),
                      pl.BlockSpec((pl.Element(PROJ_TN), pl.Element(D_MODEL)),
                                   lambda i, j, src: (src[j] * PROJ_SRC_ALIGN, 0))],
            out_specs=pl.BlockSpec((tm, PROJ_TN), lambda i, j, src: (i, j)),
            scratch_shapes=[pltpu.VMEM((tm, D_MODEL), BF16)],
        ),
        compiler_params=pltpu.CompilerParams(
            dimension_semantics=("parallel", "arbitrary"), vmem_limit_bytes=VMEM_LIMIT),
        name="proj_all",
    )(jnp.asarray([s // PROJ_SRC_ALIGN for s in tile_src], jnp.int32), xb, w_in_t)


def _proj_idx(xb, w, g, b, *, tm):
    m = xb.shape[0]
    return pl.pallas_call(
        _idx_kernel,
        out_shape=jax.ShapeDtypeStruct((m, LANES), F32),
        grid=(m // tm,),
        in_specs=[pl.BlockSpec((tm, D_MODEL), lambda i: (i, 0)),
                  pl.BlockSpec((D_MODEL, LANES), lambda i: (0, 0)),
                  pl.BlockSpec((1, LANES), lambda i: (0, 0)),
                  pl.BlockSpec((1, LANES), lambda i: (0, 0))],
        out_specs=pl.BlockSpec((tm, LANES), lambda i: (i, 0)),
        compiler_params=pltpu.CompilerParams(
            dimension_semantics=("parallel",), vmem_limit_bytes=VMEM_LIMIT),
        name="proj_idx",
    )(xb, w, g, b)


ATT_ROW_TILE = 32
ATT_BOUND_SLACK = 45.0
UNCHECKED_BITS = 16


def _key_to_f32(key):
    return pltpu.bitcast(key ^ ((key >> 31) & 0x7FFFFFFF), F32)


def _attn_kernel(aq_ref, iq_ref, iwt_ref, iklo_ref, ikhi_ref, k_ref, v_ref, km_ref, vm_ref,
                 o_ref,
                 score_scr, mb_scr, kaug_scr, vaug_scr, qaug_scr, s0_scr, s1_scr, p_scr, p1_scr,
                 macc_scr, m_scr, acc_scr, kn_scr, bound_scr, eye_scr, *, tq, kc, seq):
    qi = pl.program_id(1)
    n_kc = ((qi + 1) * tq + kc - 1) // kc
    kf = float(TOPK)

    @pl.when(qi == 0)
    def _():
        r = lax.broadcasted_iota(jnp.int32, (LANES + seq, LANES), 0)
        lane = lax.broadcasted_iota(jnp.int32, (LANES + seq, LANES), 1)
        pos = jnp.where(r < LANES, r, r - LANES + N_META)
        feat = jnp.where(lane < 3, pos >> POS_SHIFT, jnp.where(lane < 6, pos & (POS_SPLIT - 1), 0))
        feat = feat.astype(F32).astype(BF16)
        ones = jnp.ones((LANES + seq, LANES), BF16)
        eye_scr[...] = (lax.broadcasted_iota(jnp.int32, (tq, tq), 0)
                        == lax.broadcasted_iota(jnp.int32, (tq, tq), 1)).astype(BF16)
        for kvh in range(N_KV_HEADS):
            cs = slice(kvh * LANES, (kvh + 1) * LANES)
            kaug_scr[kvh, 0:LANES, 0:LANES] = km_ref[:, cs]
            kaug_scr[kvh, LANES:, 0:LANES] = k_ref[:, cs]
            kaug_scr[kvh, :, LANES:] = feat
            vaug_scr[kvh, 0:LANES, 0:LANES] = vm_ref[:, cs]
            vaug_scr[kvh, LANES:, 0:LANES] = v_ref[:, cs]
            vaug_scr[kvh, :, LANES:] = ones
            kf32 = kaug_scr[kvh, :, 0:LANES].astype(F32)
            k_norm2 = jnp.max(jnp.sum(kf32 * kf32, axis=1, keepdims=True))
            kn_scr[kvh] = jnp.broadcast_to(k_norm2, kn_scr.shape[1:])

    qcol = qi * tq + lax.broadcasted_iota(jnp.int32, (1, tq), 1)
    iwt = iwt_ref[...]

    def score_body(j, carry):
        off = pl.multiple_of(j * kc, kc)
        klo = iklo_ref[pl.ds(off, kc), :]
        khi = ikhi_ref[pl.ds(off, kc), :]
        acc = jnp.zeros((kc, tq), F32)
        for p in range(N_IDX_HEADS // 2):
            q2 = iq_ref[:, p * LANES:(p + 1) * LANES]
            acc = acc + jnp.maximum(_dot_nt(klo, q2), 0.0) * iwt[2 * p:2 * p + 1, :]
            acc = acc + jnp.maximum(_dot_nt(khi, q2), 0.0) * iwt[2 * p + 1:2 * p + 2, :]
        krow = off + lax.broadcasted_iota(jnp.int32, (kc, tq), 0)
        score_scr[pl.ds(off, kc), :] = jnp.where(krow <= qcol, acc, -jnp.inf)
        return carry

    lax.fori_loop(0, n_kc, score_body, 0)

    n_acc = 8
    acc_rows = n_acc * SUBLANES

    def count_ge(cand):
        def body(j, acc):
            off = pl.multiple_of(j * kc, kc)
            w = jnp.where(score_scr[pl.ds(off, kc), :] >= cand, 1.0, 0.0)
            return acc + jnp.sum(w.reshape(kc // acc_rows, acc_rows, tq), axis=0)

        acc = lax.fori_loop(0, n_kc, body, jnp.zeros((acc_rows, tq), F32))
        return jnp.sum(acc, axis=0, keepdims=True)

    c0 = count_ge(jnp.zeros((1, tq), F32))
    ok0 = c0 >= kf
    thr0 = jnp.where(ok0, 0, INT_MIN).astype(jnp.int32)
    cnt0 = jnp.where(ok0, c0, 0.0)

    short_row = qcol + 1 < TOPK

    def unsettled(cnt):
        return jnp.sum(jnp.where((cnt == kf) | short_row, 0.0, 1.0))

    n_bits = 31
    group = 4

    def bit_cond(carry):
        i, _, _, pending = carry
        return jnp.logical_and(i < n_bits, pending > 0.0)

    def resolve_bit(i, thr, cnt):
        shift = jnp.maximum(n_bits - 1 - i, 0)
        cand = thr | jnp.where(i < n_bits, jnp.left_shift(jnp.int32(1), shift), 0)
        c = count_ge(_key_to_f32(cand))
        ok = c >= kf
        return jnp.where(ok, cand, thr), jnp.where(ok, c, cnt)

    def bit_body(carry):
        i, thr, cnt, _ = carry
        for b in range(group):
            thr, cnt = resolve_bit(i + b, thr, cnt)
        return i + group, thr, cnt, unsettled(cnt)

    thr1, cnt1 = lax.fori_loop(0, UNCHECKED_BITS, lambda i, c: resolve_bit(i, *c), (thr0, cnt0))
    _, thr, cnt, _ = lax.while_loop(
        bit_cond, bit_body, (jnp.int32(UNCHECKED_BITS), thr1, cnt1, unsettled(cnt1)))
    t_f = jnp.where(thr == INT_MIN, F32_LOWEST, _key_to_f32(thr))

    @pl.when(jnp.max(cnt) > kf)
    def _():
        def gt_body(j, acc):
            off = pl.multiple_of(j * kc, kc)
            w = jnp.where(score_scr[pl.ds(off, kc), :] > t_f, 1.0, 0.0)
            return acc + jnp.sum(w, axis=0, keepdims=True)

        need = kf - lax.fori_loop(0, n_kc, gt_body, jnp.zeros((1, tq), F32))
        lower = (lax.broadcasted_iota(jnp.int32, (kc, kc), 0)
                 >= lax.broadcasted_iota(jnp.int32, (kc, kc), 1)).astype(BF16)

        def tie_body(j, before):
            off = pl.multiple_of(j * kc, kc)
            sc = score_scr[pl.ds(off, kc), :]
            eq = sc == t_f
            eqf = jnp.where(eq, 1.0, 0.0)
            rank = before + _dot(lower, eqf.astype(BF16))
            score_scr[pl.ds(off, kc), :] = jnp.where(eq & (rank > need), -jnp.inf, sc)
            return before + jnp.sum(eqf, axis=0, keepdims=True)

        lax.fori_loop(0, n_kc, tie_body, jnp.zeros((1, tq), F32))

    eye = eye_scr[...]

    def mask_body(j, last_sel):
        off = pl.multiple_of(j * kc, kc)
        picked = score_scr[pl.ds(off, kc), :] >= t_f
        sel = _dot_nt(eye, jnp.where(picked, 1.0, 0.0).astype(BF16))
        mb_scr[:, pl.ds(off, kc)] = jnp.where(sel > 0.5, 0.0, -jnp.inf)
        krow = (off + lax.broadcasted_iota(jnp.int32, (kc, tq), 0)).astype(F32)
        hit = jnp.where(picked, krow, -1.0)
        return jnp.maximum(last_sel, jnp.max(hit.reshape(kc // SUBLANES, SUBLANES, tq), axis=0))

    last_sel = lax.fori_loop(0, n_kc, mask_body, jnp.full((SUBLANES, tq), -1.0, F32))
    last_pos = jnp.maximum(jnp.max(last_sel, axis=0, keepdims=True) + N_META, N_META - 1.0)

    rt = ATT_ROW_TILE
    lane_q = lax.broadcasted_iota(jnp.int32, (1, LANES), 1)
    meta_mask = jnp.where(lax.broadcasted_iota(jnp.int32, (rt, LANES), 1) < N_META, 0.0, -jnp.inf)

    s_slots = (s0_scr, s1_scr)

    def logits(kvh, krow0, width, slot):
        s_slots[slot][:, 0:width] = _dot_nt(qaug_scr[kvh], kaug_scr[kvh, pl.ds(krow0, width), :])

    def max_tiles(kvh, slot, width, mask_fn):
        s_ref = s_slots[slot]
        for r0 in range(0, tq, rt):
            mb = mask_fn(r0)
            for g in range(ATT_GROUP):
                rows = slice(g * tq + r0, g * tq + r0 + rt)
                mx = s_ref[rows, 0:LANES] + mb[:, 0:LANES]
                for c in range(1, width // LANES):
                    cols = slice(c * LANES, (c + 1) * LANES)
                    mx = jnp.maximum(mx, s_ref[rows, cols] + mb[:, cols])
                macc_scr[rows, :] = jnp.maximum(macc_scr[rows, :], mx)

    p_slots = (p_scr, p1_scr)

    def exp_tiles(kvh, slot, width, mask_fn):
        s_ref, p_ref = s_slots[slot], p_slots[slot]
        for r0 in range(0, tq, rt):
            mb = mask_fn(r0)
            for g in range(ATT_GROUP):
                rows = slice(g * tq + r0, g * tq + r0 + rt)
                m_row = m_scr[rows, :]
                for c in range(width // LANES):
                    cols = slice(c * LANES, (c + 1) * LANES)
                    p_ref[rows, cols] = jnp.exp2(s_ref[rows, cols] + mb[:, cols] - m_row).astype(BF16)

    def add_pv(kvh, slot, krow0, width):
        vblk = vaug_scr[kvh, pl.ds(krow0, width), :]
        half = ATT_GROUP * tq // 2
        for r0 in (0, half):
            acc_scr[kvh, r0:r0 + half, :] += _dot(p_slots[slot][r0:r0 + half, 0:width], vblk)

    def krow(j):
        return pl.multiple_of(LANES + j * kc, LANES)

    def chunk_mask(j):
        off = pl.multiple_of(j * kc, kc)
        return lambda r0: mb_scr[r0:r0 + rt, pl.ds(off, kc)]

    def exp_pv_pass(kvh):
        def chunk(j, slot):
            logits(kvh, krow(j), kc, slot)
            exp_tiles(kvh, slot, kc, chunk_mask(j))
            add_pv(kvh, slot, krow(j), kc)

        logits(kvh, 0, LANES, 0)
        exp_tiles(kvh, 0, LANES, lambda r0: meta_mask)
        add_pv(kvh, 0, 0, LANES)

        def pair_body(i, carry):
            chunk(2 * i, 0)
            chunk(2 * i + 1, 1)
            return carry

        lax.fori_loop(0, n_kc // 2, pair_body, 0)

        @pl.when(n_kc % 2 == 1)
        def _():
            chunk(n_kc - 1, 0)

    def max_pass(kvh):
        def step(j, slot, lookahead):
            if lookahead:
                logits(kvh, krow(j + 1), kc, 1 - slot)
            max_tiles(kvh, slot, kc, chunk_mask(j))

        logits(kvh, 0, LANES, 0)
        max_tiles(kvh, 0, LANES, lambda r0: meta_mask)
        logits(kvh, krow(0), kc, 0)
        n_pairs = (n_kc - 1) // 2

        def pair_body(i, carry):
            step(2 * i, 0, True)
            step(2 * i + 1, 1, True)
            return carry

        lax.fori_loop(0, n_pairs, pair_body, 0)
        j0 = 2 * n_pairs
        two_left = n_kc - j0 == 2

        @pl.when(two_left)
        def _():
            step(j0, 0, True)
            step(j0 + 1, 1, False)

        @pl.when(jnp.logical_not(two_left))
        def _():
            step(j0, 0, False)

    for h in range(N_ATT_HEADS):
        kvh, g = divmod(h, ATT_GROUP)
        parts = _bf16_parts(2.0 ** (-8.0 * (h + 1) / N_ATT_HEADS) * LOG2E)
        vals = [POS_SPLIT * p for p in parts] + parts
        qfeat = jnp.zeros((1, LANES), F32)
        for i, val in enumerate(vals):
            qfeat = jnp.where(lane_q == i, val, qfeat)
        qaug_scr[kvh, g * tq:(g + 1) * tq, 0:LANES] = aq_ref[:, h * LANES:(h + 1) * LANES]
        qaug_scr[kvh, g * tq:(g + 1) * tq, LANES:] = jnp.broadcast_to(qfeat, (tq, LANES)).astype(BF16)

    a_pos = jnp.floor(last_pos * (1.0 / POS_SPLIT))
    b_pos = last_pos - a_pos * POS_SPLIT
    to_rows = lambda v: _dot_nt(eye, jnp.broadcast_to(v, (LANES, tq)).astype(BF16))
    last_pos_rows = to_rows(a_pos) * POS_SPLIT + to_rows(b_pos)
    slack = jnp.float32(0.0)
    for kvh in range(N_KV_HEADS):
        qf = qaug_scr[kvh, :, 0:LANES].astype(F32)
        qk_max = jnp.sqrt(jnp.sum(qf * qf, axis=1, keepdims=True) * kn_scr[kvh, 0:1, 0:1])
        slack = jnp.maximum(slack, jnp.max(qk_max))
        for g in range(ATT_GROUP):
            rows = slice(g * tq, (g + 1) * tq)
            c = 2.0 ** (-8.0 * (kvh * ATT_GROUP + g + 1) / N_ATT_HEADS) * LOG2E
            bound_scr[kvh, rows, :] = c * last_pos_rows + qk_max[rows]
    bound_ok = slack <= ATT_BOUND_SLACK

    for kvh in range(N_KV_HEADS):
        @pl.when(bound_ok)
        def _():
            m_scr[...] = bound_scr[kvh]

        @pl.when(jnp.logical_not(bound_ok))
        def _():
            macc_scr[...] = jnp.full_like(macc_scr, -jnp.inf)
            max_pass(kvh)
            m_scr[...] = jnp.broadcast_to(jnp.max(macc_scr[...], axis=1, keepdims=True), m_scr.shape)

        acc_scr[kvh] = jnp.zeros(acc_scr.shape[1:], F32)
        exp_pv_pass(kvh)

    for h in range(N_ATT_HEADS):
        kvh, g = divmod(h, ATT_GROUP)
        acc = acc_scr[kvh, g * tq:(g + 1) * tq, :]
        o_ref[:, h * LANES:(h + 1) * LANES] = (acc[:, 0:LANES] / acc[:, LANES:LANES + 1]).astype(o_ref.dtype)


def _attention(p_att, iw_t, ik_lo, ik_hi, km, vm, *, batch, seq, tq, kc):
    m = batch * seq
    nq = seq // tq
    qcol, icol, kcol = PROJ_COL["aq"] // ATT_Q_W, PROJ_COL["iq"] // IDX_Q_W, PROJ_COL["kv"] // ATT_KV_W
    rows4 = ATT_GROUP * tq
    return pl.pallas_call(
        functools.partial(_attn_kernel, tq=tq, kc=kc, seq=seq),
        out_shape=jax.ShapeDtypeStruct((m, ATT_Q_W), BF16),
        grid=(batch, nq),
        in_specs=[
            pl.BlockSpec((tq, ATT_Q_W), lambda b, q: (b * nq + q, qcol)),
            pl.BlockSpec((tq, IDX_Q_W), lambda b, q: (b * nq + q, icol)),
            pl.BlockSpec((N_IDX_HEADS, tq), lambda b, q: (0, b * nq + q)),
            pl.BlockSpec((seq, LANES), lambda b, q: (b, 0)),
            pl.BlockSpec((seq, LANES), lambda b, q: (b, 0)),
            pl.BlockSpec((seq, ATT_KV_W), lambda b, q: (b, kcol)),
            pl.BlockSpec((seq, ATT_KV_W), lambda b, q: (b, kcol + 1)),
            pl.BlockSpec((LANES, ATT_KV_W), lambda b, q: (0, 0)),
            pl.BlockSpec((LANES, ATT_KV_W), lambda b, q: (0, 0)),
        ],
        out_specs=pl.BlockSpec((tq, ATT_Q_W), lambda b, q: (b * nq + q, 0)),
        scratch_shapes=[
            pltpu.VMEM((seq, tq), F32),
            pltpu.VMEM((tq, seq), F32),
            pltpu.VMEM((N_KV_HEADS, LANES + seq, 2 * LANES), BF16),
            pltpu.VMEM((N_KV_HEADS, LANES + seq, 2 * LANES), BF16),
            pltpu.VMEM((N_KV_HEADS, rows4, 2 * LANES), BF16),
            pltpu.VMEM((rows4, kc), F32),
            pltpu.VMEM((rows4, kc), F32),
            pltpu.VMEM((rows4, kc), BF16),
            pltpu.VMEM((rows4, kc), BF16),
            pltpu.VMEM((rows4, LANES), F32),
            pltpu.VMEM((rows4, LANES), F32),
            pltpu.VMEM((N_KV_HEADS, rows4, 2 * LANES), F32),
            pltpu.VMEM((N_KV_HEADS, SUBLANES, LANES), F32),
            pltpu.VMEM((N_KV_HEADS, rows4, LANES), F32),
            pltpu.VMEM((tq, tq), BF16),
        ],
        compiler_params=pltpu.CompilerParams(
            dimension_semantics=("parallel", "arbitrary"), vmem_limit_bytes=VMEM_LIMIT),
        name="dsa_attention",
    )(p_att, p_att, iw_t, ik_lo, ik_hi, p_att, p_att, km, vm)


def _split3(x):
    hi = x.astype(BF16)
    r = x - hi.astype(F32)
    mid = r.astype(BF16)
    lo = (r - mid.astype(F32)).astype(BF16)
    return hi, mid, lo


HG_SAFE_EXPONENT = 80.0


def _hgrn_kernel(hf_ref, hr_q_ref, hr_i_ref, hr_g_ref, lb_ref, ng_ref, s0_ref,
                 y_ref, sT_out_ref, sT_scr, *, chunk, sub, heads, n_inner):
    c_idx = pl.program_id(2)

    @pl.when(c_idx == 0)
    def _():
        sT_scr[...] = s0_ref[...]

    n_sub = chunk // sub
    causal = (lax.broadcasted_iota(jnp.int32, (chunk, chunk), 0)
              >= lax.broadcasted_iota(jnp.int32, (chunk, chunk), 1))
    tri = causal.astype(BF16)
    t_iota = lax.broadcasted_iota(jnp.int32, (sub, LANES), 0)

    def head_inputs(r0, g):
        rows, cs = pl.ds(r0, chunk), slice(g * HG_DIM, (g + 1) * HG_DIM)
        lb = lb_ref[:, cs]
        q = _silu(hr_q_ref[rows, cs].astype(F32))
        fg = lb + (1.0 - lb) * jax.nn.sigmoid(hf_ref[rows, cs].astype(F32))
        v = hr_i_ref[rows, cs].astype(F32)
        l_hi, l_mid, l_lo = _split3(jnp.log(fg))
        b = _dot(tri, l_hi) + _dot(tri, l_mid) + _dot(tri, l_lo)
        return q, 1.0 - fg, v, b

    def carry_state(g, q, kk, vb, b):
        sT = sT_scr[g]
        b_last = b[chunk - 1:chunk, :]
        qe = (q * jnp.exp(b)).astype(BF16)
        o_state = _dot_nt(qe, sT.astype(BF16))
        khat = (kk * jnp.exp(b_last - b)).astype(BF16)
        sT_scr[g] = sT * jnp.exp(b_last) + _dot_tn(vb, khat)
        return qe, o_state

    def finish(r0, g, o):
        rows, cs = pl.ds(r0, chunk), slice(g * HG_DIM, (g + 1) * HG_DIM)
        o = o * lax.rsqrt(jnp.mean(o * o, axis=-1, keepdims=True) + RMS_EPS) * ng_ref[:, cs]
        o = o * _silu(hr_g_ref[rows, cs].astype(F32))
        y_ref[rows, cs] = o.astype(y_ref.dtype)

    def factored_chunk(c, carry):
        rows = pl.ds(pl.multiple_of(c * chunk, chunk), chunk)
        hs = [slice(g * HG_DIM, (g + 1) * HG_DIM) for g in range(heads)]
        lb = lb_ref[...]
        q = _silu(hr_q_ref[rows, :].astype(F32))
        fg = lb + (1.0 - lb) * jax.nn.sigmoid(hf_ref[rows, :].astype(F32))
        kk = 1.0 - fg
        vb = hr_i_ref[rows, :]
        l_hi, l_mid, l_lo = _split3(jnp.log(fg))
        b = _dot(tri, l_hi) + _dot(tri, l_mid) + _dot(tri, l_lo)
        b_last = b[chunk - 1:chunk, :]
        qe = (q * jnp.exp(b)).astype(BF16)
        ke = (kk * jnp.exp(-b)).astype(BF16)
        khat = (kk * jnp.exp(b_last - b)).astype(BF16)
        keep = jnp.exp(b_last)
        s_old = [sT_scr[g] for g in range(heads)]
        a = [_dot_nt(qe[:, cs], ke[:, cs]) for cs in hs]
        o_state = [_dot_nt(qe[:, cs], s_old[g].astype(BF16)) for g, cs in enumerate(hs)]
        s_add = [_dot_tn(vb[:, cs], khat[:, cs]) for cs in hs]
        a = [jnp.where(causal, x, 0.0).astype(BF16) for x in a]
        o = [o_state[g] + _dot(a[g], vb[:, cs]) for g, cs in enumerate(hs)]
        for g, cs in enumerate(hs):
            sT_scr[g] = s_old[g] * keep[:, cs] + s_add[g]
        o = [x * lax.rsqrt(jnp.mean(x * x, axis=-1, keepdims=True) + RMS_EPS) for x in o]
        o = jnp.concatenate(o, axis=1) * ng_ref[...] * _silu(hr_g_ref[rows, :].astype(F32))
        y_ref[rows, :] = o.astype(y_ref.dtype)
        return carry

    def guarded_chunk(c, carry):
        r0 = pl.multiple_of(c * chunk, chunk)
        for g in range(heads):
            q, kk, v, b = head_inputs(r0, g)
            vb = v.astype(BF16)
            _, o_state = carry_state(g, q, kk, vb, b)
            rows = []
            for i in range(n_sub):
                s0, s1 = i * sub, (i + 1) * sub
                bs, qs, ks, vs = b[s0:s1], q[s0:s1], kk[s0:s1], v[s0:s1]
                o_i = o_state[s0:s1]
                if i > 0:
                    r_i = b[s0 - 1:s0, :]
                    qt = (qs * jnp.exp(bs - r_i)).astype(BF16)
                    kt = (kk[:s0] * jnp.exp(r_i - b[:s0])).astype(BF16)
                    a_off = _dot_nt(qt, kt)
                    o_i = o_i + _dot(a_off.astype(BF16), vb[:s0])
                for s in range(sub):
                    e = jnp.exp(jnp.where(t_iota >= s, bs - bs[s:s + 1], -jnp.inf))
                    a_col = jnp.sum(qs * e * ks[s:s + 1], axis=1, keepdims=True)
                    o_i = o_i + a_col * vs[s:s + 1]
                rows.append(o_i)
            finish(r0, g, jnp.concatenate(rows, axis=0) if n_sub > 1 else rows[0])
        return carry

    factoring_safe = jnp.max(-jnp.log(lb_ref[...])) * chunk <= HG_SAFE_EXPONENT

    @pl.when(factoring_safe)
    def _():
        for c in range(n_inner):
            factored_chunk(c, 0)

    @pl.when(jnp.logical_not(factoring_safe))
    def _():
        lax.fori_loop(0, n_inner, guarded_chunk, 0)

    @pl.when(c_idx == pl.num_programs(2) - 1)
    def _():
        sT_out_ref[0] = sT_scr[...]


def _hgrn(p, cols, lb, ng, s0, *, batch, seq, chunk, sub, heads, n_inner):
    m = batch * seq
    rows = chunk * n_inner
    nc = seq // rows
    hw = heads * HG_DIM
    nhg = HG_HEADS // heads
    cf, cq, ci, cg = (cols[k] // hw for k in ("hf", "hq", "hi", "hg"))
    return pl.pallas_call(
        functools.partial(_hgrn_kernel, chunk=chunk, sub=sub, heads=heads, n_inner=n_inner),
        out_shape=(jax.ShapeDtypeStruct((m, HG_W), BF16),
                   jax.ShapeDtypeStruct((batch, HG_HEADS, HG_DIM, HG_DIM), F32)),
        grid=(batch, nhg, nc),
        in_specs=[
            pl.BlockSpec((rows, hw), lambda b, h, c: (b * nc + c, cf + h)),
            pl.BlockSpec((rows, hw), lambda b, h, c: (b * nc + c, cq + h)),
            pl.BlockSpec((rows, hw), lambda b, h, c: (b * nc + c, ci + h)),
            pl.BlockSpec((rows, hw), lambda b, h, c: (b * nc + c, cg + h)),
            pl.BlockSpec((1, hw), lambda b, h, c: (0, h)),
            pl.BlockSpec((1, hw), lambda b, h, c: (0, h)),
            pl.BlockSpec((heads, HG_DIM, HG_DIM), lambda b, h, c: (h, 0, 0)),
        ],
        out_specs=(
            pl.BlockSpec((rows, hw), lambda b, h, c: (b * nc + c, h)),
            pl.BlockSpec((1, heads, HG_DIM, HG_DIM), lambda b, h, c: (b, h, 0, 0)),
        ),
        scratch_shapes=[pltpu.VMEM((heads, HG_DIM, HG_DIM), F32)],
        compiler_params=pltpu.CompilerParams(
            dimension_semantics=("parallel", "parallel", "arbitrary"), vmem_limit_bytes=VMEM_LIMIT),
        name="hgrn2",
    )(p, p, p, p, lb, ng, s0)


def _merge_kernel(ya_ref, yh_ref, gts_a_ref, gts_h_ref, h_ref, wa_ref, wh_ref, wo_ref, g_ref, b_ref,
                  o_ref, acc_scr):
    j = pl.program_id(1)

    @pl.when(j == 0)
    def _():
        acc_scr[...] = jnp.zeros_like(acc_scr)

    ga = jax.nn.sigmoid(gts_a_ref[...].astype(F32))
    gh = jax.nn.sigmoid(gts_h_ref[...].astype(F32))
    merged = ga * _dot(ya_ref[...], wa_ref[...]) + gh * _dot(yh_ref[...], wh_ref[...])
    acc_scr[...] += _dot(merged.astype(BF16), wo_ref[...])

    @pl.when(j == pl.num_programs(1) - 1)
    def _():
        o_ref[...] = _layer_norm(ALPHA * h_ref[...] + acc_scr[...], g_ref[...], b_ref[...])


def _merge(ya, yh, gates, h1, wa, wh, wo, g, b, *, tm, tn):
    m = ya.shape[0]
    nj = D_MODEL // tn
    ca, cb = PROJ_COL["ga"] // tn, PROJ_COL["gb"] // tn
    return pl.pallas_call(
        _merge_kernel,
        out_shape=jax.ShapeDtypeStruct((m, D_MODEL), F32),
        grid=(m // tm, nj),
        in_specs=[
            pl.BlockSpec((tm, ATT_Q_W), lambda i, j: (i, 0)),
            pl.BlockSpec((tm, HG_W), lambda i, j: (i, 0)),
            pl.BlockSpec((tm, tn), lambda i, j: (i, ca + j)),
            pl.BlockSpec((tm, tn), lambda i, j: (i, cb + j)),
            pl.BlockSpec((tm, D_MODEL), lambda i, j: (i, 0)),
            pl.BlockSpec((ATT_Q_W, tn), lambda i, j: (0, j)),
            pl.BlockSpec((HG_W, tn), lambda i, j: (0, j)),
            pl.BlockSpec((tn, D_MODEL), lambda i, j: (j, 0)),
            pl.BlockSpec((1, D_MODEL), lambda i, j: (0, 0)),
            pl.BlockSpec((1, D_MODEL), lambda i, j: (0, 0)),
        ],
        out_specs=pl.BlockSpec((tm, D_MODEL), lambda i, j: (i, 0)),
        scratch_shapes=[pltpu.VMEM((tm, D_MODEL), F32)],
        compiler_params=pltpu.CompilerParams(
            dimension_semantics=("parallel", "arbitrary"), vmem_limit_bytes=VMEM_LIMIT),
        name="merge_ln",
    )(ya, yh, gates, gates, h1, wa, wh, wo, g, b)


def kernel(x, meta, ffn1_w_gate, ffn1_w_up, ffn1_w_down, ln1_g, ln1_b, w_in, idx_k_norm_g, idx_k_norm_b,
           hg_lb_logits, hg_norm_g, w_branch_att, w_branch_hg, w_out, ln2_g, ln2_b,
           ffn2_w_gate, ffn2_w_up, ffn2_w_down, ln3_g, ln3_b):
    batch, seq, _ = x.shape
    m = batch * seq
    xr = x.reshape(m, D_MODEL)
    bf = lambda w: w.astype(BF16)
    row = lambda v: v.reshape(1, -1)

    w_in_t = w_in[0].T
    w_idx = bf(jnp.pad(w_in_t[IN_OFFS[4]:IN_OFFS[6]].T, ((0, 0), (0, LANES - IDX_HEAD_DIM - N_IDX_HEADS))))
    idx_g = jnp.pad(idx_k_norm_g[0], (0, LANES - IDX_HEAD_DIM)).reshape(1, LANES)
    idx_b = jnp.pad(idx_k_norm_b[0], (0, LANES - IDX_HEAD_DIM)).reshape(1, LANES)
    lb = jnp.cumsum(jax.nn.softmax(hg_lb_logits.astype(F32), axis=0), axis=0)[0].reshape(1, HG_W)
    ng = hg_norm_g[0].reshape(1, HG_W)

    f1 = (ffn1_w_gate[0], ffn1_w_up[0], ffn1_w_down[0], row(ln1_g[0]), row(ln1_b[0]))
    f2 = (ffn2_w_gate[0], ffn2_w_up[0], ffn2_w_down[0], row(ln3_g[0]), row(ln3_b[0]))

    hm = _ffn_ln(meta.astype(F32), *f1, tm=N_META, tf=FFN_TF)
    pm = _proj_all(hm, w_in_t, META_TILE_SRC, tm=N_META, scaled_tiles=0)
    kv0 = META_COL["kv"]
    km = jnp.pad(pm[:, kv0:kv0 + ATT_KV_W], ((0, LANES - N_META), (0, 0)))
    vm = jnp.pad(pm[:, kv0 + ATT_KV_W:kv0 + 2 * ATT_KV_W], ((0, LANES - N_META), (0, 0)))
    s_zero = jnp.zeros((HG_HEADS, HG_DIM, HG_DIM), F32)
    meta_cols = {"hf": META_COL["hf"], "hi": META_COL["hi"], "hq": META_COL["hf"], "hg": META_COL["hi"]}
    _, s_meta = _hgrn(pm, meta_cols, lb, ng, s_zero, batch=1, seq=N_META, chunk=N_META, sub=N_META,
                      heads=HG_HEADS, n_inner=1)

    h1 = _ffn_ln(xr, *f1, tm=FFN_TM, tf=FFN_TF)
    p = _proj_all(h1, w_in_t, PROJ_TILE_SRC, tm=1024, scaled_tiles=ATT_Q_W // PROJ_TN)
    p_idx = _proj_idx(h1, w_idx, idx_g, idx_b, tm=1024)

    ikn = p_idx[:, :IDX_HEAD_DIM].astype(BF16)
    ik_lo = jnp.pad(ikn, ((0, 0), (0, LANES - IDX_HEAD_DIM)))
    ik_hi = jnp.pad(ikn, ((0, 0), (LANES - IDX_HEAD_DIM, 0)))
    iw_t = p_idx[:, IDX_HEAD_DIM:IDX_HEAD_DIM + N_IDX_HEADS].T
    y_att = _attention(p, iw_t, ik_lo, ik_hi, km, vm, batch=batch, seq=seq, tq=256, kc=512)
    y_hg, _ = _hgrn(p, PROJ_COL, lb, ng, s_meta[0], batch=batch, seq=seq, chunk=64, sub=16,
                    heads=HG_HEADS, n_inner=8)

    h2 = _merge(y_att, y_hg, p, h1, bf(w_branch_att[0]), bf(w_branch_hg[0]), bf(w_out[0]),
                row(ln2_g[0]), row(ln2_b[0]), tm=512, tn=1024)
    out = _ffn_ln(h2, *f2, tm=FFN_TM, tf=FFN_TF)
    return out.reshape(batch, seq, D_MODEL)
```

```python
import functools
import math

import jax
import jax.numpy as jnp
import numpy as np
from jax import lax
from jax.experimental import pallas as pl
from jax.experimental.pallas import tpu as pltpu

D_MODEL = 2048
N_META = 16
N_ATT_HEADS = 8
N_KV_HEADS = 2
ATT_GROUP = N_ATT_HEADS // N_KV_HEADS
ATT_HEAD_DIM = 128
N_IDX_HEADS = 16
IDX_HEAD_DIM = 64
TOPK = 256
HG_HEADS = 8
HG_DIM = 128
D_FF = 5632
LN_EPS = 1e-5
RMS_EPS = 1e-6
ALPHA = 2.0 ** 0.25

ATT_Q_W = N_ATT_HEADS * ATT_HEAD_DIM
ATT_KV_W = N_KV_HEADS * ATT_HEAD_DIM
IDX_Q_W = N_IDX_HEADS * IDX_HEAD_DIM
HG_W = HG_HEADS * HG_DIM

LANES = 128
SUBLANES = 8
VMEM_LIMIT = 56 * 1024 * 1024

F32 = jnp.float32
BF16 = jnp.bfloat16
INT_MIN = -2 ** 31
F32_LOWEST = float(np.finfo(np.float32).min)
LOG2E = math.log2(math.e)
ATT_Q_SCALE = ATT_HEAD_DIM ** -0.5 * LOG2E
POS_SHIFT = 6
POS_SPLIT = 1 << POS_SHIFT


def _dot(a, b):
    return jnp.dot(a, b, preferred_element_type=F32)


def _dot_nt(a, b):
    return lax.dot_general(a, b, (((1,), (1,)), ((), ())), preferred_element_type=F32)


def _dot_tn(a, b):
    return lax.dot_general(a, b, (((0,), (0,)), ((), ())), preferred_element_type=F32)


def _layer_norm(y, g, b):
    mu = jnp.mean(y, axis=-1, keepdims=True)
    d = y - mu
    var = jnp.mean(d * d, axis=-1, keepdims=True)
    return d * lax.rsqrt(var + LN_EPS) * g + b


def _silu(x):
    return x * jax.nn.sigmoid(x)


def _bf16_parts(c, n=3):
    parts = []
    for _ in range(n):
        p = float(np.asarray(c, np.float32).astype(jnp.bfloat16).astype(np.float32))
        parts.append(p)
        c = c - p
    return parts


FFN_TM = 1024
FFN_TF = 256
LN_ROWS = 128


def _ffn_ln_kernel(x_ref, wg_ref, wu_ref, wd_ref, g_ref, b_ref, o_ref, xb_scr):
    j = pl.program_id(1)

    @pl.when(j == 0)
    def _():
        xb_scr[...] = x_ref[...].astype(BF16)
        o_ref[...] = jnp.zeros_like(o_ref)

    xb = xb_scr[...]
    gate = _dot(xb, wg_ref[...].astype(BF16))
    up = _dot(xb, wu_ref[...].astype(BF16))
    a = (_silu(gate) * up).astype(BF16)
    o_ref[...] += _dot(a, wd_ref[...].astype(BF16))

    @pl.when(j == pl.num_programs(1) - 1)
    def _():
        n_rows = min(LN_ROWS, o_ref.shape[0])

        def ln_rows(r, carry):
            rows = pl.ds(pl.multiple_of(r * n_rows, n_rows), n_rows)
            y = ALPHA * x_ref[rows, :] + 0.5 * o_ref[rows, :]
            o_ref[rows, :] = _layer_norm(y, g_ref[...], b_ref[...])
            return carry

        lax.fori_loop(0, o_ref.shape[0] // n_rows, ln_rows, 0)


def _ffn_ln(x, wg, wu, wd, g, b, *, tm, tf):
    m = x.shape[0]
    grid = (m // tm, D_FF // tf)
    return pl.pallas_call(
        _ffn_ln_kernel,
        out_shape=jax.ShapeDtypeStruct((m, D_MODEL), F32),
        grid=grid,
        in_specs=[
            pl.BlockSpec((tm, D_MODEL), lambda i, j: (i, 0)),
            pl.BlockSpec((D_MODEL, tf), lambda i, j: (0, j)),
            pl.BlockSpec((D_MODEL, tf), lambda i, j: (0, j)),
            pl.BlockSpec((tf, D_MODEL), lambda i, j: (j, 0)),
            pl.BlockSpec((1, D_MODEL), lambda i, j: (0, 0)),
            pl.BlockSpec((1, D_MODEL), lambda i, j: (0, 0)),
        ],
        out_specs=pl.BlockSpec((tm, D_MODEL), lambda i, j: (i, 0)),
        scratch_shapes=[pltpu.VMEM((tm, D_MODEL), BF16)],
        compiler_params=pltpu.CompilerParams(
            dimension_semantics=("parallel", "arbitrary"), vmem_limit_bytes=VMEM_LIMIT),
        name="ffn_ln",
    )(x, wg, wu, wd, g, b)


IN_SPLITS = (ATT_Q_W, ATT_KV_W, ATT_KV_W, IDX_Q_W, IDX_HEAD_DIM, N_IDX_HEADS, HG_W, HG_W, HG_W, HG_W,
             D_MODEL, D_MODEL)
IN_OFFS = tuple(int(v) for v in np.cumsum((0,) + IN_SPLITS))
PROJ_TN = 1024
PROJ_GROUPS = {"aq": (IN_OFFS[0], ATT_Q_W), "iq": (IN_OFFS[3], IDX_Q_W), "hq": (IN_OFFS[6], HG_W),
               "hf": (IN_OFFS[7], HG_W), "hi": (IN_OFFS[8], HG_W), "hg": (IN_OFFS[9], HG_W),
               "ga": (IN_OFFS[10], D_MODEL), "gb": (IN_OFFS[11], D_MODEL), "kv": (IN_OFFS[1], 2 * ATT_KV_W)}
PROJ_SRC_ALIGN = 16


def _proj_layout(names):
    col, src = {}, []
    for name in names:
        start, width = PROJ_GROUPS[name]
        col[name] = len(src) * PROJ_TN
        src += [start + t * PROJ_TN for t in range(-(-width // PROJ_TN))]
    assert all(s % PROJ_SRC_ALIGN == 0 and s + PROJ_TN <= IN_OFFS[-1] for s in src)
    return col, tuple(src)


PROJ_COL, PROJ_TILE_SRC = _proj_layout(("aq", "iq", "hq", "hf", "hi", "hg", "ga", "gb", "kv"))
META_COL, META_TILE_SRC = _proj_layout(("hf", "hi", "kv"))


def _proj_all_kernel(src_ref, x_ref, wt_ref, o_ref, xb_scr, *, scaled_tiles, scale):
    @pl.when(pl.program_id(1) == 0)
    def _():
        xb_scr[...] = x_ref[...].astype(BF16)

    acc = _dot_nt(xb_scr[...], wt_ref[...].astype(BF16))
    acc = acc * jnp.where(pl.program_id(1) < scaled_tiles, scale, 1.0)
    o_ref[...] = acc.astype(o_ref.dtype)


def _idx_kernel(x_ref, w_ref, g_ref, b_ref, o_ref):
    p = _dot(x_ref[...].astype(BF16), w_ref[...])
    lane = lax.broadcasted_iota(jnp.int32, p.shape, 1)
    is_k = lane < IDX_HEAD_DIM
    mu = jnp.sum(jnp.where(is_k, p, 0.0), axis=-1, keepdims=True) * (1.0 / IDX_HEAD_DIM)
    d = jnp.where(is_k, p - mu, 0.0)
    var = jnp.sum(d * d, axis=-1, keepdims=True) * (1.0 / IDX_HEAD_DIM)
    kn = d * lax.rsqrt(var + LN_EPS) * g_ref[...] + b_ref[...]
    w_scale = (N_IDX_HEADS ** -0.5) * (IDX_HEAD_DIM ** -0.5)
    o_ref[...] = jnp.where(is_k, kn, p * w_scale)


def _proj_all(xb, w_in_t, tile_src, *, tm, scaled_tiles):
    m = xb.shape[0]
    n_tiles = len(tile_src)
    return pl.pallas_call(
        functools.partial(_proj_all_kernel, scaled_tiles=scaled_tiles, scale=ATT_Q_SCALE),
        out_shape=jax.ShapeDtypeStruct((m, n_tiles * PROJ_TN), BF16),
        grid_spec=pltpu.PrefetchScalarGridSpec(
            num_scalar_prefetch=1,
            grid=(m // tm, n_tiles),
            in_specs=[pl.BlockSpec((tm, D_MODEL), lambda i, j, src: (i, 0)),
                      pl.BlockSpec((pl.Element(PROJ_TN), pl.Element(D_MODEL)),
                                   lambda i, j, src: (src[j] * PROJ_SRC_ALIGN, 0))],
            out_specs=pl.BlockSpec((tm, PROJ_TN), lambda i, j, src: (i, j)),
            scratch_shapes=[pltpu.VMEM((tm, D_MODEL), BF16)],
        ),
        compiler_params=pltpu.CompilerParams(
            dimension_semantics=("parallel", "arbitrary"), vmem_limit_bytes=VMEM_LIMIT),
        name="proj_all",
    )(jnp.asarray([s // PROJ_SRC_ALIGN for s in tile_src], jnp.int32), xb, w_in_t)


def _proj_idx(xb, w, g, b, *, tm):
    m = xb.shape[0]
    return pl.pallas_call(
        _idx_kernel,
        out_shape=jax.ShapeDtypeStruct((m, LANES), F32),
        grid=(m // tm,),
        in_specs=[pl.BlockSpec((tm, D_MODEL), lambda i: (i, 0)),
                  pl.BlockSpec((D_MODEL, LANES), lambda i: (0, 0)),
                  pl.BlockSpec((1, LANES), lambda i: (0, 0)),
                  pl.BlockSpec((1, LANES), lambda i: (0, 0))],
        out_specs=pl.BlockSpec((tm, LANES), lambda i: (i, 0)),
        compiler_params=pltpu.CompilerParams(
            dimension_semantics=("parallel",), vmem_limit_bytes=VMEM_LIMIT),
        name="proj_idx",
    )(xb, w, g, b)


ATT_ROW_TILE = 32
ATT_BOUND_SLACK = 45.0
UNCHECKED_BITS = 20


def _key_to_f32(key):
    return pltpu.bitcast(key ^ ((key >> 31) & 0x7FFFFFFF), F32)


def _attn_kernel(aq_ref, iq_ref, iwt_ref, iklo_ref, ikhi_ref, k_ref, v_ref, km_ref, vm_ref,
                 o_ref,
                 score_scr, mb_scr, kaug_scr, vaug_scr, qaug_scr, s0_scr, s1_scr, p_scr, p1_scr,
                 macc_scr, m_scr, acc_scr, kn_scr, bound_scr, eye_scr, *, tq, kc, seq):
    qi = pl.program_id(1)
    n_kc = ((qi + 1) * tq + kc - 1) // kc
    kf = float(TOPK)

    @pl.when(qi == 0)
    def _():
        r = lax.broadcasted_iota(jnp.int32, (LANES + seq, LANES), 0)
        lane = lax.broadcasted_iota(jnp.int32, (LANES + seq, LANES), 1)
        pos = jnp.where(r < LANES, r, r - LANES + N_META)
        feat = jnp.where(lane < 3, pos >> POS_SHIFT, jnp.where(lane < 6, pos & (POS_SPLIT - 1), 0))
        feat = feat.astype(F32).astype(BF16)
        ones = jnp.ones((LANES + seq, LANES), BF16)
        eye_scr[...] = (lax.broadcasted_iota(jnp.int32, (tq, tq), 0)
                        == lax.broadcasted_iota(jnp.int32, (tq, tq), 1)).astype(BF16)
        for kvh in range(N_KV_HEADS):
            cs = slice(kvh * LANES, (kvh + 1) * LANES)
            kaug_scr[kvh, 0:LANES, 0:LANES] = km_ref[:, cs]
            kaug_scr[kvh, LANES:, 0:LANES] = k_ref[:, cs]
            kaug_scr[kvh, :, LANES:] = feat
            vaug_scr[kvh, 0:LANES, 0:LANES] = vm_ref[:, cs]
            vaug_scr[kvh, LANES:, 0:LANES] = v_ref[:, cs]
            vaug_scr[kvh, :, LANES:] = ones
            kf32 = kaug_scr[kvh, :, 0:LANES].astype(F32)
            k_norm2 = jnp.max(jnp.sum(kf32 * kf32, axis=1, keepdims=True))
            kn_scr[kvh] = jnp.broadcast_to(k_norm2, kn_scr.shape[1:])

    qcol = qi * tq + lax.broadcasted_iota(jnp.int32, (1, tq), 1)
    iwt = iwt_ref[...]

    def score_body(j, carry):
        off = pl.multiple_of(j * kc, kc)
        klo = iklo_ref[pl.ds(off, kc), :]
        khi = ikhi_ref[pl.ds(off, kc), :]
        acc = jnp.zeros((kc, tq), F32)
        for p in range(N_IDX_HEADS // 2):
            q2 = iq_ref[:, p * LANES:(p + 1) * LANES]
            acc = acc + jnp.maximum(_dot_nt(klo, q2), 0.0) * iwt[2 * p:2 * p + 1, :]
            acc = acc + jnp.maximum(_dot_nt(khi, q2), 0.0) * iwt[2 * p + 1:2 * p + 2, :]
        krow = off + lax.broadcasted_iota(jnp.int32, (kc, tq), 0)
        score_scr[pl.ds(off, kc), :] = jnp.where(krow <= qcol, acc, -jnp.inf)
        return carry

    lax.fori_loop(0, n_kc, score_body, 0)

    n_acc = 8
    acc_rows = n_acc * SUBLANES

    def count_ge(cand):
        def body(j, acc):
            off = pl.multiple_of(j * kc, kc)
            w = jnp.where(score_scr[pl.ds(off, kc), :] >= cand, 1.0, 0.0)
            return acc + jnp.sum(w.reshape(kc // acc_rows, acc_rows, tq), axis=0)

        acc = lax.fori_loop(0, n_kc, body, jnp.zeros((acc_rows, tq), F32))
        return jnp.sum(acc, axis=0, keepdims=True)

    c0 = count_ge(jnp.zeros((1, tq), F32))
    ok0 = c0 >= kf
    thr0 = jnp.where(ok0, 0, INT_MIN).astype(jnp.int32)
    cnt0 = jnp.where(ok0, c0, 0.0)

    short_row = qcol + 1 < TOPK

    def unsettled(cnt):
        return jnp.sum(jnp.where((cnt == kf) | short_row, 0.0, 1.0))

    n_bits = 31
    group = 4

    def bit_cond(carry):
        i, _, _, pending = carry
        return jnp.logical_and(i < n_bits, pending > 0.0)

    def resolve_bit(i, thr, cnt):
        shift = jnp.maximum(n_bits - 1 - i, 0)
        cand = thr | jnp.where(i < n_bits, jnp.left_shift(jnp.int32(1), shift), 0)
        c = count_ge(_key_to_f32(cand))
        ok = c >= kf
        return jnp.where(ok, cand, thr), jnp.where(ok, c, cnt)

    def bit_body(carry):
        i, thr, cnt, _ = carry
        for b in range(group):
            thr, cnt = resolve_bit(i + b, thr, cnt)
        return i + group, thr, cnt, unsettled(cnt)

    thr1, cnt1 = lax.fori_loop(0, UNCHECKED_BITS, lambda i, c: resolve_bit(i, *c), (thr0, cnt0))
    _, thr, cnt, _ = lax.while_loop(
        bit_cond, bit_body, (jnp.int32(UNCHECKED_BITS), thr1, cnt1, unsettled(cnt1)))
    t_f = jnp.where(thr == INT_MIN, F32_LOWEST, _key_to_f32(thr))

    @pl.when(jnp.max(cnt) > kf)
    def _():
        def gt_body(j, acc):
            off = pl.multiple_of(j * kc, kc)
            w = jnp.where(score_scr[pl.ds(off, kc), :] > t_f, 1.0, 0.0)
            return acc + jnp.sum(w, axis=0, keepdims=True)

        need = kf - lax.fori_loop(0, n_kc, gt_body, jnp.zeros((1, tq), F32))
        lower = (lax.broadcasted_iota(jnp.int32, (kc, kc), 0)
                 >= lax.broadcasted_iota(jnp.int32, (kc, kc), 1)).astype(BF16)

        def tie_body(j, before):
            off = pl.multiple_of(j * kc, kc)
            sc = score_scr[pl.ds(off, kc), :]
            eq = sc == t_f
            eqf = jnp.where(eq, 1.0, 0.0)
            rank = before + _dot(lower, eqf.astype(BF16))
            score_scr[pl.ds(off, kc), :] = jnp.where(eq & (rank > need), -jnp.inf, sc)
            return before + jnp.sum(eqf, axis=0, keepdims=True)

        lax.fori_loop(0, n_kc, tie_body, jnp.zeros((1, tq), F32))

    eye = eye_scr[...]

    def mask_body(j, last_sel):
        off = pl.multiple_of(j * kc, kc)
        picked = score_scr[pl.ds(off, kc), :] >= t_f
        sel = _dot_nt(eye, jnp.where(picked, 1.0, 0.0).astype(BF16))
        mb_scr[:, pl.ds(off, kc)] = jnp.where(sel > 0.5, 0.0, -jnp.inf)
        krow = (off + lax.broadcasted_iota(jnp.int32, (kc, tq), 0)).astype(F32)
        hit = jnp.where(picked, krow, -1.0)
        return jnp.maximum(last_sel, jnp.max(hit.reshape(kc // SUBLANES, SUBLANES, tq), axis=0))

    last_sel = lax.fori_loop(0, n_kc, mask_body, jnp.full((SUBLANES, tq), -1.0, F32))
    last_pos = jnp.maximum(jnp.max(last_sel, axis=0, keepdims=True) + N_META, N_META - 1.0)

    rt = ATT_ROW_TILE
    lane_q = lax.broadcasted_iota(jnp.int32, (1, LANES), 1)
    meta_mask = jnp.where(lax.broadcasted_iota(jnp.int32, (rt, LANES), 1) < N_META, 0.0, -jnp.inf)

    s_slots = (s0_scr, s1_scr)

    def logits(kvh, krow0, width, slot):
        s_slots[slot][:, 0:width] = _dot_nt(qaug_scr[kvh], kaug_scr[kvh, pl.ds(krow0, width), :])

    def max_tiles(kvh, slot, width, mask_fn):
        s_ref = s_slots[slot]
        for r0 in range(0, tq, rt):
            mb = mask_fn(r0)
            for g in range(ATT_GROUP):
                rows = slice(g * tq + r0, g * tq + r0 + rt)
                mx = s_ref[rows, 0:LANES] + mb[:, 0:LANES]
                for c in range(1, width // LANES):
                    cols = slice(c * LANES, (c + 1) * LANES)
                    mx = jnp.maximum(mx, s_ref[rows, cols] + mb[:, cols])
                macc_scr[rows, :] = jnp.maximum(macc_scr[rows, :], mx)

    p_slots = (p_scr, p1_scr)

    def exp_tiles(kvh, slot, width, mask_fn):
        s_ref, p_ref = s_slots[slot], p_slots[slot]
        for r0 in range(0, tq, rt):
            mb = mask_fn(r0)
            for g in range(ATT_GROUP):
                rows = slice(g * tq + r0, g * tq + r0 + rt)
                m_row = m_scr[rows, :]
                for c in range(width // LANES):
                    cols = slice(c * LANES, (c + 1) * LANES)
                    p_ref[rows, cols] = jnp.exp2(s_ref[rows, cols] + mb[:, cols] - m_row).astype(BF16)

    def add_pv(kvh, slot, krow0, width):
        vblk = vaug_scr[kvh, pl.ds(krow0, width), :]
        half = ATT_GROUP * tq // 2
        for r0 in (0, half):
            acc_scr[kvh, r0:r0 + half, :] += _dot(p_slots[slot][r0:r0 + half, 0:width], vblk)

    def krow(j):
        return pl.multiple_of(LANES + j * kc, LANES)

    def chunk_mask(j):
        off = pl.multiple_of(j * kc, kc)
        return lambda r0: mb_scr[r0:r0 + rt, pl.ds(off, kc)]

    def exp_pv_pass(kvh):
        def chunk(j, slot):
            logits(kvh, krow(j), kc, slot)
            exp_tiles(kvh, slot, kc, chunk_mask(j))
            add_pv(kvh, slot, krow(j), kc)

        logits(kvh, 0, LANES, 0)
        exp_tiles(kvh, 0, LANES, lambda r0: meta_mask)
        add_pv(kvh, 0, 0, LANES)

        def pair_body(i, carry):
            chunk(2 * i, 0)
            chunk(2 * i + 1, 1)
            return carry

        lax.fori_loop(0, n_kc // 2, pair_body, 0)

        @pl.when(n_kc % 2 == 1)
        def _():
            chunk(n_kc - 1, 0)

    def max_pass(kvh):
        def step(j, slot, lookahead):
            if lookahead:
                logits(kvh, krow(j + 1), kc, 1 - slot)
            max_tiles(kvh, slot, kc, chunk_mask(j))

        logits(kvh, 0, LANES, 0)
        max_tiles(kvh, 0, LANES, lambda r0: meta_mask)
        logits(kvh, krow(0), kc, 0)
        n_pairs = (n_kc - 1) // 2

        def pair_body(i, carry):
            step(2 * i, 0, True)
            step(2 * i + 1, 1, True)
            return carry

        lax.fori_loop(0, n_pairs, pair_body, 0)
        j0 = 2 * n_pairs
        two_left = n_kc - j0 == 2

        @pl.when(two_left)
        def _():
            step(j0, 0, True)
            step(j0 + 1, 1, False)

        @pl.when(jnp.logical_not(two_left))
        def _():
            step(j0, 0, False)

    for h in range(N_ATT_HEADS):
        kvh, g = divmod(h, ATT_GROUP)
        parts = _bf16_parts(2.0 ** (-8.0 * (h + 1) / N_ATT_HEADS) * LOG2E)
        vals = [POS_SPLIT * p for p in parts] + parts
        qfeat = jnp.zeros((1, LANES), F32)
        for i, val in enumerate(vals):
            qfeat = jnp.where(lane_q == i, val, qfeat)
        qaug_scr[kvh, g * tq:(g + 1) * tq, 0:LANES] = aq_ref[:, h * LANES:(h + 1) * LANES]
        qaug_scr[kvh, g * tq:(g + 1) * tq, LANES:] = jnp.broadcast_to(qfeat, (tq, LANES)).astype(BF16)

    a_pos = jnp.floor(last_pos * (1.0 / POS_SPLIT))
    b_pos = last_pos - a_pos * POS_SPLIT
    to_rows = lambda v: _dot_nt(eye, jnp.broadcast_to(v, (LANES, tq)).astype(BF16))
    last_pos_rows = to_rows(a_pos) * POS_SPLIT + to_rows(b_pos)
    slack = jnp.float32(0.0)
    for kvh in range(N_KV_HEADS):
        qf = qaug_scr[kvh, :, 0:LANES].astype(F32)
        qk_max = jnp.sqrt(jnp.sum(qf * qf, axis=1, keepdims=True) * kn_scr[kvh, 0:1, 0:1])
        slack = jnp.maximum(slack, jnp.max(qk_max))
        for g in range(ATT_GROUP):
            rows = slice(g * tq, (g + 1) * tq)
            c = 2.0 ** (-8.0 * (kvh * ATT_GROUP + g + 1) / N_ATT_HEADS) * LOG2E
            bound_scr[kvh, rows, :] = c * last_pos_rows + qk_max[rows]
    bound_ok = slack <= ATT_BOUND_SLACK

    for kvh in range(N_KV_HEADS):
        @pl.when(bound_ok)
        def _():
            m_scr[...] = bound_scr[kvh]

        @pl.when(jnp.logical_not(bound_ok))
        def _():
            macc_scr[...] = jnp.full_like(macc_scr, -jnp.inf)
            max_pass(kvh)
            m_scr[...] = jnp.broadcast_to(jnp.max(macc_scr[...], axis=1, keepdims=True), m_scr.shape)

        acc_scr[kvh] = jnp.zeros(acc_scr.shape[1:], F32)
        exp_pv_pass(kvh)

    for h in range(N_ATT_HEADS):
        kvh, g = divmod(h, ATT_GROUP)
        acc = acc_scr[kvh, g * tq:(g + 1) * tq, :]
        o_ref[:, h * LANES:(h + 1) * LANES] = (acc[:, 0:LANES] / acc[:, LANES:LANES + 1]).astype(o_ref.dtype)


def _attention(p_att, iw_t, ik_lo, ik_hi, km, vm, *, batch, seq, tq, kc):
    m = batch * seq
    nq = seq // tq
    qcol, icol, kcol = PROJ_COL["aq"] // ATT_Q_W, PROJ_COL["iq"] // IDX_Q_W, PROJ_COL["kv"] // ATT_KV_W
    rows4 = ATT_GROUP * tq
    return pl.pallas_call(
        functools.partial(_attn_kernel, tq=tq, kc=kc, seq=seq),
        out_shape=jax.ShapeDtypeStruct((m, ATT_Q_W), BF16),
        grid=(batch, nq),
        in_specs=[
            pl.BlockSpec((tq, ATT_Q_W), lambda b, q: (b * nq + q, qcol)),
            pl.BlockSpec((tq, IDX_Q_W), lambda b, q: (b * nq + q, icol)),
            pl.BlockSpec((N_IDX_HEADS, tq), lambda b, q: (0, b * nq + q)),
            pl.BlockSpec((seq, LANES), lambda b, q: (b, 0)),
            pl.BlockSpec((seq, LANES), lambda b, q: (b, 0)),
            pl.BlockSpec((seq, ATT_KV_W), lambda b, q: (b, kcol)),
            pl.BlockSpec((seq, ATT_KV_W), lambda b, q: (b, kcol + 1)),
            pl.BlockSpec((LANES, ATT_KV_W), lambda b, q: (0, 0)),
            pl.BlockSpec((LANES, ATT_KV_W), lambda b, q: (0, 0)),
        ],
        out_specs=pl.BlockSpec((tq, ATT_Q_W), lambda b, q: (b * nq + q, 0)),
        scratch_shapes=[
            pltpu.VMEM((seq, tq), F32),
            pltpu.VMEM((tq, seq), F32),
            pltpu.VMEM((N_KV_HEADS, LANES + seq, 2 * LANES), BF16),
            pltpu.VMEM((N_KV_HEADS, LANES + seq, 2 * LANES), BF16),
            pltpu.VMEM((N_KV_HEADS, rows4, 2 * LANES), BF16),
            pltpu.VMEM((rows4, kc), F32),
            pltpu.VMEM((rows4, kc), F32),
            pltpu.VMEM((rows4, kc), BF16),
            pltpu.VMEM((rows4, kc), BF16),
            pltpu.VMEM((rows4, LANES), F32),
            pltpu.VMEM((rows4, LANES), F32),
            pltpu.VMEM((N_KV_HEADS, rows4, 2 * LANES), F32),
            pltpu.VMEM((N_KV_HEADS, SUBLANES, LANES), F32),
            pltpu.VMEM((N_KV_HEADS, rows4, LANES), F32),
            pltpu.VMEM((tq, tq), BF16),
        ],
        compiler_params=pltpu.CompilerParams(
            dimension_semantics=("parallel", "arbitrary"), vmem_limit_bytes=VMEM_LIMIT),
        name="dsa_attention",
    )(p_att, p_att, iw_t, ik_lo, ik_hi, p_att, p_att, km, vm)


def _split3(x):
    hi = x.astype(BF16)
    r = x - hi.astype(F32)
    mid = r.astype(BF16)
    lo = (r - mid.astype(F32)).astype(BF16)
    return hi, mid, lo


HG_SAFE_EXPONENT = 80.0


def _hgrn_kernel(hf_ref, hr_q_ref, hr_i_ref, hr_g_ref, lb_ref, ng_ref, s0_ref,
                 y_ref, sT_out_ref, sT_scr, *, chunk, sub, heads, n_inner):
    c_idx = pl.program_id(2)

    @pl.when(c_idx == 0)
    def _():
        sT_scr[...] = s0_ref[...]

    n_sub = chunk // sub
    causal = (lax.broadcasted_iota(jnp.int32, (chunk, chunk), 0)
              >= lax.broadcasted_iota(jnp.int32, (chunk, chunk), 1))
    tri = causal.astype(BF16)
    t_iota = lax.broadcasted_iota(jnp.int32, (sub, LANES), 0)

    def head_inputs(r0, g):
        rows, cs = pl.ds(r0, chunk), slice(g * HG_DIM, (g + 1) * HG_DIM)
        lb = lb_ref[:, cs]
        q = _silu(hr_q_ref[rows, cs].astype(F32))
        fg = lb + (1.0 - lb) * jax.nn.sigmoid(hf_ref[rows, cs].astype(F32))
        v = hr_i_ref[rows, cs].astype(F32)
        l_hi, l_mid, l_lo = _split3(jnp.log(fg))
        b = _dot(tri, l_hi) + _dot(tri, l_mid) + _dot(tri, l_lo)
        return q, 1.0 - fg, v, b

    def carry_state(g, q, kk, vb, b):
        sT = sT_scr[g]
        b_last = b[chunk - 1:chunk, :]
        qe = (q * jnp.exp(b)).astype(BF16)
        o_state = _dot_nt(qe, sT.astype(BF16))
        khat = (kk * jnp.exp(b_last - b)).astype(BF16)
        sT_scr[g] = sT * jnp.exp(b_last) + _dot_tn(vb, khat)
        return qe, o_state

    def finish(r0, g, o):
        rows, cs = pl.ds(r0, chunk), slice(g * HG_DIM, (g + 1) * HG_DIM)
        o = o * lax.rsqrt(jnp.mean(o * o, axis=-1, keepdims=True) + RMS_EPS) * ng_ref[:, cs]
        o = o * _silu(hr_g_ref[rows, cs].astype(F32))
        y_ref[rows, cs] = o.astype(y_ref.dtype)

    def factored_chunk(c, carry):
        rows = pl.ds(pl.multiple_of(c * chunk, chunk), chunk)
        hs = [slice(g * HG_DIM, (g + 1) * HG_DIM) for g in range(heads)]
        lb = lb_ref[...]
        q = _silu(hr_q_ref[rows, :].astype(F32))
        fg = lb + (1.0 - lb) * jax.nn.sigmoid(hf_ref[rows, :].astype(F32))
        kk = 1.0 - fg
        vb = hr_i_ref[rows, :]
        l_hi, l_mid, l_lo = _split3(jnp.log(fg))
        b = _dot(tri, l_hi) + _dot(tri, l_mid) + _dot(tri, l_lo)
        b_last = b[chunk - 1:chunk, :]
        qe = (q * jnp.exp(b)).astype(BF16)
        ke = (kk * jnp.exp(-b)).astype(BF16)
        khat = (kk * jnp.exp(b_last - b)).astype(BF16)
        keep = jnp.exp(b_last)
        s_old = [sT_scr[g] for g in range(heads)]
        a = [_dot_nt(qe[:, cs], ke[:, cs]) for cs in hs]
        o_state = [_dot_nt(qe[:, cs], s_old[g].astype(BF16)) for g, cs in enumerate(hs)]
        s_add = [_dot_tn(vb[:, cs], khat[:, cs]) for cs in hs]
        a = [jnp.where(causal, x, 0.0).astype(BF16) for x in a]
        o = [o_state[g] + _dot(a[g], vb[:, cs]) for g, cs in enumerate(hs)]
        for g, cs in enumerate(hs):
            sT_scr[g] = s_old[g] * keep[:, cs] + s_add[g]
        o = [x * lax.rsqrt(jnp.mean(x * x, axis=-1, keepdims=True) + RMS_EPS) for x in o]
        o = jnp.concatenate(o, axis=1) * ng_ref[...] * _silu(hr_g_ref[rows, :].astype(F32))
        y_ref[rows, :] = o.astype(y_ref.dtype)
        return carry

    def guarded_chunk(c, carry):
        r0 = pl.multiple_of(c * chunk, chunk)
        for g in range(heads):
            q, kk, v, b = head_inputs(r0, g)
            vb = v.astype(BF16)
            _, o_state = carry_state(g, q, kk, vb, b)
            rows = []
            for i in range(n_sub):
                s0, s1 = i * sub, (i + 1) * sub
                bs, qs, ks, vs = b[s0:s1], q[s0:s1], kk[s0:s1], v[s0:s1]
                o_i = o_state[s0:s1]
                if i > 0:
                    r_i = b[s0 - 1:s0, :]
                    qt = (qs * jnp.exp(bs - r_i)).astype(BF16)
                    kt = (kk[:s0] * jnp.exp(r_i - b[:s0])).astype(BF16)
                    a_off = _dot_nt(qt, kt)
                    o_i = o_i + _dot(a_off.astype(BF16), vb[:s0])
                for s in range(sub):
                    e = jnp.exp(jnp.where(t_iota >= s, bs - bs[s:s + 1], -jnp.inf))
                    a_col = jnp.sum(qs * e * ks[s:s + 1], axis=1, keepdims=True)
                    o_i = o_i + a_col * vs[s:s + 1]
                rows.append(o_i)
            finish(r0, g, jnp.concatenate(rows, axis=0) if n_sub > 1 else rows[0])
        return carry

    factoring_safe = jnp.max(-jnp.log(lb_ref[...])) * chunk <= HG_SAFE_EXPONENT

    @pl.when(factoring_safe)
    def _():
        for c in range(n_inner):
            factored_chunk(c, 0)

    @pl.when(jnp.logical_not(factoring_safe))
    def _():
        lax.fori_loop(0, n_inner, guarded_chunk, 0)

    @pl.when(c_idx == pl.num_programs(2) - 1)
    def _():
        sT_out_ref[0] = sT_scr[...]


def _hgrn(p, cols, lb, ng, s0, *, batch, seq, chunk, sub, heads, n_inner):
    m = batch * seq
    rows = chunk * n_inner
    nc = seq // rows
    hw = heads * HG_DIM
    nhg = HG_HEADS // heads
    cf, cq, ci, cg = (cols[k] // hw for k in ("hf", "hq", "hi", "hg"))
    return pl.pallas_call(
        functools.partial(_hgrn_kernel, chunk=chunk, sub=sub, heads=heads, n_inner=n_inner),
        out_shape=(jax.ShapeDtypeStruct((m, HG_W), BF16),
                   jax.ShapeDtypeStruct((batch, HG_HEADS, HG_DIM, HG_DIM), F32)),
        grid=(batch, nhg, nc),
        in_specs=[
            pl.BlockSpec((rows, hw), lambda b, h, c: (b * nc + c, cf + h)),
            pl.BlockSpec((rows, hw), lambda b, h, c: (b * nc + c, cq + h)),
            pl.BlockSpec((rows, hw), lambda b, h, c: (b * nc + c, ci + h)),
            pl.BlockSpec((rows, hw), lambda b, h, c: (b * nc + c, cg + h)),
            pl.BlockSpec((1, hw), lambda b, h, c: (0, h)),
            pl.BlockSpec((1, hw), lambda b, h, c: (0, h)),
            pl.BlockSpec((heads, HG_DIM, HG_DIM), lambda b, h, c: (h, 0, 0)),
        ],
        out_specs=(
            pl.BlockSpec((rows, hw), lambda b, h, c: (b * nc + c, h)),
            pl.BlockSpec((1, heads, HG_DIM, HG_DIM), lambda b, h, c: (b, h, 0, 0)),
        ),
        scratch_shapes=[pltpu.VMEM((heads, HG_DIM, HG_DIM), F32)],
        compiler_params=pltpu.CompilerParams(
            dimension_semantics=("parallel", "parallel", "arbitrary"), vmem_limit_bytes=VMEM_LIMIT),
        name="hgrn2",
    )(p, p, p, p, lb, ng, s0)


def _merge_kernel(ya_ref, yh_ref, gts_a_ref, gts_h_ref, h_ref, wa_ref, wh_ref, wo_ref, g_ref, b_ref,
                  o_ref, acc_scr):
    j = pl.program_id(1)

    @pl.when(j == 0)
    def _():
        acc_scr[...] = jnp.zeros_like(acc_scr)

    ga = jax.nn.sigmoid(gts_a_ref[...].astype(F32))
    gh = jax.nn.sigmoid(gts_h_ref[...].astype(F32))
    merged = ga * _dot(ya_ref[...], wa_ref[...]) + gh * _dot(yh_ref[...], wh_ref[...])
    acc_scr[...] += _dot(merged.astype(BF16), wo_ref[...])

    @pl.when(j == pl.num_programs(1) - 1)
    def _():
        o_ref[...] = _layer_norm(ALPHA * h_ref[...] + acc_scr[...], g_ref[...], b_ref[...])


def _merge(ya, yh, gates, h1, wa, wh, wo, g, b, *, tm, tn):
    m = ya.shape[0]
    nj = D_MODEL // tn
    ca, cb = PROJ_COL["ga"] // tn, PROJ_COL["gb"] // tn
    return pl.pallas_call(
        _merge_kernel,
        out_shape=jax.ShapeDtypeStruct((m, D_MODEL), F32),
        grid=(m // tm, nj),
        in_specs=[
            pl.BlockSpec((tm, ATT_Q_W), lambda i, j: (i, 0)),
            pl.BlockSpec((tm, HG_W), lambda i, j: (i, 0)),
            pl.BlockSpec((tm, tn), lambda i, j: (i, ca + j)),
            pl.BlockSpec((tm, tn), lambda i, j: (i, cb + j)),
            pl.BlockSpec((tm, D_MODEL), lambda i, j: (i, 0)),
            pl.BlockSpec((ATT_Q_W, tn), lambda i, j: (0, j)),
            pl.BlockSpec((HG_W, tn), lambda i, j: (0, j)),
            pl.BlockSpec((tn, D_MODEL), lambda i, j: (j, 0)),
            pl.BlockSpec((1, D_MODEL), lambda i, j: (0, 0)),
            pl.BlockSpec((1, D_MODEL), lambda i, j: (0, 0)),
        ],
        out_specs=pl.BlockSpec((tm, D_MODEL), lambda i, j: (i, 0)),
        scratch_shapes=[pltpu.VMEM((tm, D_MODEL), F32)],
        compiler_params=pltpu.CompilerParams(
            dimension_semantics=("parallel", "arbitrary"), vmem_limit_bytes=VMEM_LIMIT),
        name="merge_ln",
    )(ya, yh, gates, gates, h1, wa, wh, wo, g, b)


def kernel(x, meta, ffn1_w_gate, ffn1_w_up, ffn1_w_down, ln1_g, ln1_b, w_in, idx_k_norm_g, idx_k_norm_b,
           hg_lb_logits, hg_norm_g, w_branch_att, w_branch_hg, w_out, ln2_g, ln2_b,
           ffn2_w_gate, ffn2_w_up, ffn2_w_down, ln3_g, ln3_b):
    batch, seq, _ = x.shape
    m = batch * seq
    xr = x.reshape(m, D_MODEL)
    bf = lambda w: w.astype(BF16)
    row = lambda v: v.reshape(1, -1)

    w_in_t = w_in[0].T
    w_idx = bf(jnp.pad(w_in_t[IN_OFFS[4]:IN_OFFS[6]].T, ((0, 0), (0, LANES - IDX_HEAD_DIM - N_IDX_HEADS))))
    idx_g = jnp.pad(idx_k_norm_g[0], (0, LANES - IDX_HEAD_DIM)).reshape(1, LANES)
    idx_b = jnp.pad(idx_k_norm_b[0], (0, LANES - IDX_HEAD_DIM)).reshape(1, LANES)
    lb = jnp.cumsum(jax.nn.softmax(hg_lb_logits.astype(F32), axis=0), axis=0)[0].reshape(1, HG_W)
    ng = hg_norm_g[0].reshape(1, HG_W)

    f1 = (ffn1_w_gate[0], ffn1_w_up[0], ffn1_w_down[0], row(ln1_g[0]), row(ln1_b[0]))
    f2 = (ffn2_w_gate[0], ffn2_w_up[0], ffn2_w_down[0], row(ln3_g[0]), row(ln3_b[0]))

    hm = _ffn_ln(meta.astype(F32), *f1, tm=N_META, tf=FFN_TF)
    pm = _proj_all(hm, w_in_t, META_TILE_SRC, tm=N_META, scaled_tiles=0)
    kv0 = META_COL["kv"]
    km = jnp.pad(pm[:, kv0:kv0 + ATT_KV_W], ((0, LANES - N_META), (0, 0)))
    vm = jnp.pad(pm[:, kv0 + ATT_KV_W:kv0 + 2 * ATT_KV_W], ((0, LANES - N_META), (0, 0)))
    s_zero = jnp.zeros((HG_HEADS, HG_DIM, HG_DIM), F32)
    meta_cols = {"hf": META_COL["hf"], "hi": META_COL["hi"], "hq": META_COL["hf"], "hg": META_COL["hi"]}
    _, s_meta = _hgrn(pm, meta_cols, lb, ng, s_zero, batch=1, seq=N_META, chunk=N_META, sub=N_META,
                      heads=HG_HEADS, n_inner=1)

    h1 = _ffn_ln(xr, *f1, tm=FFN_TM, tf=FFN_TF)
    p = _proj_all(h1, w_in_t, PROJ_TILE_SRC, tm=1024, scaled_tiles=ATT_Q_W // PROJ_TN)
    p_idx = _proj_idx(h1, w_idx, idx_g, idx_b, tm=1024)

    ikn = p_idx[:, :IDX_HEAD_DIM].astype(BF16)
    ik_lo = jnp.pad(ikn, ((0, 0), (0, LANES - IDX_HEAD_DIM)))
    ik_hi = jnp.pad(ikn, ((0, 0), (LANES - IDX_HEAD_DIM, 0)))
    iw_t = p_idx[:, IDX_HEAD_DIM:IDX_HEAD_DIM + N_IDX_HEADS].T
    y_att = _attention(p, iw_t, ik_lo, ik_hi, km, vm, batch=batch, seq=seq, tq=256, kc=512)
    y_hg, _ = _hgrn(p, PROJ_COL, lb, ng, s_meta[0], batch=batch, seq=seq, chunk=64, sub=16,
                    heads=HG_HEADS, n_inner=8)

    h2 = _merge(y_att, y_hg, p, h1, bf(w_branch_att[0]), bf(w_branch_hg[0]), bf(w_out[0]),
                row(ln2_g[0]), row(ln2_b[0]), tm=512, tn=1024)
    out = _ffn_ln(h2, *f2, tm=FFN_TM, tf=FFN_TF)
    return out.reshape(batch, seq, D_MODEL)
```
